```python
import functools
import jax, jax.numpy as jnp
from jax import lax
import numpy as np

D_MODEL = 1024
BATCH = 2
SEQ = 8192
DEPTH = 1
DEC_BATCH = 32
DEC_SEQ = 8
PAST_LEN = 16384
PAGE_SIZE = 128

RET_HEADS = 4
RET_WIDTH = D_MODEL // 2
RET_DK = RET_WIDTH // RET_HEADS
RET_DV = RET_DK
RET_THETA = 10000.0
RET_CHUNK = 128
ATT_HD = 64
ATT_WIDTH = D_MODEL - RET_WIDTH
ATT_HEADS = ATT_WIDTH // ATT_HD
DIL_PATTERNS = ((128, 1), (512, 4), (2048, 16))
MAX_WINDOW = 2048
DIL_BLOCK = 128
ROPE_THETA = 10000.0
MIX_WIDTH = RET_WIDTH + ATT_WIDTH
D_FF = 4 * D_MODEL
NORM_EPS = 1e-6
NEG_INF = -1e30
IN_SPLITS = (RET_WIDTH, RET_WIDTH, RET_HEADS * RET_DV, RET_HEADS * RET_DV, ATT_WIDTH, ATT_WIDTH, ATT_WIDTH)
IN_WIDTH = sum(IN_SPLITS)
F32 = jnp.float32

kernel_name = 'hymba_retention_dilated_swa_step'


def rmsnorm(x, gain):
    xf = x.astype(F32)
    y = xf * lax.rsqrt(jnp.mean(xf * xf, axis=-1, keepdims=True) + NORM_EPS)
    return (y * gain.astype(F32)).astype(x.dtype)


def rope_half(x, pos):
    d = x.shape[-1]
    inv = 1.0 / (ROPE_THETA ** (jnp.arange(0, d, 2, dtype=F32) / d))
    ang = pos.astype(F32)[:, None] * inv[None, :]
    cos = jnp.cos(ang)[None, :, None, :]
    sin = jnp.sin(ang)[None, :, None, :]
    xf = x.astype(F32)
    x1, x2 = xf[..., : d // 2], xf[..., d // 2:]
    return jnp.concatenate([x1 * cos - x2 * sin, x1 * sin + x2 * cos], axis=-1).astype(x.dtype)


def retention_rotate(x, pos):
    d = x.shape[-1]
    inv = 1.0 / (RET_THETA ** jnp.linspace(0.0, 1.0, d // 2, dtype=F32))
    ang = pos.astype(F32)[:, None] * inv[None, :]
    cos = jnp.cos(ang)[None, :, None, :]
    sin = jnp.sin(ang)[None, :, None, :]
    xf = x.astype(F32).reshape(x.shape[:-1] + (d // 2, 2))
    xe, xo = xf[..., 0], xf[..., 1]
    out = jnp.stack([xe * cos - xo * sin, xo * cos + xe * sin], axis=-1)
    return out.reshape(x.shape).astype(x.dtype)


def retention_chunkwise(q, k, v, s0, chunk):
    B, L, H, dk = q.shape
    dv = v.shape[-1]
    n = L // chunk
    log_g = jnp.log1p(-jnp.exp2(-5.0 - jnp.arange(H, dtype=F32)))
    idx = jnp.arange(chunk, dtype=F32)
    diff = idx[:, None] - idx[None, :]
    intra = jnp.where(diff >= 0, jnp.exp(log_g[:, None, None] * jnp.maximum(diff, 0.0)), 0.0)
    q_dec = jnp.exp(log_g[:, None] * (idx + 1.0)).T
    k_dec = jnp.exp(log_g[:, None] * (chunk - 1.0 - idx)).T
    c_dec = jnp.exp(log_g * chunk)
    qc = q.astype(F32).reshape(B, n, chunk, H, dk)
    kc = k.astype(F32).reshape(B, n, chunk, H, dk)
    vc = v.astype(F32).reshape(B, n, chunk, H, dv)
    a = jnp.einsum('bnihd,bnjhd->bnhij', qc, kc) * intra
    o = jnp.einsum('bnhij,bnjhe->bnihe', a, vc)
    kv = jnp.einsum('bnjhd,bnjhe->nbhde', kc * k_dec[:, :, None], vc)

    def step(s, kv_n):
        return s * c_dec[None, :, None, None] + kv_n, s

    s_fin, s_prev = lax.scan(step, s0, kv)
    o = o + jnp.einsum('bnihd,nbhde->bnihe', qc * q_dec[:, :, None], s_prev)
    return o.reshape(B, L, H, dv), s_fin


def retention_readout(o, g, ret_gain):
    B, L = o.shape[:2]
    on = o * lax.rsqrt(jnp.mean(o * o, axis=-1, keepdims=True) + NORM_EPS)
    on = on.reshape(B, L, -1) * ret_gain.astype(F32)
    return (on * jax.nn.silu(g.astype(F32))).astype(g.dtype)


def dilated_prompt(q, k, v, dil, span):
    B, S, H, d = q.shape
    unit = dil * DIL_BLOCK
    s_pad = -(-S // unit) * unit
    m_len = s_pad // dil
    nb = m_len // DIL_BLOCK
    padw = ((0, 0), (0, s_pad - S), (0, 0), (0, 0))

    def regroup(t):
        t = jnp.pad(t, padw).reshape(B, m_len, dil, H, d).transpose(0, 2, 1, 3, 4)
        return t.reshape(B, dil, nb, DIL_BLOCK, H, d)

    def with_prev(t):
        prev = jnp.pad(t[:, :, :-1], ((0, 0), (0, 0), (1, 0), (0, 0), (0, 0), (0, 0)))
        return jnp.concatenate([prev, t], axis=3)

    qb = regroup(q)
    kw = with_prev(regroup(k))
    vw = with_prev(regroup(v)).astype(F32)
    i = jnp.arange(DIL_BLOCK)[:, None]
    jj = jnp.arange(2 * DIL_BLOCK)[None, :]
    dist = i + DIL_BLOCK - jj
    band = (dist >= 0) & (dist <= span)
    valid = band[None] & ((jnp.arange(nb)[:, None, None] > 0) | (jj >= DIL_BLOCK)[None])
    s = jnp.einsum('brnihd,brnjhd->brnhij', qb, kw, preferred_element_type=F32) * (d ** -0.5)
    s = jnp.where(valid[None, None, :, None], s, NEG_INF)
    m = jnp.max(s, axis=-1, keepdims=True)
    p = jnp.exp(s - m)
    den = jnp.sum(p, axis=-1)
    o = jnp.einsum('brnhij,brnjhd->brnihd', p, vw) / jnp.swapaxes(den, 3, 4)[..., None]
    lse = jnp.swapaxes(m[..., 0] + jnp.log(den), 3, 4)
    o = o.reshape(B, dil, m_len, H, d).transpose(0, 2, 1, 3, 4).reshape(B, s_pad, H, d)[:, :S]
    lse = lse.reshape(B, dil, m_len, H).transpose(0, 2, 1, 3).reshape(B, s_pad, H)[:, :S]
    return o, lse


def dilated_sample(q, k_all, v_all, dil, span, w_buf):
    B, L, H, d = q.shape
    idx = w_buf + jnp.arange(L)[:, None] - dil * jnp.arange(span + 1)[None, :]
    valid = idx >= 0
    flat = jnp.maximum(idx, 0).reshape(-1)
    kg = jnp.take(k_all, flat, axis=1).reshape(B, L, span + 1, H, d)
    vg = jnp.take(v_all, flat, axis=1).reshape(B, L, span + 1, H, d).astype(F32)
    s = jnp.einsum('blhd,bljhd->bhlj', q, kg, preferred_element_type=F32) * (d ** -0.5)
    s = jnp.where(valid[None, None], s, NEG_INF)
    m = jnp.max(s, axis=-1, keepdims=True)
    p = jnp.exp(s - m)
    den = jnp.sum(p, axis=-1)
    o = jnp.einsum('bhlj,bljhd->blhd', p, vg) / jnp.swapaxes(den, 1, 2)[..., None]
    lse = jnp.swapaxes(m[..., 0] + jnp.log(den), 1, 2)
    return o, lse


def combine_dilations(outs, lses):
    wts = jax.nn.softmax(jnp.stack(lses, axis=0), axis=0)
    return jnp.einsum('pblh,pblhd->blhd', wts, jnp.stack(outs, axis=0))


def prompt_mixer(rq, rk, rv, aq, ak, av):
    B, S = rq.shape[:2]
    s0 = jnp.zeros((B, RET_HEADS, RET_DK, RET_DV), F32)
    ret_o, s_fin = retention_chunkwise(rq, rk, rv, s0, min(RET_CHUNK, S))
    res = [dilated_prompt(aq, ak, av, dil, win // dil) for win, dil in DIL_PATTERNS]
    att_o = combine_dilations([r[0] for r in res], [r[1] for r in res])
    keep = min(MAX_WINDOW, S)
    return ret_o, att_o, (s_fin.astype(rv.dtype), ak[:, S - keep:], av[:, S - keep:])


def sample_mixer(state_ret, cache_k, cache_v, rq, rk, rv, aq, ak, av):
    L = rq.shape[1]
    ret_o, s_new = retention_chunkwise(rq, rk, rv, state_ret.astype(F32), L)
    w_buf = cache_k.shape[1]
    k_all = jnp.concatenate([cache_k.astype(ak.dtype), ak], axis=1)
    v_all = jnp.concatenate([cache_v.astype(av.dtype), av], axis=1)
    res = [dilated_sample(aq, k_all, v_all, dil, win // dil, w_buf) for win, dil in DIL_PATTERNS]
    att_o = combine_dilations([r[0] for r in res], [r[1] for r in res])
    return ret_o, att_o, (s_new.astype(state_ret.dtype), k_all[:, L:], v_all[:, L:])


def project(h, w_in, pos):
    B, L, _ = h.shape
    z = jnp.einsum('bld,de->ble', h, w_in)
    bounds = [int(b) for b in np.cumsum(IN_SPLITS)[:-1]]
    rq, rk, rv, rg, aq, ak, av = jnp.split(z, bounds, axis=-1)
    rq = retention_rotate(rq.reshape(B, L, RET_HEADS, RET_DK), pos)
    rk = retention_rotate(rk.reshape(B, L, RET_HEADS, RET_DK), pos) * (RET_DK ** -0.5)
    rv = rv.reshape(B, L, RET_HEADS, RET_DV)
    aq = rope_half(aq.reshape(B, L, ATT_HEADS, ATT_HD), pos)
    ak = rope_half(ak.reshape(B, L, ATT_HEADS, ATT_HD), pos)
    av = av.reshape(B, L, ATT_HEADS, ATT_HD)
    return rq, rk, rv, rg, aq, ak, av


def decoder_layer(x, c, pos, mix_fn, w_ada, b_ada, g_pre_mix, g_post_mix, g_pre_ffn, g_post_ffn,
                  w_in, ret_gain, w_o, w_up, w_down):
    B, L, _ = x.shape
    mod = jnp.einsum('bd,de->be', jax.nn.silu(c), w_ada) + b_ada
    sh1, sc1, gt1, sh2, sc2, gt2 = [m[:, None, :] for m in jnp.split(mod, 6, axis=-1)]
    h = rmsnorm(x, g_pre_mix) * (1.0 + sc1) + sh1
    rq, rk, rv, rg, aq, ak, av = project(h, w_in, pos)
    ret_o, att_o, new_state = mix_fn(rq, rk, rv, aq, ak, av)
    heads = jnp.concatenate([retention_readout(ret_o, rg, ret_gain),
                             att_o.reshape(B, L, ATT_WIDTH).astype(h.dtype)], axis=-1)
    mixed = jnp.einsum('ble,ed->bld', heads, w_o)
    x = x + gt1 * rmsnorm(mixed, g_post_mix)
    h = rmsnorm(x, g_pre_ffn) * (1.0 + sc2) + sh2
    f = jnp.einsum('blf,fd->bld', jnp.square(jax.nn.relu(jnp.einsum('bld,df->blf', h, w_up))), w_down)
    x = x + gt2 * rmsnorm(f, g_post_ffn)
    return x, new_state


def setup_inputs(seed: int = 0) -> dict:
    key = jax.random.key(seed)
    ks = jax.random.split(key, 20)
    w_buf = min(MAX_WINDOW, PAST_LEN)

    def nrm(k, shape, scale):
        return jax.random.normal(k, shape, F32) * scale

    return {
        'x_prompt': nrm(ks[0], (BATCH, SEQ, D_MODEL), 1.0),
        'x_sample': nrm(ks[1], (DEC_BATCH, DEC_SEQ, D_MODEL), 1.0),
        'c_prompt': nrm(ks[2], (BATCH, D_MODEL), 1.0),
        'c_sample': nrm(ks[3], (DEC_BATCH, D_MODEL), 1.0),
        'state_ret': nrm(ks[4], (DEPTH, DEC_BATCH, RET_HEADS, RET_DK, RET_DV), 0.5),
        'cache_win_k': nrm(ks[5], (DEPTH, DEC_BATCH, w_buf, ATT_HEADS, ATT_HD), 1.0),
        'cache_win_v': nrm(ks[6], (DEPTH, DEC_BATCH, w_buf, ATT_HEADS, ATT_HD), 1.0),
        'w_ada': nrm(ks[7], (DEPTH, D_MODEL, 6 * D_MODEL), 0.5 * D_MODEL ** -0.5),
        'b_ada': nrm(ks[8], (DEPTH, 6 * D_MODEL), 0.02),
        'g_pre_mix': 1.0 + nrm(ks[9], (DEPTH, D_MODEL), 0.05),
        'g_post_mix': 1.0 + nrm(ks[10], (DEPTH, D_MODEL), 0.05),
        'g_pre_ffn': 1.0 + nrm(ks[11], (DEPTH, D_MODEL), 0.05),
        'g_post_ffn': 1.0 + nrm(ks[12], (DEPTH, D_MODEL), 0.05),
        'w_in': nrm(ks[13], (DEPTH, D_MODEL, IN_WIDTH), D_MODEL ** -0.5),
        'ret_gain': 1.0 + nrm(ks[14], (DEPTH, RET_HEADS * RET_DV), 0.05),
        'w_o': nrm(ks[15], (DEPTH, MIX_WIDTH, D_MODEL), MIX_WIDTH ** -0.5),
        'w_up': nrm(ks[16], (DEPTH, D_MODEL, D_FF), D_MODEL ** -0.5),
        'w_down': nrm(ks[17], (DEPTH, D_FF, D_MODEL), D_FF ** -0.5),
    }


def reference(x_prompt, x_sample, c_prompt, c_sample, state_ret, cache_win_k, cache_win_v,
              w_ada, b_ada, g_pre_mix, g_post_mix, g_pre_ffn, g_post_ffn, w_in, ret_gain, w_o,
              w_up, w_down):
    pos_p = jnp.arange(x_prompt.shape[1], dtype=jnp.int32)
    pos_s = PAST_LEN + jnp.arange(x_sample.shape[1], dtype=jnp.int32)
    y_prompt, y_sample = x_prompt, x_sample
    new_p, new_s = [], []
    for l in range(DEPTH):
        layer_w = (w_ada[l], b_ada[l], g_pre_mix[l], g_post_mix[l], g_pre_ffn[l], g_post_ffn[l],
                   w_in[l], ret_gain[l], w_o[l], w_up[l], w_down[l])
        y_prompt, st_p = decoder_layer(y_prompt, c_prompt, pos_p, prompt_mixer, *layer_w)
        mix_s = functools.partial(sample_mixer, state_ret[l], cache_win_k[l], cache_win_v[l])
        y_sample, st_s = decoder_layer(y_sample, c_sample, pos_s, mix_s, *layer_w)
        new_p.append(st_p)
        new_s.append(st_s)
    state_ret_prompt = jnp.stack([s[0] for s in new_p])
    cache_win_k_prompt = jnp.stack([s[1] for s in new_p])
    cache_win_v_prompt = jnp.stack([s[2] for s in new_p])
    state_ret_sample = jnp.stack([s[0] for s in new_s])
    cache_win_k_sample = jnp.stack([s[1] for s in new_s])
    cache_win_v_sample = jnp.stack([s[2] for s in new_s])
    return (y_prompt, y_sample, state_ret_prompt, cache_win_k_prompt, cache_win_v_prompt,
            state_ret_sample, cache_win_k_sample, cache_win_v_sample)
```

```python
import functools
import math

import numpy as np
import jax
import jax.numpy as jnp
from jax import lax
from jax.experimental import pallas as pl
from jax.experimental.pallas import tpu as pltpu

F32 = jnp.float32
BF16 = jnp.bfloat16

D_MODEL = 1024
RET_HEADS = 4
RET_WIDTH = 512
RET_DK = 128
RET_THETA = 10000.0
RET_CHUNK = 128
ATT_HD = 64
ATT_HEADS = 8
ATT_WIDTH = 512
DILATIONS = (1, 4, 16)
MAX_WINDOW = 2048
SPAN = 128
DIL_BLOCK = 128
ROPE_THETA = 10000.0
D_FF = 4096
NORM_EPS = 1e-6
NEG_INF = -1e30
PAST_LEN = 16384
N_GROUPS = 7
GROUP_W = 512
LANES = 128
VMEM_LIMIT = 56 * 1024 * 1024

LOG_G = tuple(math.log1p(-(2.0 ** (-5.0 - h))) for h in range(RET_HEADS))


def _dot(a, b):
    return jnp.dot(a, b, preferred_element_type=F32)


def _dot_nt(a, b):
    return lax.dot_general(a, b, (((1,), (1,)), ((), ())), preferred_element_type=F32)


def _dot_tn(a, b):
    return lax.dot_general(a, b, (((0,), (0,)), ((), ())), preferred_element_type=F32)


def _rms(x):
    return x * lax.rsqrt(jnp.mean(x * x, axis=-1, keepdims=True) + NORM_EPS)


def _silu(x):
    return x / (1.0 + jnp.exp(-x))


def _params(sem):
    return pltpu.CompilerParams(dimension_semantics=sem, vmem_limit_bytes=VMEM_LIMIT)


def _const_spec(shape):
    nd = len(shape)
    return pl.BlockSpec(shape, lambda *_: (0,) * nd, pipeline_mode=pl.Buffered(1))


def _mod_kernel(c_ref, w_ref, b_ref, o_ref):
    a = _silu(c_ref[...]).astype(BF16)
    o_ref[...] = _dot(a, w_ref[...].astype(BF16)) + b_ref[...]


def _modulation(c, w_ada, b_ada):
    rows, d = c.shape
    n = w_ada.shape[1]
    tn = 1536
    return pl.pallas_call(
        _mod_kernel,
        grid=(n // tn,),
        in_specs=[pl.BlockSpec((rows, d), lambda j: (0, 0)),
                  pl.BlockSpec((d, tn), lambda j: (0, j)),
                  pl.BlockSpec((1, tn), lambda j: (0, j))],
        out_specs=pl.BlockSpec((rows, tn), lambda j: (0, j)),
        out_shape=jax.ShapeDtypeStruct((rows, n), F32),
        compiler_params=_params(("arbitrary",)),
        name="adaln_mod",
    )(c, w_ada, b_ada.reshape(1, n))


def _rotate_pairs(x, cos, sin):
    lane = lax.broadcasted_iota(jnp.int32, x.shape, 1)
    partner = jnp.where(lane % 2 == 0, pltpu.roll(x, LANES - 1, 1), pltpu.roll(x, 1, 1))
    return x * cos + partner * sin


def _rotate_half(x, cos, sin):
    lane = lax.broadcasted_iota(jnp.int32, x.shape, 1)
    half = ATT_HD // 2
    partner = jnp.where(lane % ATT_HD < half, pltpu.roll(x, LANES - half, 1), pltpu.roll(x, half, 1))
    return x * cos + partner * sin


def _proj_kernel(x_ref, mod_ref, g_ref, w_ref, cr_ref, sr_ref, ca_ref, sa_ref,
                 rq_ref, rk_ref, rv_ref, rg_ref, aq_ref, ak_ref, av_ref, akf_ref, avf_ref):
    d = D_MODEL
    x = x_ref[...]
    h = (_rms(x) * g_ref[...]) * (1.0 + mod_ref[:, d:2 * d]) + mod_ref[:, 0:d]
    h = h.astype(BF16)
    cr, sr, ca, sa = cr_ref[...], sr_ref[...], ca_ref[...], sa_ref[...]
    nch = GROUP_W // LANES

    def group(gi):
        return _dot(h, w_ref[:, gi * GROUP_W:(gi + 1) * GROUP_W])

    z = group(0)
    for c in range(nch):
        sl = slice(c * LANES, (c + 1) * LANES)
        rq_ref[:, sl] = _rotate_pairs(z[:, sl], cr, sr).astype(rq_ref.dtype)
    z = group(1)
    for c in range(nch):
        sl = slice(c * LANES, (c + 1) * LANES)
        rk_ref[:, sl] = (_rotate_pairs(z[:, sl], cr, sr) * (RET_DK ** -0.5)).astype(rk_ref.dtype)
    rv_ref[...] = group(2).astype(rv_ref.dtype)
    rg_ref[...] = group(3)
    z = group(4)
    for c in range(nch):
        sl = slice(c * LANES, (c + 1) * LANES)
        aq_ref[:, sl] = (_rotate_half(z[:, sl], ca, sa) * (ATT_HD ** -0.5)).astype(aq_ref.dtype)
    z = group(5)
    for c in range(nch):
        sl = slice(c * LANES, (c + 1) * LANES)
        r = _rotate_half(z[:, sl], ca, sa)
        ak_ref[:, sl] = r.astype(ak_ref.dtype)
        akf_ref[:, sl] = r
    z = group(6)
    av_ref[...] = z.astype(av_ref.dtype)
    avf_ref[...] = z


def _project(x, mod, g_pre, w_in_bf, tabs, *, tm, keep, act_dtype):
    b, s, d = x.shape
    nt = s // tm
    mod_rows = mod.shape[1]
    per_row = mod_rows != 1
    first_keep = (s - keep) // tm
    tok = pl.BlockSpec((None, tm, GROUP_W), lambda bi, i: (bi, i, 0))
    keep_spec = pl.BlockSpec((None, tm, GROUP_W), lambda bi, i: (bi, jnp.maximum(i - first_keep, 0), 0))
    tab = pl.BlockSpec((tm, LANES), lambda bi, i: (i, 0))
    mod_spec = pl.BlockSpec((None, tm if per_row else 1, 6 * d),
                            (lambda bi, i: (bi, i, 0)) if per_row else (lambda bi, i: (bi, 0, 0)))
    act = jax.ShapeDtypeStruct((b, s, GROUP_W), act_dtype)
    full = jax.ShapeDtypeStruct((b, s, GROUP_W), F32)
    kept = jax.ShapeDtypeStruct((b, keep, GROUP_W), F32)
    return pl.pallas_call(
        _proj_kernel,
        grid=(b, nt),
        in_specs=[pl.BlockSpec((None, tm, d), lambda bi, i: (bi, i, 0)),
                  mod_spec,
                  _const_spec((1, d)),
                  _const_spec((d, N_GROUPS * GROUP_W)),
                  tab, tab, tab, tab],
        out_specs=[tok, tok, tok, tok, tok, tok, tok, keep_spec, keep_spec],
        out_shape=[act, act, act, full, act, act, act, kept, kept],
        compiler_params=_params(("arbitrary", "arbitrary")),
        name="in_proj",
    )(x, mod, g_pre.reshape(1, d), w_in_bf, *tabs)


def _rotation_tables(pos):
    posf = pos.astype(F32)[:, None]
    lane = np.arange(LANES)
    inv_r = 1.0 / (RET_THETA ** jnp.linspace(0.0, 1.0, RET_DK // 2, dtype=F32))
    ang_r = posf * inv_r[None, :]
    cr = jnp.cos(ang_r)[:, lane // 2]
    sr = jnp.sin(ang_r)[:, lane // 2] * jnp.asarray(np.where(lane % 2 == 0, -1.0, 1.0), F32)[None, :]
    inv_a = 1.0 / (ROPE_THETA ** (jnp.arange(0, ATT_HD, 2, dtype=F32) / ATT_HD))
    ang_a = posf * inv_a[None, :]
    idx = (lane % ATT_HD) % (ATT_HD // 2)
    ca = jnp.cos(ang_a)[:, idx]
    sa = jnp.sin(ang_a)[:, idx] * jnp.asarray(np.where(lane % ATT_HD < ATT_HD // 2, -1.0, 1.0), F32)[None, :]
    return cr, sr, ca, sa


def _ret_readout(o, g, gain):
    return (_rms(o) * gain) * _silu(g)


def _ret_prompt_kernel(q_ref, k_ref, v_ref, g_ref, gain_ref, o_ref, sfin_ref, s_scr, *, nchunk):
    i = pl.program_id(1)
    c_len = RET_CHUNK

    @pl.when(i == 0)
    def _():
        s_scr[...] = jnp.zeros_like(s_scr)

    row = lax.broadcasted_iota(jnp.int32, (c_len, c_len), 0)
    col = lax.broadcasted_iota(jnp.int32, (c_len, c_len), 1)
    diff = (row - col).astype(F32)
    ridx = lax.broadcasted_iota(jnp.int32, (c_len, 1), 0).astype(F32)
    for h in range(RET_HEADS):
        lg = LOG_G[h]
        intra = jnp.where(diff >= 0, jnp.exp(lg * jnp.maximum(diff, 0.0)), 0.0)
        q_dec = jnp.exp(lg * (ridx + 1.0))
        k_dec = jnp.exp(lg * (c_len - 1.0 - ridx))
        c_dec = math.exp(lg * c_len)
        hs = slice(h * RET_DK, (h + 1) * RET_DK)
        state = s_scr[h]
        for c in range(nchunk):
            rs = slice(c * c_len, (c + 1) * c_len)
            q = q_ref[rs, hs]
            k = k_ref[rs, hs]
            v = v_ref[rs, hs]
            a = _dot_nt(q, k) * intra
            o = _dot(a.astype(BF16), v)
            o = o + _dot((q.astype(F32) * q_dec).astype(BF16), state.astype(BF16))
            kd = (k.astype(F32) * k_dec).astype(BF16)
            state = state * c_dec + _dot_tn(kd, v)
            o_ref[rs, hs] = _ret_readout(o, g_ref[rs, hs], gain_ref[:, hs]).astype(o_ref.dtype)
        s_scr[h] = state

    @pl.when(i == pl.num_programs(1) - 1)
    def _():
        sfin_ref[...] = s_scr[...]


def _retention_prompt(rq, rk, rv, rg, ret_gain, *, tc):
    b, s, w = rq.shape
    nblk = s // tc
    tok = pl.BlockSpec((None, tc, w), lambda bi, i: (bi, i, 0))
    return pl.pallas_call(
        functools.partial(_ret_prompt_kernel, nchunk=tc // RET_CHUNK),
        grid=(b, nblk),
        in_specs=[tok, tok, tok, tok, _const_spec((1, w))],
        out_specs=[tok, pl.BlockSpec((None, RET_HEADS, RET_DK, RET_DK), lambda bi, i: (bi, 0, 0, 0))],
        out_shape=[jax.ShapeDtypeStruct((b, s, w), BF16),
                   jax.ShapeDtypeStruct((b, RET_HEADS, RET_DK, RET_DK), F32)],
        scratch_shapes=[pltpu.VMEM((RET_HEADS, RET_DK, RET_DK), F32)],
        compiler_params=_params(("arbitrary", "arbitrary")),
        name="retention_prompt",
    )(rq, rk, rv, rg, ret_gain.reshape(1, w))


def _att_pair(q, kk, vv, valid):
    lane = lax.broadcasted_iota(jnp.int32, q.shape, 1)
    low = lane < ATT_HD
    zero = jnp.zeros_like(q)
    q2 = jnp.concatenate([jnp.where(low, q, zero), jnp.where(low, zero, q)], axis=0)
    s = _dot_nt(q2, kk)
    s = jnp.where(valid, s, NEG_INF)
    m = jnp.max(s, axis=-1, keepdims=True)
    p = jnp.exp(s - m)
    l = jnp.sum(p, axis=-1, keepdims=True)
    pv = _dot(p.astype(BF16), vv)
    return pv, m, l


def _band_mask(first_key):
    qi = lax.broadcasted_iota(jnp.int32, (2 * DIL_BLOCK, 2 * DIL_BLOCK), 0) % DIL_BLOCK
    jj = lax.broadcasted_iota(jnp.int32, (2 * DIL_BLOCK, 2 * DIL_BLOCK), 1)
    return (jj >= jnp.maximum(qi, first_key)) & (jj <= qi + SPAN)


def _att_kernel(*refs, nsub, final):
    if final:
        (q_ref, k_ref, v_ref, kp_ref, vp_ref, oa_ref, la_ref, ob_ref, lb_ref,
         o_ref, kcat, vcat) = refs
    else:
        q_ref, k_ref, v_ref, kp_ref, vp_ref, o_ref, lse_ref, kcat, vcat = refs
    i = pl.program_id(2)
    blk = DIL_BLOCK
    kcat[0:blk, :] = kp_ref[...]
    kcat[blk:, :] = k_ref[...]
    vcat[0:blk, :] = vp_ref[...]
    vcat[blk:, :] = v_ref[...]
    lane = lax.broadcasted_iota(jnp.int32, (blk, LANES), 1)
    low = lane < ATT_HD
    valid_first = _band_mask(jnp.where(i == 0, blk, 0))
    valid_rest = _band_mask(0)
    for j in range(nsub):
        valid = valid_first if j == 0 else valid_rest
        rows = slice(j * blk, (j + 1) * blk)
        krows = slice(j * blk, (j + 2) * blk)
        for hp in range(ATT_WIDTH // LANES):
            cols = slice(hp * LANES, (hp + 1) * LANES)
            pv, m, l = _att_pair(q_ref[rows, cols], kcat[krows, cols], vcat[krows, cols], valid)
            o_n = jnp.where(low, pv[0:blk] / l[0:blk], pv[blk:] / l[blk:])
            lse2 = m + jnp.log(l)
            lse = jnp.where(low, lse2[0:blk], lse2[blk:])
            if not final:
                o_ref[rows, cols] = o_n
                lse_ref[rows, cols] = lse
            else:
                la, lb = la_ref[rows, cols], lb_ref[rows, cols]
                top = jnp.maximum(lse, jnp.maximum(la, lb))
                w0, wa, wb = jnp.exp(lse - top), jnp.exp(la - top), jnp.exp(lb - top)
                acc = w0 * o_n + wa * oa_ref[rows, cols] + wb * ob_ref[rows, cols]
                o_ref[rows, cols] = (acc / (w0 + wa + wb)).astype(o_ref.dtype)


def _dilated_prompt(aq, ak, av, dil, prev=None, *, tq):
    b, s, w = aq.shape
    m_len = s // dil
    nsub = tq // DIL_BLOCK
    nblk = m_len // tq
    view = lambda t: t.reshape(b, m_len, dil * w)
    own = pl.BlockSpec((None, tq, w), lambda bi, r, i: (bi, i, r))
    prv = pl.BlockSpec((None, DIL_BLOCK, w), lambda bi, r, i: (bi, jnp.maximum(i * nsub - 1, 0), r))
    in_specs = [own, own, own, prv, prv]
    args = [view(aq), view(ak), view(av), view(ak), view(av)]
    final = prev is not None
    if final:
        assert dil == 1
        in_specs += [own] * 4
        args += list(prev)
        out_specs = own
        out_shape = jax.ShapeDtypeStruct((b, m_len, dil * w), BF16)
    else:
        out_specs = [own, own]
        out_shape = [jax.ShapeDtypeStruct((b, m_len, dil * w), F32)] * 2
    out = pl.pallas_call(
        functools.partial(_att_kernel, nsub=nsub, final=final),
        grid=(b, dil, nblk),
        in_specs=in_specs,
        out_specs=out_specs,
        out_shape=out_shape,
        scratch_shapes=[pltpu.VMEM((tq + DIL_BLOCK, w), BF16), pltpu.VMEM((tq + DIL_BLOCK, w), BF16)],
        compiler_params=_params(("arbitrary", "arbitrary", "arbitrary")),
        name=f"dilated_attn_d{dil}",
    )(*args)
    if final:
        return out.reshape(b, s, w)
    return out[0].reshape(b, s, w), out[1].reshape(b, s, w)


def _tail_kernel(x_ref, ret_ref, att_ref, mod_ref, gpm_ref, gpf_ref, gqf_ref, wo_ref, wu_ref, wd_ref, y_ref):
    d = D_MODEL
    x = x_ref[...]
    mixed = (_dot(ret_ref[...].astype(BF16), wo_ref[0:RET_WIDTH, :])
             + _dot(att_ref[...].astype(BF16), wo_ref[RET_WIDTH:, :]))
    x1 = x + mod_ref[:, 2 * d:3 * d] * (_rms(mixed) * gpm_ref[...])
    h = ((_rms(x1) * gpf_ref[...]) * (1.0 + mod_ref[:, 4 * d:5 * d]) + mod_ref[:, 3 * d:4 * d]).astype(BF16)
    fc = 1024
    f = None
    for c in range(D_FF // fc):
        u = jnp.maximum(_dot(h, wu_ref[:, c * fc:(c + 1) * fc]), 0.0)
        part = _dot((u * u).astype(BF16), wd_ref[c * fc:(c + 1) * fc, :])
        f = part if f is None else f + part
    y_ref[...] = x1 + mod_ref[:, 5 * d:6 * d] * (_rms(f) * gqf_ref[...])


def _tail(x, ret_h, att_h, mod, g_post_mix, g_pre_ffn, g_post_ffn, wo_bf, wu_bf, wd_bf, *, tm):
    b, s, d = x.shape
    nt = s // tm
    per_row = mod.shape[1] != 1
    mod_spec = pl.BlockSpec((None, tm if per_row else 1, 6 * d),
                            (lambda bi, i: (bi, i, 0)) if per_row else (lambda bi, i: (bi, 0, 0)))
    tok = lambda w: pl.BlockSpec((None, tm, w), lambda bi, i: (bi, i, 0))
    return pl.pallas_call(
        _tail_kernel,
        grid=(b, nt),
        in_specs=[tok(d), tok(RET_WIDTH), tok(ATT_WIDTH), mod_spec,
                  _const_spec((1, d)), _const_spec((1, d)), _const_spec((1, d)),
                  _const_spec((d, d)), _const_spec((d, D_FF)), _const_spec((D_FF, d))],
        out_specs=tok(d),
        out_shape=jax.ShapeDtypeStruct((b, s, d), F32),
        compiler_params=_params(("arbitrary", "arbitrary")),
        name="out_proj_mlp",
    )(x, ret_h, att_h, mod, g_post_mix.reshape(1, d), g_pre_ffn.reshape(1, d), g_post_ffn.reshape(1, d),
      wo_bf, wu_bf, wd_bf)


def _ret_sample_kernel(q_ref, k_ref, v_ref, g_ref, gain_ref, dm_ref, qd_ref, kd_ref, cd_ref, s_ref,
                       o_ref, sn_ref, o_scr, qd_scr, kt_scr, *, nb, ln):
    q = q_ref[...].astype(F32)
    k = k_ref[...].astype(F32)
    v = v_ref[...].astype(BF16)
    a = _dot_nt(q.astype(BF16), k.astype(BF16)) * dm_ref[...]
    o_scr[...] = _dot(a.astype(BF16), v)
    qd_scr[...] = q * qd_ref[...]
    kt_scr[...] = (k * kd_ref[...]).T
    c_dec = cd_ref[0:1, :]
    col = lax.broadcasted_iota(jnp.int32, kt_scr.shape, 1)

    def body(bi, carry):
        rows = pl.ds(pl.multiple_of(bi * ln, ln), ln)
        s0 = s_ref[bi]
        o_scr[rows, :] += _dot(qd_scr[rows, :].astype(BF16), s0.astype(BF16))
        mine = (col >= bi * ln) & (col < (bi + 1) * ln)
        kt = jnp.where(mine, kt_scr[...], 0.0).astype(BF16)
        sn_ref[bi] = s0 * c_dec + _dot(kt, v)
        return carry

    lax.fori_loop(0, nb, body, 0)
    o_ref[...] = _ret_readout(o_scr[...], g_ref[...], gain_ref[...]).astype(o_ref.dtype)


def _retention_sample(rq, rk, rv, rg, ret_gain, state, *, nb, ln):
    n = nb * ln
    idx = np.arange(n)
    same = (idx[:, None] // ln == idx[None, :] // ln) & (idx[:, None] >= idx[None, :])
    diff = np.maximum(idx[:, None] - idx[None, :], 0).astype(np.float64)
    step = (idx % ln).astype(np.float64)
    lg = np.asarray(LOG_G, np.float64)
    dm = np.where(same[None], np.exp(lg[:, None, None] * diff[None]), 0.0)
    qd = np.broadcast_to(np.exp(lg[:, None] * (step + 1.0))[:, :, None], (RET_HEADS, n, RET_DK))
    kd = np.broadcast_to(np.exp(lg[:, None] * (ln - 1.0 - step))[:, :, None], (RET_HEADS, n, RET_DK))
    cd = np.broadcast_to(np.exp(lg * ln)[:, None, None], (RET_HEADS, 8, RET_DK))
    tab = lambda t: jnp.asarray(np.ascontiguousarray(t), F32)
    col = pl.BlockSpec((n, RET_DK), lambda h: (0, h))
    per_head = lambda r, c: pl.BlockSpec((None, r, c), lambda h: (h, 0, 0))
    st = pl.BlockSpec((nb, None, RET_DK, RET_DK), lambda h: (0, h, 0, 0))
    return pl.pallas_call(
        functools.partial(_ret_sample_kernel, nb=nb, ln=ln),
        grid=(RET_HEADS,),
        in_specs=[col, col, col, col, pl.BlockSpec((1, RET_DK), lambda h: (0, h)),
                  per_head(n, n), per_head(n, RET_DK), per_head(n, RET_DK), per_head(8, RET_DK), st],
        out_specs=[col, st],
        out_shape=[jax.ShapeDtypeStruct((n, RET_WIDTH), BF16),
                   jax.ShapeDtypeStruct((nb, RET_HEADS, RET_DK, RET_DK), F32)],
        scratch_shapes=[pltpu.VMEM((n, RET_DK), F32), pltpu.VMEM((n, RET_DK), F32),
                        pltpu.VMEM((RET_DK, n), F32)],
        compiler_params=_params(("arbitrary",)),
        name="retention_sample",
    )(rq, rk, rv, rg, ret_gain.reshape(1, RET_WIDTH), tab(dm), tab(qd), tab(kd), tab(cd), state)


def _att_sample_kernel(q_ref, kn_ref, vn_ref, ck_ref, cv_ref, cc_ref, cn_ref,
                       o_ref, ko_ref, vo_ref, kpad, vpad, *, ln, wbuf):
    ko_ref[0:wbuf - ln, :] = ck_ref[ln:, :]
    ko_ref[wbuf - ln:, :] = kn_ref[...]
    vo_ref[0:wbuf - ln, :] = cv_ref[ln:, :]
    vo_ref[wbuf - ln:, :] = vn_ref[...]
    kpad[...] = jnp.zeros_like(kpad)
    vpad[...] = jnp.zeros_like(vpad)
    kpad[0:ln, :] = kn_ref[...]
    vpad[0:ln, :] = vn_ref[...]

    q = q_ref[...]
    lane = lax.broadcasted_iota(jnp.int32, q.shape, 1)
    heads = [lane // ATT_HD == h for h in range(ATT_HEADS)]
    qm = jnp.concatenate([jnp.where(hm, q, 0.0) for hm in heads], axis=0).astype(BF16)
    cnt_c = jnp.concatenate([cc_ref[...]] * ATT_HEADS, axis=0)
    cnt_n = jnp.concatenate([cn_ref[...]] * ATT_HEADS, axis=0)
    s_c = jnp.where(cnt_c > 0, _dot_nt(qm, ck_ref[...].astype(BF16)), NEG_INF)
    s_n = jnp.where(cnt_n > 0, _dot_nt(qm, kpad[...].astype(BF16)), NEG_INF)
    m = jnp.maximum(jnp.max(s_c, axis=-1, keepdims=True), jnp.max(s_n, axis=-1, keepdims=True))
    p_c = cnt_c * jnp.exp(s_c - m)
    p_n = cnt_n * jnp.exp(s_n - m)
    l = jnp.sum(p_c, axis=-1, keepdims=True) + jnp.sum(p_n, axis=-1, keepdims=True)
    o = _dot(p_c.astype(BF16), cv_ref[...].astype(BF16)) + _dot(p_n.astype(BF16), vpad[...].astype(BF16))
    o = o / l
    acc = jnp.zeros_like(q)
    for h, hm in enumerate(heads):
        acc = acc + jnp.where(hm, o[h * ln:(h + 1) * ln, :], 0.0)
    o_ref[...] = acc.astype(o_ref.dtype)


def _pattern_counts(ln, wbuf):
    cnt = np.zeros((ln, wbuf + ln), np.float32)
    for dil in DILATIONS:
        for l in range(ln):
            for j in range(SPAN + 1):
                row = wbuf + l - dil * j
                if row >= 0:
                    cnt[l, row] += 1.0
    return cnt


def _attention_sample(aq, akf, avf, cache_k, cache_v, *, nb, ln):
    wbuf = cache_k.shape[1]
    w = ATT_WIDTH
    cnt = _pattern_counts(ln, wbuf)
    cnt_new = np.zeros((ln, LANES), np.float32)
    cnt_new[:, :ln] = cnt[:, wbuf:]
    new = pl.BlockSpec((ln, w), lambda bi: (bi, 0))
    cache = pl.BlockSpec((None, wbuf, w), lambda bi: (bi, 0, 0))
    return pl.pallas_call(
        functools.partial(_att_sample_kernel, ln=ln, wbuf=wbuf),
        grid=(nb,),
        in_specs=[new, new, new, cache, cache, _const_spec((ln, wbuf)), _const_spec((ln, LANES))],
        out_specs=[new, cache, cache],
        out_shape=[jax.ShapeDtypeStruct((nb * ln, w), F32),
                   jax.ShapeDtypeStruct((nb, wbuf, w), F32),
                   jax.ShapeDtypeStruct((nb, wbuf, w), F32)],
        scratch_shapes=[pltpu.VMEM((LANES, w), F32), pltpu.VMEM((LANES, w), F32)],
        compiler_params=_params(("arbitrary",)),
        name="dilated_attn_sample",
    )(aq, akf, avf, cache_k, cache_v, jnp.asarray(cnt[:, :wbuf]), jnp.asarray(cnt_new))


def _step(x_prompt, x_sample, c_prompt, c_sample, state_ret, cache_win_k, cache_win_v, w_ada, b_ada,
          g_pre_mix, g_post_mix, g_pre_ffn, g_post_ffn, w_in, ret_gain, w_o, w_up, w_down,
          *, tm, tc, tq):
    assert w_in.shape[0] == 1, "single-layer step"
    bp, sp, d = x_prompt.shape
    nb, ln, _ = x_sample.shape
    wbuf = cache_win_k.shape[2]
    n_s = nb * ln

    w_in_bf = w_in[0].astype(BF16)
    wo_bf, wu_bf, wd_bf = w_o[0].astype(BF16), w_up[0].astype(BF16), w_down[0].astype(BF16)

    rows = bp + nb
    pad = (-rows) % 8
    c_all = jnp.concatenate([c_prompt, c_sample, jnp.zeros((pad, d), F32)], axis=0)
    mod = _modulation(c_all, w_ada[0], b_ada[0])
    mod_p = mod[:bp].reshape(bp, 1, 6 * d)
    mod_s = jnp.repeat(mod[bp:rows], ln, axis=0).reshape(1, n_s, 6 * d)

    tabs_p = _rotation_tables(jnp.arange(sp, dtype=jnp.int32))
    tabs_s = _rotation_tables(jnp.tile(PAST_LEN + jnp.arange(ln, dtype=jnp.int32), nb))

    keep = min(MAX_WINDOW, sp)
    rq, rk, rv, rg, aq, ak, av, akf, avf = _project(
        x_prompt, mod_p, g_pre_mix[0], w_in_bf, tabs_p, tm=tm, keep=keep, act_dtype=BF16)
    ret_h, s_fin = _retention_prompt(rq, rk, rv, rg, ret_gain[0], tc=tc)
    o16, l16 = _dilated_prompt(aq, ak, av, 16, tq=tq)
    o4, l4 = _dilated_prompt(aq, ak, av, 4, tq=tq)
    att_h = _dilated_prompt(aq, ak, av, 1, prev=(o16, l16, o4, l4), tq=tq)
    y_prompt = _tail(x_prompt, ret_h, att_h, mod_p, g_post_mix[0], g_pre_ffn[0], g_post_ffn[0],
                     wo_bf, wu_bf, wd_bf, tm=tm)

    xs = x_sample.reshape(1, n_s, d)
    srq, srk, srv, srg, saq, _, _, sakf, savf = _project(
        xs, mod_s, g_pre_mix[0], w_in_bf, tabs_s, tm=n_s, keep=n_s, act_dtype=F32)
    flat = lambda t: t.reshape(n_s, GROUP_W)
    sret_h, s_new = _retention_sample(flat(srq), flat(srk), flat(srv), flat(srg), ret_gain[0],
                                      state_ret[0], nb=nb, ln=ln)
    satt_h, k_out, v_out = _attention_sample(
        flat(saq), flat(sakf), flat(savf),
        cache_win_k[0].reshape(nb, wbuf, ATT_WIDTH), cache_win_v[0].reshape(nb, wbuf, ATT_WIDTH),
        nb=nb, ln=ln)
    y_sample = _tail(xs, sret_h.reshape(1, n_s, RET_WIDTH), satt_h.reshape(1, n_s, ATT_WIDTH), mod_s,
                     g_post_mix[0], g_pre_ffn[0], g_post_ffn[0], wo_bf, wu_bf, wd_bf, tm=n_s)

    cache_shape = (1, -1, keep, ATT_HEADS, ATT_HD)
    return (y_prompt,
            y_sample.reshape(nb, ln, d),
            s_fin[None],
            akf.reshape(cache_shape),
            avf.reshape(cache_shape),
            s_new[None],
            k_out.reshape(1, nb, wbuf, ATT_HEADS, ATT_HD),
            v_out.reshape(1, nb, wbuf, ATT_HEADS, ATT_HD))


def kernel(x_prompt, x_sample, c_prompt, c_sample, state_ret, cache_win_k, cache_win_v, w_ada, b_ada,
           g_pre_mix, g_post_mix, g_pre_ffn, g_post_ffn, w_in, ret_gain, w_o, w_up, w_down):
    return _step(x_prompt, x_sample, c_prompt, c_sample, state_ret, cache_win_k, cache_win_v, w_ada, b_ada,
                 g_pre_mix, g_post_mix, g_pre_ffn, g_post_ffn, w_in, ret_gain, w_o, w_up, w_down,
                 tm=512, tc=1024, tq=512)
```

```python
import functools
import math

import numpy as np
import jax
import jax.numpy as jnp
from jax import lax
from jax.experimental import pallas as pl
from jax.experimental.pallas import tpu as pltpu

F32 = jnp.float32
BF16 = jnp.bfloat16

D_MODEL = 1024
RET_HEADS = 4
RET_WIDTH = 512
RET_DK = 128
RET_THETA = 10000.0
RET_CHUNK = 128
ATT_HD = 64
ATT_HEADS = 8
ATT_WIDTH = 512
DILATIONS = (1, 4, 16)
MAX_WINDOW = 2048
SPAN = 128
DIL_BLOCK = 128
ROPE_THETA = 10000.0
D_FF = 4096
NORM_EPS = 1e-6
NEG_INF = -1e30
PAST_LEN = 16384
N_GROUPS = 7
GROUP_W = 512
LANES = 128
VMEM_LIMIT = 56 * 1024 * 1024

LOG_G = tuple(math.log1p(-(2.0 ** (-5.0 - h))) for h in range(RET_HEADS))


def _dot(a, b):
    return jnp.dot(a, b, preferred_element_type=F32)


def _dot_nt(a, b):
    return lax.dot_general(a, b, (((1,), (1,)), ((), ())), preferred_element_type=F32)


def _dot_tn(a, b):
    return lax.dot_general(a, b, (((0,), (0,)), ((), ())), preferred_element_type=F32)


def _rms(x):
    return x * lax.rsqrt(jnp.mean(x * x, axis=-1, keepdims=True) + NORM_EPS)


def _silu(x):
    return x / (1.0 + jnp.exp(-x))


def _params(sem):
    return pltpu.CompilerParams(dimension_semantics=sem, vmem_limit_bytes=VMEM_LIMIT)


def _const_spec(shape):
    nd = len(shape)
    return pl.BlockSpec(shape, lambda *_: (0,) * nd, pipeline_mode=pl.Buffered(1))


def _mod_kernel(c_ref, w_ref, b_ref, o_ref):
    a = _silu(c_ref[...]).astype(BF16)
    o_ref[...] = _dot(a, w_ref[...].astype(BF16)) + b_ref[...]


def _modulation(c, w_ada, b_ada):
    rows, d = c.shape
    n = w_ada.shape[1]
    tn = 1536
    return pl.pallas_call(
        _mod_kernel,
        grid=(n // tn,),
        in_specs=[pl.BlockSpec((rows, d), lambda j: (0, 0)),
                  pl.BlockSpec((d, tn), lambda j: (0, j)),
                  pl.BlockSpec((1, tn), lambda j: (0, j))],
        out_specs=pl.BlockSpec((rows, tn), lambda j: (0, j)),
        out_shape=jax.ShapeDtypeStruct((rows, n), F32),
        compiler_params=_params(("arbitrary",)),
        name="adaln_mod",
    )(c, w_ada, b_ada.reshape(1, n))


def _rotate_pairs(x, cos, sin):
    lane = lax.broadcasted_iota(jnp.int32, x.shape, 1)
    partner = jnp.where(lane % 2 == 0, pltpu.roll(x, LANES - 1, 1), pltpu.roll(x, 1, 1))
    return x * cos + partner * sin


def _rotate_half(x, cos, sin):
    lane = lax.broadcasted_iota(jnp.int32, x.shape, 1)
    half = ATT_HD // 2
    partner = jnp.where(lane % ATT_HD < half, pltpu.roll(x, LANES - half, 1), pltpu.roll(x, half, 1))
    return x * cos + partner * sin


def _proj_kernel(x_ref, mod_ref, g_ref, w_ref, rows_ref, tile_ref,
                 rq_ref, rk_ref, rv_ref, rg_ref, aq_ref, ak_ref, av_ref, akf_ref, avf_ref):
    d = D_MODEL
    x = x_ref[...]
    h = (_rms(x) * g_ref[...]) * (1.0 + mod_ref[:, d:2 * d]) + mod_ref[:, 0:d]
    h = h.astype(BF16)
    t = tile_ref[...]
    cr = t[0:1] * rows_ref[0] - t[1:2] * rows_ref[1]
    sr = (t[1:2] * rows_ref[0] + t[0:1] * rows_ref[1]) * t[4:5]
    ca = t[2:3] * rows_ref[2] - t[3:4] * rows_ref[3]
    sa = (t[3:4] * rows_ref[2] + t[2:3] * rows_ref[3]) * t[5:6]
    nch = GROUP_W // LANES

    def group(gi):
        return _dot(h, w_ref[:, gi * GROUP_W:(gi + 1) * GROUP_W])

    z = group(0)
    for c in range(nch):
        sl = slice(c * LANES, (c + 1) * LANES)
        rq_ref[:, sl] = _rotate_pairs(z[:, sl], cr, sr).astype(rq_ref.dtype)
    z = group(1)
    for c in range(nch):
        sl = slice(c * LANES, (c + 1) * LANES)
        rk_ref[:, sl] = (_rotate_pairs(z[:, sl], cr, sr) * (RET_DK ** -0.5)).astype(rk_ref.dtype)
    rv_ref[...] = group(2).astype(rv_ref.dtype)
    rg_ref[...] = group(3)
    z = group(4)
    for c in range(nch):
        sl = slice(c * LANES, (c + 1) * LANES)
        aq_ref[:, sl] = (_rotate_half(z[:, sl], ca, sa) * (ATT_HD ** -0.5)).astype(aq_ref.dtype)
    z = group(5)
    for c in range(nch):
        sl = slice(c * LANES, (c + 1) * LANES)
        r = _rotate_half(z[:, sl], ca, sa)
        ak_ref[:, sl] = r.astype(ak_ref.dtype)
        akf_ref[:, sl] = r
    z = group(6)
    av_ref[...] = z.astype(av_ref.dtype)
    avf_ref[...] = z


def _project(x, mod, g_pre, w_in_bf, tabs, *, tm, keep, act_dtype):
    b, s, d = x.shape
    nt = s // tm
    rows_tab, tile_tab = tabs
    assert rows_tab.shape == (4, tm, LANES) and tile_tab.shape == (nt, 8, LANES)
    mod_rows = mod.shape[1]
    per_row = mod_rows != 1
    first_keep = (s - keep) // tm
    tok = pl.BlockSpec((None, tm, GROUP_W), lambda bi, i: (bi, i, 0))
    keep_spec = pl.BlockSpec((None, tm, GROUP_W), lambda bi, i: (bi, jnp.maximum(i - first_keep, 0), 0))
    tile_spec = pl.BlockSpec((None, 8, LANES), lambda bi, i: (i, 0, 0))
    mod_spec = pl.BlockSpec((None, tm if per_row else 1, 6 * d),
                            (lambda bi, i: (bi, i, 0)) if per_row else (lambda bi, i: (bi, 0, 0)))
    act = jax.ShapeDtypeStruct((b, s, GROUP_W), act_dtype)
    full = jax.ShapeDtypeStruct((b, s, GROUP_W), F32)
    kept = jax.ShapeDtypeStruct((b, keep, GROUP_W), F32)
    return pl.pallas_call(
        _proj_kernel,
        grid=(b, nt),
        in_specs=[pl.BlockSpec((None, tm, d), lambda bi, i: (bi, i, 0)),
                  mod_spec,
                  _const_spec((1, d)),
                  _const_spec((d, N_GROUPS * GROUP_W)),
                  _const_spec((4, tm, LANES)), tile_spec],
        out_specs=[tok, tok, tok, tok, tok, tok, tok, keep_spec, keep_spec],
        out_shape=[act, act, act, full, act, act, act, kept, kept],
        compiler_params=_params(("arbitrary", "arbitrary")),
        name="in_proj",
    )(x, mod, g_pre.reshape(1, d), w_in_bf, rows_tab, tile_tab)


def _rotation_constants(row_pos, tile_pos):
    lane = np.arange(LANES)
    inv_r = (1.0 / RET_THETA ** np.linspace(0.0, 1.0, RET_DK // 2))[lane // 2]
    inv_a = (1.0 / ROPE_THETA ** (np.arange(0, ATT_HD, 2) / ATT_HD))[lane % (ATT_HD // 2)]
    rp = np.asarray(row_pos, np.float64)[:, None]
    tp = np.asarray(tile_pos, np.float64)[:, None]
    rows = np.stack([np.cos(rp * inv_r), np.sin(rp * inv_r), np.cos(rp * inv_a), np.sin(rp * inv_a)])
    sign_r = np.where(lane % 2 == 0, -1.0, 1.0)
    sign_a = np.where(lane % ATT_HD < ATT_HD // 2, -1.0, 1.0)
    ones = np.ones_like(tp * inv_r)
    tiles = np.stack([np.cos(tp * inv_r), np.sin(tp * inv_r), np.cos(tp * inv_a), np.sin(tp * inv_a),
                      ones * sign_r, ones * sign_a, 0 * ones, 0 * ones], axis=1)
    return jnp.asarray(rows, F32), jnp.asarray(tiles, F32)


def _ret_readout(o, g, gain):
    return (_rms(o) * gain) * _silu(g)


def _ret_prompt_kernel(q_ref, k_ref, v_ref, g_ref, gain_ref, o_ref, sfin_ref, s_scr, *, nchunk):
    i = pl.program_id(1)
    c_len = RET_CHUNK

    @pl.when(i == 0)
    def _():
        s_scr[...] = jnp.zeros_like(s_scr)

    row = lax.broadcasted_iota(jnp.int32, (c_len, c_len), 0)
    col = lax.broadcasted_iota(jnp.int32, (c_len, c_len), 1)
    diff = (row - col).astype(F32)
    ridx = lax.broadcasted_iota(jnp.int32, (c_len, 1), 0).astype(F32)
    for h in range(RET_HEADS):
        lg = LOG_G[h]
        intra = jnp.where(diff >= 0, jnp.exp(lg * jnp.maximum(diff, 0.0)), 0.0)
        q_dec = jnp.exp(lg * (ridx + 1.0))
        k_dec = jnp.exp(lg * (c_len - 1.0 - ridx))
        c_dec = math.exp(lg * c_len)
        hs = slice(h * RET_DK, (h + 1) * RET_DK)
        state = s_scr[h]
        for c in range(nchunk):
            rs = slice(c * c_len, (c + 1) * c_len)
            q = q_ref[rs, hs]
            k = k_ref[rs, hs]
            v = v_ref[rs, hs]
            a = _dot_nt(q, k) * intra
            o = _dot(a.astype(BF16), v)
            o = o + _dot((q.astype(F32) * q_dec).astype(BF16), state.astype(BF16))
            kd = (k.astype(F32) * k_dec).astype(BF16)
            state = state * c_dec + _dot_tn(kd, v)
            o_ref[rs, hs] = _ret_readout(o, g_ref[rs, hs], gain_ref[:, hs]).astype(o_ref.dtype)
        s_scr[h] = state

    @pl.when(i == pl.num_programs(1) - 1)
    def _():
        sfin_ref[...] = s_scr[...]


def _retention_prompt(rq, rk, rv, rg, ret_gain, *, tc):
    b, s, w = rq.shape
    nblk = s // tc
    tok = pl.BlockSpec((None, tc, w), lambda bi, i: (bi, i, 0))
    return pl.pallas_call(
        functools.partial(_ret_prompt_kernel, nchunk=tc // RET_CHUNK),
        grid=(b, nblk),
        in_specs=[tok, tok, tok, tok, _const_spec((1, w))],
        out_specs=[tok, pl.BlockSpec((None, RET_HEADS, RET_DK, RET_DK), lambda bi, i: (bi, 0, 0, 0))],
        out_shape=[jax.ShapeDtypeStruct((b, s, w), BF16),
                   jax.ShapeDtypeStruct((b, RET_HEADS, RET_DK, RET_DK), F32)],
        scratch_shapes=[pltpu.VMEM((RET_HEADS, RET_DK, RET_DK), F32)],
        compiler_params=_params(("arbitrary", "arbitrary")),
        name="retention_prompt",
    )(rq, rk, rv, rg, ret_gain.reshape(1, w))


def _att_pair(q, kk, vv, valid):
    lane = lax.broadcasted_iota(jnp.int32, q.shape, 1)
    low = lane < ATT_HD
    zero = jnp.zeros_like(q)
    q2 = jnp.concatenate([jnp.where(low, q, zero), jnp.where(low, zero, q)], axis=0)
    s = _dot_nt(q2, kk)
    s = jnp.where(valid, s, NEG_INF)
    m = jnp.max(s, axis=-1, keepdims=True)
    p = jnp.exp(s - m)
    l = jnp.sum(p, axis=-1, keepdims=True)
    pv = _dot(p.astype(BF16), vv)
    return pv, m, l


def _band_mask(first_key):
    qi = lax.broadcasted_iota(jnp.int32, (2 * DIL_BLOCK, 2 * DIL_BLOCK), 0) % DIL_BLOCK
    jj = lax.broadcasted_iota(jnp.int32, (2 * DIL_BLOCK, 2 * DIL_BLOCK), 1)
    return (jj >= jnp.maximum(qi, first_key)) & (jj <= qi + SPAN)


def _att_kernel(*refs, nsub, final):
    if final:
        (q_ref, k_ref, v_ref, kp_ref, vp_ref, oa_ref, la_ref, ob_ref, lb_ref,
         o_ref, kcat, vcat) = refs
    else:
        q_ref, k_ref, v_ref, kp_ref, vp_ref, o_ref, lse_ref, kcat, vcat = refs
    i = pl.program_id(2)
    blk = DIL_BLOCK
    kcat[0:blk, :] = kp_ref[...]
    kcat[blk:, :] = k_ref[...]
    vcat[0:blk, :] = vp_ref[...]
    vcat[blk:, :] = v_ref[...]
    lane = lax.broadcasted_iota(jnp.int32, (blk, LANES), 1)
    low = lane < ATT_HD
    valid_first = _band_mask(jnp.where(i == 0, blk, 0))
    valid_rest = _band_mask(0)
    for j in range(nsub):
        valid = valid_first if j == 0 else valid_rest
        rows = slice(j * blk, (j + 1) * blk)
        krows = slice(j * blk, (j + 2) * blk)
        for hp in range(ATT_WIDTH // LANES):
            cols = slice(hp * LANES, (hp + 1) * LANES)
            pv, m, l = _att_pair(q_ref[rows, cols], kcat[krows, cols], vcat[krows, cols], valid)
            o_n = jnp.where(low, pv[0:blk] / l[0:blk], pv[blk:] / l[blk:])
            lse2 = m + jnp.log(l)
            lse = jnp.where(low, lse2[0:blk], lse2[blk:])
            if not final:
                o_ref[rows, cols] = o_n
                lse_ref[rows, cols] = lse
            else:
                la, lb = la_ref[rows, cols], lb_ref[rows, cols]
                top = jnp.maximum(lse, jnp.maximum(la, lb))
                w0, wa, wb = jnp.exp(lse - top), jnp.exp(la - top), jnp.exp(lb - top)
                acc = w0 * o_n + wa * oa_ref[rows, cols] + wb * ob_ref[rows, cols]
                o_ref[rows, cols] = (acc / (w0 + wa + wb)).astype(o_ref.dtype)


def _dilated_prompt(aq, ak, av, dil, prev=None, *, tq):
    b, s, w = aq.shape
    m_len = s // dil
    nsub = tq // DIL_BLOCK
    nblk = m_len // tq
    view = lambda t: t.reshape(b, m_len, dil * w)
    own = pl.BlockSpec((None, tq, w), lambda bi, r, i: (bi, i, r))
    prv = pl.BlockSpec((None, DIL_BLOCK, w), lambda bi, r, i: (bi, jnp.maximum(i * nsub - 1, 0), r))
    in_specs = [own, own, own, prv, prv]
    args = [view(aq), view(ak), view(av), view(ak), view(av)]
    final = prev is not None
    if final:
        assert dil == 1
        in_specs += [own] * 4
        args += list(prev)
        out_specs = own
        out_shape = jax.ShapeDtypeStruct((b, m_len, dil * w), BF16)
    else:
        out_specs = [own, own]
        out_shape = [jax.ShapeDtypeStruct((b, m_len, dil * w), F32)] * 2
    out = pl.pallas_call(
        functools.partial(_att_kernel, nsub=nsub, final=final),
        grid=(b, dil, nblk),
        in_specs=in_specs,
        out_specs=out_specs,
        out_shape=out_shape,
        scratch_shapes=[pltpu.VMEM((tq + DIL_BLOCK, w), BF16), pltpu.VMEM((tq + DIL_BLOCK, w), BF16)],
        compiler_params=_params(("arbitrary", "arbitrary", "arbitrary")),
        name=f"dilated_attn_d{dil}",
    )(*args)
    if final:
        return out.reshape(b, s, w)
    return out[0].reshape(b, s, w), out[1].reshape(b, s, w)


def _tail_kernel(x_ref, ret_ref, att_ref, mod_ref, gpm_ref, gpf_ref, gqf_ref, wo_ref, wu_ref, wd_ref, y_ref):
    d = D_MODEL
    x = x_ref[...]
    mixed = (_dot(ret_ref[...].astype(BF16), wo_ref[0:RET_WIDTH, :])
             + _dot(att_ref[...].astype(BF16), wo_ref[RET_WIDTH:, :]))
    x1 = x + mod_ref[:, 2 * d:3 * d] * (_rms(mixed) * gpm_ref[...])
    h = ((_rms(x1) * gpf_ref[...]) * (1.0 + mod_ref[:, 4 * d:5 * d]) + mod_ref[:, 3 * d:4 * d]).astype(BF16)
    fc = 1024
    f = None
    for c in range(D_FF // fc):
        u = jnp.maximum(_dot(h, wu_ref[:, c * fc:(c + 1) * fc]), 0.0)
        part = _dot((u * u).astype(BF16), wd_ref[c * fc:(c + 1) * fc, :])
        f = part if f is None else f + part
    y_ref[...] = x1 + mod_ref[:, 5 * d:6 * d] * (_rms(f) * gqf_ref[...])


def _tail(x, ret_h, att_h, mod, g_post_mix, g_pre_ffn, g_post_ffn, wo_bf, wu_bf, wd_bf, *, tm):
    b, s, d = x.shape
    nt = s // tm
    per_row = mod.shape[1] != 1
    mod_spec = pl.BlockSpec((None, tm if per_row else 1, 6 * d),
                            (lambda bi, i: (bi, i, 0)) if per_row else (lambda bi, i: (bi, 0, 0)))
    tok = lambda w: pl.BlockSpec((None, tm, w), lambda bi, i: (bi, i, 0))
    return pl.pallas_call(
        _tail_kernel,
        grid=(b, nt),
        in_specs=[tok(d), tok(RET_WIDTH), tok(ATT_WIDTH), mod_spec,
                  _const_spec((1, d)), _const_spec((1, d)), _const_spec((1, d)),
                  _const_spec((d, d)), _const_spec((d, D_FF)), _const_spec((D_FF, d))],
        out_specs=tok(d),
        out_shape=jax.ShapeDtypeStruct((b, s, d), F32),
        compiler_params=_params(("arbitrary", "arbitrary")),
        name="out_proj_mlp",
    )(x, ret_h, att_h, mod, g_post_mix.reshape(1, d), g_pre_ffn.reshape(1, d), g_post_ffn.reshape(1, d),
      wo_bf, wu_bf, wd_bf)


def _ret_sample_kernel(q_ref, k_ref, v_ref, g_ref, gain_ref, dm_ref, qd_ref, kd_ref, cd_ref, s_ref,
                       o_ref, sn_ref, o_scr, qd_scr, kt_scr, *, nb, ln):
    q = q_ref[...].astype(F32)
    k = k_ref[...].astype(F32)
    v = v_ref[...].astype(BF16)
    a = _dot_nt(q.astype(BF16), k.astype(BF16)) * dm_ref[...]
    o_scr[...] = _dot(a.astype(BF16), v)
    qd_scr[...] = q * qd_ref[...]
    kt_scr[...] = (k * kd_ref[...]).T
    c_dec = cd_ref[0:1, :]
    col = lax.broadcasted_iota(jnp.int32, kt_scr.shape, 1)

    def body(bi, carry):
        rows = pl.ds(pl.multiple_of(bi * ln, ln), ln)
        s0 = s_ref[bi]
        o_scr[rows, :] += _dot(qd_scr[rows, :].astype(BF16), s0.astype(BF16))
        mine = (col >= bi * ln) & (col < (bi + 1) * ln)
        kt = jnp.where(mine, kt_scr[...], 0.0).astype(BF16)
        sn_ref[bi] = s0 * c_dec + _dot(kt, v)
        return carry

    lax.fori_loop(0, nb, body, 0)
    o_ref[...] = _ret_readout(o_scr[...], g_ref[...], gain_ref[...]).astype(o_ref.dtype)


def _retention_sample(rq, rk, rv, rg, ret_gain, state, *, nb, ln):
    n = nb * ln
    idx = np.arange(n)
    same = (idx[:, None] // ln == idx[None, :] // ln) & (idx[:, None] >= idx[None, :])
    diff = np.maximum(idx[:, None] - idx[None, :], 0).astype(np.float64)
    step = (idx % ln).astype(np.float64)
    lg = np.asarray(LOG_G, np.float64)
    dm = np.where(same[None], np.exp(lg[:, None, None] * diff[None]), 0.0)
    qd = np.broadcast_to(np.exp(lg[:, None] * (step + 1.0))[:, :, None], (RET_HEADS, n, RET_DK))
    kd = np.broadcast_to(np.exp(lg[:, None] * (ln - 1.0 - step))[:, :, None], (RET_HEADS, n, RET_DK))
    cd = np.broadcast_to(np.exp(lg * ln)[:, None, None], (RET_HEADS, 8, RET_DK))
    tab = lambda t: jnp.asarray(np.ascontiguousarray(t), F32)
    col = pl.BlockSpec((n, RET_DK), lambda h: (0, h))
    per_head = lambda r, c: pl.BlockSpec((None, r, c), lambda h: (h, 0, 0))
    st = pl.BlockSpec((nb, None, RET_DK, RET_DK), lambda h: (0, h, 0, 0))
    return pl.pallas_call(
        functools.partial(_ret_sample_kernel, nb=nb, ln=ln),
        grid=(RET_HEADS,),
        in_specs=[col, col, col, col, pl.BlockSpec((1, RET_DK), lambda h: (0, h)),
                  per_head(n, n), per_head(n, RET_DK), per_head(n, RET_DK), per_head(8, RET_DK), st],
        out_specs=[col, st],
        out_shape=[jax.ShapeDtypeStruct((n, RET_WIDTH), BF16),
                   jax.ShapeDtypeStruct((nb, RET_HEADS, RET_DK, RET_DK), F32)],
        scratch_shapes=[pltpu.VMEM((n, RET_DK), F32), pltpu.VMEM((n, RET_DK), F32),
                        pltpu.VMEM((RET_DK, n), F32)],
        compiler_params=_params(("arbitrary",)),
        name="retention_sample",
    )(rq, rk, rv, rg, ret_gain.reshape(1, RET_WIDTH), tab(dm), tab(qd), tab(kd), tab(cd), state)


def _shift_window(old_ref, new_t, out_ref, *, ln, wbuf):
    lane = lax.broadcasted_iota(jnp.int32, new_t.shape, 1)
    ncol = wbuf // LANES
    rolled = pltpu.roll(old_ref[:, 0:LANES], LANES - ln, 1)
    for c in range(ncol):
        nxt = pltpu.roll(old_ref[:, (c + 1) * LANES:(c + 2) * LANES] if c + 1 < ncol else new_t, LANES - ln, 1)
        out_ref[:, c * LANES:(c + 1) * LANES] = jnp.where(lane < LANES - ln, rolled, nxt)
        rolled = nxt


def _att_sample_kernel(q_ref, kn_ref, vn_ref, ck_ref, cv_ref, cc_ref, cn_ref,
                       o_ref, ko_ref, vo_ref, *, ln, wbuf):
    w = ATT_WIDTH
    pad = jnp.zeros((LANES - ln, w), F32)
    kn_t = jnp.concatenate([kn_ref[...], pad], axis=0).T
    vn_t = jnp.concatenate([vn_ref[...], pad], axis=0).T
    _shift_window(ck_ref, kn_t, ko_ref, ln=ln, wbuf=wbuf)
    _shift_window(cv_ref, vn_t, vo_ref, ln=ln, wbuf=wbuf)

    q = q_ref[...]
    lane = lax.broadcasted_iota(jnp.int32, q.shape, 1)
    heads = [lane // ATT_HD == h for h in range(ATT_HEADS)]
    qm = jnp.concatenate([jnp.where(hm, q, 0.0) for hm in heads], axis=0).astype(BF16)
    cnt_c = jnp.concatenate([cc_ref[...]] * ATT_HEADS, axis=0)
    cnt_n = jnp.concatenate([cn_ref[...]] * ATT_HEADS, axis=0)
    s_c = jnp.where(cnt_c > 0, _dot(qm, ck_ref[...].astype(BF16)), NEG_INF)
    s_n = jnp.where(cnt_n > 0, _dot(qm, kn_t.astype(BF16)), NEG_INF)
    m = jnp.maximum(jnp.max(s_c, axis=-1, keepdims=True), jnp.max(s_n, axis=-1, keepdims=True))
    p_c = cnt_c * jnp.exp(s_c - m)
    p_n = cnt_n * jnp.exp(s_n - m)
    l = jnp.sum(p_c, axis=-1, keepdims=True) + jnp.sum(p_n, axis=-1, keepdims=True)
    o = _dot_nt(p_c.astype(BF16), cv_ref[...].astype(BF16)) + _dot_nt(p_n.astype(BF16), vn_t.astype(BF16))
    o = o / l
    acc = jnp.zeros_like(q)
    for h, hm in enumerate(heads):
        acc = acc + jnp.where(hm, o[h * ln:(h + 1) * ln, :], 0.0)
    o_ref[...] = acc.astype(o_ref.dtype)


def _pattern_counts(ln, wbuf):
    cnt = np.zeros((ln, wbuf + ln), np.float32)
    for dil in DILATIONS:
        for l in range(ln):
            for j in range(SPAN + 1):
                row = wbuf + l - dil * j
                if row >= 0:
                    cnt[l, row] += 1.0
    return cnt


def _attention_sample(aq, akf, avf, cache_k, cache_v, *, nb, ln):
    wbuf = cache_k.shape[2]
    w = ATT_WIDTH
    cnt = _pattern_counts(ln, wbuf)
    cnt_new = np.zeros((ln, LANES), np.float32)
    cnt_new[:, :ln] = cnt[:, wbuf:]
    new = pl.BlockSpec((ln, w), lambda bi: (bi, 0))
    cache = pl.BlockSpec((None, w, wbuf), lambda bi: (bi, 0, 0))
    return pl.pallas_call(
        functools.partial(_att_sample_kernel, ln=ln, wbuf=wbuf),
        grid=(nb,),
        in_specs=[new, new, new, cache, cache, _const_spec((ln, wbuf)), _const_spec((ln, LANES))],
        out_specs=[new, cache, cache],
        out_shape=[jax.ShapeDtypeStruct((nb * ln, w), F32),
                   jax.ShapeDtypeStruct((nb, w, wbuf), F32),
                   jax.ShapeDtypeStruct((nb, w, wbuf), F32)],
        compiler_params=_params(("arbitrary",)),
        name="dilated_attn_sample",
    )(aq, akf, avf, cache_k, cache_v, jnp.asarray(cnt[:, :wbuf]), jnp.asarray(cnt_new))


def _step(x_prompt, x_sample, c_prompt, c_sample, state_ret, cache_win_k, cache_win_v, w_ada, b_ada,
          g_pre_mix, g_post_mix, g_pre_ffn, g_post_ffn, w_in, ret_gain, w_o, w_up, w_down,
          *, tm, tc, tq):
    assert w_in.shape[0] == 1, "single-layer step"
    bp, sp, d = x_prompt.shape
    nb, ln, _ = x_sample.shape
    wbuf = cache_win_k.shape[2]
    n_s = nb * ln

    w_in_bf = w_in[0].astype(BF16)
    wo_bf, wu_bf, wd_bf = w_o[0].astype(BF16), w_up[0].astype(BF16), w_down[0].astype(BF16)

    rows = bp + nb
    pad = (-rows) % 8
    c_all = jnp.concatenate([c_prompt, c_sample, jnp.zeros((pad, d), F32)], axis=0)
    mod = _modulation(c_all, w_ada[0], b_ada[0])
    mod_p = mod[:bp].reshape(bp, 1, 6 * d)
    mod_s = jnp.repeat(mod[bp:rows], ln, axis=0).reshape(1, n_s, 6 * d)

    tabs_p = _rotation_constants(np.arange(tm), np.arange(sp // tm) * tm)
    tabs_s = _rotation_constants(np.tile(np.arange(ln), nb), [PAST_LEN])

    keep = min(MAX_WINDOW, sp)
    rq, rk, rv, rg, aq, ak, av, akf, avf = _project(
        x_prompt, mod_p, g_pre_mix[0], w_in_bf, tabs_p, tm=tm, keep=keep, act_dtype=BF16)
    ret_h, s_fin = _retention_prompt(rq, rk, rv, rg, ret_gain[0], tc=tc)
    o16, l16 = _dilated_prompt(aq, ak, av, 16, tq=tq)
    o4, l4 = _dilated_prompt(aq, ak, av, 4, tq=tq)
    att_h = _dilated_prompt(aq, ak, av, 1, prev=(o16, l16, o4, l4), tq=tq)
    y_prompt = _tail(x_prompt, ret_h, att_h, mod_p, g_post_mix[0], g_pre_ffn[0], g_post_ffn[0],
                     wo_bf, wu_bf, wd_bf, tm=tm)

    xs = x_sample.reshape(1, n_s, d)
    srq, srk, srv, srg, saq, _, _, sakf, savf = _project(
        xs, mod_s, g_pre_mix[0], w_in_bf, tabs_s, tm=n_s, keep=n_s, act_dtype=F32)
    flat = lambda t: t.reshape(n_s, GROUP_W)
    sret_h, s_new = _retention_sample(flat(srq), flat(srk), flat(srv), flat(srg), ret_gain[0],
                                      state_ret[0], nb=nb, ln=ln)
    to_minor = lambda t: jnp.transpose(t[0], (0, 2, 3, 1)).reshape(nb, ATT_WIDTH, wbuf)
    from_minor = lambda t: jnp.transpose(t.reshape(nb, ATT_HEADS, ATT_HD, wbuf), (0, 3, 1, 2))[None]
    satt_h, k_out, v_out = _attention_sample(
        flat(saq), flat(sakf), flat(savf), to_minor(cache_win_k), to_minor(cache_win_v), nb=nb, ln=ln)
    y_sample = _tail(xs, sret_h.reshape(1, n_s, RET_WIDTH), satt_h.reshape(1, n_s, ATT_WIDTH), mod_s,
                     g_post_mix[0], g_pre_ffn[0], g_post_ffn[0], wo_bf, wu_bf, wd_bf, tm=n_s)

    cache_shape = (1, -1, keep, ATT_HEADS, ATT_HD)
    return (y_prompt,
            y_sample.reshape(nb, ln, d),
            s_fin[None],
            akf.reshape(cache_shape),
            avf.reshape(cache_shape),
            s_new[None],
            from_minor(k_out),
            from_minor(v_out))


def kernel(x_prompt, x_sample, c_prompt, c_sample, state_ret, cache_win_k, cache_win_v, w_ada, b_ada,
           g_pre_mix, g_post_mix, g_pre_ffn, g_post_ffn, w_in, ret_gain, w_o, w_up, w_down):
    return _step(x_prompt, x_sample, c_prompt, c_sample, state_ret, cache_win_k, cache_win_v, w_ada, b_ada,
                 g_pre_mix, g_post_mix, g_pre_ffn, g_post_ffn, w_in, ret_gain, w_o, w_up, w_down,
                 tm=512, tc=1024, tq=512)
```

```python
import functools
import math

import numpy as np
import jax
import jax.numpy as jnp
from jax import lax
from jax.experimental import pallas as pl
from jax.experimental.pallas import tpu as pltpu

F32 = jnp.float32
BF16 = jnp.bfloat16

D_MODEL = 1024
RET_HEADS = 4
RET_WIDTH = 512
RET_DK = 128
RET_THETA = 10000.0
RET_CHUNK = 128
ATT_HD = 64
ATT_HEADS = 8
ATT_WIDTH = 512
DILATIONS = (1, 4, 16)
MAX_WINDOW = 2048
SPAN = 128
DIL_BLOCK = 128
ROPE_THETA = 10000.0
D_FF = 4096
NORM_EPS = 1e-6
NEG_INF = -1e30
PAST_LEN = 16384
N_GROUPS = 7
GROUP_W = 512
LANES = 128
VMEM_LIMIT = 56 * 1024 * 1024

LOG_G = tuple(math.log1p(-(2.0 ** (-5.0 - h))) for h in range(RET_HEADS))


def _dot(a, b):
    return jnp.dot(a, b, preferred_element_type=F32)


def _dot_nt(a, b):
    return lax.dot_general(a, b, (((1,), (1,)), ((), ())), preferred_element_type=F32)


def _dot_tn(a, b):
    return lax.dot_general(a, b, (((0,), (0,)), ((), ())), preferred_element_type=F32)


def _rms(x):
    return x * lax.rsqrt(jnp.mean(x * x, axis=-1, keepdims=True) + NORM_EPS)


def _silu(x):
    return x / (1.0 + jnp.exp(-x))


def _params(sem):
    return pltpu.CompilerParams(dimension_semantics=sem, vmem_limit_bytes=VMEM_LIMIT)


def _const_spec(shape):
    nd = len(shape)
    return pl.BlockSpec(shape, lambda *_: (0,) * nd, pipeline_mode=pl.Buffered(1))


def _mod_kernel(c_ref, w_ref, b_ref, o_ref):
    a = _silu(c_ref[...]).astype(BF16)
    o_ref[...] = _dot(a, w_ref[...].astype(BF16)) + b_ref[...]


def _modulation(c, w_ada, b_ada):
    rows, d = c.shape
    n = w_ada.shape[1]
    tn = 1536
    return pl.pallas_call(
        _mod_kernel,
        grid=(n // tn,),
        in_specs=[pl.BlockSpec((rows, d), lambda j: (0, 0)),
                  pl.BlockSpec((d, tn), lambda j: (0, j)),
                  pl.BlockSpec((1, tn), lambda j: (0, j))],
        out_specs=pl.BlockSpec((rows, tn), lambda j: (0, j)),
        out_shape=jax.ShapeDtypeStruct((rows, n), F32),
        compiler_params=_params(("arbitrary",)),
        name="adaln_mod",
    )(c, w_ada, b_ada.reshape(1, n))


def _rotate_pairs(x, cos, sin):
    lane = lax.broadcasted_iota(jnp.int32, x.shape, 1)
    partner = jnp.where(lane % 2 == 0, pltpu.roll(x, LANES - 1, 1), pltpu.roll(x, 1, 1))
    return x * cos + partner * sin


def _rotate_half(x, cos, sin):
    lane = lax.broadcasted_iota(jnp.int32, x.shape, 1)
    half = ATT_HD // 2
    partner = jnp.where(lane % ATT_HD < half, pltpu.roll(x, LANES - half, 1), pltpu.roll(x, half, 1))
    return x * cos + partner * sin


def _store_regrouped(val, c, tok_ref, r4_ref, r16_ref, zs, s4):
    sl = slice(c * LANES, (c + 1) * LANES)
    tok_ref[:, sl] = val.astype(tok_ref.dtype)
    tm = val.shape[0]
    zs[c] = val
    for q in range(4):
        g4 = zs[c, pl.ds(q, tm // 4, stride=4), :]
        r4_ref[q, :, sl] = g4.astype(r4_ref.dtype)
        s4[c, q] = g4
    for q in range(4):
        for a in range(4):
            r16_ref[4 * a + q, :, sl] = s4[c, q, pl.ds(a, tm // 16, stride=4), :].astype(r16_ref.dtype)


def _proj_kernel(x_ref, mod_ref, g_ref, w_ref, rows_ref, tile_ref, *refs, regroup):
    if regroup:
        (rq_ref, rk_ref, rv_ref, rg_ref, aq_ref, ak_ref, av_ref, akf_ref, avf_ref,
         aq4_ref, ak4_ref, av4_ref, aq16_ref, ak16_ref, av16_ref, zs, s4) = refs
    else:
        rq_ref, rk_ref, rv_ref, rg_ref, aq_ref, ak_ref, av_ref, akf_ref, avf_ref = refs
    d = D_MODEL
    x = x_ref[...]
    h = (_rms(x) * g_ref[...]) * (1.0 + mod_ref[:, d:2 * d]) + mod_ref[:, 0:d]
    h = h.astype(BF16)
    t = tile_ref[...]
    cr = t[0:1] * rows_ref[0] - t[1:2] * rows_ref[1]
    sr = (t[1:2] * rows_ref[0] + t[0:1] * rows_ref[1]) * t[4:5]
    ca = t[2:3] * rows_ref[2] - t[3:4] * rows_ref[3]
    sa = (t[3:4] * rows_ref[2] + t[2:3] * rows_ref[3]) * t[5:6]
    nch = GROUP_W // LANES

    def group(gi):
        return _dot(h, w_ref[:, gi * GROUP_W:(gi + 1) * GROUP_W])

    z = group(0)
    for c in range(nch):
        sl = slice(c * LANES, (c + 1) * LANES)
        rq_ref[:, sl] = _rotate_pairs(z[:, sl], cr, sr).astype(rq_ref.dtype)
    z = group(1)
    for c in range(nch):
        sl = slice(c * LANES, (c + 1) * LANES)
        rk_ref[:, sl] = (_rotate_pairs(z[:, sl], cr, sr) * (RET_DK ** -0.5)).astype(rk_ref.dtype)
    rv_ref[...] = group(2).astype(rv_ref.dtype)
    rg_ref[...] = group(3)
    def emit(val, c, tok_ref, r4_ref, r16_ref):
        if regroup:
            _store_regrouped(val, c, tok_ref, r4_ref, r16_ref, zs, s4)
        else:
            tok_ref[:, c * LANES:(c + 1) * LANES] = val.astype(tok_ref.dtype)

    r4 = (aq4_ref, ak4_ref, av4_ref) if regroup else (None,) * 3
    r16 = (aq16_ref, ak16_ref, av16_ref) if regroup else (None,) * 3
    z = group(4)
    for c in range(nch):
        sl = slice(c * LANES, (c + 1) * LANES)
        emit(_rotate_half(z[:, sl], ca, sa) * (ATT_HD ** -0.5), c, aq_ref, r4[0], r16[0])
    z = group(5)
    for c in range(nch):
        sl = slice(c * LANES, (c + 1) * LANES)
        r = _rotate_half(z[:, sl], ca, sa)
        akf_ref[:, sl] = r
        emit(r, c, ak_ref, r4[1], r16[1])
    z = group(6)
    avf_ref[...] = z
    for c in range(nch):
        emit(z[:, c * LANES:(c + 1) * LANES], c, av_ref, r4[2], r16[2])


def _project(x, mod, g_pre, w_in_bf, tabs, *, tm, keep, act_dtype, regroup):
    b, s, d = x.shape
    nt = s // tm
    rows_tab, tile_tab = tabs
    assert rows_tab.shape == (4, tm, LANES) and tile_tab.shape == (nt, 8, LANES)
    mod_rows = mod.shape[1]
    per_row = mod_rows != 1
    first_keep = (s - keep) // tm
    tok = pl.BlockSpec((None, tm, GROUP_W), lambda bi, i: (bi, i, 0))
    keep_spec = pl.BlockSpec((None, tm, GROUP_W), lambda bi, i: (bi, jnp.maximum(i - first_keep, 0), 0))
    tile_spec = pl.BlockSpec((None, 8, LANES), lambda bi, i: (i, 0, 0))
    mod_spec = pl.BlockSpec((None, tm if per_row else 1, 6 * d),
                            (lambda bi, i: (bi, i, 0)) if per_row else (lambda bi, i: (bi, 0, 0)))
    act = jax.ShapeDtypeStruct((b, s, GROUP_W), act_dtype)
    full = jax.ShapeDtypeStruct((b, s, GROUP_W), F32)
    kept = jax.ShapeDtypeStruct((b, keep, GROUP_W), F32)
    out_specs = [tok, tok, tok, tok, tok, tok, tok, keep_spec, keep_spec]
    out_shape = [act, act, act, full, act, act, act, kept, kept]
    scratch = []
    if regroup:
        for r in (4, 16):
            out_specs += [pl.BlockSpec((None, r, tm // r, GROUP_W), lambda bi, i: (bi, 0, i, 0))] * 3
            out_shape += [jax.ShapeDtypeStruct((b, r, s // r, GROUP_W), act_dtype)] * 3
        nch = GROUP_W // LANES
        scratch = [pltpu.VMEM((nch, tm, LANES), F32), pltpu.VMEM((nch, 4, tm // 4, LANES), F32)]
    return pl.pallas_call(
        functools.partial(_proj_kernel, regroup=regroup),
        grid=(b, nt),
        in_specs=[pl.BlockSpec((None, tm, d), lambda bi, i: (bi, i, 0)),
                  mod_spec,
                  _const_spec((1, d)),
                  _const_spec((d, N_GROUPS * GROUP_W)),
                  _const_spec((4, tm, LANES)), tile_spec],
        out_specs=out_specs,
        out_shape=out_shape,
        scratch_shapes=scratch,
        compiler_params=_params(("arbitrary", "arbitrary")),
        name="in_proj",
    )(x, mod, g_pre.reshape(1, d), w_in_bf, rows_tab, tile_tab)


def _rotation_constants(row_pos, tile_pos):
    lane = np.arange(LANES)
    inv_r = (1.0 / RET_THETA ** np.linspace(0.0, 1.0, RET_DK // 2))[lane // 2]
    inv_a = (1.0 / ROPE_THETA ** (np.arange(0, ATT_HD, 2) / ATT_HD))[lane % (ATT_HD // 2)]
    rp = np.asarray(row_pos, np.float64)[:, None]
    tp = np.asarray(tile_pos, np.float64)[:, None]
    rows = np.stack([np.cos(rp * inv_r), np.sin(rp * inv_r), np.cos(rp * inv_a), np.sin(rp * inv_a)])
    sign_r = np.where(lane % 2 == 0, -1.0, 1.0)
    sign_a = np.where(lane % ATT_HD < ATT_HD // 2, -1.0, 1.0)
    ones = np.ones_like(tp * inv_r)
    tiles = np.stack([np.cos(tp * inv_r), np.sin(tp * inv_r), np.cos(tp * inv_a), np.sin(tp * inv_a),
                      ones * sign_r, ones * sign_a, 0 * ones, 0 * ones], axis=1)
    return jnp.asarray(rows, F32), jnp.asarray(tiles, F32)


def _ret_readout(o, g, gain):
    return (_rms(o) * gain) * _silu(g)


def _ret_prompt_kernel(q_ref, k_ref, v_ref, g_ref, gain_ref, o_ref, sfin_ref, s_scr, *, nchunk):
    i = pl.program_id(1)
    c_len = RET_CHUNK

    @pl.when(i == 0)
    def _():
        s_scr[...] = jnp.zeros_like(s_scr)

    row = lax.broadcasted_iota(jnp.int32, (c_len, c_len), 0)
    col = lax.broadcasted_iota(jnp.int32, (c_len, c_len), 1)
    diff = (row - col).astype(F32)
    ridx = lax.broadcasted_iota(jnp.int32, (c_len, 1), 0).astype(F32)
    for h in range(RET_HEADS):
        lg = LOG_G[h]
        intra = jnp.where(diff >= 0, jnp.exp(lg * jnp.maximum(diff, 0.0)), 0.0)
        q_dec = jnp.exp(lg * (ridx + 1.0))
        k_dec = jnp.exp(lg * (c_len - 1.0 - ridx))
        c_dec = math.exp(lg * c_len)
        hs = slice(h * RET_DK, (h + 1) * RET_DK)
        state = s_scr[h]
        for c in range(nchunk):
            rs = slice(c * c_len, (c + 1) * c_len)
            q = q_ref[rs, hs]
            k = k_ref[rs, hs]
            v = v_ref[rs, hs]
            a = _dot_nt(q, k) * intra
            o = _dot(a.astype(BF16), v)
            o = o + _dot((q.astype(F32) * q_dec).astype(BF16), state.astype(BF16))
            kd = (k.astype(F32) * k_dec).astype(BF16)
            state = state * c_dec + _dot_tn(kd, v)
            o_ref[rs, hs] = _ret_readout(o, g_ref[rs, hs], gain_ref[:, hs]).astype(o_ref.dtype)
        s_scr[h] = state

    @pl.when(i == pl.num_programs(1) - 1)
    def _():
        sfin_ref[...] = s_scr[...]


def _retention_prompt(rq, rk, rv, rg, ret_gain, *, tc):
    b, s, w = rq.shape
    nblk = s // tc
    tok = pl.BlockSpec((None, tc, w), lambda bi, i: (bi, i, 0))
    return pl.pallas_call(
        functools.partial(_ret_prompt_kernel, nchunk=tc // RET_CHUNK),
        grid=(b, nblk),
        in_specs=[tok, tok, tok, tok, _const_spec((1, w))],
        out_specs=[tok, pl.BlockSpec((None, RET_HEADS, RET_DK, RET_DK), lambda bi, i: (bi, 0, 0, 0))],
        out_shape=[jax.ShapeDtypeStruct((b, s, w), BF16),
                   jax.ShapeDtypeStruct((b, RET_HEADS, RET_DK, RET_DK), F32)],
        scratch_shapes=[pltpu.VMEM((RET_HEADS, RET_DK, RET_DK), F32)],
        compiler_params=_params(("arbitrary", "arbitrary")),
        name="retention_prompt",
    )(rq, rk, rv, rg, ret_gain.reshape(1, w))


def _att_pair(q, kk, vv, valid):
    lane = lax.broadcasted_iota(jnp.int32, q.shape, 1)
    low = lane < ATT_HD
    zero = jnp.zeros_like(q)
    q2 = jnp.concatenate([jnp.where(low, q, zero), jnp.where(low, zero, q)], axis=0)
    s = _dot_nt(q2, kk)
    s = jnp.where(valid, s, NEG_INF)
    m = jnp.max(s, axis=-1, keepdims=True)
    p = jnp.exp(s - m)
    l = jnp.sum(p, axis=-1, keepdims=True)
    pv = _dot(p.astype(BF16), vv)
    return pv, m, l


def _band_mask(first_key):
    qi = lax.broadcasted_iota(jnp.int32, (2 * DIL_BLOCK, 2 * DIL_BLOCK), 0) % DIL_BLOCK
    jj = lax.broadcasted_iota(jnp.int32, (2 * DIL_BLOCK, 2 * DIL_BLOCK), 1)
    return (jj >= jnp.maximum(qi, first_key)) & (jj <= qi + SPAN)


def _att_kernel(q_ref, k_ref, v_ref, kp_ref, vp_ref, o_ref, lse_ref, kcat, vcat, *, nsub):
    i = pl.program_id(2)
    blk = DIL_BLOCK
    kcat[0:blk, :] = kp_ref[...]
    kcat[blk:, :] = k_ref[...]
    vcat[0:blk, :] = vp_ref[...]
    vcat[blk:, :] = v_ref[...]
    lane = lax.broadcasted_iota(jnp.int32, (blk, LANES), 1)
    low = lane < ATT_HD
    valid_first = _band_mask(jnp.where(i == 0, blk, 0))
    valid_rest = _band_mask(0)
    for j in range(nsub):
        valid = valid_first if j == 0 else valid_rest
        rows = slice(j * blk, (j + 1) * blk)
        krows = slice(j * blk, (j + 2) * blk)
        for hp in range(ATT_WIDTH // LANES):
            cols = slice(hp * LANES, (hp + 1) * LANES)
            pv, m, l = _att_pair(q_ref[rows, cols], kcat[krows, cols], vcat[krows, cols], valid)
            o_n = jnp.where(low, pv[0:blk] / l[0:blk], pv[blk:] / l[blk:])
            lse2 = m + jnp.log(l)
            o_ref[rows, cols] = o_n
            lse_ref[rows, cols] = jnp.where(low, lse2[0:blk], lse2[blk:])


def _dilated_prompt(aq, ak, av, *, tq):
    b, dil, m_len, w = aq.shape
    nsub = tq // DIL_BLOCK
    nblk = m_len // tq
    own = pl.BlockSpec((None, None, tq, w), lambda bi, r, i: (bi, r, i, 0))
    prv = pl.BlockSpec((None, None, DIL_BLOCK, w), lambda bi, r, i: (bi, r, jnp.maximum(i * nsub - 1, 0), 0))
    return pl.pallas_call(
        functools.partial(_att_kernel, nsub=nsub),
        grid=(b, dil, nblk),
        in_specs=[own, own, own, prv, prv],
        out_specs=[own, own],
        out_shape=[jax.ShapeDtypeStruct((b, dil, m_len, w), F32)] * 2,
        scratch_shapes=[pltpu.VMEM((tq + DIL_BLOCK, w), BF16), pltpu.VMEM((tq + DIL_BLOCK, w), BF16)],
        compiler_params=_params(("arbitrary", "arbitrary", "arbitrary")),
        name=f"dilated_attn_d{dil}",
    )(aq, ak, av, ak, av)


def _merge_patterns(o1_ref, l1_ref, o4_ref, l4_ref, o16_ref, l16_ref, tok, s4, att):
    tm = att.shape[0]
    for c in range(ATT_WIDTH // LANES):
        sl = slice(c * LANES, (c + 1) * LANES)
        for n, (r4, r16) in enumerate(((o4_ref, o16_ref), (l4_ref, l16_ref))):
            for q in range(4):
                tok[n, c, pl.ds(q, tm // 4, stride=4), :] = r4[q, :, sl]
                for a in range(4):
                    s4[q, pl.ds(a, tm // 16, stride=4), :] = r16[4 * a + q, :, sl]
            for q in range(4):
                tok[2 + n, c, pl.ds(q, tm // 4, stride=4), :] = s4[q]
        l1, l4, l16 = l1_ref[:, sl], tok[1, c], tok[3, c]
        top = jnp.maximum(l1, jnp.maximum(l4, l16))
        w1, w4, w16 = jnp.exp(l1 - top), jnp.exp(l4 - top), jnp.exp(l16 - top)
        acc = w1 * o1_ref[:, sl] + w4 * tok[0, c] + w16 * tok[2, c]
        att[:, sl] = (acc / (w1 + w4 + w16)).astype(att.dtype)


def _tail_kernel(*refs, merge):
    if merge:
        (x_ref, ret_ref, o1_ref, l1_ref, o4_ref, l4_ref, o16_ref, l16_ref, mod_ref, gpm_ref, gpf_ref, gqf_ref,
         wo_ref, wu_ref, wd_ref, y_ref, tok, s4, att) = refs
        _merge_patterns(o1_ref, l1_ref, o4_ref, l4_ref, o16_ref, l16_ref, tok, s4, att)
        att_h = att[...]
    else:
        x_ref, ret_ref, att_ref, mod_ref, gpm_ref, gpf_ref, gqf_ref, wo_ref, wu_ref, wd_ref, y_ref = refs
        att_h = att_ref[...].astype(BF16)
    d = D_MODEL
    x = x_ref[...]
    mixed = _dot(ret_ref[...].astype(BF16), wo_ref[0:RET_WIDTH, :]) + _dot(att_h, wo_ref[RET_WIDTH:, :])
    x1 = x + mod_ref[:, 2 * d:3 * d] * (_rms(mixed) * gpm_ref[...])
    h = ((_rms(x1) * gpf_ref[...]) * (1.0 + mod_ref[:, 4 * d:5 * d]) + mod_ref[:, 3 * d:4 * d]).astype(BF16)
    fc = 1024
    f = None
    for c in range(D_FF // fc):
        u = jnp.maximum(_dot(h, wu_ref[:, c * fc:(c + 1) * fc]), 0.0)
        part = _dot((u * u).astype(BF16), wd_ref[c * fc:(c + 1) * fc, :])
        f = part if f is None else f + part
    y_ref[...] = x1 + mod_ref[:, 5 * d:6 * d] * (_rms(f) * gqf_ref[...])


def _tail(x, ret_h, att, mod, g_post_mix, g_pre_ffn, g_post_ffn, wo_bf, wu_bf, wd_bf, *, tm):
    b, s, d = x.shape
    nt = s // tm
    per_row = mod.shape[1] != 1
    mod_spec = pl.BlockSpec((None, tm if per_row else 1, 6 * d),
                            (lambda bi, i: (bi, i, 0)) if per_row else (lambda bi, i: (bi, 0, 0)))
    tok = lambda w: pl.BlockSpec((None, tm, w), lambda bi, i: (bi, i, 0))
    merge = isinstance(att, (tuple, list))
    scratch = []
    if merge:
        split = lambda r: pl.BlockSpec((None, r, tm // r, ATT_WIDTH), lambda bi, i: (bi, 0, i, 0))
        one = pl.BlockSpec((None, None, tm, ATT_WIDTH), lambda bi, i: (bi, 0, i, 0))
        att_specs = [one, one, split(4), split(4), split(16), split(16)]
        att_args = list(att)
        nch = ATT_WIDTH // LANES
        scratch = [pltpu.VMEM((4, nch, tm, LANES), F32), pltpu.VMEM((4, tm // 4, LANES), F32),
                   pltpu.VMEM((tm, ATT_WIDTH), BF16)]
    else:
        att_specs, att_args = [tok(ATT_WIDTH)], [att]
    return pl.pallas_call(
        functools.partial(_tail_kernel, merge=merge),
        grid=(b, nt),
        in_specs=[tok(d), tok(RET_WIDTH), *att_specs, mod_spec,
                  _const_spec((1, d)), _const_spec((1, d)), _const_spec((1, d)),
                  _const_spec((d, d)), _const_spec((d, D_FF)), _const_spec((D_FF, d))],
        out_specs=tok(d),
        out_shape=jax.ShapeDtypeStruct((b, s, d), F32),
        scratch_shapes=scratch,
        compiler_params=_params(("arbitrary", "arbitrary")),
        name="out_proj_mlp",
    )(x, ret_h, *att_args, mod, g_post_mix.reshape(1, d), g_pre_ffn.reshape(1, d),
      g_post_ffn.reshape(1, d), wo_bf, wu_bf, wd_bf)


def _ret_sample_kernel(q_ref, k_ref, v_ref, g_ref, gain_ref, dm_ref, qd_ref, kd_ref, cd_ref, s_ref,
                       o_ref, sn_ref, o_scr, qd_scr, kt_scr, *, nb, ln):
    q = q_ref[...].astype(F32)
    k = k_ref[...].astype(F32)
    v = v_ref[...].astype(BF16)
    a = _dot_nt(q.astype(BF16), k.astype(BF16)) * dm_ref[...]
    o_scr[...] = _dot(a.astype(BF16), v)
    qd_scr[...] = q * qd_ref[...]
    kt_scr[...] = (k * kd_ref[...]).T
    c_dec = cd_ref[0:1, :]
    col = lax.broadcasted_iota(jnp.int32, kt_scr.shape, 1)

    def body(bi, carry):
        rows = pl.ds(pl.multiple_of(bi * ln, ln), ln)
        s0 = s_ref[bi]
        o_scr[rows, :] += _dot(qd_scr[rows, :].astype(BF16), s0.astype(BF16))
        mine = (col >= bi * ln) & (col < (bi + 1) * ln)
        kt = jnp.where(mine, kt_scr[...], 0.0).astype(BF16)
        sn_ref[bi] = s0 * c_dec + _dot(kt, v)
        return carry

    lax.fori_loop(0, nb, body, 0)
    o_ref[...] = _ret_readout(o_scr[...], g_ref[...], gain_ref[...]).astype(o_ref.dtype)


def _retention_sample(rq, rk, rv, rg, ret_gain, state, *, nb, ln):
    n = nb * ln
    idx = np.arange(n)
    same = (idx[:, None] // ln == idx[None, :] // ln) & (idx[:, None] >= idx[None, :])
    diff = np.maximum(idx[:, None] - idx[None, :], 0).astype(np.float64)
    step = (idx % ln).astype(np.float64)
    lg = np.asarray(LOG_G, np.float64)
    dm = np.where(same[None], np.exp(lg[:, None, None] * diff[None]), 0.0)
    qd = np.broadcast_to(np.exp(lg[:, None] * (step + 1.0))[:, :, None], (RET_HEADS, n, RET_DK))
    kd = np.broadcast_to(np.exp(lg[:, None] * (ln - 1.0 - step))[:, :, None], (RET_HEADS, n, RET_DK))
    cd = np.broadcast_to(np.exp(lg * ln)[:, None, None], (RET_HEADS, 8, RET_DK))
    tab = lambda t: jnp.asarray(np.ascontiguousarray(t), F32)
    col = pl.BlockSpec((n, RET_DK), lambda h: (0, h))
    per_head = lambda r, c: pl.BlockSpec((None, r, c), lambda h: (h, 0, 0))
    st = pl.BlockSpec((nb, None, RET_DK, RET_DK), lambda h: (0, h, 0, 0))
    return pl.pallas_call(
        functools.partial(_ret_sample_kernel, nb=nb, ln=ln),
        grid=(RET_HEADS,),
        in_specs=[col, col, col, col, pl.BlockSpec((1, RET_DK), lambda h: (0, h)),
                  per_head(n, n), per_head(n, RET_DK), per_head(n, RET_DK), per_head(8, RET_DK), st],
        out_specs=[col, st],
        out_shape=[jax.ShapeDtypeStruct((n, RET_WIDTH), BF16),
                   jax.ShapeDtypeStruct((nb, RET_HEADS, RET_DK, RET_DK), F32)],
        scratch_shapes=[pltpu.VMEM((n, RET_DK), F32), pltpu.VMEM((n, RET_DK), F32),
                        pltpu.VMEM((RET_DK, n), F32)],
        compiler_params=_params(("arbitrary",)),
        name="retention_sample",
    )(rq, rk, rv, rg, ret_gain.reshape(1, RET_WIDTH), tab(dm), tab(qd), tab(kd), tab(cd), state)


def _shift_window(old_ref, new_t, out_ref, *, ln, wbuf):
    lane = lax.broadcasted_iota(jnp.int32, new_t.shape, 1)
    ncol = wbuf // LANES
    rolled = pltpu.roll(old_ref[:, 0:LANES], LANES - ln, 1)
    for c in range(ncol):
        nxt = pltpu.roll(old_ref[:, (c + 1) * LANES:(c + 2) * LANES] if c + 1 < ncol else new_t, LANES - ln, 1)
        out_ref[:, c * LANES:(c + 1) * LANES] = jnp.where(lane < LANES - ln, rolled, nxt)
        rolled = nxt


def _att_sample_kernel(q_ref, kn_ref, vn_ref, ck_ref, cv_ref, cc_ref, cn_ref,
                       o_ref, ko_ref, vo_ref, *, ln, wbuf):
    w = ATT_WIDTH
    pad = jnp.zeros((LANES - ln, w), F32)
    kn_t = jnp.concatenate([kn_ref[...], pad], axis=0).T
    vn_t = jnp.concatenate([vn_ref[...], pad], axis=0).T
    _shift_window(ck_ref, kn_t, ko_ref, ln=ln, wbuf=wbuf)
    _shift_window(cv_ref, vn_t, vo_ref, ln=ln, wbuf=wbuf)

    q = q_ref[...]
    lane = lax.broadcasted_iota(jnp.int32, q.shape, 1)
    heads = [lane // ATT_HD == h for h in range(ATT_HEADS)]
    qm = jnp.concatenate([jnp.where(hm, q, 0.0) for hm in heads], axis=0).astype(BF16)
    cnt_c = jnp.concatenate([cc_ref[...]] * ATT_HEADS, axis=0)
    cnt_n = jnp.concatenate([cn_ref[...]] * ATT_HEADS, axis=0)
    s_c = jnp.where(cnt_c > 0, _dot(qm, ck_ref[...].astype(BF16)), NEG_INF)
    s_n = jnp.where(cnt_n > 0, _dot(qm, kn_t.astype(BF16)), NEG_INF)
    m = jnp.maximum(jnp.max(s_c, axis=-1, keepdims=True), jnp.max(s_n, axis=-1, keepdims=True))
    p_c = cnt_c * jnp.exp(s_c - m)
    p_n = cnt_n * jnp.exp(s_n - m)
    l = jnp.sum(p_c, axis=-1, keepdims=True) + jnp.sum(p_n, axis=-1, keepdims=True)
    o = _dot_nt(p_c.astype(BF16), cv_ref[...].astype(BF16)) + _dot_nt(p_n.astype(BF16), vn_t.astype(BF16))
    o = o / l
    acc = jnp.zeros_like(q)
    for h, hm in enumerate(heads):
        acc = acc + jnp.where(hm, o[h * ln:(h + 1) * ln, :], 0.0)
    o_ref[...] = acc.astype(o_ref.dtype)


def _pattern_counts(ln, wbuf):
    cnt = np.zeros((ln, wbuf + ln), np.float32)
    for dil in DILATIONS:
        for l in range(ln):
            for j in range(SPAN + 1):
                row = wbuf + l - dil * j
                if row >= 0:
                    cnt[l, row] += 1.0
    return cnt


def _attention_sample(aq, akf, avf, cache_k, cache_v, *, nb, ln):
    wbuf = cache_k.shape[2]
    w = ATT_WIDTH
    cnt = _pattern_counts(ln, wbuf)
    cnt_new = np.zeros((ln, LANES), np.float32)
    cnt_new[:, :ln] = cnt[:, wbuf:]
    new = pl.BlockSpec((ln, w), lambda bi: (bi, 0))
    cache = pl.BlockSpec((None, w, wbuf), lambda bi: (bi, 0, 0))
    return pl.pallas_call(
        functools.partial(_att_sample_kernel, ln=ln, wbuf=wbuf),
        grid=(nb,),
        in_specs=[new, new, new, cache, cache, _const_spec((ln, wbuf)), _const_spec((ln, LANES))],
        out_specs=[new, cache, cache],
        out_shape=[jax.ShapeDtypeStruct((nb * ln, w), F32),
                   jax.ShapeDtypeStruct((nb, w, wbuf), F32),
                   jax.ShapeDtypeStruct((nb, w, wbuf), F32)],
        compiler_params=_params(("arbitrary",)),
        name="dilated_attn_sample",
    )(aq, akf, avf, cache_k, cache_v, jnp.asarray(cnt[:, :wbuf]), jnp.asarray(cnt_new))


def _step(x_prompt, x_sample, c_prompt, c_sample, state_ret, cache_win_k, cache_win_v, w_ada, b_ada,
          g_pre_mix, g_post_mix, g_pre_ffn, g_post_ffn, w_in, ret_gain, w_o, w_up, w_down,
          *, tm, tc, tq):
    assert w_in.shape[0] == 1, "single-layer step"
    bp, sp, d = x_prompt.shape
    nb, ln, _ = x_sample.shape
    wbuf = cache_win_k.shape[2]
    n_s = nb * ln

    w_in_bf = w_in[0].astype(BF16)
    wo_bf, wu_bf, wd_bf = w_o[0].astype(BF16), w_up[0].astype(BF16), w_down[0].astype(BF16)

    rows = bp + nb
    pad = (-rows) % 8
    c_all = jnp.concatenate([c_prompt, c_sample, jnp.zeros((pad, d), F32)], axis=0)
    mod = _modulation(c_all, w_ada[0], b_ada[0])
    mod_p = mod[:bp].reshape(bp, 1, 6 * d)
    mod_s = jnp.repeat(mod[bp:rows], ln, axis=0).reshape(1, n_s, 6 * d)

    tabs_p = _rotation_constants(np.arange(tm), np.arange(sp // tm) * tm)
    tabs_s = _rotation_constants(np.tile(np.arange(ln), nb), [PAST_LEN])

    keep = min(MAX_WINDOW, sp)
    (rq, rk, rv, rg, aq, ak, av, akf, avf, aq4, ak4, av4, aq16, ak16, av16) = _project(
        x_prompt, mod_p, g_pre_mix[0], w_in_bf, tabs_p, tm=tm, keep=keep, act_dtype=BF16, regroup=True)
    ret_h, s_fin = _retention_prompt(rq, rk, rv, rg, ret_gain[0], tc=tc)
    o16, l16 = _dilated_prompt(aq16, ak16, av16, tq=tq)
    o4, l4 = _dilated_prompt(aq4, ak4, av4, tq=tq)
    o1, l1 = _dilated_prompt(aq[:, None], ak[:, None], av[:, None], tq=tq)
    y_prompt = _tail(x_prompt, ret_h, (o1, l1, o4, l4, o16, l16), mod_p, g_post_mix[0], g_pre_ffn[0],
                     g_post_ffn[0], wo_bf, wu_bf, wd_bf, tm=tm)

    xs = x_sample.reshape(1, n_s, d)
    srq, srk, srv, srg, saq, _, _, sakf, savf = _project(
        xs, mod_s, g_pre_mix[0], w_in_bf, tabs_s, tm=n_s, keep=n_s, act_dtype=F32, regroup=False)
    flat = lambda t: t.reshape(n_s, GROUP_W)
    sret_h, s_new = _retention_sample(flat(srq), flat(srk), flat(srv), flat(srg), ret_gain[0],
                                      state_ret[0], nb=nb, ln=ln)
    to_minor = lambda t: jnp.transpose(t[0], (0, 2, 3, 1)).reshape(nb, ATT_WIDTH, wbuf)
    from_minor = lambda t: jnp.transpose(t.reshape(nb, ATT_HEADS, ATT_HD, wbuf), (0, 3, 1, 2))[None]
    satt_h, k_out, v_out = _attention_sample(
        flat(saq), flat(sakf), flat(savf), to_minor(cache_win_k), to_minor(cache_win_v), nb=nb, ln=ln)
    y_sample = _tail(xs, sret_h.reshape(1, n_s, RET_WIDTH), satt_h.reshape(1, n_s, ATT_WIDTH), mod_s,
                     g_post_mix[0], g_pre_ffn[0], g_post_ffn[0], wo_bf, wu_bf, wd_bf, tm=n_s)

    cache_shape = (1, -1, keep, ATT_HEADS, ATT_HD)
    return (y_prompt,
            y_sample.reshape(nb, ln, d),
            s_fin[None],
            akf.reshape(cache_shape),
            avf.reshape(cache_shape),
            s_new[None],
            from_minor(k_out),
            from_minor(v_out))


def kernel(x_prompt, x_sample, c_prompt, c_sample, state_ret, cache_win_k, cache_win_v, w_ada, b_ada,
           g_pre_mix, g_post_mix, g_pre_ffn, g_post_ffn, w_in, ret_gain, w_o, w_up, w_down):
    return _step(x_prompt, x_sample, c_prompt, c_sample, state_ret, cache_win_k, cache_win_v, w_ada, b_ada,
                 g_pre_mix, g_post_mix, g_pre_ffn, g_post_ffn, w_in, ret_gain, w_o, w_up, w_down,
                 tm=512, tc=1024, tq=512)
```

```python
import functools
import math

import numpy as np
import jax
import jax.numpy as jnp
from jax import lax
from jax.experimental import pallas as pl
from jax.experimental.pallas import tpu as pltpu

F32 = jnp.float32
BF16 = jnp.bfloat16

D_MODEL = 1024
RET_HEADS = 4
RET_WIDTH = 512
RET_DK = 128
RET_THETA = 10000.0
RET_BLOCK = 256
ATT_HD = 64
ATT_HEADS = 8
ATT_WIDTH = 512
DILATIONS = (1, 4, 16)
MAX_WINDOW = 2048
SPAN = 128
DIL_BLOCK = 128
ROPE_THETA = 10000.0
D_FF = 4096
NORM_EPS = 1e-6
NEG_INF = -1e30
PAST_LEN = 16384
N_GROUPS = 7
GROUP_W = 512
LANES = 128
VMEM_LIMIT = 56 * 1024 * 1024

LOG2_E = math.log2(math.e)
LOG_G = tuple(math.log1p(-(2.0 ** (-5.0 - h))) for h in range(RET_HEADS))


def _dot(a, b):
    return jnp.dot(a, b, preferred_element_type=F32)


def _dot_nt(a, b):
    return lax.dot_general(a, b, (((1,), (1,)), ((), ())), preferred_element_type=F32)


def _dot_tn(a, b):
    return lax.dot_general(a, b, (((0,), (0,)), ((), ())), preferred_element_type=F32)


def _rms(x):
    return x * lax.rsqrt(jnp.mean(x * x, axis=-1, keepdims=True) + NORM_EPS)


def _silu(x):
    return x / (1.0 + jnp.exp(-x))


def _params(sem):
    return pltpu.CompilerParams(dimension_semantics=sem, vmem_limit_bytes=VMEM_LIMIT)


def _const_spec(shape):
    nd = len(shape)
    return pl.BlockSpec(shape, lambda *_: (0,) * nd, pipeline_mode=pl.Buffered(1))


def _mod_kernel(c_ref, w_ref, b_ref, o_ref):
    a = _silu(c_ref[...]).astype(BF16)
    o_ref[...] = _dot(a, w_ref[...].astype(BF16)) + b_ref[...]


def _modulation(c, w_ada, b_ada):
    rows, d = c.shape
    n = w_ada.shape[1]
    tn = 1536
    return pl.pallas_call(
        _mod_kernel,
        grid=(n // tn,),
        in_specs=[pl.BlockSpec((rows, d), lambda j: (0, 0)),
                  pl.BlockSpec((d, tn), lambda j: (0, j)),
                  pl.BlockSpec((1, tn), lambda j: (0, j))],
        out_specs=pl.BlockSpec((rows, tn), lambda j: (0, j)),
        out_shape=jax.ShapeDtypeStruct((rows, n), F32),
        compiler_params=_params(("arbitrary",)),
        name="adaln_mod",
    )(c, w_ada, b_ada.reshape(1, n))


def _rotate_pairs(x, cos, sin):
    lane = lax.broadcasted_iota(jnp.int32, x.shape, 1)
    partner = jnp.where(lane % 2 == 0, pltpu.roll(x, LANES - 1, 1), pltpu.roll(x, 1, 1))
    return x * cos + partner * sin


def _rotate_half(x, cos, sin):
    lane = lax.broadcasted_iota(jnp.int32, x.shape, 1)
    half = ATT_HD // 2
    partner = jnp.where(lane % ATT_HD < half, pltpu.roll(x, LANES - half, 1), pltpu.roll(x, half, 1))
    return x * cos + partner * sin


def _store_regrouped(val, c, tok_ref, r4_ref, r16_ref, zs, s4):
    sl = slice(c * LANES, (c + 1) * LANES)
    tok_ref[:, sl] = val.astype(tok_ref.dtype)
    tm = val.shape[0]
    zs[c] = val
    for q in range(4):
        g4 = zs[c, pl.ds(q, tm // 4, stride=4), :]
        r4_ref[q, :, sl] = g4.astype(r4_ref.dtype)
        s4[c, q] = g4
    for q in range(4):
        for a in range(4):
            r16_ref[4 * a + q, :, sl] = s4[c, q, pl.ds(a, tm // 16, stride=4), :].astype(r16_ref.dtype)


def _proj_kernel(x_ref, mod_ref, g_ref, w_ref, rows_ref, tile_ref, *refs, regroup):
    if regroup:
        (rq_ref, rk_ref, rv_ref, rg_ref, aq_ref, ak_ref, av_ref, akf_ref, avf_ref,
         aq4_ref, ak4_ref, av4_ref, aq16_ref, ak16_ref, av16_ref, zs, s4) = refs
    else:
        rq_ref, rk_ref, rv_ref, rg_ref, aq_ref, ak_ref, av_ref, akf_ref, avf_ref = refs
    d = D_MODEL
    x = x_ref[...]
    h = (_rms(x) * g_ref[...]) * (1.0 + mod_ref[:, d:2 * d]) + mod_ref[:, 0:d]
    h = h.astype(BF16)
    t = tile_ref[...]
    cr = t[0:1] * rows_ref[0] - t[1:2] * rows_ref[1]
    sr = (t[1:2] * rows_ref[0] + t[0:1] * rows_ref[1]) * t[4:5]
    ca = t[2:3] * rows_ref[2] - t[3:4] * rows_ref[3]
    sa = (t[3:4] * rows_ref[2] + t[2:3] * rows_ref[3]) * t[5:6]
    nch = GROUP_W // LANES

    def group(gi):
        return _dot(h, w_ref[:, gi * GROUP_W:(gi + 1) * GROUP_W])

    z = group(0)
    for c in range(nch):
        sl = slice(c * LANES, (c + 1) * LANES)
        rq_ref[:, sl] = _rotate_pairs(z[:, sl], cr, sr).astype(rq_ref.dtype)
    z = group(1)
    for c in range(nch):
        sl = slice(c * LANES, (c + 1) * LANES)
        rk_ref[:, sl] = (_rotate_pairs(z[:, sl], cr, sr) * (RET_DK ** -0.5)).astype(rk_ref.dtype)
    rv_ref[...] = group(2).astype(rv_ref.dtype)
    rg_ref[...] = group(3)
    def emit(val, c, tok_ref, r4_ref, r16_ref):
        if regroup:
            _store_regrouped(val, c, tok_ref, r4_ref, r16_ref, zs, s4)
        else:
            tok_ref[:, c * LANES:(c + 1) * LANES] = val.astype(tok_ref.dtype)

    r4 = (aq4_ref, ak4_ref, av4_ref) if regroup else (None,) * 3
    r16 = (aq16_ref, ak16_ref, av16_ref) if regroup else (None,) * 3
    z = group(4)
    for c in range(nch):
        sl = slice(c * LANES, (c + 1) * LANES)
        emit(_rotate_half(z[:, sl], ca, sa) * (ATT_HD ** -0.5 * LOG2_E), c, aq_ref, r4[0], r16[0])
    z = group(5)
    for c in range(nch):
        sl = slice(c * LANES, (c + 1) * LANES)
        r = _rotate_half(z[:, sl], ca, sa)
        akf_ref[:, sl] = r
        emit(r, c, ak_ref, r4[1], r16[1])
    z = group(6)
    avf_ref[...] = z
    for c in range(nch):
        emit(z[:, c * LANES:(c + 1) * LANES], c, av_ref, r4[2], r16[2])


def _project(x, mod, g_pre, w_in_bf, tabs, *, tm, keep, act_dtype, regroup):
    b, s, d = x.shape
    nt = s // tm
    rows_tab, tile_tab = tabs
    assert rows_tab.shape == (4, tm, LANES) and tile_tab.shape == (nt, 8, LANES)
    mod_rows = mod.shape[1]
    per_row = mod_rows != 1
    first_keep = (s - keep) // tm
    tok = pl.BlockSpec((None, tm, GROUP_W), lambda bi, i: (bi, i, 0))
    keep_spec = pl.BlockSpec((None, tm, GROUP_W), lambda bi, i: (bi, jnp.maximum(i - first_keep, 0), 0))
    tile_spec = pl.BlockSpec((None, 8, LANES), lambda bi, i: (i, 0, 0))
    mod_spec = pl.BlockSpec((None, tm if per_row else 1, 6 * d),
                            (lambda bi, i: (bi, i, 0)) if per_row else (lambda bi, i: (bi, 0, 0)))
    act = jax.ShapeDtypeStruct((b, s, GROUP_W), act_dtype)
    full = jax.ShapeDtypeStruct((b, s, GROUP_W), F32)
    kept = jax.ShapeDtypeStruct((b, keep, GROUP_W), F32)
    out_specs = [tok, tok, tok, tok, tok, tok, tok, keep_spec, keep_spec]
    out_shape = [act, act, act, full, act, act, act, kept, kept]
    scratch = []
    if regroup:
        for r in (4, 16):
            out_specs += [pl.BlockSpec((None, r, tm // r, GROUP_W), lambda bi, i: (bi, 0, i, 0))] * 3
            out_shape += [jax.ShapeDtypeStruct((b, r, s // r, GROUP_W), act_dtype)] * 3
        nch = GROUP_W // LANES
        scratch = [pltpu.VMEM((nch, tm, LANES), F32), pltpu.VMEM((nch, 4, tm // 4, LANES), F32)]
    return pl.pallas_call(
        functools.partial(_proj_kernel, regroup=regroup),
        grid=(b, nt),
        in_specs=[pl.BlockSpec((None, tm, d), lambda bi, i: (bi, i, 0)),
                  mod_spec,
                  _const_spec((1, d)),
                  _const_spec((d, N_GROUPS * GROUP_W)),
                  _const_spec((4, tm, LANES)), tile_spec],
        out_specs=out_specs,
        out_shape=out_shape,
        scratch_shapes=scratch,
        compiler_params=_params(("arbitrary", "arbitrary")),
        name="in_proj",
    )(x, mod, g_pre.reshape(1, d), w_in_bf, rows_tab, tile_tab)


def _rotation_constants(row_pos, tile_pos):
    lane = np.arange(LANES)
    inv_r = (1.0 / RET_THETA ** np.linspace(0.0, 1.0, RET_DK // 2))[lane // 2]
    inv_a = (1.0 / ROPE_THETA ** (np.arange(0, ATT_HD, 2) / ATT_HD))[lane % (ATT_HD // 2)]
    rp = np.asarray(row_pos, np.float64)[:, None]
    tp = np.asarray(tile_pos, np.float64)[:, None]
    rows = np.stack([np.cos(rp * inv_r), np.sin(rp * inv_r), np.cos(rp * inv_a), np.sin(rp * inv_a)])
    sign_r = np.where(lane % 2 == 0, -1.0, 1.0)
    sign_a = np.where(lane % ATT_HD < ATT_HD // 2, -1.0, 1.0)
    ones = np.ones_like(tp * inv_r)
    tiles = np.stack([np.cos(tp * inv_r), np.sin(tp * inv_r), np.cos(tp * inv_a), np.sin(tp * inv_a),
                      ones * sign_r, ones * sign_a, 0 * ones, 0 * ones], axis=1)
    return jnp.asarray(rows, F32), jnp.asarray(tiles, F32)


def _ret_readout(o, g, gain):
    return (_rms(o) * gain) * _silu(g)


def _ret_prompt_kernel(q_ref, k_ref, v_ref, g_ref, gain_ref, o_ref, sfin_ref, s_scr, *, nchunk, c_len):
    i = pl.program_id(1)

    @pl.when(i == 0)
    def _():
        s_scr[...] = jnp.zeros_like(s_scr)

    row = lax.broadcasted_iota(jnp.int32, (c_len, c_len), 0)
    col = lax.broadcasted_iota(jnp.int32, (c_len, c_len), 1)
    diff = (row - col).astype(F32)
    ridx = lax.broadcasted_iota(jnp.int32, (c_len, 1), 0).astype(F32)
    for h in range(RET_HEADS):
        lg = LOG_G[h]
        intra = jnp.where(diff >= 0, jnp.exp(lg * jnp.maximum(diff, 0.0)), 0.0)
        q_dec = jnp.exp(lg * (ridx + 1.0))
        k_dec = jnp.exp(lg * (c_len - 1.0 - ridx))
        c_dec = math.exp(lg * c_len)
        hs = slice(h * RET_DK, (h + 1) * RET_DK)
        state = s_scr[h]
        for c in range(nchunk):
            rs = slice(c * c_len, (c + 1) * c_len)
            q = q_ref[rs, hs]
            k = k_ref[rs, hs]
            v = v_ref[rs, hs]
            a = _dot_nt(q, k) * intra
            lhs = jnp.concatenate([a.astype(BF16), (q.astype(F32) * q_dec).astype(BF16)], axis=1)
            o = _dot(lhs, jnp.concatenate([v, state.astype(BF16)], axis=0))
            kd = (k.astype(F32) * k_dec).astype(BF16)
            state = state * c_dec + _dot_tn(kd, v)
            o_ref[rs, hs] = _ret_readout(o, g_ref[rs, hs], gain_ref[:, hs]).astype(o_ref.dtype)
        s_scr[h] = state

    @pl.when(i == pl.num_programs(1) - 1)
    def _():
        sfin_ref[...] = s_scr[...]


def _retention_prompt(rq, rk, rv, rg, ret_gain, *, tc, chunk):
    b, s, w = rq.shape
    nblk = s // tc
    tok = pl.BlockSpec((None, tc, w), lambda bi, i: (bi, i, 0))
    return pl.pallas_call(
        functools.partial(_ret_prompt_kernel, nchunk=tc // chunk, c_len=chunk),
        grid=(b, nblk),
        in_specs=[tok, tok, tok, tok, _const_spec((1, w))],
        out_specs=[tok, pl.BlockSpec((None, RET_HEADS, RET_DK, RET_DK), lambda bi, i: (bi, 0, 0, 0))],
        out_shape=[jax.ShapeDtypeStruct((b, s, w), BF16),
                   jax.ShapeDtypeStruct((b, RET_HEADS, RET_DK, RET_DK), F32)],
        scratch_shapes=[pltpu.VMEM((RET_HEADS, RET_DK, RET_DK), F32)],
        compiler_params=_params(("arbitrary", "arbitrary")),
        name="retention_prompt",
    )(rq, rk, rv, rg, ret_gain.reshape(1, w))


def _att_kernel(q_ref, k_ref, v_ref, kp_ref, vp_ref, oh_ref, bias_ref, pv_ref, ml_ref, kcat, vcat, *, nsub):
    i = pl.program_id(2)
    blk = DIL_BLOCK
    kcat[0:blk, :] = kp_ref[...]
    kcat[blk:, :] = k_ref[...]
    vcat[0:blk, :] = vp_ref[...]
    vcat[blk:, :] = v_ref[...]
    lane = lax.broadcasted_iota(jnp.int32, (blk, LANES), 1)
    low = lane < ATT_HD
    quarter = lane // (ATT_HD // 2)
    onehot = oh_ref[...]
    bias_rest = bias_ref[1]
    bias_first = jnp.where(i == 0, bias_ref[0], bias_rest)
    ones = jnp.ones((2 * blk, LANES), BF16)
    for j in range(nsub):
        bias = bias_first if j == 0 else bias_rest
        rows = slice(j * blk, (j + 1) * blk)
        krows = slice(j * blk, (j + 2) * blk)
        for hp in range(ATT_WIDTH // LANES):
            cols = slice(hp * LANES, (hp + 1) * LANES)
            q = q_ref[rows, cols]
            zero = jnp.zeros_like(q)
            q2 = jnp.concatenate([jnp.where(low, q, zero), jnp.where(low, zero, q)], axis=0)
            s = _dot_nt(jnp.concatenate([q2, onehot], axis=1), jnp.concatenate([kcat[krows, cols], bias], axis=1))
            m = jnp.max(s, axis=-1, keepdims=True)
            p = jnp.exp2(s - m).astype(BF16)
            pv = _dot(p, jnp.concatenate([vcat[krows, cols], ones], axis=1))
            pv_ref[rows, cols] = jnp.where(low, pv[0:blk, 0:LANES], pv[blk:, 0:LANES]).astype(pv_ref.dtype)
            ml_ref[rows, cols] = jnp.where(
                quarter == 0, m[0:blk], jnp.where(quarter == 1, pv[0:blk, LANES:],
                                                  jnp.where(quarter == 2, m[blk:], pv[blk:, LANES:])))


def _band_tables():
    jj = np.arange(2 * DIL_BLOCK)[:, None]
    qi = np.arange(DIL_BLOCK)[None, :]
    bias = np.stack([np.where((jj >= np.maximum(qi, fk)) & (jj <= qi + SPAN), 0.0, NEG_INF)
                     for fk in (DIL_BLOCK, 0)])
    onehot = np.tile(np.eye(DIL_BLOCK), (2, 1))
    return jnp.asarray(onehot, BF16), jnp.asarray(bias, BF16)


def _dilated_prompt(aq, ak, av, *, tq):
    b, dil, m_len, w = aq.shape
    nsub = tq // DIL_BLOCK
    nblk = m_len // tq
    own = pl.BlockSpec((None, None, tq, w), lambda bi, r, i: (bi, r, i, 0))
    prv = pl.BlockSpec((None, None, DIL_BLOCK, w), lambda bi, r, i: (bi, r, jnp.maximum(i * nsub - 1, 0), 0))
    onehot, bias = _band_tables()
    return pl.pallas_call(
        functools.partial(_att_kernel, nsub=nsub),
        grid=(b, dil, nblk),
        in_specs=[own, own, own, prv, prv, _const_spec(onehot.shape), _const_spec(bias.shape)],
        out_specs=[own, own],
        out_shape=[jax.ShapeDtypeStruct((b, dil, m_len, w), BF16), jax.ShapeDtypeStruct((b, dil, m_len, w), F32)],
        scratch_shapes=[pltpu.VMEM((tq + DIL_BLOCK, w), BF16), pltpu.VMEM((tq + DIL_BLOCK, w), BF16)],
        compiler_params=_params(("arbitrary", "arbitrary", "arbitrary")),
        name=f"dilated_attn_d{dil}",
    )(aq, ak, av, ak, av, onehot, bias)


def _merge_patterns(pv1_ref, ml1_ref, pv4_ref, ml4_ref, pv16_ref, ml16_ref, tok, s4, att):
    tm = att.shape[0]
    half = ATT_HD // 2
    lane = lax.broadcasted_iota(jnp.int32, (tm, LANES), 1)
    first = lane % ATT_HD < half

    def unpack(ml):
        return (jnp.where(first, ml, pltpu.roll(ml, half, 1)),
                jnp.where(first, pltpu.roll(ml, LANES - half, 1), ml))

    for c in range(ATT_WIDTH // LANES):
        sl = slice(c * LANES, (c + 1) * LANES)
        for n, (r4, r16) in enumerate(((pv4_ref, pv16_ref), (ml4_ref, ml16_ref))):
            for q in range(4):
                tok[n, c, pl.ds(q, tm // 4, stride=4), :] = r4[q, :, sl].astype(F32)
                for a in range(4):
                    s4[q, pl.ds(a, tm // 16, stride=4), :] = r16[4 * a + q, :, sl].astype(F32)
            for q in range(4):
                tok[2 + n, c, pl.ds(q, tm // 4, stride=4), :] = s4[q]
        (m1, l1), (m4, l4), (m16, l16) = unpack(ml1_ref[:, sl]), unpack(tok[1, c]), unpack(tok[3, c])
        top = jnp.maximum(m1, jnp.maximum(m4, m16))
        e1, e4, e16 = jnp.exp2(m1 - top), jnp.exp2(m4 - top), jnp.exp2(m16 - top)
        num = e1 * pv1_ref[:, sl].astype(F32) + e4 * tok[0, c] + e16 * tok[2, c]
        den = e1 * l1 + e4 * l4 + e16 * l16
        merged = num / den
        att[:, sl] = merged.astype(att.dtype)
        yield merged


def _tail_body(x_ref, ret_ref, att_h, mod_ref, gpm_ref, gpf_ref, gqf_ref, wo_ref, wu_ref, wd_ref, y_ref,
               filler=iter(())):
    d = D_MODEL
    x = x_ref[...]
    mixed = _dot(ret_ref[...].astype(BF16), wo_ref[0:RET_WIDTH, :]) + _dot(att_h, wo_ref[RET_WIDTH:, :])
    x1 = x + mod_ref[:, 2 * d:3 * d] * (_rms(mixed) * gpm_ref[...])
    h = ((_rms(x1) * gpf_ref[...]) * (1.0 + mod_ref[:, 4 * d:5 * d]) + mod_ref[:, 3 * d:4 * d]).astype(BF16)
    fc = 1024
    f = None
    for c in range(D_FF // fc):
        u = jnp.maximum(_dot(h, wu_ref[:, c * fc:(c + 1) * fc]), 0.0)
        piece = next(filler, None)
        if piece is not None:
            bits = pltpu.bitcast(piece, jnp.uint32)
            sixteen = jnp.full(bits.shape, 16, jnp.uint32)
            zero = pltpu.bitcast(lax.shift_right_logical(lax.shift_right_logical(bits, sixteen), sixteen), F32)
            u = jnp.concatenate([u[:, 0:LANES] + zero, u[:, LANES:]], axis=1)
        part = _dot((u * u).astype(BF16), wd_ref[c * fc:(c + 1) * fc, :])
        f = part if f is None else f + part
    for _ in filler:
        pass
    y_ref[...] = x1 + mod_ref[:, 5 * d:6 * d] * (_rms(f) * gqf_ref[...])


def _tail_kernel(x_ref, ret_ref, att_ref, *rest):
    _tail_body(x_ref, ret_ref, att_ref[...].astype(BF16), *rest)


def _tail_merge_kernel(x_ref, ret_ref, pv1_ref, ml1_ref, pv4_ref, ml4_ref, pv16_ref, ml16_ref, mod_ref,
                       gpm_ref, gpf_ref, gqf_ref, wo_ref, wu_ref, wd_ref, y_ref, tok, s4, att):
    t = pl.program_id(0)
    merge = functools.partial(_merge_patterns, pv1_ref, ml1_ref, pv4_ref, ml4_ref, pv16_ref, ml16_ref,
                              tok, s4, att)

    @pl.when(t == 0)
    def _():
        for _ in merge():
            pass

    @pl.when(t > 0)
    def _():
        att_h = att[...]
        _tail_body(x_ref, ret_ref, att_h, mod_ref, gpm_ref, gpf_ref, gqf_ref, wo_ref, wu_ref, wd_ref, y_ref,
                   filler=merge())


def _tail(x, ret_h, att, mod, g_post_mix, g_pre_ffn, g_post_ffn, wo_bf, wu_bf, wd_bf, *, tm):
    b, s, d = x.shape
    nt = s // tm
    per_row = mod.shape[1] != 1
    weights = [_const_spec((1, d)), _const_spec((1, d)), _const_spec((1, d)),
               _const_spec((d, d)), _const_spec((d, D_FF)), _const_spec((D_FF, d))]
    weight_args = (g_post_mix.reshape(1, d), g_pre_ffn.reshape(1, d), g_post_ffn.reshape(1, d),
                   wo_bf, wu_bf, wd_bf)
    out_shape = jax.ShapeDtypeStruct((b, s, d), F32)
    if not isinstance(att, (tuple, list)):
        mod_spec = pl.BlockSpec((None, tm if per_row else 1, 6 * d),
                                (lambda bi, i: (bi, i, 0)) if per_row else (lambda bi, i: (bi, 0, 0)))
        tok = lambda w: pl.BlockSpec((None, tm, w), lambda bi, i: (bi, i, 0))
        return pl.pallas_call(
            _tail_kernel,
            grid=(b, nt),
            in_specs=[tok(d), tok(RET_WIDTH), tok(ATT_WIDTH), mod_spec, *weights],
            out_specs=tok(d),
            out_shape=out_shape,
            compiler_params=_params(("arbitrary", "arbitrary")),
            name="out_proj_mlp",
        )(x, ret_h, att, mod, *weight_args)

    assert not per_row
    last = b * nt - 1
    cur = lambda t: jnp.maximum(t - 1, 0)
    nxt = lambda t: jnp.minimum(t, last)
    tok = lambda w: pl.BlockSpec((None, tm, w), lambda t: (cur(t) // nt, cur(t) % nt, 0))
    mod_spec = pl.BlockSpec((None, 1, 6 * d), lambda t: (cur(t) // nt, 0, 0))
    split = lambda r: pl.BlockSpec((None, r, tm // r, ATT_WIDTH), lambda t: (nxt(t) // nt, 0, nxt(t) % nt, 0))
    one = pl.BlockSpec((None, None, tm, ATT_WIDTH), lambda t: (nxt(t) // nt, 0, nxt(t) % nt, 0))
    nch = ATT_WIDTH // LANES
    return pl.pallas_call(
        _tail_merge_kernel,
        grid=(b * nt + 1,),
        in_specs=[tok(d), tok(RET_WIDTH), one, one, split(4), split(4), split(16), split(16), mod_spec, *weights],
        out_specs=tok(d),
        out_shape=out_shape,
        scratch_shapes=[pltpu.VMEM((4, nch, tm, LANES), F32), pltpu.VMEM((4, tm // 4, LANES), F32),
                        pltpu.VMEM((tm, ATT_WIDTH), BF16)],
        compiler_params=_params(("arbitrary",)),
        name="out_proj_mlp",
    )(x, ret_h, *att, mod, *weight_args)


def _ret_sample_kernel(q_ref, k_ref, v_ref, g_ref, gain_ref, dm_ref, qd_ref, kd_ref, cd_ref, s_ref,
                       o_ref, sn_ref, o_scr, qd_scr, kt_scr, *, nb, ln):
    q = q_ref[...].astype(F32)
    k = k_ref[...].astype(F32)
    v = v_ref[...].astype(BF16)
    a = _dot_nt(q.astype(BF16), k.astype(BF16)) * dm_ref[...]
    o_scr[...] = _dot(a.astype(BF16), v)
    qd_scr[...] = q * qd_ref[...]
    kt_scr[...] = (k * kd_ref[...]).T
    c_dec = cd_ref[0:1, :]
    col = lax.broadcasted_iota(jnp.int32, kt_scr.shape, 1)

    def body(bi, carry):
        rows = pl.ds(pl.multiple_of(bi * ln, ln), ln)
        s0 = s_ref[bi]
        o_scr[rows, :] += _dot(qd_scr[rows, :].astype(BF16), s0.astype(BF16))
        mine = (col >= bi * ln) & (col < (bi + 1) * ln)
        kt = jnp.where(mine, kt_scr[...], 0.0).astype(BF16)
        sn_ref[bi] = s0 * c_dec + _dot(kt, v)
        return carry

    lax.fori_loop(0, nb, body, 0)
    o_ref[...] = _ret_readout(o_scr[...], g_ref[...], gain_ref[...]).astype(o_ref.dtype)


def _retention_sample(rq, rk, rv, rg, ret_gain, state, *, nb, ln):
    n = nb * ln
    idx = np.arange(n)
    same = (idx[:, None] // ln == idx[None, :] // ln) & (idx[:, None] >= idx[None, :])
    diff = np.maximum(idx[:, None] - idx[None, :], 0).astype(np.float64)
    step = (idx % ln).astype(np.float64)
    lg = np.asarray(LOG_G, np.float64)
    dm = np.where(same[None], np.exp(lg[:, None, None] * diff[None]), 0.0)
    qd = np.broadcast_to(np.exp(lg[:, None] * (step + 1.0))[:, :, None], (RET_HEADS, n, RET_DK))
    kd = np.broadcast_to(np.exp(lg[:, None] * (ln - 1.0 - step))[:, :, None], (RET_HEADS, n, RET_DK))
    cd = np.broadcast_to(np.exp(lg * ln)[:, None, None], (RET_HEADS, 8, RET_DK))
    tab = lambda t: jnp.asarray(np.ascontiguousarray(t), F32)
    col = pl.BlockSpec((n, RET_DK), lambda h: (0, h))
    per_head = lambda r, c: pl.BlockSpec((None, r, c), lambda h: (h, 0, 0))
    st = pl.BlockSpec((nb, None, RET_DK, RET_DK), lambda h: (0, h, 0, 0))
    return pl.pallas_call(
        functools.partial(_ret_sample_kernel, nb=nb, ln=ln),
        grid=(RET_HEADS,),
        in_specs=[col, col, col, col, pl.BlockSpec((1, RET_DK), lambda h: (0, h)),
                  per_head(n, n), per_head(n, RET_DK), per_head(n, RET_DK), per_head(8, RET_DK), st],
        out_specs=[col, st],
        out_shape=[jax.ShapeDtypeStruct((n, RET_WIDTH), BF16),
                   jax.ShapeDtypeStruct((nb, RET_HEADS, RET_DK, RET_DK), F32)],
        scratch_shapes=[pltpu.VMEM((n, RET_DK), F32), pltpu.VMEM((n, RET_DK), F32),
                        pltpu.VMEM((RET_DK, n), F32)],
        compiler_params=_params(("arbitrary",)),
        name="retention_sample",
    )(rq, rk, rv, rg, ret_gain.reshape(1, RET_WIDTH), tab(dm), tab(qd), tab(kd), tab(cd), state)


def _shift_window(old_ref, new_t, out_ref, *, ln, wbuf):
    lane = lax.broadcasted_iota(jnp.int32, new_t.shape, 1)
    ncol = wbuf // LANES
    rolled = pltpu.roll(old_ref[:, 0:LANES], LANES - ln, 1)
    for c in range(ncol):
        nxt = pltpu.roll(old_ref[:, (c + 1) * LANES:(c + 2) * LANES] if c + 1 < ncol else new_t, LANES - ln, 1)
        out_ref[:, c * LANES:(c + 1) * LANES] = jnp.where(lane < LANES - ln, rolled, nxt)
        rolled = nxt


def _att_sample_kernel(q_ref, kn_ref, vn_ref, ck_ref, cv_ref, cc_ref, cn_ref,
                       o_ref, ko_ref, vo_ref, *, ln, wbuf):
    w = ATT_WIDTH
    pad = jnp.zeros((LANES - ln, w), F32)
    kn_t = jnp.concatenate([kn_ref[...], pad], axis=0).T
    vn_t = jnp.concatenate([vn_ref[...], pad], axis=0).T
    _shift_window(ck_ref, kn_t, ko_ref, ln=ln, wbuf=wbuf)
    _shift_window(cv_ref, vn_t, vo_ref, ln=ln, wbuf=wbuf)

    q = q_ref[...]
    lane = lax.broadcasted_iota(jnp.int32, q.shape, 1)
    heads = [lane // ATT_HD == h for h in range(ATT_HEADS)]
    qm = jnp.concatenate([jnp.where(hm, q, 0.0) for hm in heads], axis=0).astype(BF16)
    cnt_c = jnp.concatenate([cc_ref[...]] * ATT_HEADS, axis=0)
    cnt_n = jnp.concatenate([cn_ref[...]] * ATT_HEADS, axis=0)
    s_c = jnp.where(cnt_c > 0, _dot(qm, ck_ref[...].astype(BF16)), NEG_INF)
    s_n = jnp.where(cnt_n > 0, _dot(qm, kn_t.astype(BF16)), NEG_INF)
    m = jnp.maximum(jnp.max(s_c, axis=-1, keepdims=True), jnp.max(s_n, axis=-1, keepdims=True))
    p_c = cnt_c * jnp.exp2(s_c - m)
    p_n = cnt_n * jnp.exp2(s_n - m)
    l = jnp.sum(p_c, axis=-1, keepdims=True) + jnp.sum(p_n, axis=-1, keepdims=True)
    o = _dot_nt(p_c.astype(BF16), cv_ref[...].astype(BF16)) + _dot_nt(p_n.astype(BF16), vn_t.astype(BF16))
    o = o / l
    acc = jnp.zeros_like(q)
    for h, hm in enumerate(heads):
        acc = acc + jnp.where(hm, o[h * ln:(h + 1) * ln, :], 0.0)
    o_ref[...] = acc.astype(o_ref.dtype)


def _pattern_counts(ln, wbuf):
    cnt = np.zeros((ln, wbuf + ln), np.float32)
    for dil in DILATIONS:
        for l in range(ln):
            for j in range(SPAN + 1):
                row = wbuf + l - dil * j
                if row >= 0:
                    cnt[l, row] += 1.0
    return cnt


def _attention_sample(aq, akf, avf, cache_k, cache_v, *, nb, ln):
    wbuf = cache_k.shape[2]
    w = ATT_WIDTH
    cnt = _pattern_counts(ln, wbuf)
    cnt_new = np.zeros((ln, LANES), np.float32)
    cnt_new[:, :ln] = cnt[:, wbuf:]
    new = pl.BlockSpec((ln, w), lambda bi: (bi, 0))
    cache = pl.BlockSpec((None, w, wbuf), lambda bi: (bi, 0, 0))
    return pl.pallas_call(
        functools.partial(_att_sample_kernel, ln=ln, wbuf=wbuf),
        grid=(nb,),
        in_specs=[new, new, new, cache, cache, _const_spec((ln, wbuf)), _const_spec((ln, LANES))],
        out_specs=[new, cache, cache],
        out_shape=[jax.ShapeDtypeStruct((nb * ln, w), F32),
                   jax.ShapeDtypeStruct((nb, w, wbuf), F32),
                   jax.ShapeDtypeStruct((nb, w, wbuf), F32)],
        compiler_params=_params(("arbitrary",)),
        name="dilated_attn_sample",
    )(aq, akf, avf, cache_k, cache_v, jnp.asarray(cnt[:, :wbuf]), jnp.asarray(cnt_new))


def _step(x_prompt, x_sample, c_prompt, c_sample, state_ret, cache_win_k, cache_win_v, w_ada, b_ada,
          g_pre_mix, g_post_mix, g_pre_ffn, g_post_ffn, w_in, ret_gain, w_o, w_up, w_down,
          *, tm, tc, tq):
    assert w_in.shape[0] == 1, "single-layer step"
    bp, sp, d = x_prompt.shape
    nb, ln, _ = x_sample.shape
    wbuf = cache_win_k.shape[2]
    n_s = nb * ln

    w_in_bf = w_in[0].astype(BF16)
    wo_bf, wu_bf, wd_bf = w_o[0].astype(BF16), w_up[0].astype(BF16), w_down[0].astype(BF16)

    rows = bp + nb
    pad = (-rows) % 8
    c_all = jnp.concatenate([c_prompt, c_sample, jnp.zeros((pad, d), F32)], axis=0)
    mod = _modulation(c_all, w_ada[0], b_ada[0])
    mod_p = mod[:bp].reshape(bp, 1, 6 * d)
    mod_s = jnp.repeat(mod[bp:rows], ln, axis=0).reshape(1, n_s, 6 * d)

    tabs_p = _rotation_constants(np.arange(tm), np.arange(sp // tm) * tm)
    tabs_s = _rotation_constants(np.tile(np.arange(ln), nb), [PAST_LEN])

    keep = min(MAX_WINDOW, sp)
    (rq, rk, rv, rg, aq, ak, av, akf, avf, aq4, ak4, av4, aq16, ak16, av16) = _project(
        x_prompt, mod_p, g_pre_mix[0], w_in_bf, tabs_p, tm=tm, keep=keep, act_dtype=BF16, regroup=True)
    ret_h, s_fin = _retention_prompt(rq, rk, rv, rg, ret_gain[0], tc=tc, chunk=RET_BLOCK)
    o16, l16 = _dilated_prompt(aq16, ak16, av16, tq=tq)
    o4, l4 = _dilated_prompt(aq4, ak4, av4, tq=tq)
    o1, l1 = _dilated_prompt(aq[:, None], ak[:, None], av[:, None], tq=tq)
    y_prompt = _tail(x_prompt, ret_h, (o1, l1, o4, l4, o16, l16), mod_p, g_post_mix[0], g_pre_ffn[0],
                     g_post_ffn[0], wo_bf, wu_bf, wd_bf, tm=tm)

    xs = x_sample.reshape(1, n_s, d)
    srq, srk, srv, srg, saq, _, _, sakf, savf = _project(
        xs, mod_s, g_pre_mix[0], w_in_bf, tabs_s, tm=n_s, keep=n_s, act_dtype=F32, regroup=False)
    flat = lambda t: t.reshape(n_s, GROUP_W)
    sret_h, s_new = _retention_sample(flat(srq), flat(srk), flat(srv), flat(srg), ret_gain[0],
                                      state_ret[0], nb=nb, ln=ln)
    to_minor = lambda t: jnp.transpose(t[0], (0, 2, 3, 1)).reshape(nb, ATT_WIDTH, wbuf)
    from_minor = lambda t: jnp.transpose(t.reshape(nb, ATT_HEADS, ATT_HD, wbuf), (0, 3, 1, 2))[None]
    satt_h, k_out, v_out = _attention_sample(
        flat(saq), flat(sakf), flat(savf), to_minor(cache_win_k), to_minor(cache_win_v), nb=nb, ln=ln)
    y_sample = _tail(xs, sret_h.reshape(1, n_s, RET_WIDTH), satt_h.reshape(1, n_s, ATT_WIDTH), mod_s,
                     g_post_mix[0], g_pre_ffn[0], g_post_ffn[0], wo_bf, wu_bf, wd_bf, tm=n_s)

    cache_shape = (1, -1, keep, ATT_HEADS, ATT_HD)
    return (y_prompt,
            y_sample.reshape(nb, ln, d),
            s_fin[None],
            akf.reshape(cache_shape),
            avf.reshape(cache_shape),
            s_new[None],
            from_minor(k_out),
            from_minor(v_out))


def kernel(x_prompt, x_sample, c_prompt, c_sample, state_ret, cache_win_k, cache_win_v, w_ada, b_ada,
           g_pre_mix, g_post_mix, g_pre_ffn, g_post_ffn, w_in, ret_gain, w_o, w_up, w_down):
    return _step(x_prompt, x_sample, c_prompt, c_sample, state_ret, cache_win_k, cache_win_v, w_ada, b_ada,
                 g_pre_mix, g_post_mix, g_pre_ffn, g_post_ffn, w_in, ret_gain, w_o, w_up, w_down,
                 tm=512, tc=1024, tq=512)
```

```python
import functools
import math

import numpy as np
import jax
import jax.numpy as jnp
from jax import lax
from jax.experimental import pallas as pl
from jax.experimental.pallas import tpu as pltpu

F32 = jnp.float32
BF16 = jnp.bfloat16

D_MODEL = 1024
RET_HEADS = 4
RET_WIDTH = 512
RET_DK = 128
RET_THETA = 10000.0
RET_BLOCK = 256
ATT_HD = 64
ATT_HEADS = 8
ATT_WIDTH = 512
DILATIONS = (1, 4, 16)
MAX_WINDOW = 2048
SPAN = 128
DIL_BLOCK = 128
ROPE_THETA = 10000.0
D_FF = 4096
NORM_EPS = 1e-6
NEG_INF = -1e30
PAST_LEN = 16384
N_GROUPS = 7
GROUP_W = 512
LANES = 128
VMEM_LIMIT = 56 * 1024 * 1024

LOG2_E = math.log2(math.e)
LOG_G = tuple(math.log1p(-(2.0 ** (-5.0 - h))) for h in range(RET_HEADS))


def _dot(a, b):
    return jnp.dot(a, b, preferred_element_type=F32)


def _dot_nt(a, b):
    return lax.dot_general(a, b, (((1,), (1,)), ((), ())), preferred_element_type=F32)


def _dot_tn(a, b):
    return lax.dot_general(a, b, (((0,), (0,)), ((), ())), preferred_element_type=F32)


def _rms(x):
    return x * lax.rsqrt(jnp.mean(x * x, axis=-1, keepdims=True) + NORM_EPS)


def _silu(x):
    return x / (1.0 + jnp.exp(-x))


def _tied_zero(values, rows):
    bits = None
    for v in values:
        assert v.shape[0] % rows == 0 and v.shape[1] % LANES == 0
        for r0 in range(0, v.shape[0], rows):
            for c0 in range(0, v.shape[1], LANES):
                b = pltpu.bitcast(v[r0:r0 + rows, c0:c0 + LANES], jnp.uint32)
                bits = b if bits is None else bits | b
    sixteen = jnp.full(bits.shape, 16, jnp.uint32)
    return pltpu.bitcast(lax.shift_right_logical(lax.shift_right_logical(bits, sixteen), sixteen), F32)


def _params(sem):
    return pltpu.CompilerParams(dimension_semantics=sem, vmem_limit_bytes=VMEM_LIMIT)


def _const_spec(shape):
    nd = len(shape)
    return pl.BlockSpec(shape, lambda *_: (0,) * nd, pipeline_mode=pl.Buffered(1))


def _mod_kernel(c_ref, w_ref, b_ref, o_ref):
    a = _silu(c_ref[...]).astype(BF16)
    o_ref[...] = _dot(a, w_ref[...].astype(BF16)) + b_ref[...]


def _modulation(c, w_ada, b_ada):
    rows, d = c.shape
    n = w_ada.shape[1]
    tn = 1536
    return pl.pallas_call(
        _mod_kernel,
        grid=(n // tn,),
        in_specs=[pl.BlockSpec((rows, d), lambda j: (0, 0)),
                  pl.BlockSpec((d, tn), lambda j: (0, j)),
                  pl.BlockSpec((1, tn), lambda j: (0, j))],
        out_specs=pl.BlockSpec((rows, tn), lambda j: (0, j)),
        out_shape=jax.ShapeDtypeStruct((rows, n), F32),
        compiler_params=_params(("arbitrary",)),
        name="adaln_mod",
    )(c, w_ada, b_ada.reshape(1, n))


def _rotate_pairs(x, cos, sin):
    lane = lax.broadcasted_iota(jnp.int32, x.shape, 1)
    partner = jnp.where(lane % 2 == 0, pltpu.roll(x, LANES - 1, 1), pltpu.roll(x, 1, 1))
    return x * cos + partner * sin


def _rotate_half(x, cos, sin):
    lane = lax.broadcasted_iota(jnp.int32, x.shape, 1)
    half = ATT_HD // 2
    partner = jnp.where(lane % ATT_HD < half, pltpu.roll(x, LANES - half, 1), pltpu.roll(x, half, 1))
    return x * cos + partner * sin


def _store_regrouped(val, c, tok_ref, r4_ref, r16_ref, zs, s4):
    sl = slice(c * LANES, (c + 1) * LANES)
    tok_ref[:, sl] = val.astype(tok_ref.dtype)
    tm = val.shape[0]
    zs[c] = val
    for q in range(4):
        g4 = zs[c, pl.ds(q, tm // 4, stride=4), :]
        r4_ref[q, :, sl] = g4.astype(r4_ref.dtype)
        s4[c, q] = g4
    for q in range(4):
        for a in range(4):
            r16_ref[4 * a + q, :, sl] = s4[c, q, pl.ds(a, tm // 16, stride=4), :].astype(r16_ref.dtype)


def _proj_kernel(x_ref, mod_ref, g_ref, w_ref, rows_ref, tile_ref, *refs, regroup):
    if regroup:
        (rq_ref, rk_ref, rv_ref, rg_ref, aq_ref, ak_ref, av_ref, akf_ref, avf_ref,
         aq4_ref, ak4_ref, av4_ref, aq16_ref, ak16_ref, av16_ref, zs, s4) = refs
    else:
        rq_ref, rk_ref, rv_ref, rg_ref, aq_ref, ak_ref, av_ref, akf_ref, avf_ref = refs
    d = D_MODEL
    x = x_ref[...]
    h = (_rms(x) * g_ref[...]) * (1.0 + mod_ref[:, d:2 * d]) + mod_ref[:, 0:d]
    h = h.astype(BF16)
    t = tile_ref[...]
    cr = t[0:1] * rows_ref[0] - t[1:2] * rows_ref[1]
    sr = (t[1:2] * rows_ref[0] + t[0:1] * rows_ref[1]) * t[4:5]
    ca = t[2:3] * rows_ref[2] - t[3:4] * rows_ref[3]
    sa = (t[3:4] * rows_ref[2] + t[2:3] * rows_ref[3]) * t[5:6]
    nch = GROUP_W // LANES

    def group(gi):
        return _dot(h, w_ref[:, gi * GROUP_W:(gi + 1) * GROUP_W])

    z = group(0)
    for c in range(nch):
        sl = slice(c * LANES, (c + 1) * LANES)
        rq_ref[:, sl] = _rotate_pairs(z[:, sl], cr, sr).astype(rq_ref.dtype)
    z = group(1)
    for c in range(nch):
        sl = slice(c * LANES, (c + 1) * LANES)
        rk_ref[:, sl] = (_rotate_pairs(z[:, sl], cr, sr) * (RET_DK ** -0.5)).astype(rk_ref.dtype)
    rv_ref[...] = group(2).astype(rv_ref.dtype)
    rg_ref[...] = group(3)
    def emit(val, c, tok_ref, r4_ref, r16_ref):
        if regroup:
            _store_regrouped(val, c, tok_ref, r4_ref, r16_ref, zs, s4)
        else:
            tok_ref[:, c * LANES:(c + 1) * LANES] = val.astype(tok_ref.dtype)

    r4 = (aq4_ref, ak4_ref, av4_ref) if regroup else (None,) * 3
    r16 = (aq16_ref, ak16_ref, av16_ref) if regroup else (None,) * 3
    z = group(4)
    for c in range(nch):
        sl = slice(c * LANES, (c + 1) * LANES)
        emit(_rotate_half(z[:, sl], ca, sa) * (ATT_HD ** -0.5 * LOG2_E), c, aq_ref, r4[0], r16[0])
    z = group(5)
    for c in range(nch):
        sl = slice(c * LANES, (c + 1) * LANES)
        r = _rotate_half(z[:, sl], ca, sa)
        akf_ref[:, sl] = r
        emit(r, c, ak_ref, r4[1], r16[1])
    z = group(6)
    avf_ref[...] = z
    for c in range(nch):
        emit(z[:, c * LANES:(c + 1) * LANES], c, av_ref, r4[2], r16[2])


def _project(x, mod, g_pre, w_in_bf, tabs, *, tm, keep, act_dtype, regroup):
    b, s, d = x.shape
    nt = s // tm
    rows_tab, tile_tab = tabs
    assert rows_tab.shape == (4, tm, LANES) and tile_tab.shape == (nt, 8, LANES)
    mod_rows = mod.shape[1]
    per_row = mod_rows != 1
    first_keep = (s - keep) // tm
    tok = pl.BlockSpec((None, tm, GROUP_W), lambda bi, i: (bi, i, 0))
    keep_spec = pl.BlockSpec((None, tm, GROUP_W), lambda bi, i: (bi, jnp.maximum(i - first_keep, 0), 0))
    tile_spec = pl.BlockSpec((None, 8, LANES), lambda bi, i: (i, 0, 0))
    mod_spec = pl.BlockSpec((None, tm if per_row else 1, 6 * d),
                            (lambda bi, i: (bi, i, 0)) if per_row else (lambda bi, i: (bi, 0, 0)))
    act = jax.ShapeDtypeStruct((b, s, GROUP_W), act_dtype)
    full = jax.ShapeDtypeStruct((b, s, GROUP_W), F32)
    kept = jax.ShapeDtypeStruct((b, keep, GROUP_W), F32)
    out_specs = [tok, tok, tok, tok, tok, tok, tok, keep_spec, keep_spec]
    out_shape = [act, act, act, full, act, act, act, kept, kept]
    scratch = []
    if regroup:
        for r in (4, 16):
            out_specs += [pl.BlockSpec((None, r, tm // r, GROUP_W), lambda bi, i: (bi, 0, i, 0))] * 3
            out_shape += [jax.ShapeDtypeStruct((b, r, s // r, GROUP_W), act_dtype)] * 3
        nch = GROUP_W // LANES
        scratch = [pltpu.VMEM((nch, tm, LANES), F32), pltpu.VMEM((nch, 4, tm // 4, LANES), F32)]
    return pl.pallas_call(
        functools.partial(_proj_kernel, regroup=regroup),
        grid=(b, nt),
        in_specs=[pl.BlockSpec((None, tm, d), lambda bi, i: (bi, i, 0)),
                  mod_spec,
                  _const_spec((1, d)),
                  _const_spec((d, N_GROUPS * GROUP_W)),
                  _const_spec((4, tm, LANES)), tile_spec],
        out_specs=out_specs,
        out_shape=out_shape,
        scratch_shapes=scratch,
        compiler_params=_params(("arbitrary", "arbitrary")),
        name="in_proj",
    )(x, mod, g_pre.reshape(1, d), w_in_bf, rows_tab, tile_tab)


def _rotation_constants(row_pos, tile_pos):
    lane = np.arange(LANES)
    inv_r = (1.0 / RET_THETA ** np.linspace(0.0, 1.0, RET_DK // 2))[lane // 2]
    inv_a = (1.0 / ROPE_THETA ** (np.arange(0, ATT_HD, 2) / ATT_HD))[lane % (ATT_HD // 2)]
    rp = np.asarray(row_pos, np.float64)[:, None]
    tp = np.asarray(tile_pos, np.float64)[:, None]
    rows = np.stack([np.cos(rp * inv_r), np.sin(rp * inv_r), np.cos(rp * inv_a), np.sin(rp * inv_a)])
    sign_r = np.where(lane % 2 == 0, -1.0, 1.0)
    sign_a = np.where(lane % ATT_HD < ATT_HD // 2, -1.0, 1.0)
    ones = np.ones_like(tp * inv_r)
    tiles = np.stack([np.cos(tp * inv_r), np.sin(tp * inv_r), np.cos(tp * inv_a), np.sin(tp * inv_a),
                      ones * sign_r, ones * sign_a, 0 * ones, 0 * ones], axis=1)
    return jnp.asarray(rows, F32), jnp.asarray(tiles, F32)


def _ret_readout(o, g, gain):
    return (_rms(o) * gain) * _silu(g)


def _ret_prompt_kernel(q_ref, k_ref, v_ref, g_ref, gain_ref, o_ref, sfin_ref, s_scr, *, nchunk, c_len):
    i = pl.program_id(1)

    @pl.when(i == 0)
    def _():
        s_scr[...] = jnp.zeros_like(s_scr)

    row = lax.broadcasted_iota(jnp.int32, (c_len, c_len), 0)
    col = lax.broadcasted_iota(jnp.int32, (c_len, c_len), 1)
    diff = (row - col).astype(F32)
    ridx = lax.broadcasted_iota(jnp.int32, (c_len, 1), 0).astype(F32)
    for h in range(RET_HEADS):
        lg = LOG_G[h]
        intra = jnp.where(diff >= 0, jnp.exp(lg * jnp.maximum(diff, 0.0)), 0.0)
        q_dec = jnp.exp(lg * (ridx + 1.0))
        k_dec = jnp.exp(lg * (c_len - 1.0 - ridx))
        c_dec = math.exp(lg * c_len)
        hs = slice(h * RET_DK, (h + 1) * RET_DK)
        state = s_scr[h]
        for c in range(nchunk):
            rs = slice(c * c_len, (c + 1) * c_len)
            q = q_ref[rs, hs]
            k = k_ref[rs, hs]
            v = v_ref[rs, hs]
            a = _dot_nt(q, k) * intra
            lhs = jnp.concatenate([a.astype(BF16), (q.astype(F32) * q_dec).astype(BF16)], axis=1)
            o = _dot(lhs, jnp.concatenate([v, state.astype(BF16)], axis=0))
            kd = (k.astype(F32) * k_dec).astype(BF16)
            state = state * c_dec + _dot_tn(kd, v)
            o_ref[rs, hs] = _ret_readout(o, g_ref[rs, hs], gain_ref[:, hs]).astype(o_ref.dtype)
        s_scr[h] = state

    @pl.when(i == pl.num_programs(1) - 1)
    def _():
        sfin_ref[...] = s_scr[...]


def _retention_prompt(rq, rk, rv, rg, ret_gain, *, tc, chunk):
    b, s, w = rq.shape
    nblk = s // tc
    tok = pl.BlockSpec((None, tc, w), lambda bi, i: (bi, i, 0))
    return pl.pallas_call(
        functools.partial(_ret_prompt_kernel, nchunk=tc // chunk, c_len=chunk),
        grid=(b, nblk),
        in_specs=[tok, tok, tok, tok, _const_spec((1, w))],
        out_specs=[tok, pl.BlockSpec((None, RET_HEADS, RET_DK, RET_DK), lambda bi, i: (bi, 0, 0, 0))],
        out_shape=[jax.ShapeDtypeStruct((b, s, w), BF16),
                   jax.ShapeDtypeStruct((b, RET_HEADS, RET_DK, RET_DK), F32)],
        scratch_shapes=[pltpu.VMEM((RET_HEADS, RET_DK, RET_DK), F32)],
        compiler_params=_params(("arbitrary", "arbitrary")),
        name="retention_prompt",
    )(rq, rk, rv, rg, ret_gain.reshape(1, w))


def _att_host_kernel(q_ref, k_ref, v_ref, kp_ref, vp_ref, oh_ref, bias_ref, gq_ref, gkn_ref, gvn_ref, gck_ref,
                     gcv_ref, gcc_ref, gcn_ref, *rest, nsub, ln, wbuf):
    pv_ref, ml_ref, go_ref, gko_ref, gvo_ref, kcat, vcat = rest[-7:]
    _sample_heads(gq_ref, gkn_ref, gvn_ref, gck_ref, gcv_ref, gcc_ref, gcn_ref, go_ref, gko_ref, gvo_ref,
                  ln=ln, wbuf=wbuf)
    _att_block(q_ref, k_ref, v_ref, kp_ref, vp_ref, oh_ref, bias_ref, pv_ref, ml_ref, kcat, vcat, nsub=nsub)


def _att_block(q_ref, k_ref, v_ref, kp_ref, vp_ref, oh_ref, bias_ref, pv_ref, ml_ref, kcat, vcat, *, nsub):
    i = pl.program_id(2)
    blk = DIL_BLOCK
    kcat[0:blk, :] = kp_ref[...]
    kcat[blk:, :] = k_ref[...]
    vcat[0:blk, :] = vp_ref[...]
    vcat[blk:, :] = v_ref[...]
    lane = lax.broadcasted_iota(jnp.int32, (blk, LANES), 1)
    low = lane < ATT_HD
    quarter = lane // (ATT_HD // 2)
    onehot = oh_ref[...]
    bias_rest = bias_ref[1]
    bias_first = jnp.where(i == 0, bias_ref[0], bias_rest)
    ones = jnp.ones((2 * blk, LANES), BF16)
    for j in range(nsub):
        bias = bias_first if j == 0 else bias_rest
        rows = slice(j * blk, (j + 1) * blk)
        krows = slice(j * blk, (j + 2) * blk)
        for hp in range(ATT_WIDTH // LANES):
            cols = slice(hp * LANES, (hp + 1) * LANES)
            q = q_ref[rows, cols]
            zero = jnp.zeros_like(q)
            q2 = jnp.concatenate([jnp.where(low, q, zero), jnp.where(low, zero, q)], axis=0)
            s = _dot_nt(jnp.concatenate([q2, onehot], axis=1), jnp.concatenate([kcat[krows, cols], bias], axis=1))
            m = jnp.max(s, axis=-1, keepdims=True)
            p = jnp.exp2(s - m).astype(BF16)
            pv = _dot(p, jnp.concatenate([vcat[krows, cols], ones], axis=1))
            pv_ref[rows, cols] = jnp.where(low, pv[0:blk, 0:LANES], pv[blk:, 0:LANES]).astype(pv_ref.dtype)
            ml_ref[rows, cols] = jnp.where(
                quarter == 0, m[0:blk], jnp.where(quarter == 1, pv[0:blk, LANES:],
                                                  jnp.where(quarter == 2, m[blk:], pv[blk:, LANES:])))


def _band_tables():
    jj = np.arange(2 * DIL_BLOCK)[:, None]
    qi = np.arange(DIL_BLOCK)[None, :]
    bias = np.stack([np.where((jj >= np.maximum(qi, fk)) & (jj <= qi + SPAN), 0.0, NEG_INF)
                     for fk in (DIL_BLOCK, 0)])
    onehot = np.tile(np.eye(DIL_BLOCK), (2, 1))
    return jnp.asarray(onehot, BF16), jnp.asarray(bias, BF16)


def _hosted_capacity(steps):
    return steps // 2


def _dilated_attention(aq, ak, av, sample, carry, *, tq, seq0, nseq):
    b, dil, m_len, w = aq.shape
    nsub = tq // DIL_BLOCK
    nblk = m_len // tq
    own = pl.BlockSpec((None, None, tq, w), lambda bi, r, i: (bi, r, i, 0))
    prv = pl.BlockSpec((None, None, DIL_BLOCK, w), lambda bi, r, i: (bi, r, jnp.maximum(i * nsub - 1, 0), 0))
    onehot, bias = _band_tables()
    in_specs = [own, own, own, prv, prv, _const_spec(onehot.shape), _const_spec(bias.shape)]
    args = [aq, ak, av, ak, av, onehot, bias]
    out_shape = [jax.ShapeDtypeStruct((b, dil, m_len, w), BF16), jax.ShapeDtypeStruct((b, dil, m_len, w), F32)]
    scratch = [pltpu.VMEM((tq + DIL_BLOCK, w), BF16), pltpu.VMEM((tq + DIL_BLOCK, w), BF16)]
    common = dict(grid=(b, dil, nblk), scratch_shapes=scratch, name=f"dilated_attn_d{dil}",
                  compiler_params=_params(("arbitrary", "arbitrary", "arbitrary")))
    if nseq == 0:
        return pl.pallas_call(functools.partial(_att_block, nsub=nsub), in_specs=in_specs,
                              out_specs=[own, own], out_shape=out_shape, **common)(*args)

    gq, gkn, gvn, gck, gcv, cnt_c, cnt_n = sample
    nb, _, wbuf = gck.shape
    ln = gq.shape[0] // nb
    hw = w // 2
    assert nseq <= _hosted_capacity(b * dil * nblk)

    def unit(bi, r, i):
        t = jnp.minimum((bi * dil + r) * nblk + i, 2 * nseq - 1)
        return seq0 + t // 2, t % 2

    rows = pl.BlockSpec((ln, hw), lambda bi, r, i: unit(bi, r, i))
    window = pl.BlockSpec((None, hw, wbuf), lambda bi, r, i: (*unit(bi, r, i), 0))
    in_specs += [rows, rows, rows, window, window, _const_spec(cnt_c.shape), _const_spec(cnt_n.shape)]
    args += [gq, gkn, gvn, gck, gcv, cnt_c, cnt_n]
    aliases = {}
    if carry is not None:
        aliases = {len(args) + n: 2 + n for n in range(3)}
        in_specs += [pl.BlockSpec(memory_space=pl.ANY)] * 3
        args += list(carry)
    out_shape += [jax.ShapeDtypeStruct((nb * ln, w), F32), jax.ShapeDtypeStruct((nb, w, wbuf), F32),
                  jax.ShapeDtypeStruct((nb, w, wbuf), F32)]
    return pl.pallas_call(
        functools.partial(_att_host_kernel, nsub=nsub, ln=ln, wbuf=wbuf),
        in_specs=in_specs,
        out_specs=[own, own, rows, window, window],
        out_shape=out_shape,
        input_output_aliases=aliases,
        **common,
    )(*args)


def _merge_patterns(pv1_ref, ml1_ref, pv4_ref, ml4_ref, pv16_ref, ml16_ref, tok, s4, att):
    tm = att.shape[0]
    half = ATT_HD // 2
    lane = lax.broadcasted_iota(jnp.int32, (tm, LANES), 1)
    first = lane % ATT_HD < half

    def unpack(ml):
        return (jnp.where(first, ml, pltpu.roll(ml, half, 1)),
                jnp.where(first, pltpu.roll(ml, LANES - half, 1), ml))

    for c in range(ATT_WIDTH // LANES):
        sl = slice(c * LANES, (c + 1) * LANES)
        for n, (r4, r16) in enumerate(((pv4_ref, pv16_ref), (ml4_ref, ml16_ref))):
            for q in range(4):
                tok[n, c, pl.ds(q, tm // 4, stride=4), :] = r4[q, :, sl].astype(F32)
                for a in range(4):
                    s4[q, pl.ds(a, tm // 16, stride=4), :] = r16[4 * a + q, :, sl].astype(F32)
            for q in range(4):
                tok[2 + n, c, pl.ds(q, tm // 4, stride=4), :] = s4[q]
        (m1, l1), (m4, l4), (m16, l16) = unpack(ml1_ref[:, sl]), unpack(tok[1, c]), unpack(tok[3, c])
        top = jnp.maximum(m1, jnp.maximum(m4, m16))
        e1, e4, e16 = jnp.exp2(m1 - top), jnp.exp2(m4 - top), jnp.exp2(m16 - top)
        num = e1 * pv1_ref[:, sl].astype(F32) + e4 * tok[0, c] + e16 * tok[2, c]
        den = e1 * l1 + e4 * l4 + e16 * l16
        merged = num / den
        att[:, sl] = merged.astype(att.dtype)
        yield merged


def _tail_body(x_ref, ret_ref, att_h, mod_ref, gpm_ref, gpf_ref, gqf_ref, wo_ref, wu_ref, wd_ref, y_ref,
               filler=iter(())):
    d = D_MODEL
    x = x_ref[...]
    mixed = _dot(ret_ref[...].astype(BF16), wo_ref[0:RET_WIDTH, :]) + _dot(att_h, wo_ref[RET_WIDTH:, :])
    x1 = x + mod_ref[:, 2 * d:3 * d] * (_rms(mixed) * gpm_ref[...])
    h = ((_rms(x1) * gpf_ref[...]) * (1.0 + mod_ref[:, 4 * d:5 * d]) + mod_ref[:, 3 * d:4 * d]).astype(BF16)
    fc = 1024
    f = None
    for c in range(D_FF // fc):
        u = jnp.maximum(_dot(h, wu_ref[:, c * fc:(c + 1) * fc]), 0.0)
        piece = next(filler, None)
        if piece is not None:
            u = jnp.concatenate([u[:, 0:LANES] + _tied_zero([piece], u.shape[0]), u[:, LANES:]], axis=1)
        part = _dot((u * u).astype(BF16), wd_ref[c * fc:(c + 1) * fc, :])
        f = part if f is None else f + part
    for _ in filler:
        pass
    y_ref[...] = x1 + mod_ref[:, 5 * d:6 * d] * (_rms(f) * gqf_ref[...])


def _tail_kernel(x_ref, ret_ref, att_ref, *rest):
    _tail_body(x_ref, ret_ref, att_ref[...].astype(BF16), *rest)


def _tail_merge_kernel(x_ref, ret_ref, pv1_ref, ml1_ref, pv4_ref, ml4_ref, pv16_ref, ml16_ref, mod_ref,
                       gpm_ref, gpf_ref, gqf_ref, wo_ref, wu_ref, wd_ref, y_ref, tok, s4, att):
    t = pl.program_id(0)
    merge = functools.partial(_merge_patterns, pv1_ref, ml1_ref, pv4_ref, ml4_ref, pv16_ref, ml16_ref,
                              tok, s4, att)

    @pl.when(t == 0)
    def _():
        for _ in merge():
            pass

    @pl.when(t > 0)
    def _():
        att_h = att[...]
        _tail_body(x_ref, ret_ref, att_h, mod_ref, gpm_ref, gpf_ref, gqf_ref, wo_ref, wu_ref, wd_ref, y_ref,
                   filler=merge())


def _tail(x, ret_h, att, mod, g_post_mix, g_pre_ffn, g_post_ffn, wo_bf, wu_bf, wd_bf, *, tm):
    b, s, d = x.shape
    nt = s // tm
    per_row = mod.shape[1] != 1
    weights = [_const_spec((1, d)), _const_spec((1, d)), _const_spec((1, d)),
               _const_spec((d, d)), _const_spec((d, D_FF)), _const_spec((D_FF, d))]
    weight_args = (g_post_mix.reshape(1, d), g_pre_ffn.reshape(1, d), g_post_ffn.reshape(1, d),
                   wo_bf, wu_bf, wd_bf)
    out_shape = jax.ShapeDtypeStruct((b, s, d), F32)
    if not isinstance(att, (tuple, list)):
        mod_spec = pl.BlockSpec((None, tm if per_row else 1, 6 * d),
                                (lambda bi, i: (bi, i, 0)) if per_row else (lambda bi, i: (bi, 0, 0)))
        tok = lambda w: pl.BlockSpec((None, tm, w), lambda bi, i: (bi, i, 0))
        return pl.pallas_call(
            _tail_kernel,
            grid=(b, nt),
            in_specs=[tok(d), tok(RET_WIDTH), tok(ATT_WIDTH), mod_spec, *weights],
            out_specs=tok(d),
            out_shape=out_shape,
            compiler_params=_params(("arbitrary", "arbitrary")),
            name="out_proj_mlp",
        )(x, ret_h, att, mod, *weight_args)

    assert not per_row
    last = b * nt - 1
    cur = lambda t: jnp.maximum(t - 1, 0)
    nxt = lambda t: jnp.minimum(t, last)
    tok = lambda w: pl.BlockSpec((None, tm, w), lambda t: (cur(t) // nt, cur(t) % nt, 0))
    mod_spec = pl.BlockSpec((None, 1, 6 * d), lambda t: (cur(t) // nt, 0, 0))
    split = lambda r: pl.BlockSpec((None, r, tm // r, ATT_WIDTH), lambda t: (nxt(t) // nt, 0, nxt(t) % nt, 0))
    one = pl.BlockSpec((None, None, tm, ATT_WIDTH), lambda t: (nxt(t) // nt, 0, nxt(t) % nt, 0))
    nch = ATT_WIDTH // LANES
    return pl.pallas_call(
        _tail_merge_kernel,
        grid=(b * nt + 1,),
        in_specs=[tok(d), tok(RET_WIDTH), one, one, split(4), split(4), split(16), split(16), mod_spec, *weights],
        out_specs=tok(d),
        out_shape=out_shape,
        scratch_shapes=[pltpu.VMEM((4, nch, tm, LANES), F32), pltpu.VMEM((4, tm // 4, LANES), F32),
                        pltpu.VMEM((tm, ATT_WIDTH), BF16)],
        compiler_params=_params(("arbitrary",)),
        name="out_proj_mlp",
    )(x, ret_h, *att, mod, *weight_args)


def _ret_sample_kernel(q_ref, k_ref, v_ref, g_ref, gain_ref, dm_ref, qd_ref, kd_ref, cd_ref, s_ref,
                       o_ref, sn_ref, o_scr, qd_scr, kt_scr, *, nb, ln):
    q = q_ref[...].astype(F32)
    k = k_ref[...].astype(F32)
    v = v_ref[...].astype(BF16)
    a = _dot_nt(q.astype(BF16), k.astype(BF16)) * dm_ref[...]
    o_scr[...] = _dot(a.astype(BF16), v)
    qd_scr[...] = q * qd_ref[...]
    kt_scr[...] = (k * kd_ref[...]).T
    c_dec = cd_ref[0:1, :]
    col = lax.broadcasted_iota(jnp.int32, kt_scr.shape, 1)

    def body(bi, carry):
        rows = pl.ds(pl.multiple_of(bi * ln, ln), ln)
        s0 = s_ref[bi]
        o_scr[rows, :] += _dot(qd_scr[rows, :].astype(BF16), s0.astype(BF16))
        mine = (col >= bi * ln) & (col < (bi + 1) * ln)
        kt = jnp.where(mine, kt_scr[...], 0.0).astype(BF16)
        sn_ref[bi] = s0 * c_dec + _dot(kt, v)
        return carry

    lax.fori_loop(0, nb, body, 0)
    o_ref[...] = _ret_readout(o_scr[...], g_ref[...], gain_ref[...]).astype(o_ref.dtype)


def _retention_sample(rq, rk, rv, rg, ret_gain, state, *, nb, ln):
    n = nb * ln
    idx = np.arange(n)
    same = (idx[:, None] // ln == idx[None, :] // ln) & (idx[:, None] >= idx[None, :])
    diff = np.maximum(idx[:, None] - idx[None, :], 0).astype(np.float64)
    step = (idx % ln).astype(np.float64)
    lg = np.asarray(LOG_G, np.float64)
    dm = np.where(same[None], np.exp(lg[:, None, None] * diff[None]), 0.0)
    qd = np.broadcast_to(np.exp(lg[:, None] * (step + 1.0))[:, :, None], (RET_HEADS, n, RET_DK))
    kd = np.broadcast_to(np.exp(lg[:, None] * (ln - 1.0 - step))[:, :, None], (RET_HEADS, n, RET_DK))
    cd = np.broadcast_to(np.exp(lg * ln)[:, None, None], (RET_HEADS, 8, RET_DK))
    tab = lambda t: jnp.asarray(np.ascontiguousarray(t), F32)
    col = pl.BlockSpec((n, RET_DK), lambda h: (0, h))
    per_head = lambda r, c: pl.BlockSpec((None, r, c), lambda h: (h, 0, 0))
    st = pl.BlockSpec((nb, None, RET_DK, RET_DK), lambda h: (0, h, 0, 0))
    return pl.pallas_call(
        functools.partial(_ret_sample_kernel, nb=nb, ln=ln),
        grid=(RET_HEADS,),
        in_specs=[col, col, col, col, pl.BlockSpec((1, RET_DK), lambda h: (0, h)),
                  per_head(n, n), per_head(n, RET_DK), per_head(n, RET_DK), per_head(8, RET_DK), st],
        out_specs=[col, st],
        out_shape=[jax.ShapeDtypeStruct((n, RET_WIDTH), BF16),
                   jax.ShapeDtypeStruct((nb, RET_HEADS, RET_DK, RET_DK), F32)],
        scratch_shapes=[pltpu.VMEM((n, RET_DK), F32), pltpu.VMEM((n, RET_DK), F32),
                        pltpu.VMEM((RET_DK, n), F32)],
        compiler_params=_params(("arbitrary",)),
        name="retention_sample",
    )(rq, rk, rv, rg, ret_gain.reshape(1, RET_WIDTH), tab(dm), tab(qd), tab(kd), tab(cd), state)


def _shift_window(old_ref, new_t, out_ref, *, ln, wbuf):
    lane = lax.broadcasted_iota(jnp.int32, new_t.shape, 1)
    ncol = wbuf // LANES
    rolled = pltpu.roll(old_ref[:, 0:LANES], LANES - ln, 1)
    for c in range(ncol):
        nxt = pltpu.roll(old_ref[:, (c + 1) * LANES:(c + 2) * LANES] if c + 1 < ncol else new_t, LANES - ln, 1)
        out_ref[:, c * LANES:(c + 1) * LANES] = jnp.where(lane < LANES - ln, rolled, nxt)
        rolled = nxt


def _new_rows_minor(new_ref, ln):
    pad = jnp.zeros((LANES - ln, new_ref.shape[1]), F32)
    return jnp.concatenate([new_ref[...], pad], axis=0).T


def _sample_heads(q_ref, kn_ref, vn_ref, ck_ref, cv_ref, cc_ref, cn_ref, o_ref, ko_ref, vo_ref, *, ln, wbuf):
    kn_t = _new_rows_minor(kn_ref, ln)
    vn_t = _new_rows_minor(vn_ref, ln)
    _shift_window(ck_ref, kn_t, ko_ref, ln=ln, wbuf=wbuf)
    _shift_window(cv_ref, vn_t, vo_ref, ln=ln, wbuf=wbuf)
    q = q_ref[...]
    nh = q.shape[1] // ATT_HD
    lane = lax.broadcasted_iota(jnp.int32, q.shape, 1)
    heads = [lane // ATT_HD == h for h in range(nh)]
    qm = jnp.concatenate([jnp.where(hm, q, 0.0) for hm in heads], axis=0).astype(BF16)
    cnt_c = jnp.concatenate([cc_ref[...]] * nh, axis=0)
    cnt_n = jnp.concatenate([cn_ref[...]] * nh, axis=0)
    s_c = jnp.where(cnt_c > 0, _dot(qm, ck_ref[...].astype(BF16)), NEG_INF)
    s_n = jnp.where(cnt_n > 0, _dot(qm, kn_t.astype(BF16)), NEG_INF)
    m = jnp.maximum(jnp.max(s_c, axis=-1, keepdims=True), jnp.max(s_n, axis=-1, keepdims=True))
    p_c = cnt_c * jnp.exp2(s_c - m)
    p_n = cnt_n * jnp.exp2(s_n - m)
    l = jnp.sum(p_c, axis=-1, keepdims=True) + jnp.sum(p_n, axis=-1, keepdims=True)
    o = _dot_nt(p_c.astype(BF16), cv_ref[...].astype(BF16)) + _dot_nt(p_n.astype(BF16), vn_t.astype(BF16))
    o = o / l
    acc = jnp.zeros_like(q)
    for h, hm in enumerate(heads):
        acc = acc + jnp.where(hm, o[h * ln:(h + 1) * ln, :], 0.0)
    o_ref[...] = acc.astype(o_ref.dtype)


def _pattern_counts(ln, wbuf):
    cnt = np.zeros((ln, wbuf + ln), np.float32)
    for dil in DILATIONS:
        for l in range(ln):
            for j in range(SPAN + 1):
                row = wbuf + l - dil * j
                if row >= 0:
                    cnt[l, row] += 1.0
    return cnt


def _count_tables(ln, wbuf):
    cnt = _pattern_counts(ln, wbuf)
    cnt_new = np.zeros((ln, LANES), np.float32)
    cnt_new[:, :ln] = cnt[:, wbuf:]
    return jnp.asarray(cnt[:, :wbuf]), jnp.asarray(cnt_new)


def _step(x_prompt, x_sample, c_prompt, c_sample, state_ret, cache_win_k, cache_win_v, w_ada, b_ada,
          g_pre_mix, g_post_mix, g_pre_ffn, g_post_ffn, w_in, ret_gain, w_o, w_up, w_down,
          *, tm, tc, tq):
    assert w_in.shape[0] == 1, "single-layer step"
    bp, sp, d = x_prompt.shape
    nb, ln, _ = x_sample.shape
    wbuf = cache_win_k.shape[2]
    n_s = nb * ln

    w_in_bf = w_in[0].astype(BF16)
    wo_bf, wu_bf, wd_bf = w_o[0].astype(BF16), w_up[0].astype(BF16), w_down[0].astype(BF16)

    rows = bp + nb
    pad = (-rows) % 8
    c_all = jnp.concatenate([c_prompt, c_sample, jnp.zeros((pad, d), F32)], axis=0)
    mod = _modulation(c_all, w_ada[0], b_ada[0])
    mod_p = mod[:bp].reshape(bp, 1, 6 * d)
    mod_s = jnp.repeat(mod[bp:rows], ln, axis=0).reshape(1, n_s, 6 * d)

    tabs_p = _rotation_constants(np.arange(tm), np.arange(sp // tm) * tm)
    tabs_s = _rotation_constants(np.tile(np.arange(ln), nb), [PAST_LEN])

    keep = min(MAX_WINDOW, sp)
    (rq, rk, rv, rg, aq, ak, av, akf, avf, aq4, ak4, av4, aq16, ak16, av16) = _project(
        x_prompt, mod_p, g_pre_mix[0], w_in_bf, tabs_p, tm=tm, keep=keep, act_dtype=BF16, regroup=True)
    xs = x_sample.reshape(1, n_s, d)
    srq, srk, srv, srg, saq, _, _, sakf, savf = _project(
        xs, mod_s, g_pre_mix[0], w_in_bf, tabs_s, tm=n_s, keep=n_s, act_dtype=F32, regroup=False)
    flat = lambda t: t.reshape(n_s, GROUP_W)

    to_minor = lambda t: jnp.transpose(t[0], (0, 2, 3, 1)).reshape(nb, ATT_WIDTH, wbuf)
    from_minor = lambda t: jnp.transpose(t.reshape(nb, ATT_HEADS, ATT_HD, wbuf), (0, 3, 1, 2))[None]
    sample = (flat(saq), flat(sakf), flat(savf), to_minor(cache_win_k), to_minor(cache_win_v),
              *_count_tables(ln, wbuf))
    patterns = ((aq16, ak16, av16), (aq4, ak4, av4), (aq[:, None], ak[:, None], av[:, None]))
    capacity = _hosted_capacity(bp * sp // tq)
    assert nb <= len(patterns) * capacity
    carry, seq0, merged_in = None, 0, []
    for qkv in patterns:
        nseq = min(capacity, nb - seq0)
        out = _dilated_attention(*qkv, sample, carry, tq=tq, seq0=seq0, nseq=nseq)
        merged_in = list(out[:2]) + merged_in
        carry = out[2:] or carry
        seq0 += nseq
    satt_h, k_out, v_out = carry

    ret_h, s_fin = _retention_prompt(rq, rk, rv, rg, ret_gain[0], tc=tc, chunk=RET_BLOCK)
    y_prompt = _tail(x_prompt, ret_h, tuple(merged_in), mod_p, g_post_mix[0], g_pre_ffn[0],
                     g_post_ffn[0], wo_bf, wu_bf, wd_bf, tm=tm)

    sret_h, s_new = _retention_sample(flat(srq), flat(srk), flat(srv), flat(srg), ret_gain[0],
                                      state_ret[0], nb=nb, ln=ln)
    y_sample = _tail(xs, sret_h.reshape(1, n_s, RET_WIDTH), satt_h.reshape(1, n_s, ATT_WIDTH), mod_s,
                     g_post_mix[0], g_pre_ffn[0], g_post_ffn[0], wo_bf, wu_bf, wd_bf, tm=n_s)

    cache_shape = (1, -1, keep, ATT_HEADS, ATT_HD)
    return (y_prompt,
            y_sample.reshape(nb, ln, d),
            s_fin[None],
            akf.reshape(cache_shape),
            avf.reshape(cache_shape),
            s_new[None],
            from_minor(k_out),
            from_minor(v_out))


def kernel(x_prompt, x_sample, c_prompt, c_sample, state_ret, cache_win_k, cache_win_v, w_ada, b_ada,
           g_pre_mix, g_post_mix, g_pre_ffn, g_post_ffn, w_in, ret_gain, w_o, w_up, w_down):
    return _step(x_prompt, x_sample, c_prompt, c_sample, state_ret, cache_win_k, cache_win_v, w_ada, b_ada,
                 g_pre_mix, g_post_mix, g_pre_ffn, g_post_ffn, w_in, ret_gain, w_o, w_up, w_down,
                 tm=512, tc=1024, tq=512)
```

```python
import functools
import math

import numpy as np
import jax
import jax.numpy as jnp
from jax import lax
from jax.experimental import pallas as pl
from jax.experimental.pallas import tpu as pltpu

F32 = jnp.float32
BF16 = jnp.bfloat16

D_MODEL = 1024
RET_HEADS = 4
RET_WIDTH = 512
RET_DK = 128
RET_THETA = 10000.0
RET_BLOCK = 256
ATT_HD = 64
ATT_HEADS = 8
ATT_WIDTH = 512
DILATIONS = (1, 4, 16)
MAX_WINDOW = 2048
SPAN = 128
DIL_BLOCK = 128
ROPE_THETA = 10000.0
D_FF = 4096
NORM_EPS = 1e-6
NEG_INF = -1e30
PAST_LEN = 16384
N_GROUPS = 7
GROUP_W = 512
LANES = 128
VMEM_LIMIT = 56 * 1024 * 1024

LOG2_E = math.log2(math.e)
LOG_G = tuple(math.log1p(-(2.0 ** (-5.0 - h))) for h in range(RET_HEADS))


def _dot(a, b):
    return jnp.dot(a, b, preferred_element_type=F32)


def _dot_nt(a, b):
    return lax.dot_general(a, b, (((1,), (1,)), ((), ())), preferred_element_type=F32)


def _dot_tn(a, b):
    return lax.dot_general(a, b, (((0,), (0,)), ((), ())), preferred_element_type=F32)


def _rms(x):
    return x * lax.rsqrt(jnp.mean(x * x, axis=-1, keepdims=True) + NORM_EPS)


def _silu(x):
    return x / (1.0 + jnp.exp(-x))


def _tied_zero(values, rows):
    bits = None
    for v in values:
        assert v.shape[0] % rows == 0 and v.shape[1] % LANES == 0
        for r0 in range(0, v.shape[0], rows):
            for c0 in range(0, v.shape[1], LANES):
                b = pltpu.bitcast(v[r0:r0 + rows, c0:c0 + LANES], jnp.uint32)
                bits = b if bits is None else bits | b
    sixteen = jnp.full(bits.shape, 16, jnp.uint32)
    return pltpu.bitcast(lax.shift_right_logical(lax.shift_right_logical(bits, sixteen), sixteen), F32)


def _params(sem):
    return pltpu.CompilerParams(dimension_semantics=sem, vmem_limit_bytes=VMEM_LIMIT)


def _const_spec(shape):
    nd = len(shape)
    return pl.BlockSpec(shape, lambda *_: (0,) * nd, pipeline_mode=pl.Buffered(1))


def _mod_kernel(c_ref, w_ref, b_ref, o_ref):
    a = _silu(c_ref[...]).astype(BF16)
    o_ref[...] = _dot(a, w_ref[...].astype(BF16)) + b_ref[...]


def _modulation(c, w_ada, b_ada):
    rows, d = c.shape
    n = w_ada.shape[1]
    tn = 1536
    return pl.pallas_call(
        _mod_kernel,
        grid=(n // tn,),
        in_specs=[pl.BlockSpec((rows, d), lambda j: (0, 0)),
                  pl.BlockSpec((d, tn), lambda j: (0, j)),
                  pl.BlockSpec((1, tn), lambda j: (0, j))],
        out_specs=pl.BlockSpec((rows, tn), lambda j: (0, j)),
        out_shape=jax.ShapeDtypeStruct((rows, n), F32),
        compiler_params=_params(("arbitrary",)),
        name="adaln_mod",
    )(c, w_ada, b_ada.reshape(1, n))


def _rotate_pairs(x, cos, sin):
    lane = lax.broadcasted_iota(jnp.int32, x.shape, 1)
    partner = jnp.where(lane % 2 == 0, pltpu.roll(x, LANES - 1, 1), pltpu.roll(x, 1, 1))
    return x * cos + partner * sin


def _rotate_half(x, cos, sin):
    lane = lax.broadcasted_iota(jnp.int32, x.shape, 1)
    half = ATT_HD // 2
    partner = jnp.where(lane % ATT_HD < half, pltpu.roll(x, LANES - half, 1), pltpu.roll(x, half, 1))
    return x * cos + partner * sin


def _store_regrouped(val, c, tok_ref, r4_ref, r16_ref, zs, s4):
    sl = slice(c * LANES, (c + 1) * LANES)
    tok_ref[:, sl] = val.astype(tok_ref.dtype)
    tm = val.shape[0]
    zs[c] = val
    for q in range(4):
        g4 = zs[c, pl.ds(q, tm // 4, stride=4), :]
        r4_ref[q, :, sl] = g4.astype(r4_ref.dtype)
        s4[c, q] = g4
    for q in range(4):
        for a in range(4):
            r16_ref[4 * a + q, :, sl] = s4[c, q, pl.ds(a, tm // 16, stride=4), :].astype(r16_ref.dtype)


def _proj_kernel(x_ref, mod_ref, g_ref, w_ref, rows_ref, tile_ref, *refs, regroup):
    if regroup:
        (rq_ref, rk_ref, rv_ref, rg_ref, aq_ref, ak_ref, av_ref, akf_ref, avf_ref,
         aq4_ref, ak4_ref, av4_ref, aq16_ref, ak16_ref, av16_ref, zs, s4) = refs
    else:
        rq_ref, rk_ref, rv_ref, rg_ref, aq_ref, ak_ref, av_ref, akf_ref, avf_ref = refs
    d = D_MODEL
    x = x_ref[...]
    h = (_rms(x) * g_ref[...]) * (1.0 + mod_ref[:, d:2 * d]) + mod_ref[:, 0:d]
    h = h.astype(BF16)
    t = tile_ref[...]
    cr = t[0:1] * rows_ref[0] - t[1:2] * rows_ref[1]
    sr = (t[1:2] * rows_ref[0] + t[0:1] * rows_ref[1]) * t[4:5]
    ca = t[2:3] * rows_ref[2] - t[3:4] * rows_ref[3]
    sa = (t[3:4] * rows_ref[2] + t[2:3] * rows_ref[3]) * t[5:6]
    nch = GROUP_W // LANES

    def group(gi):
        return _dot(h, w_ref[:, gi * GROUP_W:(gi + 1) * GROUP_W])

    z = group(0)
    for c in range(nch):
        sl = slice(c * LANES, (c + 1) * LANES)
        rq_ref[:, sl] = _rotate_pairs(z[:, sl], cr, sr).astype(rq_ref.dtype)
    z = group(1)
    for c in range(nch):
        sl = slice(c * LANES, (c + 1) * LANES)
        rk_ref[:, sl] = (_rotate_pairs(z[:, sl], cr, sr) * (RET_DK ** -0.5)).astype(rk_ref.dtype)
    rv_ref[...] = group(2).astype(rv_ref.dtype)
    rg_ref[...] = group(3)
    def emit(val, c, tok_ref, r4_ref, r16_ref):
        if regroup:
            _store_regrouped(val, c, tok_ref, r4_ref, r16_ref, zs, s4)
        else:
            tok_ref[:, c * LANES:(c + 1) * LANES] = val.astype(tok_ref.dtype)

    r4 = (aq4_ref, ak4_ref, av4_ref) if regroup else (None,) * 3
    r16 = (aq16_ref, ak16_ref, av16_ref) if regroup else (None,) * 3
    z = group(4)
    for c in range(nch):
        sl = slice(c * LANES, (c + 1) * LANES)
        emit(_rotate_half(z[:, sl], ca, sa) * (ATT_HD ** -0.5 * LOG2_E), c, aq_ref, r4[0], r16[0])
    z = group(5)
    for c in range(nch):
        sl = slice(c * LANES, (c + 1) * LANES)
        r = _rotate_half(z[:, sl], ca, sa)
        akf_ref[:, sl] = r
        emit(r, c, ak_ref, r4[1], r16[1])
    z = group(6)
    avf_ref[...] = z
    for c in range(nch):
        emit(z[:, c * LANES:(c + 1) * LANES], c, av_ref, r4[2], r16[2])


def _project(x, mod, g_pre, w_in_bf, tabs, *, tm, keep, act_dtype, regroup):
    b, s, d = x.shape
    nt = s // tm
    rows_tab, tile_tab = tabs
    assert rows_tab.shape == (4, tm, LANES) and tile_tab.shape == (nt, 8, LANES)
    mod_rows = mod.shape[1]
    per_row = mod_rows != 1
    first_keep = (s - keep) // tm
    tok = pl.BlockSpec((None, tm, GROUP_W), lambda bi, i: (bi, i, 0))
    keep_spec = pl.BlockSpec((None, tm, GROUP_W), lambda bi, i: (bi, jnp.maximum(i - first_keep, 0), 0))
    tile_spec = pl.BlockSpec((None, 8, LANES), lambda bi, i: (i, 0, 0))
    mod_spec = pl.BlockSpec((None, tm if per_row else 1, 6 * d),
                            (lambda bi, i: (bi, i, 0)) if per_row else (lambda bi, i: (bi, 0, 0)))
    act = jax.ShapeDtypeStruct((b, s, GROUP_W), act_dtype)
    full = jax.ShapeDtypeStruct((b, s, GROUP_W), F32)
    kept = jax.ShapeDtypeStruct((b, keep, GROUP_W), F32)
    out_specs = [tok, tok, tok, tok, tok, tok, tok, keep_spec, keep_spec]
    out_shape = [act, act, act, full, act, act, act, kept, kept]
    scratch = []
    if regroup:
        for r in (4, 16):
            out_specs += [pl.BlockSpec((None, r, tm // r, GROUP_W), lambda bi, i: (bi, 0, i, 0))] * 3
            out_shape += [jax.ShapeDtypeStruct((b, r, s // r, GROUP_W), act_dtype)] * 3
        nch = GROUP_W // LANES
        scratch = [pltpu.VMEM((nch, tm, LANES), F32), pltpu.VMEM((nch, 4, tm // 4, LANES), F32)]
    return pl.pallas_call(
        functools.partial(_proj_kernel, regroup=regroup),
        grid=(b, nt),
        in_specs=[pl.BlockSpec((None, tm, d), lambda bi, i: (bi, i, 0)),
                  mod_spec,
                  _const_spec((1, d)),
                  _const_spec((d, N_GROUPS * GROUP_W)),
                  _const_spec((4, tm, LANES)), tile_spec],
        out_specs=out_specs,
        out_shape=out_shape,
        scratch_shapes=scratch,
        compiler_params=_params(("arbitrary", "arbitrary")),
        name="in_proj",
    )(x, mod, g_pre.reshape(1, d), w_in_bf, rows_tab, tile_tab)


def _rotation_constants(row_pos, tile_pos):
    lane = np.arange(LANES)
    inv_r = (1.0 / RET_THETA ** np.linspace(0.0, 1.0, RET_DK // 2))[lane // 2]
    inv_a = (1.0 / ROPE_THETA ** (np.arange(0, ATT_HD, 2) / ATT_HD))[lane % (ATT_HD // 2)]
    rp = np.asarray(row_pos, np.float64)[:, None]
    tp = np.asarray(tile_pos, np.float64)[:, None]
    rows = np.stack([np.cos(rp * inv_r), np.sin(rp * inv_r), np.cos(rp * inv_a), np.sin(rp * inv_a)])
    sign_r = np.where(lane % 2 == 0, -1.0, 1.0)
    sign_a = np.where(lane % ATT_HD < ATT_HD // 2, -1.0, 1.0)
    ones = np.ones_like(tp * inv_r)
    tiles = np.stack([np.cos(tp * inv_r), np.sin(tp * inv_r), np.cos(tp * inv_a), np.sin(tp * inv_a),
                      ones * sign_r, ones * sign_a, 0 * ones, 0 * ones], axis=1)
    return jnp.asarray(rows, F32), jnp.asarray(tiles, F32)


def _ret_readout(o, g, gain):
    return (_rms(o) * gain) * _silu(g)


def _ret_prompt_kernel(q_ref, k_ref, v_ref, g_ref, gain_ref, *rest, nchunk, c_len, ncast):
    cast_in, (o_ref, sfin_ref), cast_out, s_scr = rest[:ncast], rest[ncast:ncast + 2], rest[ncast + 2:-1], rest[-1]
    for src, dst in zip(cast_in, cast_out):
        dst[...] = src[...].astype(dst.dtype)
    i = pl.program_id(1)

    @pl.when(i == 0)
    def _():
        s_scr[...] = jnp.zeros_like(s_scr)

    row = lax.broadcasted_iota(jnp.int32, (c_len, c_len), 0)
    col = lax.broadcasted_iota(jnp.int32, (c_len, c_len), 1)
    diff = (row - col).astype(F32)
    ridx = lax.broadcasted_iota(jnp.int32, (c_len, 1), 0).astype(F32)
    for h in range(RET_HEADS):
        lg = LOG_G[h]
        intra = jnp.where(diff >= 0, jnp.exp(lg * jnp.maximum(diff, 0.0)), 0.0)
        q_dec = jnp.exp(lg * (ridx + 1.0))
        k_dec = jnp.exp(lg * (c_len - 1.0 - ridx))
        c_dec = math.exp(lg * c_len)
        hs = slice(h * RET_DK, (h + 1) * RET_DK)
        state = s_scr[h]
        for c in range(nchunk):
            rs = slice(c * c_len, (c + 1) * c_len)
            q = q_ref[rs, hs]
            k = k_ref[rs, hs]
            v = v_ref[rs, hs]
            a = _dot_nt(q, k) * intra
            lhs = jnp.concatenate([a.astype(BF16), (q.astype(F32) * q_dec).astype(BF16)], axis=1)
            o = _dot(lhs, jnp.concatenate([v, state.astype(BF16)], axis=0))
            kd = (k.astype(F32) * k_dec).astype(BF16)
            state = state * c_dec + _dot_tn(kd, v)
            o_ref[rs, hs] = _ret_readout(o, g_ref[rs, hs], gain_ref[:, hs]).astype(o_ref.dtype)
        s_scr[h] = state

    @pl.when(i == pl.num_programs(1) - 1)
    def _():
        sfin_ref[...] = s_scr[...]


def _retention_prompt(rq, rk, rv, rg, ret_gain, weights, *, tc, chunk):
    b, s, w = rq.shape
    nblk = s // tc
    steps = b * nblk
    tok = pl.BlockSpec((None, tc, w), lambda bi, i: (bi, i, 0))
    slab = lambda m: pl.BlockSpec((m.shape[0] // steps, m.shape[1]), lambda bi, i: (bi * nblk + i, 0))
    assert all(m.shape[0] % (16 * steps) == 0 for m in weights)
    out = pl.pallas_call(
        functools.partial(_ret_prompt_kernel, nchunk=tc // chunk, c_len=chunk, ncast=len(weights)),
        grid=(b, nblk),
        in_specs=[tok, tok, tok, tok, _const_spec((1, w))] + [slab(m) for m in weights],
        out_specs=[tok, pl.BlockSpec((None, RET_HEADS, RET_DK, RET_DK), lambda bi, i: (bi, 0, 0, 0))]
        + [slab(m) for m in weights],
        out_shape=[jax.ShapeDtypeStruct((b, s, w), BF16),
                   jax.ShapeDtypeStruct((b, RET_HEADS, RET_DK, RET_DK), F32)]
        + [jax.ShapeDtypeStruct(m.shape, BF16) for m in weights],
        scratch_shapes=[pltpu.VMEM((RET_HEADS, RET_DK, RET_DK), F32)],
        compiler_params=_params(("arbitrary", "arbitrary")),
        name="retention_prompt",
    )(rq, rk, rv, rg, ret_gain.reshape(1, w), *weights)
    return out[0], out[1], out[2:]


def _att_host_kernel(q_ref, k_ref, v_ref, kp_ref, vp_ref, oh_ref, bias_ref, gq_ref, gkn_ref, gvn_ref, gck_ref,
                     gcv_ref, gcc_ref, gcn_ref, *rest, nsub, ln, wbuf, nunit):
    pv_ref, ml_ref, go_ref, gko_ref, gvo_ref, kcat, vcat = rest[-7:]
    steps = pl.num_programs(0) * pl.num_programs(1) * pl.num_programs(2)
    t = (pl.program_id(0) * pl.num_programs(1) + pl.program_id(1)) * pl.num_programs(2) + pl.program_id(2)
    fresh = jnp.logical_or(t == 0, _hosted_unit(t, 0, nunit, steps) != _hosted_unit(t - 1, 0, nunit, steps))

    @pl.when(fresh)
    def _():
        _sample_heads(gq_ref, gkn_ref, gvn_ref, gck_ref, gcv_ref, gcc_ref, gcn_ref, go_ref, gko_ref, gvo_ref,
                      ln=ln, wbuf=wbuf)

    _att_block(q_ref, k_ref, v_ref, kp_ref, vp_ref, oh_ref, bias_ref, pv_ref, ml_ref, kcat, vcat, nsub=nsub)


def _att_block(q_ref, k_ref, v_ref, kp_ref, vp_ref, oh_ref, bias_ref, pv_ref, ml_ref, kcat, vcat, *, nsub):
    i = pl.program_id(2)
    blk = DIL_BLOCK
    kcat[0:blk, :] = kp_ref[...]
    kcat[blk:, :] = k_ref[...]
    vcat[0:blk, :] = vp_ref[...]
    vcat[blk:, :] = v_ref[...]
    lane = lax.broadcasted_iota(jnp.int32, (blk, LANES), 1)
    low = lane < ATT_HD
    quarter = lane // (ATT_HD // 2)
    onehot = oh_ref[...]
    bias_rest = bias_ref[1]
    bias_first = jnp.where(i == 0, bias_ref[0], bias_rest)
    ones = jnp.ones((2 * blk, LANES), BF16)
    for j in range(nsub):
        bias = bias_first if j == 0 else bias_rest
        rows = slice(j * blk, (j + 1) * blk)
        krows = slice(j * blk, (j + 2) * blk)
        for hp in range(ATT_WIDTH // LANES):
            cols = slice(hp * LANES, (hp + 1) * LANES)
            q = q_ref[rows, cols]
            zero = jnp.zeros_like(q)
            q2 = jnp.concatenate([jnp.where(low, q, zero), jnp.where(low, zero, q)], axis=0)
            s = _dot_nt(jnp.concatenate([q2, onehot], axis=1), jnp.concatenate([kcat[krows, cols], bias], axis=1))
            m = jnp.max(s, axis=-1, keepdims=True)
            p = jnp.exp2(s - m).astype(BF16)
            pv = _dot(p, jnp.concatenate([vcat[krows, cols], ones], axis=1))
            pv_ref[rows, cols] = jnp.where(low, pv[0:blk, 0:LANES], pv[blk:, 0:LANES]).astype(pv_ref.dtype)
            ml_ref[rows, cols] = jnp.where(
                quarter == 0, m[0:blk], jnp.where(quarter == 1, pv[0:blk, LANES:],
                                                  jnp.where(quarter == 2, m[blk:], pv[blk:, LANES:])))


def _band_tables():
    jj = np.arange(2 * DIL_BLOCK)[:, None]
    qi = np.arange(DIL_BLOCK)[None, :]
    bias = np.stack([np.where((jj >= np.maximum(qi, fk)) & (jj <= qi + SPAN), 0.0, NEG_INF)
                     for fk in (DIL_BLOCK, 0)])
    onehot = np.tile(np.eye(DIL_BLOCK), (2, 1))
    return jnp.asarray(onehot, BF16), jnp.asarray(bias, BF16)


def _hosted_unit(t, unit0, nunit, steps):
    return unit0 + (t * nunit) // steps


def _dilated_attention(aq, ak, av, sample, carry, *, tq, unit0, nunit):
    b, dil, m_len, w = aq.shape
    nsub = tq // DIL_BLOCK
    nblk = m_len // tq
    own = pl.BlockSpec((None, None, tq, w), lambda bi, r, i: (bi, r, i, 0))
    prv = pl.BlockSpec((None, None, DIL_BLOCK, w), lambda bi, r, i: (bi, r, jnp.maximum(i * nsub - 1, 0), 0))
    onehot, bias = _band_tables()
    in_specs = [own, own, own, prv, prv, _const_spec(onehot.shape), _const_spec(bias.shape)]
    args = [aq, ak, av, ak, av, onehot, bias]
    out_shape = [jax.ShapeDtypeStruct((b, dil, m_len, w), BF16), jax.ShapeDtypeStruct((b, dil, m_len, w), F32)]
    scratch = [pltpu.VMEM((tq + DIL_BLOCK, w), BF16), pltpu.VMEM((tq + DIL_BLOCK, w), BF16)]
    common = dict(grid=(b, dil, nblk), scratch_shapes=scratch, name=f"dilated_attn_d{dil}",
                  compiler_params=_params(("arbitrary", "arbitrary", "arbitrary")))
    if nunit == 0:
        return pl.pallas_call(functools.partial(_att_block, nsub=nsub), in_specs=in_specs,
                              out_specs=[own, own], out_shape=out_shape, **common)(*args)

    gq, gkn, gvn, gck, gcv, cnt_c, cnt_n = sample
    nb, _, wbuf = gck.shape
    ln = gq.shape[0] // nb
    hw = w // 2
    steps = b * dil * nblk
    assert nunit <= steps

    def unit(bi, r, i):
        u = _hosted_unit((bi * dil + r) * nblk + i, unit0, nunit, steps)
        return u // 2, u % 2

    rows = pl.BlockSpec((ln, hw), lambda bi, r, i: unit(bi, r, i))
    window = pl.BlockSpec((None, hw, wbuf), lambda bi, r, i: (*unit(bi, r, i), 0))
    in_specs += [rows, rows, rows, window, window, _const_spec(cnt_c.shape), _const_spec(cnt_n.shape)]
    args += [gq, gkn, gvn, gck, gcv, cnt_c, cnt_n]
    aliases = {}
    if carry is not None:
        aliases = {len(args) + n: 2 + n for n in range(3)}
        in_specs += [pl.BlockSpec(memory_space=pl.ANY)] * 3
        args += list(carry)
    out_shape += [jax.ShapeDtypeStruct((nb * ln, w), F32), jax.ShapeDtypeStruct((nb, w, wbuf), F32),
                  jax.ShapeDtypeStruct((nb, w, wbuf), F32)]
    return pl.pallas_call(
        functools.partial(_att_host_kernel, nsub=nsub, ln=ln, wbuf=wbuf, nunit=nunit),
        in_specs=in_specs,
        out_specs=[own, own, rows, window, window],
        out_shape=out_shape,
        input_output_aliases=aliases,
        **common,
    )(*args)


def _merge_patterns(pv1_ref, ml1_ref, pv4_ref, ml4_ref, pv16_ref, ml16_ref, tok, s4, att):
    tm = att.shape[0]
    half = ATT_HD // 2
    lane = lax.broadcasted_iota(jnp.int32, (tm, LANES), 1)
    first = lane % ATT_HD < half

    def unpack(ml):
        return (jnp.where(first, ml, pltpu.roll(ml, half, 1)),
                jnp.where(first, pltpu.roll(ml, LANES - half, 1), ml))

    for c in range(ATT_WIDTH // LANES):
        sl = slice(c * LANES, (c + 1) * LANES)
        for n, (r4, r16) in enumerate(((pv4_ref, pv16_ref), (ml4_ref, ml16_ref))):
            for q in range(4):
                tok[n, c, pl.ds(q, tm // 4, stride=4), :] = r4[q, :, sl].astype(F32)
                for a in range(4):
                    s4[q, pl.ds(a, tm // 16, stride=4), :] = r16[4 * a + q, :, sl].astype(F32)
            for q in range(4):
                tok[2 + n, c, pl.ds(q, tm // 4, stride=4), :] = s4[q]
        (m1, l1), (m4, l4), (m16, l16) = unpack(ml1_ref[:, sl]), unpack(tok[1, c]), unpack(tok[3, c])
        top = jnp.maximum(m1, jnp.maximum(m4, m16))
        e1, e4, e16 = jnp.exp2(m1 - top), jnp.exp2(m4 - top), jnp.exp2(m16 - top)
        num = e1 * pv1_ref[:, sl].astype(F32) + e4 * tok[0, c] + e16 * tok[2, c]
        den = e1 * l1 + e4 * l4 + e16 * l16
        merged = num / den
        att[:, sl] = merged.astype(att.dtype)
        yield merged


def _tail_body(x_ref, ret_ref, att_h, mod_ref, gpm_ref, gpf_ref, gqf_ref, wo_ref, wu_ref, wd_ref, y_ref,
               filler=iter(())):
    d = D_MODEL
    x = x_ref[...]
    mixed = _dot(ret_ref[...].astype(BF16), wo_ref[0:RET_WIDTH, :]) + _dot(att_h, wo_ref[RET_WIDTH:, :])
    x1 = x + mod_ref[:, 2 * d:3 * d] * (_rms(mixed) * gpm_ref[...])
    h = ((_rms(x1) * gpf_ref[...]) * (1.0 + mod_ref[:, 4 * d:5 * d]) + mod_ref[:, 3 * d:4 * d]).astype(BF16)
    fc = 1024
    f = None
    for c in range(D_FF // fc):
        u = jnp.maximum(_dot(h, wu_ref[:, c * fc:(c + 1) * fc]), 0.0)
        piece = next(filler, None)
        if piece is not None:
            u = jnp.concatenate([u[:, 0:LANES] + _tied_zero([piece], u.shape[0]), u[:, LANES:]], axis=1)
        part = _dot((u * u).astype(BF16), wd_ref[c * fc:(c + 1) * fc, :])
        f = part if f is None else f + part
    for _ in filler:
        pass
    y_ref[...] = x1 + mod_ref[:, 5 * d:6 * d] * (_rms(f) * gqf_ref[...])


def _tail_kernel(x_ref, ret_ref, att_ref, *rest):
    _tail_body(x_ref, ret_ref, att_ref[...].astype(BF16), *rest)


def _tail_merge_kernel(x_ref, ret_ref, pv1_ref, ml1_ref, pv4_ref, ml4_ref, pv16_ref, ml16_ref, mod_ref,
                       gpm_ref, gpf_ref, gqf_ref, wo_ref, wu_ref, wd_ref, y_ref, tok, s4, att):
    t = pl.program_id(0)
    merge = functools.partial(_merge_patterns, pv1_ref, ml1_ref, pv4_ref, ml4_ref, pv16_ref, ml16_ref,
                              tok, s4, att)

    @pl.when(t == 0)
    def _():
        for _ in merge():
            pass

    @pl.when(t > 0)
    def _():
        att_h = att[...]
        _tail_body(x_ref, ret_ref, att_h, mod_ref, gpm_ref, gpf_ref, gqf_ref, wo_ref, wu_ref, wd_ref, y_ref,
                   filler=merge())


def _tail(x, ret_h, att, mod, g_post_mix, g_pre_ffn, g_post_ffn, wo_bf, wu_bf, wd_bf, *, tm):
    b, s, d = x.shape
    nt = s // tm
    per_row = mod.shape[1] != 1
    weights = [_const_spec((1, d)), _const_spec((1, d)), _const_spec((1, d)),
               _const_spec((d, d)), _const_spec((d, D_FF)), _const_spec((D_FF, d))]
    weight_args = (g_post_mix.reshape(1, d), g_pre_ffn.reshape(1, d), g_post_ffn.reshape(1, d),
                   wo_bf, wu_bf, wd_bf)
    out_shape = jax.ShapeDtypeStruct((b, s, d), F32)
    if not isinstance(att, (tuple, list)):
        mod_spec = pl.BlockSpec((None, tm if per_row else 1, 6 * d),
                                (lambda bi, i: (bi, i, 0)) if per_row else (lambda bi, i: (bi, 0, 0)))
        tok = lambda w: pl.BlockSpec((None, tm, w), lambda bi, i: (bi, i, 0))
        return pl.pallas_call(
            _tail_kernel,
            grid=(b, nt),
            in_specs=[tok(d), tok(RET_WIDTH), tok(ATT_WIDTH), mod_spec, *weights],
            out_specs=tok(d),
            out_shape=out_shape,
            compiler_params=_params(("arbitrary", "arbitrary")),
            name="out_proj_mlp",
        )(x, ret_h, att, mod, *weight_args)

    assert not per_row
    last = b * nt - 1
    cur = lambda t: jnp.maximum(t - 1, 0)
    nxt = lambda t: jnp.minimum(t, last)
    tok = lambda w: pl.BlockSpec((None, tm, w), lambda t: (cur(t) // nt, cur(t) % nt, 0))
    mod_spec = pl.BlockSpec((None, 1, 6 * d), lambda t: (cur(t) // nt, 0, 0))
    split = lambda r: pl.BlockSpec((None, r, tm // r, ATT_WIDTH), lambda t: (nxt(t) // nt, 0, nxt(t) % nt, 0))
    one = pl.BlockSpec((None, None, tm, ATT_WIDTH), lambda t: (nxt(t) // nt, 0, nxt(t) % nt, 0))
    nch = ATT_WIDTH // LANES
    return pl.pallas_call(
        _tail_merge_kernel,
        grid=(b * nt + 1,),
        in_specs=[tok(d), tok(RET_WIDTH), one, one, split(4), split(4), split(16), split(16), mod_spec, *weights],
        out_specs=tok(d),
        out_shape=out_shape,
        scratch_shapes=[pltpu.VMEM((4, nch, tm, LANES), F32), pltpu.VMEM((4, tm // 4, LANES), F32),
                        pltpu.VMEM((tm, ATT_WIDTH), BF16)],
        compiler_params=_params(("arbitrary",)),
        name="out_proj_mlp",
    )(x, ret_h, *att, mod, *weight_args)


def _ret_sample_kernel(q_ref, k_ref, v_ref, g_ref, gain_ref, dm_ref, qd_ref, kd_ref, cd_ref, s_ref,
                       o_ref, sn_ref, o_scr, qd_scr, kt_scr, *, nb, ln):
    q = q_ref[...].astype(F32)
    k = k_ref[...].astype(F32)
    v = v_ref[...].astype(BF16)
    a = _dot_nt(q.astype(BF16), k.astype(BF16)) * dm_ref[...]
    o_scr[...] = _dot(a.astype(BF16), v)
    qd_scr[...] = q * qd_ref[...]
    kt_scr[...] = (k * kd_ref[...]).T
    c_dec = cd_ref[0:1, :]
    col = lax.broadcasted_iota(jnp.int32, kt_scr.shape, 1)

    def body(bi, carry):
        rows = pl.ds(pl.multiple_of(bi * ln, ln), ln)
        s0 = s_ref[bi]
        o_scr[rows, :] += _dot(qd_scr[rows, :].astype(BF16), s0.astype(BF16))
        mine = (col >= bi * ln) & (col < (bi + 1) * ln)
        kt = jnp.where(mine, kt_scr[...], 0.0).astype(BF16)
        sn_ref[bi] = s0 * c_dec + _dot(kt, v)
        return carry

    lax.fori_loop(0, nb, body, 0, unroll=math.gcd(nb, 8))
    o_ref[...] = _ret_readout(o_scr[...], g_ref[...], gain_ref[...]).astype(o_ref.dtype)


def _retention_sample(rq, rk, rv, rg, ret_gain, state, *, nb, ln):
    n = nb * ln
    idx = np.arange(n)
    same = (idx[:, None] // ln == idx[None, :] // ln) & (idx[:, None] >= idx[None, :])
    diff = np.maximum(idx[:, None] - idx[None, :], 0).astype(np.float64)
    step = (idx % ln).astype(np.float64)
    lg = np.asarray(LOG_G, np.float64)
    dm = np.where(same[None], np.exp(lg[:, None, None] * diff[None]), 0.0)
    qd = np.broadcast_to(np.exp(lg[:, None] * (step + 1.0))[:, :, None], (RET_HEADS, n, RET_DK))
    kd = np.broadcast_to(np.exp(lg[:, None] * (ln - 1.0 - step))[:, :, None], (RET_HEADS, n, RET_DK))
    cd = np.broadcast_to(np.exp(lg * ln)[:, None, None], (RET_HEADS, 8, RET_DK))
    tab = lambda t: jnp.asarray(np.ascontiguousarray(t), F32)
    col = pl.BlockSpec((n, RET_DK), lambda h: (0, h))
    per_head = lambda r, c: pl.BlockSpec((None, r, c), lambda h: (h, 0, 0))
    st = pl.BlockSpec((nb, None, RET_DK, RET_DK), lambda h: (0, h, 0, 0))
    return pl.pallas_call(
        functools.partial(_ret_sample_kernel, nb=nb, ln=ln),
        grid=(RET_HEADS,),
        in_specs=[col, col, col, col, pl.BlockSpec((1, RET_DK), lambda h: (0, h)),
                  per_head(n, n), per_head(n, RET_DK), per_head(n, RET_DK), per_head(8, RET_DK), st],
        out_specs=[col, st],
        out_shape=[jax.ShapeDtypeStruct((n, RET_WIDTH), BF16),
                   jax.ShapeDtypeStruct((nb, RET_HEADS, RET_DK, RET_DK), F32)],
        scratch_shapes=[pltpu.VMEM((n, RET_DK), F32), pltpu.VMEM((n, RET_DK), F32),
                        pltpu.VMEM((RET_DK, n), F32)],
        compiler_params=_params(("arbitrary",)),
        name="retention_sample",
    )(rq, rk, rv, rg, ret_gain.reshape(1, RET_WIDTH), tab(dm), tab(qd), tab(kd), tab(cd), state)


def _shift_window(old_ref, new_t, out_ref, *, ln, wbuf):
    lane = lax.broadcasted_iota(jnp.int32, new_t.shape, 1)
    ncol = wbuf // LANES
    rolled = pltpu.roll(old_ref[:, 0:LANES], LANES - ln, 1)
    for c in range(ncol):
        nxt = pltpu.roll(old_ref[:, (c + 1) * LANES:(c + 2) * LANES] if c + 1 < ncol else new_t, LANES - ln, 1)
        out_ref[:, c * LANES:(c + 1) * LANES] = jnp.where(lane < LANES - ln, rolled, nxt)
        rolled = nxt


def _new_rows_minor(new_ref, ln):
    pad = jnp.zeros((LANES - ln, new_ref.shape[1]), F32)
    return jnp.concatenate([new_ref[...], pad], axis=0).T


def _sample_heads(q_ref, kn_ref, vn_ref, ck_ref, cv_ref, cc_ref, cn_ref, o_ref, ko_ref, vo_ref, *, ln, wbuf):
    kn_t = _new_rows_minor(kn_ref, ln)
    vn_t = _new_rows_minor(vn_ref, ln)
    _shift_window(ck_ref, kn_t, ko_ref, ln=ln, wbuf=wbuf)
    _shift_window(cv_ref, vn_t, vo_ref, ln=ln, wbuf=wbuf)
    q = q_ref[...]
    nh = q.shape[1] // ATT_HD
    lane = lax.broadcasted_iota(jnp.int32, q.shape, 1)
    heads = [lane // ATT_HD == h for h in range(nh)]
    qm = jnp.concatenate([jnp.where(hm, q, 0.0) for hm in heads], axis=0).astype(BF16)
    cnt_c = jnp.concatenate([cc_ref[...]] * nh, axis=0)
    cnt_n = jnp.concatenate([cn_ref[...]] * nh, axis=0)
    s_c = jnp.where(cnt_c > 0, _dot(qm, ck_ref[...].astype(BF16)), NEG_INF)
    s_n = jnp.where(cnt_n > 0, _dot(qm, kn_t.astype(BF16)), NEG_INF)
    m = jnp.maximum(jnp.max(s_c, axis=-1, keepdims=True), jnp.max(s_n, axis=-1, keepdims=True))
    p_c = cnt_c * jnp.exp2(s_c - m)
    p_n = cnt_n * jnp.exp2(s_n - m)
    l = jnp.sum(p_c, axis=-1, keepdims=True) + jnp.sum(p_n, axis=-1, keepdims=True)
    o = _dot_nt(p_c.astype(BF16), cv_ref[...].astype(BF16)) + _dot_nt(p_n.astype(BF16), vn_t.astype(BF16))
    o = o / l
    acc = jnp.zeros_like(q)
    for h, hm in enumerate(heads):
        acc = acc + jnp.where(hm, o[h * ln:(h + 1) * ln, :], 0.0)
    o_ref[...] = acc.astype(o_ref.dtype)


def _pattern_counts(ln, wbuf):
    cnt = np.zeros((ln, wbuf + ln), np.float32)
    for dil in DILATIONS:
        for l in range(ln):
            for j in range(SPAN + 1):
                row = wbuf + l - dil * j
                if row >= 0:
                    cnt[l, row] += 1.0
    return cnt


def _count_tables(ln, wbuf):
    cnt = _pattern_counts(ln, wbuf)
    cnt_new = np.zeros((ln, LANES), np.float32)
    cnt_new[:, :ln] = cnt[:, wbuf:]
    return jnp.asarray(cnt[:, :wbuf]), jnp.asarray(cnt_new)


def _step(x_prompt, x_sample, c_prompt, c_sample, state_ret, cache_win_k, cache_win_v, w_ada, b_ada,
          g_pre_mix, g_post_mix, g_pre_ffn, g_post_ffn, w_in, ret_gain, w_o, w_up, w_down,
          *, tm, tc, tq):
    assert w_in.shape[0] == 1, "single-layer step"
    bp, sp, d = x_prompt.shape
    nb, ln, _ = x_sample.shape
    wbuf = cache_win_k.shape[2]
    n_s = nb * ln

    w_in_bf = w_in[0].astype(BF16)

    rows = bp + nb
    pad = (-rows) % 8
    c_all = jnp.concatenate([c_prompt, c_sample, jnp.zeros((pad, d), F32)], axis=0)
    mod = _modulation(c_all, w_ada[0], b_ada[0])
    mod_p = mod[:bp].reshape(bp, 1, 6 * d)
    mod_s = jnp.repeat(mod[bp:rows], ln, axis=0).reshape(1, n_s, 6 * d)

    tabs_p = _rotation_constants(np.arange(tm), np.arange(sp // tm) * tm)
    tabs_s = _rotation_constants(np.tile(np.arange(ln), nb), [PAST_LEN])

    keep = min(MAX_WINDOW, sp)
    (rq, rk, rv, rg, aq, ak, av, akf, avf, aq4, ak4, av4, aq16, ak16, av16) = _project(
        x_prompt, mod_p, g_pre_mix[0], w_in_bf, tabs_p, tm=tm, keep=keep, act_dtype=BF16, regroup=True)
    xs = x_sample.reshape(1, n_s, d)
    srq, srk, srv, srg, saq, _, _, sakf, savf = _project(
        xs, mod_s, g_pre_mix[0], w_in_bf, tabs_s, tm=n_s, keep=n_s, act_dtype=F32, regroup=False)
    flat = lambda t: t.reshape(n_s, GROUP_W)

    to_minor = lambda t: jnp.transpose(t[0], (0, 2, 3, 1)).reshape(nb, ATT_WIDTH, wbuf)
    from_minor = lambda t: jnp.transpose(t.reshape(nb, ATT_HEADS, ATT_HD, wbuf), (0, 3, 1, 2))[None]
    sample = (flat(saq), flat(sakf), flat(savf), to_minor(cache_win_k), to_minor(cache_win_v),
              *_count_tables(ln, wbuf))
    patterns = ((aq16, ak16, av16), (aq4, ak4, av4), (aq[:, None], ak[:, None], av[:, None]))
    units, steps = 2 * nb, bp * sp // tq
    assert units <= len(patterns) * steps
    carry, unit0, merged_in = None, 0, []
    for n, qkv in enumerate(patterns):
        nunit = min(steps, units // len(patterns) + (n < units % len(patterns)))
        nunit = units - unit0 if n + 1 == len(patterns) else nunit
        out = _dilated_attention(*qkv, sample, carry, tq=tq, unit0=unit0, nunit=nunit)
        merged_in = list(out[:2]) + merged_in
        carry = out[2:] or carry
        unit0 += nunit
    satt_h, k_out, v_out = carry

    ret_h, s_fin, (wo_bf, wu_bf, wd_bf) = _retention_prompt(
        rq, rk, rv, rg, ret_gain[0], (w_o[0], w_up[0], w_down[0]), tc=tc, chunk=RET_BLOCK)
    y_prompt = _tail(x_prompt, ret_h, tuple(merged_in), mod_p, g_post_mix[0], g_pre_ffn[0],
                     g_post_ffn[0], wo_bf, wu_bf, wd_bf, tm=tm)

    sret_h, s_new = _retention_sample(flat(srq), flat(srk), flat(srv), flat(srg), ret_gain[0],
                                      state_ret[0], nb=nb, ln=ln)
    y_sample = _tail(xs, sret_h.reshape(1, n_s, RET_WIDTH), satt_h.reshape(1, n_s, ATT_WIDTH), mod_s,
                     g_post_mix[0], g_pre_ffn[0], g_post_ffn[0], wo_bf, wu_bf, wd_bf, tm=n_s)

    cache_shape = (1, -1, keep, ATT_HEADS, ATT_HD)
    return (y_prompt,
            y_sample.reshape(nb, ln, d),
            s_fin[None],
            akf.reshape(cache_shape),
            avf.reshape(cache_shape),
            s_new[None],
            from_minor(k_out),
            from_minor(v_out))


def kernel(x_prompt, x_sample, c_prompt, c_sample, state_ret, cache_win_k, cache_win_v, w_ada, b_ada,
           g_pre_mix, g_post_mix, g_pre_ffn, g_post_ffn, w_in, ret_gain, w_o, w_up, w_down):
    return _step(x_prompt, x_sample, c_prompt, c_sample, state_ret, cache_win_k, cache_win_v, w_ada, b_ada,
                 g_pre_mix, g_post_mix, g_pre_ffn, g_post_ffn, w_in, ret_gain, w_o, w_up, w_down,
                 tm=512, tc=1024, tq=512)
```

```python
import functools
import math

import numpy as np
import jax
import jax.numpy as jnp
from jax import lax
from jax.experimental import pallas as pl
from jax.experimental.pallas import tpu as pltpu

F32 = jnp.float32
BF16 = jnp.bfloat16

D_MODEL = 1024
RET_HEADS = 4
RET_WIDTH = 512
RET_DK = 128
RET_THETA = 10000.0
RET_BLOCK = 256
ATT_HD = 64
ATT_HEADS = 8
ATT_WIDTH = 512
DILATIONS = (1, 4, 16)
MAX_WINDOW = 2048
SPAN = 128
DIL_BLOCK = 128
ROPE_THETA = 10000.0
D_FF = 4096
NORM_EPS = 1e-6
NEG_INF = -1e30
PAST_LEN = 16384
N_GROUPS = 7
GROUP_W = 512
LANES = 128
VMEM_LIMIT = 56 * 1024 * 1024

LOG2_E = math.log2(math.e)
LOG_G = tuple(math.log1p(-(2.0 ** (-5.0 - h))) for h in range(RET_HEADS))


def _dot(a, b):
    return jnp.dot(a, b, preferred_element_type=F32)


def _dot_nt(a, b):
    return lax.dot_general(a, b, (((1,), (1,)), ((), ())), preferred_element_type=F32)


def _dot_tn(a, b):
    return lax.dot_general(a, b, (((0,), (0,)), ((), ())), preferred_element_type=F32)


def _rms(x):
    return x * lax.rsqrt(jnp.mean(x * x, axis=-1, keepdims=True) + NORM_EPS)


def _silu(x):
    return x / (1.0 + jnp.exp(-x))


def _tied_zero(values, rows):
    bits = None
    for v in values:
        assert v.shape[0] % rows == 0 and v.shape[1] % LANES == 0
        for r0 in range(0, v.shape[0], rows):
            for c0 in range(0, v.shape[1], LANES):
                b = pltpu.bitcast(v[r0:r0 + rows, c0:c0 + LANES], jnp.uint32)
                bits = b if bits is None else bits | b
    sixteen = jnp.full(bits.shape, 16, jnp.uint32)
    return pltpu.bitcast(lax.shift_right_logical(lax.shift_right_logical(bits, sixteen), sixteen), F32)


def _params(sem):
    return pltpu.CompilerParams(dimension_semantics=sem, vmem_limit_bytes=VMEM_LIMIT)


def _const_spec(shape):
    nd = len(shape)
    return pl.BlockSpec(shape, lambda *_: (0,) * nd, pipeline_mode=pl.Buffered(1))


def _mod_kernel(c_ref, w_ref, b_ref, o_ref):
    a = _silu(c_ref[...]).astype(BF16)
    o_ref[...] = _dot(a, w_ref[...].astype(BF16)) + b_ref[...]


def _modulation(c, w_ada, b_ada):
    rows, d = c.shape
    n = w_ada.shape[1]
    tn = 1536
    return pl.pallas_call(
        _mod_kernel,
        grid=(n // tn,),
        in_specs=[pl.BlockSpec((rows, d), lambda j: (0, 0)),
                  pl.BlockSpec((d, tn), lambda j: (0, j)),
                  pl.BlockSpec((1, tn), lambda j: (0, j))],
        out_specs=pl.BlockSpec((rows, tn), lambda j: (0, j)),
        out_shape=jax.ShapeDtypeStruct((rows, n), F32),
        compiler_params=_params(("arbitrary",)),
        name="adaln_mod",
    )(c, w_ada, b_ada.reshape(1, n))


def _rotate_pairs(x, cos, sin):
    lane = lax.broadcasted_iota(jnp.int32, x.shape, 1)
    partner = jnp.where(lane % 2 == 0, pltpu.roll(x, LANES - 1, 1), pltpu.roll(x, 1, 1))
    return x * cos + partner * sin


def _rotate_half(x, cos, sin):
    lane = lax.broadcasted_iota(jnp.int32, x.shape, 1)
    half = ATT_HD // 2
    partner = jnp.where(lane % ATT_HD < half, pltpu.roll(x, LANES - half, 1), pltpu.roll(x, half, 1))
    return x * cos + partner * sin


def _store_regrouped(val, c, tok_ref, r4_ref, r16_ref, zs, s4):
    sl = slice(c * LANES, (c + 1) * LANES)
    tok_ref[:, sl] = val.astype(tok_ref.dtype)
    tm = val.shape[0]
    zs[c] = val
    for q in range(4):
        g4 = zs[c, pl.ds(q, tm // 4, stride=4), :]
        r4_ref[q, :, sl] = g4.astype(r4_ref.dtype)
        s4[c, q] = g4
    for q in range(4):
        for a in range(4):
            r16_ref[4 * a + q, :, sl] = s4[c, q, pl.ds(a, tm // 16, stride=4), :].astype(r16_ref.dtype)


def _proj_kernel(x_ref, mod_ref, g_ref, wf_ref, rows_ref, tile_ref, *refs, regroup):
    if regroup:
        (rq_ref, rk_ref, rv_ref, rg_ref, aq_ref, ak_ref, av_ref, akf_ref, avf_ref,
         aq4_ref, ak4_ref, av4_ref, aq16_ref, ak16_ref, av16_ref, w_ref, zs, s4) = refs
    else:
        rq_ref, rk_ref, rv_ref, rg_ref, aq_ref, ak_ref, av_ref, akf_ref, avf_ref, w_ref = refs

    @pl.when(jnp.logical_and(pl.program_id(0) == 0, pl.program_id(1) == 0))
    def _():
        for gi in range(N_GROUPS):
            sl = slice(gi * GROUP_W, (gi + 1) * GROUP_W)
            w_ref[:, sl] = wf_ref[:, sl].astype(w_ref.dtype)

    d = D_MODEL
    x = x_ref[...]
    h = (_rms(x) * g_ref[...]) * (1.0 + mod_ref[:, d:2 * d]) + mod_ref[:, 0:d]
    h = h.astype(BF16)
    t = tile_ref[...]
    cr = t[0:1] * rows_ref[0] - t[1:2] * rows_ref[1]
    sr = (t[1:2] * rows_ref[0] + t[0:1] * rows_ref[1]) * t[4:5]
    ca = t[2:3] * rows_ref[2] - t[3:4] * rows_ref[3]
    sa = (t[3:4] * rows_ref[2] + t[2:3] * rows_ref[3]) * t[5:6]
    nch = GROUP_W // LANES

    def group(gi):
        return _dot(h, w_ref[:, gi * GROUP_W:(gi + 1) * GROUP_W])

    z = group(0)
    for c in range(nch):
        sl = slice(c * LANES, (c + 1) * LANES)
        rq_ref[:, sl] = _rotate_pairs(z[:, sl], cr, sr).astype(rq_ref.dtype)
    z = group(1)
    for c in range(nch):
        sl = slice(c * LANES, (c + 1) * LANES)
        rk_ref[:, sl] = (_rotate_pairs(z[:, sl], cr, sr) * (RET_DK ** -0.5)).astype(rk_ref.dtype)
    rv_ref[...] = group(2).astype(rv_ref.dtype)
    rg_ref[...] = group(3)
    def emit(val, c, tok_ref, r4_ref, r16_ref):
        if regroup:
            _store_regrouped(val, c, tok_ref, r4_ref, r16_ref, zs, s4)
        else:
            tok_ref[:, c * LANES:(c + 1) * LANES] = val.astype(tok_ref.dtype)

    r4 = (aq4_ref, ak4_ref, av4_ref) if regroup else (None,) * 3
    r16 = (aq16_ref, ak16_ref, av16_ref) if regroup else (None,) * 3
    z = group(4)
    for c in range(nch):
        sl = slice(c * LANES, (c + 1) * LANES)
        emit(_rotate_half(z[:, sl], ca, sa) * (ATT_HD ** -0.5 * LOG2_E), c, aq_ref, r4[0], r16[0])
    z = group(5)
    for c in range(nch):
        sl = slice(c * LANES, (c + 1) * LANES)
        r = _rotate_half(z[:, sl], ca, sa)
        akf_ref[:, sl] = r
        emit(r, c, ak_ref, r4[1], r16[1])
    z = group(6)
    avf_ref[...] = z
    for c in range(nch):
        emit(z[:, c * LANES:(c + 1) * LANES], c, av_ref, r4[2], r16[2])


def _project(x, mod, g_pre, w_in_f32, tabs, *, tm, keep, act_dtype, regroup):
    b, s, d = x.shape
    nt = s // tm
    rows_tab, tile_tab = tabs
    assert rows_tab.shape == (4, tm, LANES) and tile_tab.shape == (nt, 8, LANES)
    mod_rows = mod.shape[1]
    per_row = mod_rows != 1
    first_keep = (s - keep) // tm
    tok = pl.BlockSpec((None, tm, GROUP_W), lambda bi, i: (bi, i, 0))
    keep_spec = pl.BlockSpec((None, tm, GROUP_W), lambda bi, i: (bi, jnp.maximum(i - first_keep, 0), 0))
    tile_spec = pl.BlockSpec((None, 8, LANES), lambda bi, i: (i, 0, 0))
    mod_spec = pl.BlockSpec((None, tm if per_row else 1, 6 * d),
                            (lambda bi, i: (bi, i, 0)) if per_row else (lambda bi, i: (bi, 0, 0)))
    act = jax.ShapeDtypeStruct((b, s, GROUP_W), act_dtype)
    full = jax.ShapeDtypeStruct((b, s, GROUP_W), F32)
    kept = jax.ShapeDtypeStruct((b, keep, GROUP_W), F32)
    out_specs = [tok, tok, tok, tok, tok, tok, tok, keep_spec, keep_spec]
    out_shape = [act, act, act, full, act, act, act, kept, kept]
    scratch = [pltpu.VMEM((d, N_GROUPS * GROUP_W), BF16)]
    if regroup:
        for r in (4, 16):
            out_specs += [pl.BlockSpec((None, r, tm // r, GROUP_W), lambda bi, i: (bi, 0, i, 0))] * 3
            out_shape += [jax.ShapeDtypeStruct((b, r, s // r, GROUP_W), act_dtype)] * 3
        nch = GROUP_W // LANES
        scratch += [pltpu.VMEM((nch, tm, LANES), F32), pltpu.VMEM((nch, 4, tm // 4, LANES), F32)]
    return pl.pallas_call(
        functools.partial(_proj_kernel, regroup=regroup),
        grid=(b, nt),
        in_specs=[pl.BlockSpec((None, tm, d), lambda bi, i: (bi, i, 0)),
                  mod_spec,
                  _const_spec((1, d)),
                  _const_spec((d, N_GROUPS * GROUP_W)),
                  _const_spec((4, tm, LANES)), tile_spec],
        out_specs=out_specs,
        out_shape=out_shape,
        scratch_shapes=scratch,
        compiler_params=_params(("arbitrary", "arbitrary")),
        name="in_proj",
    )(x, mod, g_pre.reshape(1, d), w_in_f32, rows_tab, tile_tab)


def _rotation_constants(row_pos, tile_pos):
    lane = np.arange(LANES)
    inv_r = (1.0 / RET_THETA ** np.linspace(0.0, 1.0, RET_DK // 2))[lane // 2]
    inv_a = (1.0 / ROPE_THETA ** (np.arange(0, ATT_HD, 2) / ATT_HD))[lane % (ATT_HD // 2)]
    rp = np.asarray(row_pos, np.float64)[:, None]
    tp = np.asarray(tile_pos, np.float64)[:, None]
    rows = np.stack([np.cos(rp * inv_r), np.sin(rp * inv_r), np.cos(rp * inv_a), np.sin(rp * inv_a)])
    sign_r = np.where(lane % 2 == 0, -1.0, 1.0)
    sign_a = np.where(lane % ATT_HD < ATT_HD // 2, -1.0, 1.0)
    ones = np.ones_like(tp * inv_r)
    tiles = np.stack([np.cos(tp * inv_r), np.sin(tp * inv_r), np.cos(tp * inv_a), np.sin(tp * inv_a),
                      ones * sign_r, ones * sign_a, 0 * ones, 0 * ones], axis=1)
    return jnp.asarray(rows, F32), jnp.asarray(tiles, F32)


def _ret_readout(o, g, gain):
    return (_rms(o) * gain) * _silu(g)


def _ret_prompt_kernel(q_ref, k_ref, v_ref, g_ref, gain_ref, *rest, nchunk, c_len, ncast):
    cast_in, (o_ref, sfin_ref), cast_out, s_scr = rest[:ncast], rest[ncast:ncast + 2], rest[ncast + 2:-1], rest[-1]
    for src, dst in zip(cast_in, cast_out):
        dst[...] = src[...].astype(dst.dtype)
    i = pl.program_id(1)

    @pl.when(i == 0)
    def _():
        s_scr[...] = jnp.zeros_like(s_scr)

    row = lax.broadcasted_iota(jnp.int32, (c_len, c_len), 0)
    col = lax.broadcasted_iota(jnp.int32, (c_len, c_len), 1)
    diff = (row - col).astype(F32)
    ridx = lax.broadcasted_iota(jnp.int32, (c_len, 1), 0).astype(F32)
    for h in range(RET_HEADS):
        lg = LOG_G[h]
        intra = jnp.where(diff >= 0, jnp.exp(lg * jnp.maximum(diff, 0.0)), 0.0)
        q_dec = jnp.exp(lg * (ridx + 1.0))
        k_dec = jnp.exp(lg * (c_len - 1.0 - ridx))
        c_dec = math.exp(lg * c_len)
        hs = slice(h * RET_DK, (h + 1) * RET_DK)
        state = s_scr[h]
        for c in range(nchunk):
            rs = slice(c * c_len, (c + 1) * c_len)
            q = q_ref[rs, hs]
            k = k_ref[rs, hs]
            v = v_ref[rs, hs]
            a = _dot_nt(q, k) * intra
            lhs = jnp.concatenate([a.astype(BF16), (q.astype(F32) * q_dec).astype(BF16)], axis=1)
            o = _dot(lhs, jnp.concatenate([v, state.astype(BF16)], axis=0))
            kd = (k.astype(F32) * k_dec).astype(BF16)
            state = state * c_dec + _dot_tn(kd, v)
            o_ref[rs, hs] = _ret_readout(o, g_ref[rs, hs], gain_ref[:, hs]).astype(o_ref.dtype)
        s_scr[h] = state

    @pl.when(i == pl.num_programs(1) - 1)
    def _():
        sfin_ref[...] = s_scr[...]


def _retention_prompt(rq, rk, rv, rg, ret_gain, weights, *, tc, chunk):
    b, s, w = rq.shape
    nblk = s // tc
    steps = b * nblk
    tok = pl.BlockSpec((None, tc, w), lambda bi, i: (bi, i, 0))
    slab = lambda m: pl.BlockSpec((m.shape[0] // steps, m.shape[1]), lambda bi, i: (bi * nblk + i, 0))
    assert all(m.shape[0] % (16 * steps) == 0 for m in weights)
    out = pl.pallas_call(
        functools.partial(_ret_prompt_kernel, nchunk=tc // chunk, c_len=chunk, ncast=len(weights)),
        grid=(b, nblk),
        in_specs=[tok, tok, tok, tok, _const_spec((1, w))] + [slab(m) for m in weights],
        out_specs=[tok, pl.BlockSpec((None, RET_HEADS, RET_DK, RET_DK), lambda bi, i: (bi, 0, 0, 0))]
        + [slab(m) for m in weights],
        out_shape=[jax.ShapeDtypeStruct((b, s, w), BF16),
                   jax.ShapeDtypeStruct((b, RET_HEADS, RET_DK, RET_DK), F32)]
        + [jax.ShapeDtypeStruct(m.shape, BF16) for m in weights],
        scratch_shapes=[pltpu.VMEM((RET_HEADS, RET_DK, RET_DK), F32)],
        compiler_params=_params(("arbitrary", "arbitrary")),
        name="retention_prompt",
    )(rq, rk, rv, rg, ret_gain.reshape(1, w), *weights)
    return out[0], out[1], out[2:]


def _att_host_kernel(q_ref, k_ref, v_ref, kp_ref, vp_ref, oh_ref, bias_ref, gq_ref, gkn_ref, gvn_ref, gck_ref,
                     gcv_ref, gcc_ref, gcn_ref, *rest, nsub, ln, wbuf, nunit):
    pv_ref, ml_ref, go_ref, gko_ref, gvo_ref, kcat, vcat = rest[-7:]
    steps = pl.num_programs(0) * pl.num_programs(1) * pl.num_programs(2)
    t = (pl.program_id(0) * pl.num_programs(1) + pl.program_id(1)) * pl.num_programs(2) + pl.program_id(2)
    fresh = jnp.logical_or(t == 0, _hosted_unit(t, 0, nunit, steps) != _hosted_unit(t - 1, 0, nunit, steps))

    @pl.when(fresh)
    def _():
        _sample_heads(gq_ref, gkn_ref, gvn_ref, gck_ref, gcv_ref, gcc_ref, gcn_ref, go_ref, gko_ref, gvo_ref,
                      ln=ln, wbuf=wbuf)

    _att_block(q_ref, k_ref, v_ref, kp_ref, vp_ref, oh_ref, bias_ref, pv_ref, ml_ref, kcat, vcat, nsub=nsub)


def _att_block(q_ref, k_ref, v_ref, kp_ref, vp_ref, oh_ref, bias_ref, pv_ref, ml_ref, kcat, vcat, *, nsub):
    i = pl.program_id(2)
    blk = DIL_BLOCK
    kcat[0:blk, :] = kp_ref[...]
    kcat[blk:, :] = k_ref[...]
    vcat[0:blk, :] = vp_ref[...]
    vcat[blk:, :] = v_ref[...]
    lane = lax.broadcasted_iota(jnp.int32, (blk, LANES), 1)
    low = lane < ATT_HD
    quarter = lane // (ATT_HD // 2)
    onehot = oh_ref[...]
    bias_rest = bias_ref[1]
    bias_first = jnp.where(i == 0, bias_ref[0], bias_rest)
    ones = jnp.ones((2 * blk, LANES), BF16)
    for j in range(nsub):
        bias = bias_first if j == 0 else bias_rest
        rows = slice(j * blk, (j + 1) * blk)
        krows = slice(j * blk, (j + 2) * blk)
        for hp in range(ATT_WIDTH // LANES):
            cols = slice(hp * LANES, (hp + 1) * LANES)
            q = q_ref[rows, cols]
            zero = jnp.zeros_like(q)
            q2 = jnp.concatenate([jnp.where(low, q, zero), jnp.where(low, zero, q)], axis=0)
            s = _dot_nt(jnp.concatenate([q2, onehot], axis=1), jnp.concatenate([kcat[krows, cols], bias], axis=1))
            m = jnp.max(s, axis=-1, keepdims=True)
            p = jnp.exp2(s - m).astype(BF16)
            pv = _dot(p, jnp.concatenate([vcat[krows, cols], ones], axis=1))
            pv_ref[rows, cols] = jnp.where(low, pv[0:blk, 0:LANES], pv[blk:, 0:LANES]).astype(pv_ref.dtype)
            ml_ref[rows, cols] = jnp.where(
                quarter == 0, m[0:blk], jnp.where(quarter == 1, pv[0:blk, LANES:],
                                                  jnp.where(quarter == 2, m[blk:], pv[blk:, LANES:])))


def _band_tables():
    jj = np.arange(2 * DIL_BLOCK)[:, None]
    qi = np.arange(DIL_BLOCK)[None, :]
    bias = np.stack([np.where((jj >= np.maximum(qi, fk)) & (jj <= qi + SPAN), 0.0, NEG_INF)
                     for fk in (DIL_BLOCK, 0)])
    onehot = np.tile(np.eye(DIL_BLOCK), (2, 1))
    return jnp.asarray(onehot, BF16), jnp.asarray(bias, BF16)


def _hosted_unit(t, unit0, nunit, steps):
    return unit0 + (t * nunit) // steps


def _dilated_attention(aq, ak, av, sample, carry, *, tq, unit0, nunit):
    b, dil, m_len, w = aq.shape
    nsub = tq // DIL_BLOCK
    nblk = m_len // tq
    own = pl.BlockSpec((None, None, tq, w), lambda bi, r, i: (bi, r, i, 0))
    prv = pl.BlockSpec((None, None, DIL_BLOCK, w), lambda bi, r, i: (bi, r, jnp.maximum(i * nsub - 1, 0), 0))
    onehot, bias = _band_tables()
    in_specs = [own, own, own, prv, prv, _const_spec(onehot.shape), _const_spec(bias.shape)]
    args = [aq, ak, av, ak, av, onehot, bias]
    out_shape = [jax.ShapeDtypeStruct((b, dil, m_len, w), BF16), jax.ShapeDtypeStruct((b, dil, m_len, w), F32)]
    scratch = [pltpu.VMEM((tq + DIL_BLOCK, w), BF16), pltpu.VMEM((tq + DIL_BLOCK, w), BF16)]
    common = dict(grid=(b, dil, nblk), scratch_shapes=scratch, name=f"dilated_attn_d{dil}",
                  compiler_params=_params(("arbitrary", "arbitrary", "arbitrary")))
    if nunit == 0:
        return pl.pallas_call(functools.partial(_att_block, nsub=nsub), in_specs=in_specs,
                              out_specs=[own, own], out_shape=out_shape, **common)(*args)

    gq, gkn, gvn, gck, gcv, cnt_c, cnt_n = sample
    nb, _, wbuf = gck.shape
    ln = gq.shape[0] // nb
    hw = w // 2
    steps = b * dil * nblk
    assert nunit <= steps

    def unit(bi, r, i):
        u = _hosted_unit((bi * dil + r) * nblk + i, unit0, nunit, steps)
        return u // 2, u % 2

    rows = pl.BlockSpec((ln, hw), lambda bi, r, i: unit(bi, r, i))
    window = pl.BlockSpec((None, hw, wbuf), lambda bi, r, i: (*unit(bi, r, i), 0))
    in_specs += [rows, rows, rows, window, window, _const_spec(cnt_c.shape), _const_spec(cnt_n.shape)]
    args += [gq, gkn, gvn, gck, gcv, cnt_c, cnt_n]
    aliases = {}
    if carry is not None:
        aliases = {len(args) + n: 2 + n for n in range(3)}
        in_specs += [pl.BlockSpec(memory_space=pl.ANY)] * 3
        args += list(carry)
    out_shape += [jax.ShapeDtypeStruct((nb * ln, w), F32), jax.ShapeDtypeStruct((nb, w, wbuf), F32),
                  jax.ShapeDtypeStruct((nb, w, wbuf), F32)]
    return pl.pallas_call(
        functools.partial(_att_host_kernel, nsub=nsub, ln=ln, wbuf=wbuf, nunit=nunit),
        in_specs=in_specs,
        out_specs=[own, own, rows, window, window],
        out_shape=out_shape,
        input_output_aliases=aliases,
        **common,
    )(*args)


def _merge_patterns(pv1_ref, ml1_ref, pv4_ref, ml4_ref, pv16_ref, ml16_ref, tok, s4, att):
    tm = att.shape[0]
    half = ATT_HD // 2
    lane = lax.broadcasted_iota(jnp.int32, (tm, LANES), 1)
    first = lane % ATT_HD < half

    def unpack(ml):
        return (jnp.where(first, ml, pltpu.roll(ml, half, 1)),
                jnp.where(first, pltpu.roll(ml, LANES - half, 1), ml))

    for c in range(ATT_WIDTH // LANES):
        sl = slice(c * LANES, (c + 1) * LANES)
        for n, (r4, r16) in enumerate(((pv4_ref, pv16_ref), (ml4_ref, ml16_ref))):
            for q in range(4):
                tok[n, c, pl.ds(q, tm // 4, stride=4), :] = r4[q, :, sl].astype(F32)
                for a in range(4):
                    s4[q, pl.ds(a, tm // 16, stride=4), :] = r16[4 * a + q, :, sl].astype(F32)
            for q in range(4):
                tok[2 + n, c, pl.ds(q, tm // 4, stride=4), :] = s4[q]
        (m1, l1), (m4, l4), (m16, l16) = unpack(ml1_ref[:, sl]), unpack(tok[1, c]), unpack(tok[3, c])
        top = jnp.maximum(m1, jnp.maximum(m4, m16))
        e1, e4, e16 = jnp.exp2(m1 - top), jnp.exp2(m4 - top), jnp.exp2(m16 - top)
        num = e1 * pv1_ref[:, sl].astype(F32) + e4 * tok[0, c] + e16 * tok[2, c]
        den = e1 * l1 + e4 * l4 + e16 * l16
        merged = num / den
        att[:, sl] = merged.astype(att.dtype)
        yield merged


def _tail_body(x_ref, ret_ref, att_h, mod_ref, gpm_ref, gpf_ref, gqf_ref, wo_ref, wu_ref, wd_ref, y_ref,
               filler=iter(())):
    d = D_MODEL
    x = x_ref[...]
    mixed = _dot(ret_ref[...].astype(BF16), wo_ref[0:RET_WIDTH, :]) + _dot(att_h, wo_ref[RET_WIDTH:, :])
    x1 = x + mod_ref[:, 2 * d:3 * d] * (_rms(mixed) * gpm_ref[...])
    h = ((_rms(x1) * gpf_ref[...]) * (1.0 + mod_ref[:, 4 * d:5 * d]) + mod_ref[:, 3 * d:4 * d]).astype(BF16)
    fc = 1024
    f = None
    for c in range(D_FF // fc):
        u = jnp.maximum(_dot(h, wu_ref[:, c * fc:(c + 1) * fc]), 0.0)
        piece = next(filler, None)
        if piece is not None:
            u = jnp.concatenate([u[:, 0:LANES] + _tied_zero([piece], u.shape[0]), u[:, LANES:]], axis=1)
        part = _dot((u * u).astype(BF16), wd_ref[c * fc:(c + 1) * fc, :])
        f = part if f is None else f + part
    for _ in filler:
        pass
    y_ref[...] = x1 + mod_ref[:, 5 * d:6 * d] * (_rms(f) * gqf_ref[...])


def _tail_kernel(x_ref, ret_ref, att_ref, *rest):
    _tail_body(x_ref, ret_ref, att_ref[...].astype(BF16), *rest)


def _tail_merge_kernel(x_ref, ret_ref, pv1_ref, ml1_ref, pv4_ref, ml4_ref, pv16_ref, ml16_ref, mod_ref,
                       gpm_ref, gpf_ref, gqf_ref, wo_ref, wu_ref, wd_ref, y_ref, tok, s4, att):
    t = pl.program_id(0)
    merge = functools.partial(_merge_patterns, pv1_ref, ml1_ref, pv4_ref, ml4_ref, pv16_ref, ml16_ref,
                              tok, s4, att)

    @pl.when(t == 0)
    def _():
        for _ in merge():
            pass

    @pl.when(t > 0)
    def _():
        att_h = att[...]
        _tail_body(x_ref, ret_ref, att_h, mod_ref, gpm_ref, gpf_ref, gqf_ref, wo_ref, wu_ref, wd_ref, y_ref,
                   filler=merge())


def _tail(x, ret_h, att, mod, g_post_mix, g_pre_ffn, g_post_ffn, wo_bf, wu_bf, wd_bf, *, tm):
    b, s, d = x.shape
    nt = s // tm
    per_row = mod.shape[1] != 1
    weights = [_const_spec((1, d)), _const_spec((1, d)), _const_spec((1, d)),
               _const_spec((d, d)), _const_spec((d, D_FF)), _const_spec((D_FF, d))]
    weight_args = (g_post_mix.reshape(1, d), g_pre_ffn.reshape(1, d), g_post_ffn.reshape(1, d),
                   wo_bf, wu_bf, wd_bf)
    out_shape = jax.ShapeDtypeStruct((b, s, d), F32)
    if not isinstance(att, (tuple, list)):
        mod_spec = pl.BlockSpec((None, tm if per_row else 1, 6 * d),
                                (lambda bi, i: (bi, i, 0)) if per_row else (lambda bi, i: (bi, 0, 0)))
        tok = lambda w: pl.BlockSpec((None, tm, w), lambda bi, i: (bi, i, 0))
        return pl.pallas_call(
            _tail_kernel,
            grid=(b, nt),
            in_specs=[tok(d), tok(RET_WIDTH), tok(ATT_WIDTH), mod_spec, *weights],
            out_specs=tok(d),
            out_shape=out_shape,
            compiler_params=_params(("arbitrary", "arbitrary")),
            name="out_proj_mlp",
        )(x, ret_h, att, mod, *weight_args)

    assert not per_row
    last = b * nt - 1
    cur = lambda t: jnp.maximum(t - 1, 0)
    nxt = lambda t: jnp.minimum(t, last)
    tok = lambda w: pl.BlockSpec((None, tm, w), lambda t: (cur(t) // nt, cur(t) % nt, 0))
    mod_spec = pl.BlockSpec((None, 1, 6 * d), lambda t: (cur(t) // nt, 0, 0))
    split = lambda r: pl.BlockSpec((None, r, tm // r, ATT_WIDTH), lambda t: (nxt(t) // nt, 0, nxt(t) % nt, 0))
    one = pl.BlockSpec((None, None, tm, ATT_WIDTH), lambda t: (nxt(t) // nt, 0, nxt(t) % nt, 0))
    nch = ATT_WIDTH // LANES
    return pl.pallas_call(
        _tail_merge_kernel,
        grid=(b * nt + 1,),
        in_specs=[tok(d), tok(RET_WIDTH), one, one, split(4), split(4), split(16), split(16), mod_spec, *weights],
        out_specs=tok(d),
        out_shape=out_shape,
        scratch_shapes=[pltpu.VMEM((4, nch, tm, LANES), F32), pltpu.VMEM((4, tm // 4, LANES), F32),
                        pltpu.VMEM((tm, ATT_WIDTH), BF16)],
        compiler_params=_params(("arbitrary",)),
        name="out_proj_mlp",
    )(x, ret_h, *att, mod, *weight_args)


def _ret_sample_kernel(q_ref, k_ref, v_ref, g_ref, gain_ref, dm_ref, qd_ref, kd_ref, cd_ref, s_ref,
                       o_ref, sn_ref, o_scr, qd_scr, kt_scr, *, nb, ln):
    q = q_ref[...].astype(F32)
    k = k_ref[...].astype(F32)
    v = v_ref[...].astype(BF16)
    a = _dot_nt(q.astype(BF16), k.astype(BF16)) * dm_ref[...]
    o_scr[...] = _dot(a.astype(BF16), v)
    qd_scr[...] = q * qd_ref[...]
    kt_scr[...] = (k * kd_ref[...]).T
    c_dec = cd_ref[0:1, :]
    col = lax.broadcasted_iota(jnp.int32, kt_scr.shape, 1)

    def body(bi, carry):
        rows = pl.ds(pl.multiple_of(bi * ln, ln), ln)
        s0 = s_ref[bi]
        o_scr[rows, :] += _dot(qd_scr[rows, :].astype(BF16), s0.astype(BF16))
        mine = (col >= bi * ln) & (col < (bi + 1) * ln)
        kt = jnp.where(mine, kt_scr[...], 0.0).astype(BF16)
        sn_ref[bi] = s0 * c_dec + _dot(kt, v)
        return carry

    lax.fori_loop(0, nb, body, 0, unroll=math.gcd(nb, 8))
    o_ref[...] = _ret_readout(o_scr[...], g_ref[...], gain_ref[...]).astype(o_ref.dtype)


def _retention_sample(rq, rk, rv, rg, ret_gain, state, *, nb, ln):
    n = nb * ln
    idx = np.arange(n)
    same = (idx[:, None] // ln == idx[None, :] // ln) & (idx[:, None] >= idx[None, :])
    diff = np.maximum(idx[:, None] - idx[None, :], 0).astype(np.float64)
    step = (idx % ln).astype(np.float64)
    lg = np.asarray(LOG_G, np.float64)
    dm = np.where(same[None], np.exp(lg[:, None, None] * diff[None]), 0.0)
    qd = np.broadcast_to(np.exp(lg[:, None] * (step + 1.0))[:, :, None], (RET_HEADS, n, RET_DK))
    kd = np.broadcast_to(np.exp(lg[:, None] * (ln - 1.0 - step))[:, :, None], (RET_HEADS, n, RET_DK))
    cd = np.broadcast_to(np.exp(lg * ln)[:, None, None], (RET_HEADS, 8, RET_DK))
    tab = lambda t: jnp.asarray(np.ascontiguousarray(t), F32)
    col = pl.BlockSpec((n, RET_DK), lambda h: (0, h))
    per_head = lambda r, c: pl.BlockSpec((None, r, c), lambda h: (h, 0, 0))
    st = pl.BlockSpec((nb, None, RET_DK, RET_DK), lambda h: (0, h, 0, 0))
    return pl.pallas_call(
        functools.partial(_ret_sample_kernel, nb=nb, ln=ln),
        grid=(RET_HEADS,),
        in_specs=[col, col, col, col, pl.BlockSpec((1, RET_DK), lambda h: (0, h)),
                  per_head(n, n), per_head(n, RET_DK), per_head(n, RET_DK), per_head(8, RET_DK), st],
        out_specs=[col, st],
        out_shape=[jax.ShapeDtypeStruct((n, RET_WIDTH), BF16),
                   jax.ShapeDtypeStruct((nb, RET_HEADS, RET_DK, RET_DK), F32)],
        scratch_shapes=[pltpu.VMEM((n, RET_DK), F32), pltpu.VMEM((n, RET_DK), F32),
                        pltpu.VMEM((RET_DK, n), F32)],
        compiler_params=_params(("arbitrary",)),
        name="retention_sample",
    )(rq, rk, rv, rg, ret_gain.reshape(1, RET_WIDTH), tab(dm), tab(qd), tab(kd), tab(cd), state)


def _shift_window(old_ref, new_t, out_ref, *, ln, wbuf):
    lane = lax.broadcasted_iota(jnp.int32, new_t.shape, 1)
    ncol = wbuf // LANES
    rolled = pltpu.roll(old_ref[:, 0:LANES], LANES - ln, 1)
    for c in range(ncol):
        nxt = pltpu.roll(old_ref[:, (c + 1) * LANES:(c + 2) * LANES] if c + 1 < ncol else new_t, LANES - ln, 1)
        out_ref[:, c * LANES:(c + 1) * LANES] = jnp.where(lane < LANES - ln, rolled, nxt)
        rolled = nxt


def _new_rows_minor(new_ref, ln):
    pad = jnp.zeros((LANES - ln, new_ref.shape[1]), F32)
    return jnp.concatenate([new_ref[...], pad], axis=0).T


def _sample_heads(q_ref, kn_ref, vn_ref, ck_ref, cv_ref, cc_ref, cn_ref, o_ref, ko_ref, vo_ref, *, ln, wbuf):
    kn_t = _new_rows_minor(kn_ref, ln)
    vn_t = _new_rows_minor(vn_ref, ln)
    _shift_window(ck_ref, kn_t, ko_ref, ln=ln, wbuf=wbuf)
    _shift_window(cv_ref, vn_t, vo_ref, ln=ln, wbuf=wbuf)
    q = q_ref[...]
    nh = q.shape[1] // ATT_HD
    lane = lax.broadcasted_iota(jnp.int32, q.shape, 1)
    heads = [lane // ATT_HD == h for h in range(nh)]
    qm = jnp.concatenate([jnp.where(hm, q, 0.0) for hm in heads], axis=0).astype(BF16)
    cnt_c = jnp.concatenate([cc_ref[...]] * nh, axis=0)
    cnt_n = jnp.concatenate([cn_ref[...]] * nh, axis=0)
    s_c = jnp.where(cnt_c > 0, _dot(qm, ck_ref[...].astype(BF16)), NEG_INF)
    s_n = jnp.where(cnt_n > 0, _dot(qm, kn_t.astype(BF16)), NEG_INF)
    m = jnp.maximum(jnp.max(s_c, axis=-1, keepdims=True), jnp.max(s_n, axis=-1, keepdims=True))
    p_c = cnt_c * jnp.exp2(s_c - m)
    p_n = cnt_n * jnp.exp2(s_n - m)
    l = jnp.sum(p_c, axis=-1, keepdims=True) + jnp.sum(p_n, axis=-1, keepdims=True)
    o = _dot_nt(p_c.astype(BF16), cv_ref[...].astype(BF16)) + _dot_nt(p_n.astype(BF16), vn_t.astype(BF16))
    o = o / l
    acc = jnp.zeros_like(q)
    for h, hm in enumerate(heads):
        acc = acc + jnp.where(hm, o[h * ln:(h + 1) * ln, :], 0.0)
    o_ref[...] = acc.astype(o_ref.dtype)


def _pattern_counts(ln, wbuf):
    cnt = np.zeros((ln, wbuf + ln), np.float32)
    for dil in DILATIONS:
        for l in range(ln):
            for j in range(SPAN + 1):
                row = wbuf + l - dil * j
                if row >= 0:
                    cnt[l, row] += 1.0
    return cnt


def _count_tables(ln, wbuf):
    cnt = _pattern_counts(ln, wbuf)
    cnt_new = np.zeros((ln, LANES), np.float32)
    cnt_new[:, :ln] = cnt[:, wbuf:]
    return jnp.asarray(cnt[:, :wbuf]), jnp.asarray(cnt_new)


def _step(x_prompt, x_sample, c_prompt, c_sample, state_ret, cache_win_k, cache_win_v, w_ada, b_ada,
          g_pre_mix, g_post_mix, g_pre_ffn, g_post_ffn, w_in, ret_gain, w_o, w_up, w_down,
          *, tm, tc, tq):
    assert w_in.shape[0] == 1, "single-layer step"
    bp, sp, d = x_prompt.shape
    nb, ln, _ = x_sample.shape
    wbuf = cache_win_k.shape[2]
    n_s = nb * ln

    w_in_f32 = w_in[0]

    rows = bp + nb
    pad = (-rows) % 8
    c_all = jnp.concatenate([c_prompt, c_sample, jnp.zeros((pad, d), F32)], axis=0)
    mod = _modulation(c_all, w_ada[0], b_ada[0])
    mod_p = mod[:bp].reshape(bp, 1, 6 * d)
    mod_s = jnp.repeat(mod[bp:rows], ln, axis=0).reshape(1, n_s, 6 * d)

    tabs_p = _rotation_constants(np.arange(tm), np.arange(sp // tm) * tm)
    tabs_s = _rotation_constants(np.tile(np.arange(ln), nb), [PAST_LEN])

    keep = min(MAX_WINDOW, sp)
    (rq, rk, rv, rg, aq, ak, av, akf, avf, aq4, ak4, av4, aq16, ak16, av16) = _project(
        x_prompt, mod_p, g_pre_mix[0], w_in_f32, tabs_p, tm=tm, keep=keep, act_dtype=BF16, regroup=True)
    xs = x_sample.reshape(1, n_s, d)
    srq, srk, srv, srg, saq, _, _, sakf, savf = _project(
        xs, mod_s, g_pre_mix[0], w_in_f32, tabs_s, tm=n_s, keep=n_s, act_dtype=F32, regroup=False)
    flat = lambda t: t.reshape(n_s, GROUP_W)

    to_minor = lambda t: jnp.transpose(t[0], (0, 2, 3, 1)).reshape(nb, ATT_WIDTH, wbuf)
    from_minor = lambda t: jnp.transpose(t.reshape(nb, ATT_HEADS, ATT_HD, wbuf), (0, 3, 1, 2))[None]
    sample = (flat(saq), flat(sakf), flat(savf), to_minor(cache_win_k), to_minor(cache_win_v),
              *_count_tables(ln, wbuf))
    patterns = ((aq16, ak16, av16), (aq4, ak4, av4), (aq[:, None], ak[:, None], av[:, None]))
    units, steps = 2 * nb, bp * sp // tq
    assert units <= len(patterns) * steps
    carry, unit0, merged_in = None, 0, []
    for qkv in patterns:
        nunit = min(steps, units - unit0)
        out = _dilated_attention(*qkv, sample, carry, tq=tq, unit0=unit0, nunit=nunit)
        merged_in = list(out[:2]) + merged_in
        carry = out[2:] or carry
        unit0 += nunit
    satt_h, k_out, v_out = carry

    ret_h, s_fin, (wo_bf, wu_bf, wd_bf) = _retention_prompt(
        rq, rk, rv, rg, ret_gain[0], (w_o[0], w_up[0], w_down[0]), tc=tc, chunk=RET_BLOCK)
    y_prompt = _tail(x_prompt, ret_h, tuple(merged_in), mod_p, g_post_mix[0], g_pre_ffn[0],
                     g_post_ffn[0], wo_bf, wu_bf, wd_bf, tm=tm)

    sret_h, s_new = _retention_sample(flat(srq), flat(srk), flat(srv), flat(srg), ret_gain[0],
                                      state_ret[0], nb=nb, ln=ln)
    y_sample = _tail(xs, sret_h.reshape(1, n_s, RET_WIDTH), satt_h.reshape(1, n_s, ATT_WIDTH), mod_s,
                     g_post_mix[0], g_pre_ffn[0], g_post_ffn[0], wo_bf, wu_bf, wd_bf, tm=n_s)

    cache_shape = (1, -1, keep, ATT_HEADS, ATT_HD)
    return (y_prompt,
            y_sample.reshape(nb, ln, d),
            s_fin[None],
            akf.reshape(cache_shape),
            avf.reshape(cache_shape),
            s_new[None],
            from_minor(k_out),
            from_minor(v_out))


def kernel(x_prompt, x_sample, c_prompt, c_sample, state_ret, cache_win_k, cache_win_v, w_ada, b_ada,
           g_pre_mix, g_post_mix, g_pre_ffn, g_post_ffn, w_in, ret_gain, w_o, w_up, w_down):
    return _step(x_prompt, x_sample, c_prompt, c_sample, state_ret, cache_win_k, cache_win_v, w_ada, b_ada,
                 g_pre_mix, g_post_mix, g_pre_ffn, g_post_ffn, w_in, ret_gain, w_o, w_up, w_down,
                 tm=512, tc=1024, tq=512)
```

```python
import functools
import itertools
import math

import numpy as np
import jax
import jax.numpy as jnp
from jax import lax
from jax.experimental import pallas as pl
from jax.experimental.pallas import tpu as pltpu

F32 = jnp.float32
BF16 = jnp.bfloat16

D_MODEL = 1024
RET_HEADS = 4
RET_WIDTH = 512
RET_DK = 128
RET_THETA = 10000.0
RET_BLOCK = 256
ATT_HD = 64
ATT_HEADS = 8
ATT_WIDTH = 512
DILATIONS = (1, 4, 16)
MAX_WINDOW = 2048
SPAN = 128
DIL_BLOCK = 128
ROPE_THETA = 10000.0
D_FF = 4096
NORM_EPS = 1e-6
NEG_INF = -1e30
PAST_LEN = 16384
N_GROUPS = 7
GROUP_W = 512
LANES = 128
VMEM_LIMIT = 56 * 1024 * 1024

LOG2_E = math.log2(math.e)
LOG_G = tuple(math.log1p(-(2.0 ** (-5.0 - h))) for h in range(RET_HEADS))


def _dot(a, b):
    return jnp.dot(a, b, preferred_element_type=F32)


def _dot_nt(a, b):
    return lax.dot_general(a, b, (((1,), (1,)), ((), ())), preferred_element_type=F32)


def _dot_tn(a, b):
    return lax.dot_general(a, b, (((0,), (0,)), ((), ())), preferred_element_type=F32)


def _rms(x):
    return x * lax.rsqrt(jnp.mean(x * x, axis=-1, keepdims=True) + NORM_EPS)


def _silu(x):
    return x / (1.0 + jnp.exp(-x))


def _tied_zero(values, rows):
    bits = None
    for v in values:
        assert v.shape[0] % rows == 0 and v.shape[1] % LANES == 0
        for r0 in range(0, v.shape[0], rows):
            for c0 in range(0, v.shape[1], LANES):
                b = pltpu.bitcast(v[r0:r0 + rows, c0:c0 + LANES], jnp.uint32)
                bits = b if bits is None else bits | b
    sixteen = jnp.full(bits.shape, 16, jnp.uint32)
    return pltpu.bitcast(lax.shift_right_logical(lax.shift_right_logical(bits, sixteen), sixteen), F32)


def _params(sem):
    return pltpu.CompilerParams(dimension_semantics=sem, vmem_limit_bytes=VMEM_LIMIT)


def _const_spec(shape):
    nd = len(shape)
    return pl.BlockSpec(shape, lambda *_: (0,) * nd, pipeline_mode=pl.Buffered(1))


def _mod_kernel(c_ref, w_ref, b_ref, o_ref):
    a = _silu(c_ref[...]).astype(BF16)
    o_ref[...] = _dot(a, w_ref[...].astype(BF16)) + b_ref[...]


def _modulation(c, w_ada, b_ada):
    rows, d = c.shape
    n = w_ada.shape[1]
    tn = 1536
    return pl.pallas_call(
        _mod_kernel,
        grid=(n // tn,),
        in_specs=[pl.BlockSpec((rows, d), lambda j: (0, 0)),
                  pl.BlockSpec((d, tn), lambda j: (0, j)),
                  pl.BlockSpec((1, tn), lambda j: (0, j))],
        out_specs=pl.BlockSpec((rows, tn), lambda j: (0, j)),
        out_shape=jax.ShapeDtypeStruct((rows, n), F32),
        compiler_params=_params(("arbitrary",)),
        name="adaln_mod",
    )(c, w_ada, b_ada.reshape(1, n))


def _rotate_pairs(x, cos, sin):
    lane = lax.broadcasted_iota(jnp.int32, x.shape, 1)
    partner = jnp.where(lane % 2 == 0, pltpu.roll(x, LANES - 1, 1), pltpu.roll(x, 1, 1))
    return x * cos + partner * sin


def _rotate_half(x, cos, sin):
    lane = lax.broadcasted_iota(jnp.int32, x.shape, 1)
    half = ATT_HD // 2
    partner = jnp.where(lane % ATT_HD < half, pltpu.roll(x, LANES - half, 1), pltpu.roll(x, half, 1))
    return x * cos + partner * sin


def _store_regrouped(val, c, tok_ref, r4_ref, r16_ref, zs, s4):
    sl = slice(c * LANES, (c + 1) * LANES)
    tok_ref[:, sl] = val.astype(tok_ref.dtype)
    tm = val.shape[0]
    zs[c] = val
    for q in range(4):
        g4 = zs[c, pl.ds(q, tm // 4, stride=4), :]
        r4_ref[q, :, sl] = g4.astype(r4_ref.dtype)
        s4[c, q] = g4
    for q in range(4):
        for a in range(4):
            r16_ref[4 * a + q, :, sl] = s4[c, q, pl.ds(a, tm // 16, stride=4), :].astype(r16_ref.dtype)


def _proj_kernel(x_ref, mod_ref, g_ref, wf_ref, rows_ref, tile_ref, *refs, regroup):
    if regroup:
        (rq_ref, rk_ref, rv_ref, rg_ref, aq_ref, ak_ref, av_ref, akf_ref, avf_ref,
         aq4_ref, ak4_ref, av4_ref, aq16_ref, ak16_ref, av16_ref, w_ref, zs, s4) = refs
    else:
        rq_ref, rk_ref, rv_ref, rg_ref, aq_ref, ak_ref, av_ref, akf_ref, avf_ref, w_ref = refs

    @pl.when(jnp.logical_and(pl.program_id(0) == 0, pl.program_id(1) == 0))
    def _():
        for gi in range(N_GROUPS):
            sl = slice(gi * GROUP_W, (gi + 1) * GROUP_W)
            w_ref[:, sl] = wf_ref[:, sl].astype(w_ref.dtype)

    d = D_MODEL
    x = x_ref[...]
    h = (_rms(x) * g_ref[...]) * (1.0 + mod_ref[:, d:2 * d]) + mod_ref[:, 0:d]
    h = h.astype(BF16)
    t = tile_ref[...]
    cr = t[0:1] * rows_ref[0] - t[1:2] * rows_ref[1]
    sr = (t[1:2] * rows_ref[0] + t[0:1] * rows_ref[1]) * t[4:5]
    ca = t[2:3] * rows_ref[2] - t[3:4] * rows_ref[3]
    sa = (t[3:4] * rows_ref[2] + t[2:3] * rows_ref[3]) * t[5:6]
    nch = GROUP_W // LANES

    def group(gi):
        return _dot(h, w_ref[:, gi * GROUP_W:(gi + 1) * GROUP_W])

    z = group(0)
    for c in range(nch):
        sl = slice(c * LANES, (c + 1) * LANES)
        rq_ref[:, sl] = _rotate_pairs(z[:, sl], cr, sr).astype(rq_ref.dtype)
    z = group(1)
    for c in range(nch):
        sl = slice(c * LANES, (c + 1) * LANES)
        rk_ref[:, sl] = (_rotate_pairs(z[:, sl], cr, sr) * (RET_DK ** -0.5)).astype(rk_ref.dtype)
    rv_ref[...] = group(2).astype(rv_ref.dtype)
    rg_ref[...] = group(3)
    def emit(val, c, tok_ref, r4_ref, r16_ref):
        if regroup:
            _store_regrouped(val, c, tok_ref, r4_ref, r16_ref, zs, s4)
        else:
            tok_ref[:, c * LANES:(c + 1) * LANES] = val.astype(tok_ref.dtype)

    r4 = (aq4_ref, ak4_ref, av4_ref) if regroup else (None,) * 3
    r16 = (aq16_ref, ak16_ref, av16_ref) if regroup else (None,) * 3
    z = group(4)
    for c in range(nch):
        sl = slice(c * LANES, (c + 1) * LANES)
        emit(_rotate_half(z[:, sl], ca, sa) * (ATT_HD ** -0.5 * LOG2_E), c, aq_ref, r4[0], r16[0])
    z = group(5)
    for c in range(nch):
        sl = slice(c * LANES, (c + 1) * LANES)
        r = _rotate_half(z[:, sl], ca, sa)
        akf_ref[:, sl] = r
        emit(r, c, ak_ref, r4[1], r16[1])
    z = group(6)
    avf_ref[...] = z
    for c in range(nch):
        emit(z[:, c * LANES:(c + 1) * LANES], c, av_ref, r4[2], r16[2])


def _project(x, mod, g_pre, w_in_f32, tabs, *, tm, keep, act_dtype, regroup):
    b, s, d = x.shape
    nt = s // tm
    rows_tab, tile_tab = tabs
    assert rows_tab.shape == (4, tm, LANES) and tile_tab.shape == (nt, 8, LANES)
    mod_rows = mod.shape[1]
    per_row = mod_rows != 1
    first_keep = (s - keep) // tm
    tok = pl.BlockSpec((None, tm, GROUP_W), lambda bi, i: (bi, i, 0))
    keep_spec = pl.BlockSpec((None, tm, GROUP_W), lambda bi, i: (bi, jnp.maximum(i - first_keep, 0), 0))
    tile_spec = pl.BlockSpec((None, 8, LANES), lambda bi, i: (i, 0, 0))
    mod_spec = pl.BlockSpec((None, tm if per_row else 1, 6 * d),
                            (lambda bi, i: (bi, i, 0)) if per_row else (lambda bi, i: (bi, 0, 0)))
    act = jax.ShapeDtypeStruct((b, s, GROUP_W), act_dtype)
    full = jax.ShapeDtypeStruct((b, s, GROUP_W), F32)
    kept = jax.ShapeDtypeStruct((b, keep, GROUP_W), F32)
    out_specs = [tok, tok, tok, tok, tok, tok, tok, keep_spec, keep_spec]
    out_shape = [act, act, act, full, act, act, act, kept, kept]
    scratch = [pltpu.VMEM((d, N_GROUPS * GROUP_W), BF16)]
    if regroup:
        for r in (4, 16):
            out_specs += [pl.BlockSpec((None, r, tm // r, GROUP_W), lambda bi, i: (bi, 0, i, 0))] * 3
            out_shape += [jax.ShapeDtypeStruct((b, r, s // r, GROUP_W), act_dtype)] * 3
        nch = GROUP_W // LANES
        scratch += [pltpu.VMEM((nch, tm, LANES), F32), pltpu.VMEM((nch, 4, tm // 4, LANES), F32)]
    return pl.pallas_call(
        functools.partial(_proj_kernel, regroup=regroup),
        grid=(b, nt),
        in_specs=[pl.BlockSpec((None, tm, d), lambda bi, i: (bi, i, 0)),
                  mod_spec,
                  _const_spec((1, d)),
                  _const_spec((d, N_GROUPS * GROUP_W)),
                  _const_spec((4, tm, LANES)), tile_spec],
        out_specs=out_specs,
        out_shape=out_shape,
        scratch_shapes=scratch,
        compiler_params=_params(("arbitrary", "arbitrary")),
        name="in_proj",
    )(x, mod, g_pre.reshape(1, d), w_in_f32, rows_tab, tile_tab)


def _rotation_constants(row_pos, tile_pos):
    lane = np.arange(LANES)
    inv_r = (1.0 / RET_THETA ** np.linspace(0.0, 1.0, RET_DK // 2))[lane // 2]
    inv_a = (1.0 / ROPE_THETA ** (np.arange(0, ATT_HD, 2) / ATT_HD))[lane % (ATT_HD // 2)]
    rp = np.asarray(row_pos, np.float64)[:, None]
    tp = np.asarray(tile_pos, np.float64)[:, None]
    rows = np.stack([np.cos(rp * inv_r), np.sin(rp * inv_r), np.cos(rp * inv_a), np.sin(rp * inv_a)])
    sign_r = np.where(lane % 2 == 0, -1.0, 1.0)
    sign_a = np.where(lane % ATT_HD < ATT_HD // 2, -1.0, 1.0)
    ones = np.ones_like(tp * inv_r)
    tiles = np.stack([np.cos(tp * inv_r), np.sin(tp * inv_r), np.cos(tp * inv_a), np.sin(tp * inv_a),
                      ones * sign_r, ones * sign_a, 0 * ones, 0 * ones], axis=1)
    return jnp.asarray(rows, F32), jnp.asarray(tiles, F32)


def _ret_readout(o, g, gain):
    return (_rms(o) * gain) * _silu(g)


def _ret_prompt_kernel(q_ref, k_ref, v_ref, g_ref, gain_ref, *rest, nchunk, c_len, ncast):
    cast_in, (o_ref, sfin_ref), cast_out, s_scr = rest[:ncast], rest[ncast:ncast + 2], rest[ncast + 2:-1], rest[-1]
    for src, dst in zip(cast_in, cast_out):
        dst[...] = src[...].astype(dst.dtype)
    i = pl.program_id(1)

    @pl.when(i == 0)
    def _():
        s_scr[...] = jnp.zeros_like(s_scr)

    row = lax.broadcasted_iota(jnp.int32, (c_len, c_len), 0)
    col = lax.broadcasted_iota(jnp.int32, (c_len, c_len), 1)
    diff = (row - col).astype(F32)
    ridx = lax.broadcasted_iota(jnp.int32, (c_len, 1), 0).astype(F32)
    for h in range(RET_HEADS):
        lg = LOG_G[h]
        intra = jnp.where(diff >= 0, jnp.exp(lg * jnp.maximum(diff, 0.0)), 0.0)
        q_dec = jnp.exp(lg * (ridx + 1.0))
        k_dec = jnp.exp(lg * (c_len - 1.0 - ridx))
        c_dec = math.exp(lg * c_len)
        hs = slice(h * RET_DK, (h + 1) * RET_DK)
        state = s_scr[h]
        for c in range(nchunk):
            rs = slice(c * c_len, (c + 1) * c_len)
            q = q_ref[rs, hs]
            k = k_ref[rs, hs]
            v = v_ref[rs, hs]
            a = _dot_nt(q, k) * intra
            lhs = jnp.concatenate([a.astype(BF16), (q.astype(F32) * q_dec).astype(BF16)], axis=1)
            o = _dot(lhs, jnp.concatenate([v, state.astype(BF16)], axis=0))
            kd = (k.astype(F32) * k_dec).astype(BF16)
            state = state * c_dec + _dot_tn(kd, v)
            o_ref[rs, hs] = _ret_readout(o, g_ref[rs, hs], gain_ref[:, hs]).astype(o_ref.dtype)
        s_scr[h] = state

    @pl.when(i == pl.num_programs(1) - 1)
    def _():
        sfin_ref[...] = s_scr[...]


def _retention_prompt(rq, rk, rv, rg, ret_gain, weights, *, tc, chunk):
    b, s, w = rq.shape
    nblk = s // tc
    steps = b * nblk
    tok = pl.BlockSpec((None, tc, w), lambda bi, i: (bi, i, 0))
    slab = lambda m: pl.BlockSpec((m.shape[0] // steps, m.shape[1]), lambda bi, i: (bi * nblk + i, 0))
    assert all(m.shape[0] % (16 * steps) == 0 for m in weights)
    out = pl.pallas_call(
        functools.partial(_ret_prompt_kernel, nchunk=tc // chunk, c_len=chunk, ncast=len(weights)),
        grid=(b, nblk),
        in_specs=[tok, tok, tok, tok, _const_spec((1, w))] + [slab(m) for m in weights],
        out_specs=[tok, pl.BlockSpec((None, RET_HEADS, RET_DK, RET_DK), lambda bi, i: (bi, 0, 0, 0))]
        + [slab(m) for m in weights],
        out_shape=[jax.ShapeDtypeStruct((b, s, w), BF16),
                   jax.ShapeDtypeStruct((b, RET_HEADS, RET_DK, RET_DK), F32)]
        + [jax.ShapeDtypeStruct(m.shape, BF16) for m in weights],
        scratch_shapes=[pltpu.VMEM((RET_HEADS, RET_DK, RET_DK), F32)],
        compiler_params=_params(("arbitrary", "arbitrary")),
        name="retention_prompt",
    )(rq, rk, rv, rg, ret_gain.reshape(1, w), *weights)
    return out[0], out[1], out[2:]


def _att_host_kernel(q_ref, k_ref, v_ref, kp_ref, vp_ref, oh_ref, bias_ref, gq_ref, gkn_ref, gvn_ref, gck_ref,
                     gcv_ref, gcc_ref, gcn_ref, *rest, nsub, ln, wbuf, nunit):
    pv_ref, ml_ref, go_ref, gko_ref, gvo_ref, kcat, vcat = rest[-7:]
    steps = pl.num_programs(0) * pl.num_programs(1) * pl.num_programs(2)
    t = (pl.program_id(0) * pl.num_programs(1) + pl.program_id(1)) * pl.num_programs(2) + pl.program_id(2)
    fresh = jnp.logical_or(t == 0, _hosted_unit(t, 0, nunit, steps) != _hosted_unit(t - 1, 0, nunit, steps))

    @pl.when(fresh)
    def _():
        _sample_heads(gq_ref, gkn_ref, gvn_ref, gck_ref, gcv_ref, gcc_ref, gcn_ref, go_ref, gko_ref, gvo_ref,
                      ln=ln, wbuf=wbuf)

    _att_block(q_ref, k_ref, v_ref, kp_ref, vp_ref, oh_ref, bias_ref, pv_ref, ml_ref, kcat, vcat, nsub=nsub)


def _att_block(q_ref, k_ref, v_ref, kp_ref, vp_ref, oh_ref, bias_ref, pv_ref, ml_ref, kcat, vcat, *, nsub):
    i = pl.program_id(2)
    blk = DIL_BLOCK
    for g in range(q_ref.shape[0]):
        kcat[g, 0:blk, :] = kp_ref[g]
        kcat[g, blk:, :] = k_ref[g]
        vcat[g, 0:blk, :] = vp_ref[g]
        vcat[g, blk:, :] = v_ref[g]
    lane = lax.broadcasted_iota(jnp.int32, (blk, LANES), 1)
    low = lane < ATT_HD
    quarter = lane // (ATT_HD // 2)
    onehot = oh_ref[...]
    bias_rest = bias_ref[1]
    bias_first = jnp.where(i == 0, bias_ref[0], bias_rest)
    ones = jnp.ones((2 * blk, LANES), BF16)
    low2 = lax.broadcasted_iota(jnp.int32, (2 * blk, LANES), 1) < ATT_HD
    for g, j, hp in itertools.product(range(q_ref.shape[0]), range(nsub), range(ATT_WIDTH // LANES)):
        bias = bias_first if j == 0 else bias_rest
        rows = slice(j * blk, (j + 1) * blk)
        krows = slice(j * blk, (j + 2) * blk)
        cols = slice(hp * LANES, (hp + 1) * LANES)
        q = q_ref[g, rows, cols]
        zero = jnp.zeros_like(q)
        q2 = jnp.concatenate([jnp.where(low, q, zero), jnp.where(low, zero, q)], axis=0)
        s = _dot_nt(jnp.concatenate([q2, onehot], axis=1), jnp.concatenate([kcat[g, krows, cols], bias], axis=1))
        m = jnp.max(s, axis=-1, keepdims=True)
        p = jnp.exp2(s - m).astype(BF16)
        vv = vcat[g, krows, cols]
        pv0 = _dot(p[0:blk], jnp.where(low2, vv, ones))
        pv1 = _dot(p[blk:], jnp.where(low2, ones, vv))
        pv_ref[g, rows, cols] = jnp.where(low, pv0, pv1).astype(pv_ref.dtype)
        ml_ref[g, rows, cols] = jnp.where(
            quarter == 0, m[0:blk], jnp.where(quarter == 1, pv1, jnp.where(quarter == 2, m[blk:], pv0)))


def _band_tables():
    jj = np.arange(2 * DIL_BLOCK)[:, None]
    qi = np.arange(DIL_BLOCK)[None, :]
    bias = np.stack([np.where((jj >= np.maximum(qi, fk)) & (jj <= qi + SPAN), 0.0, NEG_INF)
                     for fk in (DIL_BLOCK, 0)])
    onehot = np.tile(np.eye(DIL_BLOCK), (2, 1))
    return jnp.asarray(onehot, BF16), jnp.asarray(bias, BF16)


def _hosted_unit(t, unit0, nunit, steps):
    return unit0 + (t * nunit) // steps


def _dilated_attention(aq, ak, av, sample, carry, *, tq, groups, unit0, nunit):
    b, dil, m_len, w = aq.shape
    nsub = tq // DIL_BLOCK
    nblk = m_len // tq
    ngrp = dil // groups
    own = pl.BlockSpec((None, groups, tq, w), lambda bi, r, i: (bi, r, i, 0))
    prv = pl.BlockSpec((None, groups, DIL_BLOCK, w), lambda bi, r, i: (bi, r, jnp.maximum(i * nsub - 1, 0), 0))
    onehot, bias = _band_tables()
    in_specs = [own, own, own, prv, prv, _const_spec(onehot.shape), _const_spec(bias.shape)]
    args = [aq, ak, av, ak, av, onehot, bias]
    out_shape = [jax.ShapeDtypeStruct((b, dil, m_len, w), BF16), jax.ShapeDtypeStruct((b, dil, m_len, w), F32)]
    scratch = [pltpu.VMEM((groups, tq + DIL_BLOCK, w), BF16), pltpu.VMEM((groups, tq + DIL_BLOCK, w), BF16)]
    common = dict(grid=(b, ngrp, nblk), scratch_shapes=scratch, name=f"dilated_attn_d{dil}",
                  compiler_params=_params(("arbitrary", "arbitrary", "arbitrary")))
    if nunit == 0:
        return pl.pallas_call(functools.partial(_att_block, nsub=nsub), in_specs=in_specs,
                              out_specs=[own, own], out_shape=out_shape, **common)(*args)

    gq, gkn, gvn, gck, gcv, cnt_c, cnt_n = sample
    nb, _, wbuf = gck.shape
    ln = gq.shape[0] // nb
    steps = b * ngrp * nblk
    assert nunit <= steps

    def unit(bi, r, i):
        return _hosted_unit((bi * ngrp + r) * nblk + i, unit0, nunit, steps)

    rows = pl.BlockSpec((ln, w), lambda bi, r, i: (unit(bi, r, i), 0))
    window = pl.BlockSpec((None, w, wbuf), lambda bi, r, i: (unit(bi, r, i), 0, 0))
    in_specs += [rows, rows, rows, window, window, _const_spec(cnt_c.shape), _const_spec(cnt_n.shape)]
    args += [gq, gkn, gvn, gck, gcv, cnt_c, cnt_n]
    aliases = {}
    if carry is not None:
        aliases = {len(args) + n: 2 + n for n in range(3)}
        in_specs += [pl.BlockSpec(memory_space=pl.ANY)] * 3
        args += list(carry)
    out_shape += [jax.ShapeDtypeStruct((nb * ln, w), F32), jax.ShapeDtypeStruct((nb, w, wbuf), F32),
                  jax.ShapeDtypeStruct((nb, w, wbuf), F32)]
    return pl.pallas_call(
        functools.partial(_att_host_kernel, nsub=nsub, ln=ln, wbuf=wbuf, nunit=nunit),
        in_specs=in_specs,
        out_specs=[own, own, rows, window, window],
        out_shape=out_shape,
        input_output_aliases=aliases,
        **common,
    )(*args)


def _merge_patterns(pv1_ref, ml1_ref, pv4_ref, ml4_ref, pv16_ref, ml16_ref, tok, s4, att):
    tm = att.shape[0]
    half = ATT_HD // 2
    lane = lax.broadcasted_iota(jnp.int32, (tm, LANES), 1)
    first = lane % ATT_HD < half

    def unpack(ml):
        near = pltpu.roll(ml, half, 1)
        return jnp.where(first, ml, near), jnp.where(first, near, pltpu.roll(ml, 2 * half, 1))

    for c in range(ATT_WIDTH // LANES):
        sl = slice(c * LANES, (c + 1) * LANES)
        for n, (r4, r16) in enumerate(((pv4_ref, pv16_ref), (ml4_ref, ml16_ref))):
            for q in range(4):
                tok[n, c, pl.ds(q, tm // 4, stride=4), :] = r4[q, :, sl].astype(F32)
                for a in range(4):
                    s4[q, pl.ds(a, tm // 16, stride=4), :] = r16[4 * a + q, :, sl].astype(F32)
            for q in range(4):
                tok[2 + n, c, pl.ds(q, tm // 4, stride=4), :] = s4[q]
        (m1, l1), (m4, l4), (m16, l16) = unpack(ml1_ref[:, sl]), unpack(tok[1, c]), unpack(tok[3, c])
        top = jnp.maximum(m1, jnp.maximum(m4, m16))
        e1, e4, e16 = jnp.exp2(m1 - top), jnp.exp2(m4 - top), jnp.exp2(m16 - top)
        num = e1 * pv1_ref[:, sl].astype(F32) + e4 * tok[0, c] + e16 * tok[2, c]
        den = e1 * l1 + e4 * l4 + e16 * l16
        merged = num / den
        att[:, sl] = merged.astype(att.dtype)
        yield merged


def _tail_body(x_ref, ret_ref, att_h, mod_ref, gpm_ref, gpf_ref, gqf_ref, wo_ref, wu_ref, wd_ref, y_ref,
               filler=iter(())):
    d = D_MODEL
    x = x_ref[...]
    mixed = _dot(ret_ref[...].astype(BF16), wo_ref[0:RET_WIDTH, :]) + _dot(att_h, wo_ref[RET_WIDTH:, :])
    x1 = x + mod_ref[:, 2 * d:3 * d] * (_rms(mixed) * gpm_ref[...])
    h = ((_rms(x1) * gpf_ref[...]) * (1.0 + mod_ref[:, 4 * d:5 * d]) + mod_ref[:, 3 * d:4 * d]).astype(BF16)
    fc = 1024
    f = None
    for c in range(D_FF // fc):
        u = jnp.maximum(_dot(h, wu_ref[:, c * fc:(c + 1) * fc]), 0.0)
        piece = next(filler, None)
        if piece is not None:
            u = jnp.concatenate([u[:, 0:LANES] + _tied_zero([piece], u.shape[0]), u[:, LANES:]], axis=1)
        part = _dot((u * u).astype(BF16), wd_ref[c * fc:(c + 1) * fc, :])
        f = part if f is None else f + part
    for _ in filler:
        pass
    y_ref[...] = x1 + mod_ref[:, 5 * d:6 * d] * (_rms(f) * gqf_ref[...])


def _tail_kernel(x_ref, ret_ref, att_ref, *rest):
    _tail_body(x_ref, ret_ref, att_ref[...].astype(BF16), *rest)


def _tail_merge_kernel(x_ref, ret_ref, pv1_ref, ml1_ref, pv4_ref, ml4_ref, pv16_ref, ml16_ref, mod_ref,
                       gpm_ref, gpf_ref, gqf_ref, wo_ref, wu_ref, wd_ref, y_ref, tok, s4, att):
    t = pl.program_id(0)
    merge = functools.partial(_merge_patterns, pv1_ref, ml1_ref, pv4_ref, ml4_ref, pv16_ref, ml16_ref,
                              tok, s4, att)

    @pl.when(t == 0)
    def _():
        for _ in merge():
            pass

    @pl.when(t > 0)
    def _():
        att_h = att[...]
        _tail_body(x_ref, ret_ref, att_h, mod_ref, gpm_ref, gpf_ref, gqf_ref, wo_ref, wu_ref, wd_ref, y_ref,
                   filler=merge())


def _tail(x, ret_h, att, mod, g_post_mix, g_pre_ffn, g_post_ffn, wo_bf, wu_bf, wd_bf, *, tm):
    b, s, d = x.shape
    nt = s // tm
    per_row = mod.shape[1] != 1
    weights = [_const_spec((1, d)), _const_spec((1, d)), _const_spec((1, d)),
               _const_spec((d, d)), _const_spec((d, D_FF)), _const_spec((D_FF, d))]
    weight_args = (g_post_mix.reshape(1, d), g_pre_ffn.reshape(1, d), g_post_ffn.reshape(1, d),
                   wo_bf, wu_bf, wd_bf)
    out_shape = jax.ShapeDtypeStruct((b, s, d), F32)
    if not isinstance(att, (tuple, list)):
        mod_spec = pl.BlockSpec((None, tm if per_row else 1, 6 * d),
                                (lambda bi, i: (bi, i, 0)) if per_row else (lambda bi, i: (bi, 0, 0)))
        tok = lambda w: pl.BlockSpec((None, tm, w), lambda bi, i: (bi, i, 0))
        return pl.pallas_call(
            _tail_kernel,
            grid=(b, nt),
            in_specs=[tok(d), tok(RET_WIDTH), tok(ATT_WIDTH), mod_spec, *weights],
            out_specs=tok(d),
            out_shape=out_shape,
            compiler_params=_params(("arbitrary", "arbitrary")),
            name="out_proj_mlp",
        )(x, ret_h, att, mod, *weight_args)

    assert not per_row
    last = b * nt - 1
    cur = lambda t: jnp.maximum(t - 1, 0)
    nxt = lambda t: jnp.minimum(t, last)
    tok = lambda w: pl.BlockSpec((None, tm, w), lambda t: (cur(t) // nt, cur(t) % nt, 0))
    mod_spec = pl.BlockSpec((None, 1, 6 * d), lambda t: (cur(t) // nt, 0, 0))
    split = lambda r: pl.BlockSpec((None, r, tm // r, ATT_WIDTH), lambda t: (nxt(t) // nt, 0, nxt(t) % nt, 0))
    one = pl.BlockSpec((None, None, tm, ATT_WIDTH), lambda t: (nxt(t) // nt, 0, nxt(t) % nt, 0))
    nch = ATT_WIDTH // LANES
    return pl.pallas_call(
        _tail_merge_kernel,
        grid=(b * nt + 1,),
        in_specs=[tok(d), tok(RET_WIDTH), one, one, split(4), split(4), split(16), split(16), mod_spec, *weights],
        out_specs=tok(d),
        out_shape=out_shape,
        scratch_shapes=[pltpu.VMEM((4, nch, tm, LANES), F32), pltpu.VMEM((4, tm // 4, LANES), F32),
                        pltpu.VMEM((tm, ATT_WIDTH), BF16)],
        compiler_params=_params(("arbitrary",)),
        name="out_proj_mlp",
    )(x, ret_h, *att, mod, *weight_args)


def _ret_sample_kernel(q_ref, k_ref, v_ref, g_ref, gain_ref, dm_ref, qd_ref, kd_ref, cd_ref, s_ref,
                       o_ref, sn_ref, o_scr, qd_scr, kt_scr, *, nb, ln):
    q = q_ref[...].astype(F32)
    k = k_ref[...].astype(F32)
    v = v_ref[...].astype(BF16)
    a = _dot_nt(q.astype(BF16), k.astype(BF16)) * dm_ref[...]
    o_scr[...] = _dot(a.astype(BF16), v)
    qd_scr[...] = q * qd_ref[...]
    kt_scr[...] = (k * kd_ref[...]).T
    c_dec = cd_ref[0:1, :]
    col = lax.broadcasted_iota(jnp.int32, kt_scr.shape, 1)

    def body(bi, carry):
        rows = pl.ds(pl.multiple_of(bi * ln, ln), ln)
        s0 = s_ref[bi]
        o_scr[rows, :] += _dot(qd_scr[rows, :].astype(BF16), s0.astype(BF16))
        mine = (col >= bi * ln) & (col < (bi + 1) * ln)
        kt = jnp.where(mine, kt_scr[...], 0.0).astype(BF16)
        sn_ref[bi] = s0 * c_dec + _dot(kt, v)
        return carry

    lax.fori_loop(0, nb, body, 0, unroll=math.gcd(nb, 8))
    o_ref[...] = _ret_readout(o_scr[...], g_ref[...], gain_ref[...]).astype(o_ref.dtype)


def _retention_sample(rq, rk, rv, rg, ret_gain, state, *, nb, ln):
    n = nb * ln
    idx = np.arange(n)
    same = (idx[:, None] // ln == idx[None, :] // ln) & (idx[:, None] >= idx[None, :])
    diff = np.maximum(idx[:, None] - idx[None, :], 0).astype(np.float64)
    step = (idx % ln).astype(np.float64)
    lg = np.asarray(LOG_G, np.float64)
    dm = np.where(same[None], np.exp(lg[:, None, None] * diff[None]), 0.0)
    qd = np.broadcast_to(np.exp(lg[:, None] * (step + 1.0))[:, :, None], (RET_HEADS, n, RET_DK))
    kd = np.broadcast_to(np.exp(lg[:, None] * (ln - 1.0 - step))[:, :, None], (RET_HEADS, n, RET_DK))
    cd = np.broadcast_to(np.exp(lg * ln)[:, None, None], (RET_HEADS, 8, RET_DK))
    tab = lambda t: jnp.asarray(np.ascontiguousarray(t), F32)
    col = pl.BlockSpec((n, RET_DK), lambda h: (0, h))
    per_head = lambda r, c: pl.BlockSpec((None, r, c), lambda h: (h, 0, 0))
    st = pl.BlockSpec((nb, None, RET_DK, RET_DK), lambda h: (0, h, 0, 0))
    return pl.pallas_call(
        functools.partial(_ret_sample_kernel, nb=nb, ln=ln),
        grid=(RET_HEADS,),
        in_specs=[col, col, col, col, pl.BlockSpec((1, RET_DK), lambda h: (0, h)),
                  per_head(n, n), per_head(n, RET_DK), per_head(n, RET_DK), per_head(8, RET_DK), st],
        out_specs=[col, st],
        out_shape=[jax.ShapeDtypeStruct((n, RET_WIDTH), BF16),
                   jax.ShapeDtypeStruct((nb, RET_HEADS, RET_DK, RET_DK), F32)],
        scratch_shapes=[pltpu.VMEM((n, RET_DK), F32), pltpu.VMEM((n, RET_DK), F32),
                        pltpu.VMEM((RET_DK, n), F32)],
        compiler_params=_params(("arbitrary",)),
        name="retention_sample",
    )(rq, rk, rv, rg, ret_gain.reshape(1, RET_WIDTH), tab(dm), tab(qd), tab(kd), tab(cd), state)


def _shift_window(old_ref, new_t, out_ref, *, ln, wbuf):
    lane = lax.broadcasted_iota(jnp.int32, new_t.shape, 1)
    ncol = wbuf // LANES
    rolled = pltpu.roll(old_ref[:, 0:LANES], LANES - ln, 1)
    for c in range(ncol):
        nxt = pltpu.roll(old_ref[:, (c + 1) * LANES:(c + 2) * LANES] if c + 1 < ncol else new_t, LANES - ln, 1)
        out_ref[:, c * LANES:(c + 1) * LANES] = jnp.where(lane < LANES - ln, rolled, nxt)
        rolled = nxt


def _new_rows_minor(new_ref, ln):
    pad = jnp.zeros((LANES - ln, new_ref.shape[1]), F32)
    return jnp.concatenate([new_ref[...], pad], axis=0).T


def _sample_heads(q_ref, kn_ref, vn_ref, ck_ref, cv_ref, cc_ref, cn_ref, o_ref, ko_ref, vo_ref, *, ln, wbuf):
    kn_t = _new_rows_minor(kn_ref, ln)
    vn_t = _new_rows_minor(vn_ref, ln)
    _shift_window(ck_ref, kn_t, ko_ref, ln=ln, wbuf=wbuf)
    _shift_window(cv_ref, vn_t, vo_ref, ln=ln, wbuf=wbuf)
    q = q_ref[...]
    nh = q.shape[1] // ATT_HD
    lane = lax.broadcasted_iota(jnp.int32, q.shape, 1)
    heads = [lane // ATT_HD == h for h in range(nh)]
    qm = jnp.concatenate([jnp.where(hm, q, 0.0) for hm in heads], axis=0).astype(BF16)
    cnt_c = jnp.concatenate([cc_ref[...]] * nh, axis=0)
    cnt_n = jnp.concatenate([cn_ref[...]] * nh, axis=0)
    s_c = jnp.where(cnt_c > 0, _dot(qm, ck_ref[...].astype(BF16)), NEG_INF)
    s_n = jnp.where(cnt_n > 0, _dot(qm, kn_t.astype(BF16)), NEG_INF)
    m = jnp.maximum(jnp.max(s_c, axis=-1, keepdims=True), jnp.max(s_n, axis=-1, keepdims=True))
    p_c = cnt_c * jnp.exp2(s_c - m)
    p_n = cnt_n * jnp.exp2(s_n - m)
    l = jnp.sum(p_c, axis=-1, keepdims=True) + jnp.sum(p_n, axis=-1, keepdims=True)
    o = _dot_nt(p_c.astype(BF16), cv_ref[...].astype(BF16)) + _dot_nt(p_n.astype(BF16), vn_t.astype(BF16))
    o = o / l
    acc = jnp.zeros_like(q)
    for h, hm in enumerate(heads):
        acc = acc + jnp.where(hm, o[h * ln:(h + 1) * ln, :], 0.0)
    o_ref[...] = acc.astype(o_ref.dtype)


def _pattern_counts(ln, wbuf):
    cnt = np.zeros((ln, wbuf + ln), np.float32)
    for dil in DILATIONS:
        for l in range(ln):
            for j in range(SPAN + 1):
                row = wbuf + l - dil * j
                if row >= 0:
                    cnt[l, row] += 1.0
    return cnt


def _count_tables(ln, wbuf):
    cnt = _pattern_counts(ln, wbuf)
    cnt_new = np.zeros((ln, LANES), np.float32)
    cnt_new[:, :ln] = cnt[:, wbuf:]
    return jnp.asarray(cnt[:, :wbuf]), jnp.asarray(cnt_new)


def _step(x_prompt, x_sample, c_prompt, c_sample, state_ret, cache_win_k, cache_win_v, w_ada, b_ada,
          g_pre_mix, g_post_mix, g_pre_ffn, g_post_ffn, w_in, ret_gain, w_o, w_up, w_down,
          *, tm, tc, tq):
    assert w_in.shape[0] == 1, "single-layer step"
    bp, sp, d = x_prompt.shape
    nb, ln, _ = x_sample.shape
    wbuf = cache_win_k.shape[2]
    n_s = nb * ln

    w_in_f32 = w_in[0]

    rows = bp + nb
    pad = (-rows) % 8
    c_all = jnp.concatenate([c_prompt, c_sample, jnp.zeros((pad, d), F32)], axis=0)
    mod = _modulation(c_all, w_ada[0], b_ada[0])
    mod_p = mod[:bp].reshape(bp, 1, 6 * d)
    mod_s = jnp.repeat(mod[bp:rows], ln, axis=0).reshape(1, n_s, 6 * d)

    tabs_p = _rotation_constants(np.arange(tm), np.arange(sp // tm) * tm)
    tabs_s = _rotation_constants(np.tile(np.arange(ln), nb), [PAST_LEN])

    keep = min(MAX_WINDOW, sp)
    (rq, rk, rv, rg, aq, ak, av, akf, avf, aq4, ak4, av4, aq16, ak16, av16) = _project(
        x_prompt, mod_p, g_pre_mix[0], w_in_f32, tabs_p, tm=tm, keep=keep, act_dtype=BF16, regroup=True)
    xs = x_sample.reshape(1, n_s, d)
    srq, srk, srv, srg, saq, _, _, sakf, savf = _project(
        xs, mod_s, g_pre_mix[0], w_in_f32, tabs_s, tm=n_s, keep=n_s, act_dtype=F32, regroup=False)
    flat = lambda t: t.reshape(n_s, GROUP_W)

    to_minor = lambda t: jnp.transpose(t[0], (0, 2, 3, 1)).reshape(nb, ATT_WIDTH, wbuf)
    from_minor = lambda t: jnp.transpose(t.reshape(nb, ATT_HEADS, ATT_HD, wbuf), (0, 3, 1, 2))[None]
    sample = (flat(saq), flat(sakf), flat(savf), to_minor(cache_win_k), to_minor(cache_win_v),
              *_count_tables(ln, wbuf))
    patterns = ((aq16, ak16, av16), (aq4, ak4, av4), (aq[:, None], ak[:, None], av[:, None]))
    steps = bp * sp // tq
    hosted, first = {}, 0
    for n in (1, 2, 0):
        hosted[n] = (first, min(steps, nb - first))
        first += hosted[n][1]
    assert first == nb
    carry, merged_in = None, []
    for n, qkv in enumerate(patterns):
        m_len = qkv[0].shape[2]
        out = _dilated_attention(*qkv, sample, carry, tq=min(tq, m_len), groups=max(1, tq // m_len),
                                 unit0=hosted[n][0], nunit=hosted[n][1])
        merged_in = list(out[:2]) + merged_in
        carry = out[2:] or carry
    satt_h, k_out, v_out = carry

    ret_h, s_fin, (wo_bf, wu_bf, wd_bf) = _retention_prompt(
        rq, rk, rv, rg, ret_gain[0], (w_o[0], w_up[0], w_down[0]), tc=tc, chunk=RET_BLOCK)
    y_prompt = _tail(x_prompt, ret_h, tuple(merged_in), mod_p, g_post_mix[0], g_pre_ffn[0],
                     g_post_ffn[0], wo_bf, wu_bf, wd_bf, tm=tm)

    sret_h, s_new = _retention_sample(flat(srq), flat(srk), flat(srv), flat(srg), ret_gain[0],
                                      state_ret[0], nb=nb, ln=ln)
    y_sample = _tail(xs, sret_h.reshape(1, n_s, RET_WIDTH), satt_h.reshape(1, n_s, ATT_WIDTH), mod_s,
                     g_post_mix[0], g_pre_ffn[0], g_post_ffn[0], wo_bf, wu_bf, wd_bf, tm=n_s)

    cache_shape = (1, -1, keep, ATT_HEADS, ATT_HD)
    return (y_prompt,
            y_sample.reshape(nb, ln, d),
            s_fin[None],
            akf.reshape(cache_shape),
            avf.reshape(cache_shape),
            s_new[None],
            from_minor(k_out),
            from_minor(v_out))


def kernel(x_prompt, x_sample, c_prompt, c_sample, state_ret, cache_win_k, cache_win_v, w_ada, b_ada,
           g_pre_mix, g_post_mix, g_pre_ffn, g_post_ffn, w_in, ret_gain, w_o, w_up, w_down):
    return _step(x_prompt, x_sample, c_prompt, c_sample, state_ret, cache_win_k, cache_win_v, w_ada, b_ada,
                 g_pre_mix, g_post_mix, g_pre_ffn, g_post_ffn, w_in, ret_gain, w_o, w_up, w_down,
                 tm=512, tc=1024, tq=1024)
```

```python
import functools
import itertools
import math

import numpy as np
import jax
import jax.numpy as jnp
from jax import lax
from jax.experimental import pallas as pl
from jax.experimental.pallas import tpu as pltpu

F32 = jnp.float32
BF16 = jnp.bfloat16

D_MODEL = 1024
RET_HEADS = 4
RET_WIDTH = 512
RET_DK = 128
RET_THETA = 10000.0
RET_BLOCK = 256
ATT_HD = 64
ATT_HEADS = 8
ATT_WIDTH = 512
DILATIONS = (1, 4, 16)
MAX_WINDOW = 2048
SPAN = 128
DIL_BLOCK = 128
ROPE_THETA = 10000.0
D_FF = 4096
NORM_EPS = 1e-6
NEG_INF = -1e30
PAST_LEN = 16384
N_GROUPS = 7
GROUP_W = 512
LANES = 128
VMEM_LIMIT = 56 * 1024 * 1024

LOG2_E = math.log2(math.e)
LOG_G = tuple(math.log1p(-(2.0 ** (-5.0 - h))) for h in range(RET_HEADS))


def _dot(a, b):
    return jnp.dot(a, b, preferred_element_type=F32)


def _dot_nt(a, b):
    return lax.dot_general(a, b, (((1,), (1,)), ((), ())), preferred_element_type=F32)


def _dot_tn(a, b):
    return lax.dot_general(a, b, (((0,), (0,)), ((), ())), preferred_element_type=F32)


def _rms(x):
    return x * lax.rsqrt(jnp.mean(x * x, axis=-1, keepdims=True) + NORM_EPS)


def _silu(x):
    return x / (1.0 + jnp.exp(-x))


def _tied_zero(values, rows):
    bits = None
    for v in values:
        assert v.shape[0] % rows == 0 and v.shape[1] % LANES == 0
        for r0 in range(0, v.shape[0], rows):
            for c0 in range(0, v.shape[1], LANES):
                b = pltpu.bitcast(v[r0:r0 + rows, c0:c0 + LANES], jnp.uint32)
                bits = b if bits is None else bits | b
    sixteen = jnp.full(bits.shape, 16, jnp.uint32)
    return pltpu.bitcast(lax.shift_right_logical(lax.shift_right_logical(bits, sixteen), sixteen), F32)


def _params(sem):
    return pltpu.CompilerParams(dimension_semantics=sem, vmem_limit_bytes=VMEM_LIMIT)


def _const_spec(shape):
    nd = len(shape)
    return pl.BlockSpec(shape, lambda *_: (0,) * nd, pipeline_mode=pl.Buffered(1))


def _mod_kernel(c_ref, w_ref, b_ref, o_ref):
    a = _silu(c_ref[...]).astype(BF16)
    o_ref[...] = _dot(a, w_ref[...].astype(BF16)) + b_ref[...]


def _modulation(c, w_ada, b_ada):
    rows, d = c.shape
    n = w_ada.shape[1]
    tn = 1536
    return pl.pallas_call(
        _mod_kernel,
        grid=(n // tn,),
        in_specs=[pl.BlockSpec((rows, d), lambda j: (0, 0)),
                  pl.BlockSpec((d, tn), lambda j: (0, j)),
                  pl.BlockSpec((1, tn), lambda j: (0, j))],
        out_specs=pl.BlockSpec((rows, tn), lambda j: (0, j)),
        out_shape=jax.ShapeDtypeStruct((rows, n), F32),
        compiler_params=_params(("arbitrary",)),
        name="adaln_mod",
    )(c, w_ada, b_ada.reshape(1, n))


def _rotate_pairs(x, cos, sin):
    lane = lax.broadcasted_iota(jnp.int32, x.shape, 1)
    partner = jnp.where(lane % 2 == 0, pltpu.roll(x, LANES - 1, 1), pltpu.roll(x, 1, 1))
    return x * cos + partner * sin


def _rotate_half(x, cos, sin):
    lane = lax.broadcasted_iota(jnp.int32, x.shape, 1)
    half = ATT_HD // 2
    partner = jnp.where(lane % ATT_HD < half, pltpu.roll(x, LANES - half, 1), pltpu.roll(x, half, 1))
    return x * cos + partner * sin


def _store_regrouped(val, c, tok_ref, r4_ref, r16_ref, zs, s4):
    sl = slice(c * LANES, (c + 1) * LANES)
    tok_ref[:, sl] = val.astype(tok_ref.dtype)
    tm = val.shape[0]
    zs[c] = val
    for q in range(4):
        g4 = zs[c, pl.ds(q, tm // 4, stride=4), :]
        r4_ref[q, :, sl] = g4.astype(r4_ref.dtype)
        s4[c, q] = g4
    for q in range(4):
        for a in range(4):
            r16_ref[4 * a + q, :, sl] = s4[c, q, pl.ds(a, tm // 16, stride=4), :].astype(r16_ref.dtype)


def _proj_kernel(x_ref, mod_ref, g_ref, wf_ref, rows_ref, tile_ref, *refs, regroup):
    if regroup:
        (rq_ref, rk_ref, rv_ref, rg_ref, aq_ref, ak_ref, av_ref, akf_ref, avf_ref,
         aq4_ref, ak4_ref, av4_ref, aq16_ref, ak16_ref, av16_ref, w_ref, zs, s4) = refs
    else:
        rq_ref, rk_ref, rv_ref, rg_ref, aq_ref, ak_ref, av_ref, akf_ref, avf_ref, w_ref = refs

    @pl.when(jnp.logical_and(pl.program_id(0) == 0, pl.program_id(1) == 0))
    def _():
        for gi in range(N_GROUPS):
            sl = slice(gi * GROUP_W, (gi + 1) * GROUP_W)
            w_ref[:, sl] = wf_ref[:, sl].astype(w_ref.dtype)

    d = D_MODEL
    x = x_ref[...]
    h = (_rms(x) * g_ref[...]) * (1.0 + mod_ref[:, d:2 * d]) + mod_ref[:, 0:d]
    h = h.astype(BF16)
    t = tile_ref[...]
    cr = t[0:1] * rows_ref[0] - t[1:2] * rows_ref[1]
    sr = (t[1:2] * rows_ref[0] + t[0:1] * rows_ref[1]) * t[4:5]
    ca = t[2:3] * rows_ref[2] - t[3:4] * rows_ref[3]
    sa = (t[3:4] * rows_ref[2] + t[2:3] * rows_ref[3]) * t[5:6]
    nch = GROUP_W // LANES

    def group(gi):
        return _dot(h, w_ref[:, gi * GROUP_W:(gi + 1) * GROUP_W])

    z = group(0)
    for c in range(nch):
        sl = slice(c * LANES, (c + 1) * LANES)
        rq_ref[:, sl] = _rotate_pairs(z[:, sl], cr, sr).astype(rq_ref.dtype)
    z = group(1)
    for c in range(nch):
        sl = slice(c * LANES, (c + 1) * LANES)
        rk_ref[:, sl] = (_rotate_pairs(z[:, sl], cr, sr) * (RET_DK ** -0.5)).astype(rk_ref.dtype)
    rv_ref[...] = group(2).astype(rv_ref.dtype)
    rg_ref[...] = group(3)
    def emit(val, c, tok_ref, r4_ref, r16_ref):
        if regroup:
            _store_regrouped(val, c, tok_ref, r4_ref, r16_ref, zs, s4)
        else:
            tok_ref[:, c * LANES:(c + 1) * LANES] = val.astype(tok_ref.dtype)

    r4 = (aq4_ref, ak4_ref, av4_ref) if regroup else (None,) * 3
    r16 = (aq16_ref, ak16_ref, av16_ref) if regroup else (None,) * 3
    z = group(4)
    for c in range(nch):
        sl = slice(c * LANES, (c + 1) * LANES)
        emit(_rotate_half(z[:, sl], ca, sa) * (ATT_HD ** -0.5 * LOG2_E), c, aq_ref, r4[0], r16[0])
    z = group(5)
    for c in range(nch):
        sl = slice(c * LANES, (c + 1) * LANES)
        r = _rotate_half(z[:, sl], ca, sa)
        akf_ref[:, sl] = r
        emit(r, c, ak_ref, r4[1], r16[1])
    z = group(6)
    avf_ref[...] = z
    for c in range(nch):
        emit(z[:, c * LANES:(c + 1) * LANES], c, av_ref, r4[2], r16[2])


def _project(x, mod, g_pre, w_in_f32, tabs, *, tm, keep, act_dtype, regroup):
    b, s, d = x.shape
    nt = s // tm
    rows_tab, tile_tab = tabs
    assert rows_tab.shape == (4, tm, LANES) and tile_tab.shape == (nt, 8, LANES)
    mod_rows = mod.shape[1]
    per_row = mod_rows != 1
    first_keep = (s - keep) // tm
    tok = pl.BlockSpec((None, tm, GROUP_W), lambda bi, i: (bi, i, 0))
    keep_spec = pl.BlockSpec((None, tm, GROUP_W), lambda bi, i: (bi, jnp.maximum(i - first_keep, 0), 0))
    tile_spec = pl.BlockSpec((None, 8, LANES), lambda bi, i: (i, 0, 0))
    mod_spec = pl.BlockSpec((None, tm if per_row else 1, 6 * d),
                            (lambda bi, i: (bi, i, 0)) if per_row else (lambda bi, i: (bi, 0, 0)))
    act = jax.ShapeDtypeStruct((b, s, GROUP_W), act_dtype)
    full = jax.ShapeDtypeStruct((b, s, GROUP_W), F32)
    kept = jax.ShapeDtypeStruct((b, keep, GROUP_W), F32)
    out_specs = [tok, tok, tok, tok, tok, tok, tok, keep_spec, keep_spec]
    out_shape = [act, act, act, full, act, act, act, kept, kept]
    scratch = [pltpu.VMEM((d, N_GROUPS * GROUP_W), BF16)]
    if regroup:
        for r in (4, 16):
            out_specs += [pl.BlockSpec((None, r, tm // r, GROUP_W), lambda bi, i: (bi, 0, i, 0))] * 3
            out_shape += [jax.ShapeDtypeStruct((b, r, s // r, GROUP_W), act_dtype)] * 3
        nch = GROUP_W // LANES
        scratch += [pltpu.VMEM((nch, tm, LANES), F32), pltpu.VMEM((nch, 4, tm // 4, LANES), F32)]
    return pl.pallas_call(
        functools.partial(_proj_kernel, regroup=regroup),
        grid=(b, nt),
        in_specs=[pl.BlockSpec((None, tm, d), lambda bi, i: (bi, i, 0)),
                  mod_spec,
                  _const_spec((1, d)),
                  _const_spec((d, N_GROUPS * GROUP_W)),
                  _const_spec((4, tm, LANES)), tile_spec],
        out_specs=out_specs,
        out_shape=out_shape,
        scratch_shapes=scratch,
        compiler_params=_params(("arbitrary", "arbitrary")),
        name="in_proj",
    )(x, mod, g_pre.reshape(1, d), w_in_f32, rows_tab, tile_tab)


def _rotation_constants(row_pos, tile_pos):
    lane = np.arange(LANES)
    inv_r = (1.0 / RET_THETA ** np.linspace(0.0, 1.0, RET_DK // 2))[lane // 2]
    inv_a = (1.0 / ROPE_THETA ** (np.arange(0, ATT_HD, 2) / ATT_HD))[lane % (ATT_HD // 2)]
    rp = np.asarray(row_pos, np.float64)[:, None]
    tp = np.asarray(tile_pos, np.float64)[:, None]
    rows = np.stack([np.cos(rp * inv_r), np.sin(rp * inv_r), np.cos(rp * inv_a), np.sin(rp * inv_a)])
    sign_r = np.where(lane % 2 == 0, -1.0, 1.0)
    sign_a = np.where(lane % ATT_HD < ATT_HD // 2, -1.0, 1.0)
    ones = np.ones_like(tp * inv_r)
    tiles = np.stack([np.cos(tp * inv_r), np.sin(tp * inv_r), np.cos(tp * inv_a), np.sin(tp * inv_a),
                      ones * sign_r, ones * sign_a, 0 * ones, 0 * ones], axis=1)
    return jnp.asarray(rows, F32), jnp.asarray(tiles, F32)


def _ret_readout(o, g, gain):
    return (_rms(o) * gain) * _silu(g)


def _ret_prompt_kernel(q_ref, k_ref, v_ref, g_ref, gain_ref, *rest, nchunk, c_len, ncast):
    cast_in, (o_ref, sfin_ref), cast_out, s_scr = rest[:ncast], rest[ncast:ncast + 2], rest[ncast + 2:-1], rest[-1]
    for src, dst in zip(cast_in, cast_out):
        dst[...] = src[...].astype(dst.dtype)
    i = pl.program_id(1)

    @pl.when(i == 0)
    def _():
        s_scr[...] = jnp.zeros_like(s_scr)

    row = lax.broadcasted_iota(jnp.int32, (c_len, c_len), 0)
    col = lax.broadcasted_iota(jnp.int32, (c_len, c_len), 1)
    diff = (row - col).astype(F32)
    ridx = lax.broadcasted_iota(jnp.int32, (c_len, 1), 0).astype(F32)
    for h in range(RET_HEADS):
        lg = LOG_G[h]
        intra = jnp.where(diff >= 0, jnp.exp(lg * jnp.maximum(diff, 0.0)), 0.0)
        q_dec = jnp.exp(lg * (ridx + 1.0))
        k_dec = jnp.exp(lg * (c_len - 1.0 - ridx))
        c_dec = math.exp(lg * c_len)
        hs = slice(h * RET_DK, (h + 1) * RET_DK)
        state = s_scr[h]
        for c in range(nchunk):
            rs = slice(c * c_len, (c + 1) * c_len)
            q = q_ref[rs, hs]
            k = k_ref[rs, hs]
            v = v_ref[rs, hs]
            a = _dot_nt(q, k) * intra
            lhs = jnp.concatenate([a.astype(BF16), (q.astype(F32) * q_dec).astype(BF16)], axis=1)
            o = _dot(lhs, jnp.concatenate([v, state.astype(BF16)], axis=0))
            kd = (k.astype(F32) * k_dec).astype(BF16)
            state = state * c_dec + _dot_tn(kd, v)
            o_ref[rs, hs] = _ret_readout(o, g_ref[rs, hs], gain_ref[:, hs]).astype(o_ref.dtype)
        s_scr[h] = state

    @pl.when(i == pl.num_programs(1) - 1)
    def _():
        sfin_ref[...] = s_scr[...]


def _retention_prompt(rq, rk, rv, rg, ret_gain, weights, *, tc, chunk):
    b, s, w = rq.shape
    nblk = s // tc
    steps = b * nblk
    tok = pl.BlockSpec((None, tc, w), lambda bi, i: (bi, i, 0))
    slab = lambda m: pl.BlockSpec((m.shape[0] // steps, m.shape[1]), lambda bi, i: (bi * nblk + i, 0))
    assert all(m.shape[0] % (16 * steps) == 0 for m in weights)
    out = pl.pallas_call(
        functools.partial(_ret_prompt_kernel, nchunk=tc // chunk, c_len=chunk, ncast=len(weights)),
        grid=(b, nblk),
        in_specs=[tok, tok, tok, tok, _const_spec((1, w))] + [slab(m) for m in weights],
        out_specs=[tok, pl.BlockSpec((None, RET_HEADS, RET_DK, RET_DK), lambda bi, i: (bi, 0, 0, 0))]
        + [slab(m) for m in weights],
        out_shape=[jax.ShapeDtypeStruct((b, s, w), BF16),
                   jax.ShapeDtypeStruct((b, RET_HEADS, RET_DK, RET_DK), F32)]
        + [jax.ShapeDtypeStruct(m.shape, BF16) for m in weights],
        scratch_shapes=[pltpu.VMEM((RET_HEADS, RET_DK, RET_DK), F32)],
        compiler_params=_params(("arbitrary", "arbitrary")),
        name="retention_prompt",
    )(rq, rk, rv, rg, ret_gain.reshape(1, w), *weights)
    return out[0], out[1], out[2:]


def _att_host_kernel(q_ref, k_ref, v_ref, kp_ref, vp_ref, oh_ref, bias_ref, gq_ref, gkn_ref, gvn_ref, gck_ref,
                     gcv_ref, gcc_ref, gcn_ref, *rest, nsub, ln, wbuf, nunit):
    pv_ref, ml_ref, go_ref, gko_ref, gvo_ref, kcat, vcat = rest[-7:]
    steps = pl.num_programs(0) * pl.num_programs(1) * pl.num_programs(2)
    t = (pl.program_id(0) * pl.num_programs(1) + pl.program_id(1)) * pl.num_programs(2) + pl.program_id(2)
    fresh = jnp.logical_or(t == 0, _hosted_unit(t, 0, nunit, steps) != _hosted_unit(t - 1, 0, nunit, steps))

    @pl.when(fresh)
    def _():
        _sample_heads(gq_ref, gkn_ref, gvn_ref, gck_ref, gcv_ref, gcc_ref, gcn_ref, go_ref, gko_ref, gvo_ref,
                      ln=ln, wbuf=wbuf)

    _att_block(q_ref, k_ref, v_ref, kp_ref, vp_ref, oh_ref, bias_ref, pv_ref, ml_ref, kcat, vcat, nsub=nsub)


def _att_block(q_ref, k_ref, v_ref, kp_ref, vp_ref, oh_ref, bias_ref, pv_ref, ml_ref, kcat, vcat, *, nsub):
    i = pl.program_id(2)
    blk = DIL_BLOCK
    for g in range(q_ref.shape[0]):
        kcat[g, 0:blk, :] = kp_ref[g]
        kcat[g, blk:, :] = k_ref[g]
        vcat[g, 0:blk, :] = vp_ref[g]
        vcat[g, blk:, :] = v_ref[g]
    lane = lax.broadcasted_iota(jnp.int32, (blk, LANES), 1)
    low = lane < ATT_HD
    quarter = lane // (ATT_HD // 2)
    onehot = oh_ref[...]
    bias_rest = bias_ref[1]
    bias_first = jnp.where(i == 0, bias_ref[0], bias_rest)
    ones = jnp.ones((2 * blk, LANES), BF16)
    low2 = lax.broadcasted_iota(jnp.int32, (2 * blk, LANES), 1) < ATT_HD
    for g, j, hp in itertools.product(range(q_ref.shape[0]), range(nsub), range(ATT_WIDTH // LANES)):
        bias = bias_first if j == 0 else bias_rest
        rows = slice(j * blk, (j + 1) * blk)
        krows = slice(j * blk, (j + 2) * blk)
        cols = slice(hp * LANES, (hp + 1) * LANES)
        q = q_ref[g, rows, cols]
        zero = jnp.zeros_like(q)
        q2 = jnp.concatenate([jnp.where(low, q, zero), jnp.where(low, zero, q)], axis=0)
        s = _dot_nt(jnp.concatenate([q2, onehot], axis=1), jnp.concatenate([kcat[g, krows, cols], bias], axis=1))
        m = jnp.max(s, axis=-1, keepdims=True)
        p = jnp.exp2(s - m).astype(BF16)
        vv = vcat[g, krows, cols]
        pv0 = _dot(p[0:blk], jnp.where(low2, vv, ones))
        pv1 = _dot(p[blk:], jnp.where(low2, ones, vv))
        pv_ref[g, rows, cols] = jnp.where(low, pv0, pv1).astype(pv_ref.dtype)
        ml_ref[g, rows, cols] = jnp.where(
            quarter == 0, m[0:blk], jnp.where(quarter == 1, pv1, jnp.where(quarter == 2, m[blk:], pv0)))


def _band_tables():
    jj = np.arange(2 * DIL_BLOCK)[:, None]
    qi = np.arange(DIL_BLOCK)[None, :]
    bias = np.stack([np.where((jj >= np.maximum(qi, fk)) & (jj <= qi + SPAN), 0.0, NEG_INF)
                     for fk in (DIL_BLOCK, 0)])
    onehot = np.tile(np.eye(DIL_BLOCK), (2, 1))
    return jnp.asarray(onehot, BF16), jnp.asarray(bias, BF16)


def _hosted_unit(t, unit0, nunit, steps):
    return unit0 + (t * nunit) // steps


def _dilated_attention(aq, ak, av, sample, carry, *, tq, groups, unit0, nunit):
    b, dil, m_len, w = aq.shape
    nsub = tq // DIL_BLOCK
    nblk = m_len // tq
    ngrp = dil // groups
    own = pl.BlockSpec((None, groups, tq, w), lambda bi, r, i: (bi, r, i, 0))
    prv = pl.BlockSpec((None, groups, DIL_BLOCK, w), lambda bi, r, i: (bi, r, jnp.maximum(i * nsub - 1, 0), 0))
    onehot, bias = _band_tables()
    in_specs = [own, own, own, prv, prv, _const_spec(onehot.shape), _const_spec(bias.shape)]
    args = [aq, ak, av, ak, av, onehot, bias]
    out_shape = [jax.ShapeDtypeStruct((b, dil, m_len, w), BF16), jax.ShapeDtypeStruct((b, dil, m_len, w), F32)]
    scratch = [pltpu.VMEM((groups, tq + DIL_BLOCK, w), BF16), pltpu.VMEM((groups, tq + DIL_BLOCK, w), BF16)]
    common = dict(grid=(b, ngrp, nblk), scratch_shapes=scratch, name=f"dilated_attn_d{dil}",
                  compiler_params=_params(("arbitrary", "arbitrary", "arbitrary")))
    if nunit == 0:
        return pl.pallas_call(functools.partial(_att_block, nsub=nsub), in_specs=in_specs,
                              out_specs=[own, own], out_shape=out_shape, **common)(*args)

    gq, gkn, gvn, gck, gcv, cnt_c, cnt_n = sample
    nb, _, wbuf = gck.shape
    ln = gq.shape[0] // nb
    steps = b * ngrp * nblk
    assert nunit <= steps

    def unit(bi, r, i):
        return _hosted_unit((bi * ngrp + r) * nblk + i, unit0, nunit, steps)

    rows = pl.BlockSpec((ln, w), lambda bi, r, i: (unit(bi, r, i), 0))
    window = pl.BlockSpec((None, w, wbuf), lambda bi, r, i: (unit(bi, r, i), 0, 0))
    in_specs += [rows, rows, rows, window, window, _const_spec(cnt_c.shape), _const_spec(cnt_n.shape)]
    args += [gq, gkn, gvn, gck, gcv, cnt_c, cnt_n]
    aliases = {}
    if carry is not None:
        aliases = {len(args) + n: 2 + n for n in range(3)}
        in_specs += [pl.BlockSpec(memory_space=pl.ANY)] * 3
        args += list(carry)
    out_shape += [jax.ShapeDtypeStruct((nb * ln, w), F32), jax.ShapeDtypeStruct((nb, w, wbuf), F32),
                  jax.ShapeDtypeStruct((nb, w, wbuf), F32)]
    return pl.pallas_call(
        functools.partial(_att_host_kernel, nsub=nsub, ln=ln, wbuf=wbuf, nunit=nunit),
        in_specs=in_specs,
        out_specs=[own, own, rows, window, window],
        out_shape=out_shape,
        input_output_aliases=aliases,
        **common,
    )(*args)


def _merge_patterns(pv1_ref, ml1_ref, pv4_ref, ml4_ref, pv16_ref, ml16_ref, tok, s4, att):
    tm = att.shape[0]
    half = ATT_HD // 2
    lane = lax.broadcasted_iota(jnp.int32, (tm, LANES), 1)
    first = lane % ATT_HD < half

    def unpack(ml):
        near = pltpu.roll(ml, half, 1)
        return jnp.where(first, ml, near), jnp.where(first, near, pltpu.roll(ml, 2 * half, 1))

    for c in range(ATT_WIDTH // LANES):
        sl = slice(c * LANES, (c + 1) * LANES)
        for n, (r4, r16) in enumerate(((pv4_ref, pv16_ref), (ml4_ref, ml16_ref))):
            for q in range(4):
                tok[n, c, pl.ds(q, tm // 4, stride=4), :] = r4[q, :, sl].astype(F32)
                for a in range(4):
                    s4[q, pl.ds(a, tm // 16, stride=4), :] = r16[4 * a + q, :, sl].astype(F32)
            for q in range(4):
                tok[2 + n, c, pl.ds(q, tm // 4, stride=4), :] = s4[q]
        (m1, l1), (m4, l4), (m16, l16) = unpack(ml1_ref[:, sl]), unpack(tok[1, c]), unpack(tok[3, c])
        top = jnp.maximum(m1, jnp.maximum(m4, m16))
        e1, e4, e16 = jnp.exp2(m1 - top), jnp.exp2(m4 - top), jnp.exp2(m16 - top)
        num = e1 * pv1_ref[:, sl].astype(F32) + e4 * tok[0, c] + e16 * tok[2, c]
        den = e1 * l1 + e4 * l4 + e16 * l16
        merged = num / den
        att[:, sl] = merged.astype(att.dtype)
        yield merged


def _tail_body(x_ref, ret_ref, att_h, mod_ref, gpm_ref, gpf_ref, gqf_ref, wo_ref, wu_ref, wd_ref, y_ref,
               filler=iter(())):
    d = D_MODEL
    x = x_ref[...]
    mixed = _dot(ret_ref[...].astype(BF16), wo_ref[0:RET_WIDTH, :]) + _dot(att_h, wo_ref[RET_WIDTH:, :])
    x1 = x + mod_ref[:, 2 * d:3 * d] * (_rms(mixed) * gpm_ref[...])
    h = ((_rms(x1) * gpf_ref[...]) * (1.0 + mod_ref[:, 4 * d:5 * d]) + mod_ref[:, 3 * d:4 * d]).astype(BF16)
    fc = 1024
    f = None
    for c in range(D_FF // fc):
        u = jnp.maximum(_dot(h, wu_ref[:, c * fc:(c + 1) * fc]), 0.0)
        piece = next(filler, None)
        if piece is not None:
            u = jnp.concatenate([u[:, 0:LANES] + _tied_zero([piece], u.shape[0]), u[:, LANES:]], axis=1)
        part = _dot((u * u).astype(BF16), wd_ref[c * fc:(c + 1) * fc, :])
        f = part if f is None else f + part
    for _ in filler:
        pass
    y_ref[...] = x1 + mod_ref[:, 5 * d:6 * d] * (_rms(f) * gqf_ref[...])


def _tail_kernel(x_ref, ret_ref, att_ref, *rest):
    _tail_body(x_ref, ret_ref, att_ref[...].astype(BF16), *rest)


def _tail_merge_kernel(x_ref, ret_ref, pv1_ref, ml1_ref, pv4_ref, ml4_ref, pv16_ref, ml16_ref, mod_ref,
                       gpm_ref, gpf_ref, gqf_ref, wo_ref, wu_ref, wd_ref, y_ref, tok, s4, att):
    t = pl.program_id(0)
    merge = functools.partial(_merge_patterns, pv1_ref, ml1_ref, pv4_ref, ml4_ref, pv16_ref, ml16_ref,
                              tok, s4, att)

    @pl.when(t == 0)
    def _():
        for _ in merge():
            pass

    @pl.when(t > 0)
    def _():
        att_h = att[...]
        _tail_body(x_ref, ret_ref, att_h, mod_ref, gpm_ref, gpf_ref, gqf_ref, wo_ref, wu_ref, wd_ref, y_ref,
                   filler=merge())


def _tail(x, ret_h, att, mod, g_post_mix, g_pre_ffn, g_post_ffn, wo_bf, wu_bf, wd_bf, *, tm):
    b, s, d = x.shape
    nt = s // tm
    per_row = mod.shape[1] != 1
    weights = [_const_spec((1, d)), _const_spec((1, d)), _const_spec((1, d)),
               _const_spec((d, d)), _const_spec((d, D_FF)), _const_spec((D_FF, d))]
    weight_args = (g_post_mix.reshape(1, d), g_pre_ffn.reshape(1, d), g_post_ffn.reshape(1, d),
                   wo_bf, wu_bf, wd_bf)
    out_shape = jax.ShapeDtypeStruct((b, s, d), F32)
    if not isinstance(att, (tuple, list)):
        mod_spec = pl.BlockSpec((None, tm if per_row else 1, 6 * d),
                                (lambda bi, i: (bi, i, 0)) if per_row else (lambda bi, i: (bi, 0, 0)))
        tok = lambda w: pl.BlockSpec((None, tm, w), lambda bi, i: (bi, i, 0))
        return pl.pallas_call(
            _tail_kernel,
            grid=(b, nt),
            in_specs=[tok(d), tok(RET_WIDTH), tok(ATT_WIDTH), mod_spec, *weights],
            out_specs=tok(d),
            out_shape=out_shape,
            compiler_params=_params(("arbitrary", "arbitrary")),
            name="out_proj_mlp",
        )(x, ret_h, att, mod, *weight_args)

    assert not per_row
    last = b * nt - 1
    cur = lambda t: jnp.maximum(t - 1, 0)
    nxt = lambda t: jnp.minimum(t, last)
    tok = lambda w: pl.BlockSpec((None, tm, w), lambda t: (cur(t) // nt, cur(t) % nt, 0))
    mod_spec = pl.BlockSpec((None, 1, 6 * d), lambda t: (cur(t) // nt, 0, 0))
    split = lambda r: pl.BlockSpec((None, r, tm // r, ATT_WIDTH), lambda t: (nxt(t) // nt, 0, nxt(t) % nt, 0))
    one = pl.BlockSpec((None, None, tm, ATT_WIDTH), lambda t: (nxt(t) // nt, 0, nxt(t) % nt, 0))
    nch = ATT_WIDTH // LANES
    return pl.pallas_call(
        _tail_merge_kernel,
        grid=(b * nt + 1,),
        in_specs=[tok(d), tok(RET_WIDTH), one, one, split(4), split(4), split(16), split(16), mod_spec, *weights],
        out_specs=tok(d),
        out_shape=out_shape,
        scratch_shapes=[pltpu.VMEM((4, nch, tm, LANES), F32), pltpu.VMEM((4, tm // 4, LANES), F32),
                        pltpu.VMEM((tm, ATT_WIDTH), BF16)],
        compiler_params=_params(("arbitrary",)),
        name="out_proj_mlp",
    )(x, ret_h, *att, mod, *weight_args)


def _ret_sample_kernel(q_ref, k_ref, v_ref, g_ref, gain_ref, dm_ref, qd_ref, kd_ref, cd_ref, s_ref,
                       o_ref, sn_ref, o_scr, qd_scr, kt_scr, *, nb, ln):
    q = q_ref[...].astype(F32)
    k = k_ref[...].astype(F32)
    v = v_ref[...].astype(BF16)
    a = _dot_nt(q.astype(BF16), k.astype(BF16)) * dm_ref[...]
    o_scr[...] = _dot(a.astype(BF16), v)
    qd_scr[...] = q * qd_ref[...]
    kt_scr[...] = (k * kd_ref[...]).T
    c_dec = cd_ref[0:1, :]
    col = lax.broadcasted_iota(jnp.int32, kt_scr.shape, 1)

    def body(bi, carry):
        rows = pl.ds(pl.multiple_of(bi * ln, ln), ln)
        s0 = s_ref[bi]
        o_scr[rows, :] += _dot(qd_scr[rows, :].astype(BF16), s0.astype(BF16))
        mine = (col >= bi * ln) & (col < (bi + 1) * ln)
        kt = jnp.where(mine, kt_scr[...], 0.0).astype(BF16)
        sn_ref[bi] = s0 * c_dec + _dot(kt, v)
        return carry

    lax.fori_loop(0, nb, body, 0, unroll=math.gcd(nb, 8))
    o_ref[...] = _ret_readout(o_scr[...], g_ref[...], gain_ref[...]).astype(o_ref.dtype)


def _retention_sample(rq, rk, rv, rg, ret_gain, state, *, nb, ln):
    n = nb * ln
    idx = np.arange(n)
    same = (idx[:, None] // ln == idx[None, :] // ln) & (idx[:, None] >= idx[None, :])
    diff = np.maximum(idx[:, None] - idx[None, :], 0).astype(np.float64)
    step = (idx % ln).astype(np.float64)
    lg = np.asarray(LOG_G, np.float64)
    dm = np.where(same[None], np.exp(lg[:, None, None] * diff[None]), 0.0)
    qd = np.broadcast_to(np.exp(lg[:, None] * (step + 1.0))[:, :, None], (RET_HEADS, n, RET_DK))
    kd = np.broadcast_to(np.exp(lg[:, None] * (ln - 1.0 - step))[:, :, None], (RET_HEADS, n, RET_DK))
    cd = np.broadcast_to(np.exp(lg * ln)[:, None, None], (RET_HEADS, 8, RET_DK))
    tab = lambda t: jnp.asarray(np.ascontiguousarray(t), F32)
    col = pl.BlockSpec((n, RET_DK), lambda h: (0, h))
    per_head = lambda r, c: pl.BlockSpec((None, r, c), lambda h: (h, 0, 0))
    st = pl.BlockSpec((nb, None, RET_DK, RET_DK), lambda h: (0, h, 0, 0))
    return pl.pallas_call(
        functools.partial(_ret_sample_kernel, nb=nb, ln=ln),
        grid=(RET_HEADS,),
        in_specs=[col, col, col, col, pl.BlockSpec((1, RET_DK), lambda h: (0, h)),
                  per_head(n, n), per_head(n, RET_DK), per_head(n, RET_DK), per_head(8, RET_DK), st],
        out_specs=[col, st],
        out_shape=[jax.ShapeDtypeStruct((n, RET_WIDTH), BF16),
                   jax.ShapeDtypeStruct((nb, RET_HEADS, RET_DK, RET_DK), F32)],
        scratch_shapes=[pltpu.VMEM((n, RET_DK), F32), pltpu.VMEM((n, RET_DK), F32),
                        pltpu.VMEM((RET_DK, n), F32)],
        compiler_params=_params(("arbitrary",)),
        name="retention_sample",
    )(rq, rk, rv, rg, ret_gain.reshape(1, RET_WIDTH), tab(dm), tab(qd), tab(kd), tab(cd), state)


def _shift_window(old_ref, new_t, out_ref, *, ln, wbuf):
    lane = lax.broadcasted_iota(jnp.int32, new_t.shape, 1)
    ncol = wbuf // LANES
    rolled = pltpu.roll(old_ref[:, 0:LANES], LANES - ln, 1)
    for c in range(ncol):
        nxt = pltpu.roll(old_ref[:, (c + 1) * LANES:(c + 2) * LANES] if c + 1 < ncol else new_t, LANES - ln, 1)
        out_ref[:, c * LANES:(c + 1) * LANES] = jnp.where(lane < LANES - ln, rolled, nxt)
        rolled = nxt


def _new_rows_minor(new_ref, ln):
    pad = jnp.zeros((LANES - ln, new_ref.shape[1]), F32)
    return jnp.concatenate([new_ref[...], pad], axis=0).T


def _sample_heads(q_ref, kn_ref, vn_ref, ck_ref, cv_ref, cc_ref, cn_ref, o_ref, ko_ref, vo_ref, *, ln, wbuf):
    kn_t = _new_rows_minor(kn_ref, ln)
    vn_t = _new_rows_minor(vn_ref, ln)
    _shift_window(ck_ref, kn_t, ko_ref, ln=ln, wbuf=wbuf)
    _shift_window(cv_ref, vn_t, vo_ref, ln=ln, wbuf=wbuf)
    q = q_ref[...]
    nh = q.shape[1] // ATT_HD
    lane = lax.broadcasted_iota(jnp.int32, q.shape, 1)
    heads = [lane // ATT_HD == h for h in range(nh)]
    qm = jnp.concatenate([jnp.where(hm, q, 0.0) for hm in heads], axis=0).astype(BF16)
    cnt_c = jnp.concatenate([cc_ref[...]] * nh, axis=0)
    cnt_n = jnp.concatenate([cn_ref[...]] * nh, axis=0)
    s_c = jnp.where(cnt_c > 0, _dot(qm, ck_ref[...].astype(BF16)), NEG_INF)
    s_n = jnp.where(cnt_n > 0, _dot(qm, kn_t.astype(BF16)), NEG_INF)
    m = jnp.maximum(jnp.max(s_c, axis=-1, keepdims=True), jnp.max(s_n, axis=-1, keepdims=True))
    p_c = cnt_c * jnp.exp2(s_c - m)
    p_n = cnt_n * jnp.exp2(s_n - m)
    l = jnp.sum(p_c, axis=-1, keepdims=True) + jnp.sum(p_n, axis=-1, keepdims=True)
    o = _dot_nt(p_c.astype(BF16), cv_ref[...].astype(BF16)) + _dot_nt(p_n.astype(BF16), vn_t.astype(BF16))
    o = o / l
    acc = jnp.zeros_like(q)
    for h, hm in enumerate(heads):
        acc = acc + jnp.where(hm, o[h * ln:(h + 1) * ln, :], 0.0)
    o_ref[...] = acc.astype(o_ref.dtype)


def _pattern_counts(ln, wbuf):
    cnt = np.zeros((ln, wbuf + ln), np.float32)
    for dil in DILATIONS:
        for l in range(ln):
            for j in range(SPAN + 1):
                row = wbuf + l - dil * j
                if row >= 0:
                    cnt[l, row] += 1.0
    return cnt


def _count_tables(ln, wbuf):
    cnt = _pattern_counts(ln, wbuf)
    cnt_new = np.zeros((ln, LANES), np.float32)
    cnt_new[:, :ln] = cnt[:, wbuf:]
    return jnp.asarray(cnt[:, :wbuf]), jnp.asarray(cnt_new)


def _step(x_prompt, x_sample, c_prompt, c_sample, state_ret, cache_win_k, cache_win_v, w_ada, b_ada,
          g_pre_mix, g_post_mix, g_pre_ffn, g_post_ffn, w_in, ret_gain, w_o, w_up, w_down,
          *, tm, tc, tq):
    assert w_in.shape[0] == 1, "single-layer step"
    bp, sp, d = x_prompt.shape
    nb, ln, _ = x_sample.shape
    wbuf = cache_win_k.shape[2]
    n_s = nb * ln

    w_in_f32 = w_in[0]

    rows = bp + nb
    pad = (-rows) % 8
    c_all = jnp.concatenate([c_prompt, c_sample, jnp.zeros((pad, d), F32)], axis=0)
    mod = _modulation(c_all, w_ada[0], b_ada[0])
    mod_p = mod[:bp].reshape(bp, 1, 6 * d)
    mod_s = jnp.repeat(mod[bp:rows], ln, axis=0).reshape(1, n_s, 6 * d)

    tabs_p = _rotation_constants(np.arange(tm), np.arange(sp // tm) * tm)
    tabs_s = _rotation_constants(np.tile(np.arange(ln), nb), [PAST_LEN])

    keep = min(MAX_WINDOW, sp)
    (rq, rk, rv, rg, aq, ak, av, akf, avf, aq4, ak4, av4, aq16, ak16, av16) = _project(
        x_prompt, mod_p, g_pre_mix[0], w_in_f32, tabs_p, tm=tm, keep=keep, act_dtype=BF16, regroup=True)
    xs = x_sample.reshape(1, n_s, d)
    srq, srk, srv, srg, saq, _, _, sakf, savf = _project(
        xs, mod_s, g_pre_mix[0], w_in_f32, tabs_s, tm=n_s, keep=n_s, act_dtype=F32, regroup=False)
    flat = lambda t: t.reshape(n_s, GROUP_W)

    to_minor = lambda t: jnp.transpose(t[0], (0, 2, 3, 1)).reshape(nb, ATT_WIDTH, wbuf)
    from_minor = lambda t: jnp.transpose(t.reshape(nb, ATT_HEADS, ATT_HD, wbuf), (0, 3, 1, 2))[None]
    sample = (flat(saq), flat(sakf), flat(savf), to_minor(cache_win_k), to_minor(cache_win_v),
              *_count_tables(ln, wbuf))
    patterns = ((aq16, ak16, av16), (aq4, ak4, av4), (aq[:, None], ak[:, None], av[:, None]))
    steps = bp * sp // tq
    hosted, first = {}, 0
    for n in range(len(patterns)):
        hosted[n] = (first, nb // len(patterns) + (n < nb % len(patterns)))
        first += hosted[n][1]
    assert first == nb and max(h[1] for h in hosted.values()) <= steps
    carry, merged_in = None, []
    for n, qkv in enumerate(patterns):
        m_len = qkv[0].shape[2]
        out = _dilated_attention(*qkv, sample, carry, tq=min(tq, m_len), groups=max(1, tq // m_len),
                                 unit0=hosted[n][0], nunit=hosted[n][1])
        merged_in = list(out[:2]) + merged_in
        carry = out[2:] or carry
    satt_h, k_out, v_out = carry

    ret_h, s_fin, (wo_bf, wu_bf, wd_bf) = _retention_prompt(
        rq, rk, rv, rg, ret_gain[0], (w_o[0], w_up[0], w_down[0]), tc=tc, chunk=RET_BLOCK)
    y_prompt = _tail(x_prompt, ret_h, tuple(merged_in), mod_p, g_post_mix[0], g_pre_ffn[0],
                     g_post_ffn[0], wo_bf, wu_bf, wd_bf, tm=tm)

    sret_h, s_new = _retention_sample(flat(srq), flat(srk), flat(srv), flat(srg), ret_gain[0],
                                      state_ret[0], nb=nb, ln=ln)
    y_sample = _tail(xs, sret_h.reshape(1, n_s, RET_WIDTH), satt_h.reshape(1, n_s, ATT_WIDTH), mod_s,
                     g_post_mix[0], g_pre_ffn[0], g_post_ffn[0], wo_bf, wu_bf, wd_bf, tm=n_s)

    cache_shape = (1, -1, keep, ATT_HEADS, ATT_HD)
    return (y_prompt,
            y_sample.reshape(nb, ln, d),
            s_fin[None],
            akf.reshape(cache_shape),
            avf.reshape(cache_shape),
            s_new[None],
            from_minor(k_out),
            from_minor(v_out))


def kernel(x_prompt, x_sample, c_prompt, c_sample, state_ret, cache_win_k, cache_win_v, w_ada, b_ada,
           g_pre_mix, g_post_mix, g_pre_ffn, g_post_ffn, w_in, ret_gain, w_o, w_up, w_down):
    return _step(x_prompt, x_sample, c_prompt, c_sample, state_ret, cache_win_k, cache_win_v, w_ada, b_ada,
                 g_pre_mix, g_post_mix, g_pre_ffn, g_post_ffn, w_in, ret_gain, w_o, w_up, w_down,
                 tm=512, tc=1024, tq=1024)
```

```python
import functools
import itertools
import math

import numpy as np
import jax
import jax.numpy as jnp
from jax import lax
from jax.experimental import pallas as pl
from jax.experimental.pallas import tpu as pltpu

F32 = jnp.float32
BF16 = jnp.bfloat16

D_MODEL = 1024
RET_HEADS = 4
RET_WIDTH = 512
RET_DK = 128
RET_THETA = 10000.0
RET_BLOCK = 256
ATT_HD = 64
ATT_HEADS = 8
ATT_WIDTH = 512
DILATIONS = (1, 4, 16)
MAX_WINDOW = 2048
SPAN = 128
DIL_BLOCK = 128
ROPE_THETA = 10000.0
D_FF = 4096
NORM_EPS = 1e-6
NEG_INF = -1e30
PAST_LEN = 16384
N_GROUPS = 7
GROUP_W = 512
LANES = 128
VMEM_LIMIT = 56 * 1024 * 1024

LOG2_E = math.log2(math.e)
LOG_G = tuple(math.log1p(-(2.0 ** (-5.0 - h))) for h in range(RET_HEADS))


def _dot(a, b):
    return jnp.dot(a, b, preferred_element_type=F32)


def _dot_nt(a, b):
    return lax.dot_general(a, b, (((1,), (1,)), ((), ())), preferred_element_type=F32)


def _dot_tn(a, b):
    return lax.dot_general(a, b, (((0,), (0,)), ((), ())), preferred_element_type=F32)


def _rms(x):
    return x * lax.rsqrt(jnp.mean(x * x, axis=-1, keepdims=True) + NORM_EPS)


def _silu(x):
    return x / (1.0 + jnp.exp(-x))


def _tied_zero(values, rows):
    bits = None
    for v in values:
        assert v.shape[0] % rows == 0 and v.shape[1] % LANES == 0
        for r0 in range(0, v.shape[0], rows):
            for c0 in range(0, v.shape[1], LANES):
                b = pltpu.bitcast(v[r0:r0 + rows, c0:c0 + LANES], jnp.uint32)
                bits = b if bits is None else bits | b
    sixteen = jnp.full(bits.shape, 16, jnp.uint32)
    return pltpu.bitcast(lax.shift_right_logical(lax.shift_right_logical(bits, sixteen), sixteen), F32)


def _params(sem):
    return pltpu.CompilerParams(dimension_semantics=sem, vmem_limit_bytes=VMEM_LIMIT)


def _const_spec(shape):
    nd = len(shape)
    return pl.BlockSpec(shape, lambda *_: (0,) * nd, pipeline_mode=pl.Buffered(1))


def _mod_kernel(c_ref, w_ref, b_ref, o_ref):
    a = _silu(c_ref[...]).astype(BF16)
    o_ref[...] = _dot(a, w_ref[...].astype(BF16)) + b_ref[...]


def _modulation(c, w_ada, b_ada):
    rows, d = c.shape
    n = w_ada.shape[1]
    tn = 1536
    return pl.pallas_call(
        _mod_kernel,
        grid=(n // tn,),
        in_specs=[pl.BlockSpec((rows, d), lambda j: (0, 0)),
                  pl.BlockSpec((d, tn), lambda j: (0, j)),
                  pl.BlockSpec((1, tn), lambda j: (0, j))],
        out_specs=pl.BlockSpec((rows, tn), lambda j: (0, j)),
        out_shape=jax.ShapeDtypeStruct((rows, n), F32),
        compiler_params=_params(("arbitrary",)),
        name="adaln_mod",
    )(c, w_ada, b_ada.reshape(1, n))


def _rotate_pairs(x, cos, sin):
    lane = lax.broadcasted_iota(jnp.int32, x.shape, 1)
    partner = jnp.where(lane % 2 == 0, pltpu.roll(x, LANES - 1, 1), pltpu.roll(x, 1, 1))
    return x * cos + partner * sin


def _rotate_half(x, cos, sin):
    lane = lax.broadcasted_iota(jnp.int32, x.shape, 1)
    half = ATT_HD // 2
    partner = jnp.where(lane % ATT_HD < half, pltpu.roll(x, LANES - half, 1), pltpu.roll(x, half, 1))
    return x * cos + partner * sin


def _store_regrouped(val, c, tok_ref, r4_ref, r16_ref, zs, s4):
    sl = slice(c * LANES, (c + 1) * LANES)
    tok_ref[:, sl] = val.astype(tok_ref.dtype)
    tm = val.shape[0]
    zs[c] = val
    for q in range(4):
        g4 = zs[c, pl.ds(q, tm // 4, stride=4), :]
        r4_ref[q, :, sl] = g4.astype(r4_ref.dtype)
        s4[c, q] = g4
    for q in range(4):
        for a in range(4):
            r16_ref[4 * a + q, :, sl] = s4[c, q, pl.ds(a, tm // 16, stride=4), :].astype(r16_ref.dtype)


def _proj_kernel(x_ref, mod_ref, g_ref, wf_ref, rows_ref, tile_ref, *refs, regroup):
    if regroup:
        (rq_ref, rk_ref, rv_ref, rg_ref, aq_ref, ak_ref, av_ref, akf_ref, avf_ref,
         aq4_ref, ak4_ref, av4_ref, aq16_ref, ak16_ref, av16_ref, w_ref, zs, s4) = refs
    else:
        rq_ref, rk_ref, rv_ref, rg_ref, aq_ref, ak_ref, av_ref, akf_ref, avf_ref, w_ref = refs

    @pl.when(jnp.logical_and(pl.program_id(0) == 0, pl.program_id(1) == 0))
    def _():
        for gi in range(N_GROUPS):
            sl = slice(gi * GROUP_W, (gi + 1) * GROUP_W)
            w_ref[:, sl] = wf_ref[:, sl].astype(w_ref.dtype)

    d = D_MODEL
    x = x_ref[...]
    h = (_rms(x) * g_ref[...]) * (1.0 + mod_ref[:, d:2 * d]) + mod_ref[:, 0:d]
    h = h.astype(BF16)
    t = tile_ref[...]
    cr = t[0:1] * rows_ref[0] - t[1:2] * rows_ref[1]
    sr = (t[1:2] * rows_ref[0] + t[0:1] * rows_ref[1]) * t[4:5]
    ca = t[2:3] * rows_ref[2] - t[3:4] * rows_ref[3]
    sa = (t[3:4] * rows_ref[2] + t[2:3] * rows_ref[3]) * t[5:6]
    nch = GROUP_W // LANES

    def group(gi):
        return _dot(h, w_ref[:, gi * GROUP_W:(gi + 1) * GROUP_W])

    z = group(0)
    for c in range(nch):
        sl = slice(c * LANES, (c + 1) * LANES)
        rq_ref[:, sl] = _rotate_pairs(z[:, sl], cr, sr).astype(rq_ref.dtype)
    z = group(1)
    for c in range(nch):
        sl = slice(c * LANES, (c + 1) * LANES)
        rk_ref[:, sl] = (_rotate_pairs(z[:, sl], cr, sr) * (RET_DK ** -0.5)).astype(rk_ref.dtype)
    rv_ref[...] = group(2).astype(rv_ref.dtype)
    rg_ref[...] = group(3)
    def emit(val, c, tok_ref, r4_ref, r16_ref):
        if regroup:
            _store_regrouped(val, c, tok_ref, r4_ref, r16_ref, zs, s4)
        else:
            tok_ref[:, c * LANES:(c + 1) * LANES] = val.astype(tok_ref.dtype)

    r4 = (aq4_ref, ak4_ref, av4_ref) if regroup else (None,) * 3
    r16 = (aq16_ref, ak16_ref, av16_ref) if regroup else (None,) * 3
    z = group(4)
    for c in range(nch):
        sl = slice(c * LANES, (c + 1) * LANES)
        emit(_rotate_half(z[:, sl], ca, sa) * (ATT_HD ** -0.5 * LOG2_E), c, aq_ref, r4[0], r16[0])
    z = group(5)
    for c in range(nch):
        sl = slice(c * LANES, (c + 1) * LANES)
        r = _rotate_half(z[:, sl], ca, sa)
        akf_ref[:, sl] = r
        emit(r, c, ak_ref, r4[1], r16[1])
    z = group(6)
    avf_ref[...] = z
    for c in range(nch):
        emit(z[:, c * LANES:(c + 1) * LANES], c, av_ref, r4[2], r16[2])


def _project(x, mod, g_pre, w_in_f32, tabs, *, tm, keep, act_dtype, regroup):
    b, s, d = x.shape
    nt = s // tm
    rows_tab, tile_tab = tabs
    assert rows_tab.shape == (4, tm, LANES) and tile_tab.shape == (nt, 8, LANES)
    mod_rows = mod.shape[1]
    per_row = mod_rows != 1
    first_keep = (s - keep) // tm
    tok = pl.BlockSpec((None, tm, GROUP_W), lambda bi, i: (bi, i, 0))
    keep_spec = pl.BlockSpec((None, tm, GROUP_W), lambda bi, i: (bi, jnp.maximum(i - first_keep, 0), 0))
    tile_spec = pl.BlockSpec((None, 8, LANES), lambda bi, i: (i, 0, 0))
    mod_spec = pl.BlockSpec((None, tm if per_row else 1, 6 * d),
                            (lambda bi, i: (bi, i, 0)) if per_row else (lambda bi, i: (bi, 0, 0)))
    act = jax.ShapeDtypeStruct((b, s, GROUP_W), act_dtype)
    full = jax.ShapeDtypeStruct((b, s, GROUP_W), F32)
    kept = jax.ShapeDtypeStruct((b, keep, GROUP_W), F32)
    out_specs = [tok, tok, tok, tok, tok, tok, tok, keep_spec, keep_spec]
    out_shape = [act, act, act, full, act, act, act, kept, kept]
    scratch = [pltpu.VMEM((d, N_GROUPS * GROUP_W), BF16)]
    if regroup:
        for r in (4, 16):
            out_specs += [pl.BlockSpec((None, r, tm // r, GROUP_W), lambda bi, i: (bi, 0, i, 0))] * 3
            out_shape += [jax.ShapeDtypeStruct((b, r, s // r, GROUP_W), act_dtype)] * 3
        nch = GROUP_W // LANES
        scratch += [pltpu.VMEM((nch, tm, LANES), F32), pltpu.VMEM((nch, 4, tm // 4, LANES), F32)]
    return pl.pallas_call(
        functools.partial(_proj_kernel, regroup=regroup),
        grid=(b, nt),
        in_specs=[pl.BlockSpec((None, tm, d), lambda bi, i: (bi, i, 0)),
                  mod_spec,
                  _const_spec((1, d)),
                  _const_spec((d, N_GROUPS * GROUP_W)),
                  _const_spec((4, tm, LANES)), tile_spec],
        out_specs=out_specs,
        out_shape=out_shape,
        scratch_shapes=scratch,
        compiler_params=_params(("arbitrary", "arbitrary")),
        name="in_proj",
    )(x, mod, g_pre.reshape(1, d), w_in_f32, rows_tab, tile_tab)


def _rotation_constants(row_pos, tile_pos):
    lane = np.arange(LANES)
    inv_r = (1.0 / RET_THETA ** np.linspace(0.0, 1.0, RET_DK // 2))[lane // 2]
    inv_a = (1.0 / ROPE_THETA ** (np.arange(0, ATT_HD, 2) / ATT_HD))[lane % (ATT_HD // 2)]
    rp = np.asarray(row_pos, np.float64)[:, None]
    tp = np.asarray(tile_pos, np.float64)[:, None]
    rows = np.stack([np.cos(rp * inv_r), np.sin(rp * inv_r), np.cos(rp * inv_a), np.sin(rp * inv_a)])
    sign_r = np.where(lane % 2 == 0, -1.0, 1.0)
    sign_a = np.where(lane % ATT_HD < ATT_HD // 2, -1.0, 1.0)
    ones = np.ones_like(tp * inv_r)
    tiles = np.stack([np.cos(tp * inv_r), np.sin(tp * inv_r), np.cos(tp * inv_a), np.sin(tp * inv_a),
                      ones * sign_r, ones * sign_a, 0 * ones, 0 * ones], axis=1)
    return jnp.asarray(rows, F32), jnp.asarray(tiles, F32)


def _ret_readout(o, g, gain):
    return (_rms(o) * gain) * _silu(g)


def _ret_prompt_kernel(q_ref, k_ref, v_ref, g_ref, gain_ref, intra_ref, qd_ref, kd_ref, *rest, nchunk, c_len,
                       ncast):
    cast_in, (o_ref, sfin_ref), cast_out, s_scr = rest[:ncast], rest[ncast:ncast + 2], rest[ncast + 2:-1], rest[-1]
    for src, dst in zip(cast_in, cast_out):
        dst[...] = src[...].astype(dst.dtype)
    i = pl.program_id(1)

    @pl.when(i == 0)
    def _():
        s_scr[...] = jnp.zeros_like(s_scr)

    for h in range(RET_HEADS):
        intra, q_dec, k_dec = intra_ref[h], qd_ref[h], kd_ref[h]
        c_dec = math.exp(LOG_G[h] * c_len)
        hs = slice(h * RET_DK, (h + 1) * RET_DK)
        state = s_scr[h]
        for c in range(nchunk):
            rs = slice(c * c_len, (c + 1) * c_len)
            q = q_ref[rs, hs]
            k = k_ref[rs, hs]
            v = v_ref[rs, hs]
            a = _dot_nt(q, k) * intra
            lhs = jnp.concatenate([a.astype(BF16), (q.astype(F32) * q_dec).astype(BF16)], axis=1)
            o = _dot(lhs, jnp.concatenate([v, state.astype(BF16)], axis=0))
            kd = (k.astype(F32) * k_dec).astype(BF16)
            state = state * c_dec + _dot_tn(kd, v)
            o_ref[rs, hs] = _ret_readout(o, g_ref[rs, hs], gain_ref[:, hs]).astype(o_ref.dtype)
        s_scr[h] = state

    @pl.when(i == pl.num_programs(1) - 1)
    def _():
        sfin_ref[...] = s_scr[...]


def _retention_prompt(rq, rk, rv, rg, ret_gain, weights, *, tc, chunk):
    b, s, w = rq.shape
    nblk = s // tc
    steps = b * nblk
    tok = pl.BlockSpec((None, tc, w), lambda bi, i: (bi, i, 0))
    slab = lambda m: pl.BlockSpec((m.shape[0] // steps, m.shape[1]), lambda bi, i: (bi * nblk + i, 0))
    assert all(m.shape[0] % (16 * steps) == 0 for m in weights)
    idx = np.arange(chunk, dtype=np.float64)
    lg = np.asarray(LOG_G, np.float64)[:, None, None]
    diff = idx[:, None] - idx[None, :]
    intra = np.where(diff >= 0, np.exp(lg * np.maximum(diff, 0.0)), 0.0)
    lanes = np.ones((1, 1, RET_DK))
    qd = np.exp(lg * (idx[None, :, None] + 1.0)) * lanes
    kd = np.exp(lg * (chunk - 1.0 - idx[None, :, None])) * lanes
    tables = [jnp.asarray(t, F32) for t in (intra, qd, kd)]
    out = pl.pallas_call(
        functools.partial(_ret_prompt_kernel, nchunk=tc // chunk, c_len=chunk, ncast=len(weights)),
        grid=(b, nblk),
        in_specs=[tok, tok, tok, tok, _const_spec((1, w))] + [_const_spec(t.shape) for t in tables]
        + [slab(m) for m in weights],
        out_specs=[tok, pl.BlockSpec((None, RET_HEADS, RET_DK, RET_DK), lambda bi, i: (bi, 0, 0, 0))]
        + [slab(m) for m in weights],
        out_shape=[jax.ShapeDtypeStruct((b, s, w), BF16),
                   jax.ShapeDtypeStruct((b, RET_HEADS, RET_DK, RET_DK), F32)]
        + [jax.ShapeDtypeStruct(m.shape, BF16) for m in weights],
        scratch_shapes=[pltpu.VMEM((RET_HEADS, RET_DK, RET_DK), F32)],
        compiler_params=_params(("arbitrary", "arbitrary")),
        name="retention_prompt",
    )(rq, rk, rv, rg, ret_gain.reshape(1, w), *tables, *weights)
    return out[0], out[1], out[2:]


def _att_host_kernel(q_ref, k_ref, v_ref, kp_ref, vp_ref, oh_ref, bias_ref, gq_ref, gkn_ref, gvn_ref, gck_ref,
                     gcv_ref, gcc_ref, gcn_ref, *rest, nsub, ln, wbuf, nunit):
    pv_ref, ml_ref, go_ref, gko_ref, gvo_ref, kcat, vcat = rest[-7:]
    steps = pl.num_programs(0) * pl.num_programs(1) * pl.num_programs(2)
    t = (pl.program_id(0) * pl.num_programs(1) + pl.program_id(1)) * pl.num_programs(2) + pl.program_id(2)
    fresh = jnp.logical_or(t == 0, _hosted_unit(t, 0, nunit, steps) != _hosted_unit(t - 1, 0, nunit, steps))

    @pl.when(fresh)
    def _():
        _sample_heads(gq_ref, gkn_ref, gvn_ref, gck_ref, gcv_ref, gcc_ref, gcn_ref, go_ref, gko_ref, gvo_ref,
                      ln=ln, wbuf=wbuf)

    _att_block(q_ref, k_ref, v_ref, kp_ref, vp_ref, oh_ref, bias_ref, pv_ref, ml_ref, kcat, vcat, nsub=nsub)


def _att_block(q_ref, k_ref, v_ref, kp_ref, vp_ref, oh_ref, bias_ref, pv_ref, ml_ref, kcat, vcat, *, nsub):
    i = pl.program_id(2)
    blk = DIL_BLOCK
    for g in range(q_ref.shape[0]):
        kcat[g, 0:blk, :] = kp_ref[g]
        kcat[g, blk:, :] = k_ref[g]
        vcat[g, 0:blk, :] = vp_ref[g]
        vcat[g, blk:, :] = v_ref[g]
    lane = lax.broadcasted_iota(jnp.int32, (blk, LANES), 1)
    low = lane < ATT_HD
    quarter = lane // (ATT_HD // 2)
    onehot = oh_ref[...]
    bias_rest = bias_ref[1]
    bias_first = jnp.where(i == 0, bias_ref[0], bias_rest)
    ones = jnp.ones((2 * blk, LANES), BF16)
    low2 = lax.broadcasted_iota(jnp.int32, (2 * blk, LANES), 1) < ATT_HD
    for g, j, hp in itertools.product(range(q_ref.shape[0]), range(nsub), range(ATT_WIDTH // LANES)):
        bias = bias_first if j == 0 else bias_rest
        rows = slice(j * blk, (j + 1) * blk)
        krows = slice(j * blk, (j + 2) * blk)
        cols = slice(hp * LANES, (hp + 1) * LANES)
        q = q_ref[g, rows, cols]
        zero = jnp.zeros_like(q)
        q2 = jnp.concatenate([jnp.where(low, q, zero), jnp.where(low, zero, q)], axis=0)
        s = _dot_nt(jnp.concatenate([q2, onehot], axis=1), jnp.concatenate([kcat[g, krows, cols], bias], axis=1))
        m = jnp.max(s, axis=-1, keepdims=True)
        p = jnp.exp2(s - m).astype(BF16)
        vv = vcat[g, krows, cols]
        pv0 = _dot(p[0:blk], jnp.where(low2, vv, ones))
        pv1 = _dot(p[blk:], jnp.where(low2, ones, vv))
        pv_ref[g, rows, cols] = jnp.where(low, pv0, pv1).astype(pv_ref.dtype)
        ml_ref[g, rows, cols] = jnp.where(
            quarter == 0, m[0:blk], jnp.where(quarter == 1, pv1, jnp.where(quarter == 2, m[blk:], pv0)))


def _band_tables():
    jj = np.arange(2 * DIL_BLOCK)[:, None]
    qi = np.arange(DIL_BLOCK)[None, :]
    bias = np.stack([np.where((jj >= np.maximum(qi, fk)) & (jj <= qi + SPAN), 0.0, NEG_INF)
                     for fk in (DIL_BLOCK, 0)])
    onehot = np.tile(np.eye(DIL_BLOCK), (2, 1))
    return jnp.asarray(onehot, BF16), jnp.asarray(bias, BF16)


def _hosted_unit(t, unit0, nunit, steps):
    return unit0 + (t * nunit) // steps


def _dilated_attention(aq, ak, av, sample, carry, *, tq, groups, unit0, nunit):
    b, dil, m_len, w = aq.shape
    nsub = tq // DIL_BLOCK
    nblk = m_len // tq
    ngrp = dil // groups
    own = pl.BlockSpec((None, groups, tq, w), lambda bi, r, i: (bi, r, i, 0))
    prv = pl.BlockSpec((None, groups, DIL_BLOCK, w), lambda bi, r, i: (bi, r, jnp.maximum(i * nsub - 1, 0), 0))
    onehot, bias = _band_tables()
    in_specs = [own, own, own, prv, prv, _const_spec(onehot.shape), _const_spec(bias.shape)]
    args = [aq, ak, av, ak, av, onehot, bias]
    out_shape = [jax.ShapeDtypeStruct((b, dil, m_len, w), BF16), jax.ShapeDtypeStruct((b, dil, m_len, w), F32)]
    scratch = [pltpu.VMEM((groups, tq + DIL_BLOCK, w), BF16), pltpu.VMEM((groups, tq + DIL_BLOCK, w), BF16)]
    common = dict(grid=(b, ngrp, nblk), scratch_shapes=scratch, name=f"dilated_attn_d{dil}",
                  compiler_params=_params(("arbitrary", "arbitrary", "arbitrary")))
    if nunit == 0:
        return pl.pallas_call(functools.partial(_att_block, nsub=nsub), in_specs=in_specs,
                              out_specs=[own, own], out_shape=out_shape, **common)(*args)

    gq, gkn, gvn, gck, gcv, cnt_c, cnt_n = sample
    nb, _, wbuf = gck.shape
    ln = gq.shape[0] // nb
    steps = b * ngrp * nblk
    assert nunit <= steps

    def unit(bi, r, i):
        return _hosted_unit((bi * ngrp + r) * nblk + i, unit0, nunit, steps)

    rows = pl.BlockSpec((ln, w), lambda bi, r, i: (unit(bi, r, i), 0))
    window = pl.BlockSpec((None, w, wbuf), lambda bi, r, i: (unit(bi, r, i), 0, 0))
    in_specs += [rows, rows, rows, window, window, _const_spec(cnt_c.shape), _const_spec(cnt_n.shape)]
    args += [gq, gkn, gvn, gck, gcv, cnt_c, cnt_n]
    aliases = {}
    if carry is not None:
        aliases = {len(args) + n: 2 + n for n in range(3)}
        in_specs += [pl.BlockSpec(memory_space=pl.ANY)] * 3
        args += list(carry)
    out_shape += [jax.ShapeDtypeStruct((nb * ln, w), F32), jax.ShapeDtypeStruct((nb, w, wbuf), F32),
                  jax.ShapeDtypeStruct((nb, w, wbuf), F32)]
    return pl.pallas_call(
        functools.partial(_att_host_kernel, nsub=nsub, ln=ln, wbuf=wbuf, nunit=nunit),
        in_specs=in_specs,
        out_specs=[own, own, rows, window, window],
        out_shape=out_shape,
        input_output_aliases=aliases,
        **common,
    )(*args)


def _merge_patterns(pv1_ref, ml1_ref, pv4_ref, ml4_ref, pv16_ref, ml16_ref, tok, s4, att):
    tm = att.shape[0]
    half = ATT_HD // 2
    lane = lax.broadcasted_iota(jnp.int32, (tm, LANES), 1)
    first = lane % ATT_HD < half

    def unpack(ml):
        near = pltpu.roll(ml, half, 1)
        return jnp.where(first, ml, near), jnp.where(first, near, pltpu.roll(ml, 2 * half, 1))

    for c in range(ATT_WIDTH // LANES):
        sl = slice(c * LANES, (c + 1) * LANES)
        for n, (r4, r16) in enumerate(((pv4_ref, pv16_ref), (ml4_ref, ml16_ref))):
            for q in range(4):
                tok[n, c, pl.ds(q, tm // 4, stride=4), :] = r4[q, :, sl].astype(F32)
                for a in range(4):
                    s4[q, pl.ds(a, tm // 16, stride=4), :] = r16[4 * a + q, :, sl].astype(F32)
            for q in range(4):
                tok[2 + n, c, pl.ds(q, tm // 4, stride=4), :] = s4[q]
        (m1, l1), (m4, l4), (m16, l16) = unpack(ml1_ref[:, sl]), unpack(tok[1, c]), unpack(tok[3, c])
        top = jnp.maximum(m1, jnp.maximum(m4, m16))
        e1, e4, e16 = jnp.exp2(m1 - top), jnp.exp2(m4 - top), jnp.exp2(m16 - top)
        num = e1 * pv1_ref[:, sl].astype(F32) + e4 * tok[0, c] + e16 * tok[2, c]
        den = e1 * l1 + e4 * l4 + e16 * l16
        merged = num / den
        att[:, sl] = merged.astype(att.dtype)
        yield merged


def _tail_body(x_ref, ret_ref, att_h, mod_ref, gpm_ref, gpf_ref, gqf_ref, wo_ref, wu_ref, wd_ref, y_ref,
               filler=iter(())):
    d = D_MODEL
    x = x_ref[...]
    mixed = _dot(ret_ref[...].astype(BF16), wo_ref[0:RET_WIDTH, :]) + _dot(att_h, wo_ref[RET_WIDTH:, :])
    x1 = x + mod_ref[:, 2 * d:3 * d] * (_rms(mixed) * gpm_ref[...])
    h = ((_rms(x1) * gpf_ref[...]) * (1.0 + mod_ref[:, 4 * d:5 * d]) + mod_ref[:, 3 * d:4 * d]).astype(BF16)
    fc = 1024
    f = None
    for c in range(D_FF // fc):
        u = jnp.maximum(_dot(h, wu_ref[:, c * fc:(c + 1) * fc]), 0.0)
        piece = next(filler, None)
        if piece is not None:
            u = jnp.concatenate([u[:, 0:LANES] + _tied_zero([piece], u.shape[0]), u[:, LANES:]], axis=1)
        part = _dot((u * u).astype(BF16), wd_ref[c * fc:(c + 1) * fc, :])
        f = part if f is None else f + part
    for _ in filler:
        pass
    y_ref[...] = x1 + mod_ref[:, 5 * d:6 * d] * (_rms(f) * gqf_ref[...])


def _tail_kernel(x_ref, ret_ref, att_ref, *rest):
    _tail_body(x_ref, ret_ref, att_ref[...].astype(BF16), *rest)


def _tail_merge_kernel(x_ref, ret_ref, pv1_ref, ml1_ref, pv4_ref, ml4_ref, pv16_ref, ml16_ref, mod_ref,
                       gpm_ref, gpf_ref, gqf_ref, wo_ref, wu_ref, wd_ref, y_ref, tok, s4, att):
    t = pl.program_id(0)
    merge = functools.partial(_merge_patterns, pv1_ref, ml1_ref, pv4_ref, ml4_ref, pv16_ref, ml16_ref,
                              tok, s4, att)

    @pl.when(t == 0)
    def _():
        for _ in merge():
            pass

    @pl.when(t > 0)
    def _():
        att_h = att[...]
        _tail_body(x_ref, ret_ref, att_h, mod_ref, gpm_ref, gpf_ref, gqf_ref, wo_ref, wu_ref, wd_ref, y_ref,
                   filler=merge())


def _tail(x, ret_h, att, mod, g_post_mix, g_pre_ffn, g_post_ffn, wo_bf, wu_bf, wd_bf, *, tm):
    b, s, d = x.shape
    nt = s // tm
    per_row = mod.shape[1] != 1
    weights = [_const_spec((1, d)), _const_spec((1, d)), _const_spec((1, d)),
               _const_spec((d, d)), _const_spec((d, D_FF)), _const_spec((D_FF, d))]
    weight_args = (g_post_mix.reshape(1, d), g_pre_ffn.reshape(1, d), g_post_ffn.reshape(1, d),
                   wo_bf, wu_bf, wd_bf)
    out_shape = jax.ShapeDtypeStruct((b, s, d), F32)
    if not isinstance(att, (tuple, list)):
        mod_spec = pl.BlockSpec((None, tm if per_row else 1, 6 * d),
                                (lambda bi, i: (bi, i, 0)) if per_row else (lambda bi, i: (bi, 0, 0)))
        tok = lambda w: pl.BlockSpec((None, tm, w), lambda bi, i: (bi, i, 0))
        return pl.pallas_call(
            _tail_kernel,
            grid=(b, nt),
            in_specs=[tok(d), tok(RET_WIDTH), tok(ATT_WIDTH), mod_spec, *weights],
            out_specs=tok(d),
            out_shape=out_shape,
            compiler_params=_params(("arbitrary", "arbitrary")),
            name="out_proj_mlp",
        )(x, ret_h, att, mod, *weight_args)

    assert not per_row
    last = b * nt - 1
    cur = lambda t: jnp.maximum(t - 1, 0)
    nxt = lambda t: jnp.minimum(t, last)
    tok = lambda w: pl.BlockSpec((None, tm, w), lambda t: (cur(t) // nt, cur(t) % nt, 0))
    mod_spec = pl.BlockSpec((None, 1, 6 * d), lambda t: (cur(t) // nt, 0, 0))
    split = lambda r: pl.BlockSpec((None, r, tm // r, ATT_WIDTH), lambda t: (nxt(t) // nt, 0, nxt(t) % nt, 0))
    one = pl.BlockSpec((None, None, tm, ATT_WIDTH), lambda t: (nxt(t) // nt, 0, nxt(t) % nt, 0))
    nch = ATT_WIDTH // LANES
    return pl.pallas_call(
        _tail_merge_kernel,
        grid=(b * nt + 1,),
        in_specs=[tok(d), tok(RET_WIDTH), one, one, split(4), split(4), split(16), split(16), mod_spec, *weights],
        out_specs=tok(d),
        out_shape=out_shape,
        scratch_shapes=[pltpu.VMEM((4, nch, tm, LANES), F32), pltpu.VMEM((4, tm // 4, LANES), F32),
                        pltpu.VMEM((tm, ATT_WIDTH), BF16)],
        compiler_params=_params(("arbitrary",)),
        name="out_proj_mlp",
    )(x, ret_h, *att, mod, *weight_args)


def _ret_sample_kernel(q_ref, k_ref, v_ref, g_ref, gain_ref, dm_ref, qd_ref, kd_ref, cd_ref, s_ref,
                       o_ref, sn_ref, o_scr, qd_scr, kt_scr, *, nb, ln):
    q = q_ref[...].astype(F32)
    k = k_ref[...].astype(F32)
    v = v_ref[...].astype(BF16)
    a = _dot_nt(q.astype(BF16), k.astype(BF16)) * dm_ref[...]
    o_scr[...] = _dot(a.astype(BF16), v)
    qd_scr[...] = q * qd_ref[...]
    kt_scr[...] = (k * kd_ref[...]).T
    c_dec = cd_ref[0:1, :]
    col = lax.broadcasted_iota(jnp.int32, kt_scr.shape, 1)

    def body(bi, carry):
        rows = pl.ds(pl.multiple_of(bi * ln, ln), ln)
        s0 = s_ref[bi]
        o_scr[rows, :] += _dot(qd_scr[rows, :].astype(BF16), s0.astype(BF16))
        mine = (col >= bi * ln) & (col < (bi + 1) * ln)
        kt = jnp.where(mine, kt_scr[...], 0.0).astype(BF16)
        sn_ref[bi] = s0 * c_dec + _dot(kt, v)
        return carry

    lax.fori_loop(0, nb, body, 0, unroll=math.gcd(nb, 8))
    o_ref[...] = _ret_readout(o_scr[...], g_ref[...], gain_ref[...]).astype(o_ref.dtype)


def _retention_sample(rq, rk, rv, rg, ret_gain, state, *, nb, ln):
    n = nb * ln
    idx = np.arange(n)
    same = (idx[:, None] // ln == idx[None, :] // ln) & (idx[:, None] >= idx[None, :])
    diff = np.maximum(idx[:, None] - idx[None, :], 0).astype(np.float64)
    step = (idx % ln).astype(np.float64)
    lg = np.asarray(LOG_G, np.float64)
    dm = np.where(same[None], np.exp(lg[:, None, None] * diff[None]), 0.0)
    qd = np.broadcast_to(np.exp(lg[:, None] * (step + 1.0))[:, :, None], (RET_HEADS, n, RET_DK))
    kd = np.broadcast_to(np.exp(lg[:, None] * (ln - 1.0 - step))[:, :, None], (RET_HEADS, n, RET_DK))
    cd = np.broadcast_to(np.exp(lg * ln)[:, None, None], (RET_HEADS, 8, RET_DK))
    tab = lambda t: jnp.asarray(np.ascontiguousarray(t), F32)
    col = pl.BlockSpec((n, RET_DK), lambda h: (0, h))
    per_head = lambda r, c: pl.BlockSpec((None, r, c), lambda h: (h, 0, 0))
    st = pl.BlockSpec((nb, None, RET_DK, RET_DK), lambda h: (0, h, 0, 0))
    return pl.pallas_call(
        functools.partial(_ret_sample_kernel, nb=nb, ln=ln),
        grid=(RET_HEADS,),
        in_specs=[col, col, col, col, pl.BlockSpec((1, RET_DK), lambda h: (0, h)),
                  per_head(n, n), per_head(n, RET_DK), per_head(n, RET_DK), per_head(8, RET_DK), st],
        out_specs=[col, st],
        out_shape=[jax.ShapeDtypeStruct((n, RET_WIDTH), BF16),
                   jax.ShapeDtypeStruct((nb, RET_HEADS, RET_DK, RET_DK), F32)],
        scratch_shapes=[pltpu.VMEM((n, RET_DK), F32), pltpu.VMEM((n, RET_DK), F32),
                        pltpu.VMEM((RET_DK, n), F32)],
        compiler_params=_params(("arbitrary",)),
        name="retention_sample",
    )(rq, rk, rv, rg, ret_gain.reshape(1, RET_WIDTH), tab(dm), tab(qd), tab(kd), tab(cd), state)


def _shift_window(old_ref, new_t, out_ref, *, ln, wbuf):
    lane = lax.broadcasted_iota(jnp.int32, new_t.shape, 1)
    ncol = wbuf // LANES
    rolled = pltpu.roll(old_ref[:, 0:LANES], LANES - ln, 1)
    for c in range(ncol):
        nxt = pltpu.roll(old_ref[:, (c + 1) * LANES:(c + 2) * LANES] if c + 1 < ncol else new_t, LANES - ln, 1)
        out_ref[:, c * LANES:(c + 1) * LANES] = jnp.where(lane < LANES - ln, rolled, nxt)
        rolled = nxt


def _new_rows_minor(new_ref, ln):
    pad = jnp.zeros((LANES - ln, new_ref.shape[1]), F32)
    return jnp.concatenate([new_ref[...], pad], axis=0).T


def _sample_heads(q_ref, kn_ref, vn_ref, ck_ref, cv_ref, cc_ref, cn_ref, o_ref, ko_ref, vo_ref, *, ln, wbuf):
    kn_t = _new_rows_minor(kn_ref, ln)
    vn_t = _new_rows_minor(vn_ref, ln)
    _shift_window(ck_ref, kn_t, ko_ref, ln=ln, wbuf=wbuf)
    _shift_window(cv_ref, vn_t, vo_ref, ln=ln, wbuf=wbuf)
    q = q_ref[...]
    nh = q.shape[1] // ATT_HD
    lane = lax.broadcasted_iota(jnp.int32, q.shape, 1)
    heads = [lane // ATT_HD == h for h in range(nh)]
    qm = jnp.concatenate([jnp.where(hm, q, 0.0) for hm in heads], axis=0).astype(BF16)
    cnt_c = jnp.concatenate([cc_ref[...]] * nh, axis=0)
    cnt_n = jnp.concatenate([cn_ref[...]] * nh, axis=0)
    s_c = jnp.where(cnt_c > 0, _dot(qm, ck_ref[...].astype(BF16)), NEG_INF)
    s_n = jnp.where(cnt_n > 0, _dot(qm, kn_t.astype(BF16)), NEG_INF)
    m = jnp.maximum(jnp.max(s_c, axis=-1, keepdims=True), jnp.max(s_n, axis=-1, keepdims=True))
    p_c = cnt_c * jnp.exp2(s_c - m)
    p_n = cnt_n * jnp.exp2(s_n - m)
    l = jnp.sum(p_c, axis=-1, keepdims=True) + jnp.sum(p_n, axis=-1, keepdims=True)
    o = _dot_nt(p_c.astype(BF16), cv_ref[...].astype(BF16)) + _dot_nt(p_n.astype(BF16), vn_t.astype(BF16))
    o = o / l
    acc = jnp.zeros_like(q)
    for h, hm in enumerate(heads):
        acc = acc + jnp.where(hm, o[h * ln:(h + 1) * ln, :], 0.0)
    o_ref[...] = acc.astype(o_ref.dtype)


def _pattern_counts(ln, wbuf):
    cnt = np.zeros((ln, wbuf + ln), np.float32)
    for dil in DILATIONS:
        for l in range(ln):
            for j in range(SPAN + 1):
                row = wbuf + l - dil * j
                if row >= 0:
                    cnt[l, row] += 1.0
    return cnt


def _count_tables(ln, wbuf):
    cnt = _pattern_counts(ln, wbuf)
    cnt_new = np.zeros((ln, LANES), np.float32)
    cnt_new[:, :ln] = cnt[:, wbuf:]
    return jnp.asarray(cnt[:, :wbuf]), jnp.asarray(cnt_new)


def _step(x_prompt, x_sample, c_prompt, c_sample, state_ret, cache_win_k, cache_win_v, w_ada, b_ada,
          g_pre_mix, g_post_mix, g_pre_ffn, g_post_ffn, w_in, ret_gain, w_o, w_up, w_down,
          *, tm, tc, tq):
    assert w_in.shape[0] == 1, "single-layer step"
    bp, sp, d = x_prompt.shape
    nb, ln, _ = x_sample.shape
    wbuf = cache_win_k.shape[2]
    n_s = nb * ln

    w_in_f32 = w_in[0]

    rows = bp + nb
    pad = (-rows) % 8
    c_all = jnp.concatenate([c_prompt, c_sample, jnp.zeros((pad, d), F32)], axis=0)
    mod = _modulation(c_all, w_ada[0], b_ada[0])
    mod_p = mod[:bp].reshape(bp, 1, 6 * d)
    mod_s = jnp.repeat(mod[bp:rows], ln, axis=0).reshape(1, n_s, 6 * d)

    tabs_p = _rotation_constants(np.arange(tm), np.arange(sp // tm) * tm)
    tabs_s = _rotation_constants(np.tile(np.arange(ln), nb), [PAST_LEN])

    keep = min(MAX_WINDOW, sp)
    (rq, rk, rv, rg, aq, ak, av, akf, avf, aq4, ak4, av4, aq16, ak16, av16) = _project(
        x_prompt, mod_p, g_pre_mix[0], w_in_f32, tabs_p, tm=tm, keep=keep, act_dtype=BF16, regroup=True)
    xs = x_sample.reshape(1, n_s, d)
    srq, srk, srv, srg, saq, _, _, sakf, savf = _project(
        xs, mod_s, g_pre_mix[0], w_in_f32, tabs_s, tm=n_s, keep=n_s, act_dtype=F32, regroup=False)
    flat = lambda t: t.reshape(n_s, GROUP_W)

    to_minor = lambda t: jnp.transpose(t[0], (0, 2, 3, 1)).reshape(nb, ATT_WIDTH, wbuf)
    from_minor = lambda t: jnp.transpose(t.reshape(nb, ATT_HEADS, ATT_HD, wbuf), (0, 3, 1, 2))[None]
    sample = (flat(saq), flat(sakf), flat(savf), to_minor(cache_win_k), to_minor(cache_win_v),
              *_count_tables(ln, wbuf))
    patterns = ((aq16, ak16, av16), (aq4, ak4, av4), (aq[:, None], ak[:, None], av[:, None]))
    steps = bp * sp // tq
    hosted, first = {}, 0
    for n in (1, 2, 0):
        hosted[n] = (first, min(steps, nb - first))
        first += hosted[n][1]
    assert first == nb
    carry, merged_in = None, []
    for n, qkv in enumerate(patterns):
        m_len = qkv[0].shape[2]
        rows = tq if hosted[n][1] else 2 * tq
        out = _dilated_attention(*qkv, sample, carry, tq=min(rows, m_len), groups=max(1, rows // m_len),
                                 unit0=hosted[n][0], nunit=hosted[n][1])
        merged_in = list(out[:2]) + merged_in
        carry = out[2:] or carry
    satt_h, k_out, v_out = carry

    ret_h, s_fin, (wo_bf, wu_bf, wd_bf) = _retention_prompt(
        rq, rk, rv, rg, ret_gain[0], (w_o[0], w_up[0], w_down[0]), tc=tc, chunk=RET_BLOCK)
    y_prompt = _tail(x_prompt, ret_h, tuple(merged_in), mod_p, g_post_mix[0], g_pre_ffn[0],
                     g_post_ffn[0], wo_bf, wu_bf, wd_bf, tm=tm)

    sret_h, s_new = _retention_sample(flat(srq), flat(srk), flat(srv), flat(srg), ret_gain[0],
                                      state_ret[0], nb=nb, ln=ln)
    y_sample = _tail(xs, sret_h.reshape(1, n_s, RET_WIDTH), satt_h.reshape(1, n_s, ATT_WIDTH), mod_s,
                     g_post_mix[0], g_pre_ffn[0], g_post_ffn[0], wo_bf, wu_bf, wd_bf, tm=n_s)

    cache_shape = (1, -1, keep, ATT_HEADS, ATT_HD)
    return (y_prompt,
            y_sample.reshape(nb, ln, d),
            s_fin[None],
            akf.reshape(cache_shape),
            avf.reshape(cache_shape),
            s_new[None],
            from_minor(k_out),
            from_minor(v_out))


def kernel(x_prompt, x_sample, c_prompt, c_sample, state_ret, cache_win_k, cache_win_v, w_ada, b_ada,
           g_pre_mix, g_post_mix, g_pre_ffn, g_post_ffn, w_in, ret_gain, w_o, w_up, w_down):
    return _step(x_prompt, x_sample, c_prompt, c_sample, state_ret, cache_win_k, cache_win_v, w_ada, b_ada,
                 g_pre_mix, g_post_mix, g_pre_ffn, g_post_ffn, w_in, ret_gain, w_o, w_up, w_down,
                 tm=512, tc=2048, tq=1024)
```

```python
import functools
import itertools
import math

import numpy as np
import jax
import jax.numpy as jnp
from jax import lax
from jax.experimental import pallas as pl
from jax.experimental.pallas import tpu as pltpu

F32 = jnp.float32
BF16 = jnp.bfloat16

D_MODEL = 1024
RET_HEADS = 4
RET_WIDTH = 512
RET_DK = 128
RET_THETA = 10000.0
RET_BLOCK = 256
ATT_HD = 64
ATT_HEADS = 8
ATT_WIDTH = 512
DILATIONS = (1, 4, 16)
MAX_WINDOW = 2048
SPAN = 128
DIL_BLOCK = 128
ROPE_THETA = 10000.0
D_FF = 4096
NORM_EPS = 1e-6
NEG_INF = -1e30
PAST_LEN = 16384
N_GROUPS = 7
GROUP_W = 512
LANES = 128
VMEM_LIMIT = 56 * 1024 * 1024

LOG2_E = math.log2(math.e)
LOG_G = tuple(math.log1p(-(2.0 ** (-5.0 - h))) for h in range(RET_HEADS))


def _dot(a, b):
    return jnp.dot(a, b, preferred_element_type=F32)


def _dot_nt(a, b):
    return lax.dot_general(a, b, (((1,), (1,)), ((), ())), preferred_element_type=F32)


def _dot_tn(a, b):
    return lax.dot_general(a, b, (((0,), (0,)), ((), ())), preferred_element_type=F32)


def _rms(x):
    return x * lax.rsqrt(jnp.mean(x * x, axis=-1, keepdims=True) + NORM_EPS)


def _silu(x):
    return x / (1.0 + jnp.exp(-x))


def _tied_zero(values, rows):
    bits = None
    for v in values:
        assert v.shape[0] % rows == 0 and v.shape[1] % LANES == 0
        for r0 in range(0, v.shape[0], rows):
            for c0 in range(0, v.shape[1], LANES):
                b = pltpu.bitcast(v[r0:r0 + rows, c0:c0 + LANES], jnp.uint32)
                bits = b if bits is None else bits | b
    sixteen = jnp.full(bits.shape, 16, jnp.uint32)
    return pltpu.bitcast(lax.shift_right_logical(lax.shift_right_logical(bits, sixteen), sixteen), F32)


def _params(sem):
    return pltpu.CompilerParams(dimension_semantics=sem, vmem_limit_bytes=VMEM_LIMIT)


def _const_spec(shape):
    nd = len(shape)
    return pl.BlockSpec(shape, lambda *_: (0,) * nd, pipeline_mode=pl.Buffered(1))


def _mod_kernel(c_ref, w_ref, b_ref, o_ref):
    a = _silu(c_ref[...]).astype(BF16)
    o_ref[...] = _dot(a, w_ref[...].astype(BF16)) + b_ref[...]


def _modulation(c, w_ada, b_ada):
    rows, d = c.shape
    n = w_ada.shape[1]
    tn = 1536
    return pl.pallas_call(
        _mod_kernel,
        grid=(n // tn,),
        in_specs=[pl.BlockSpec((rows, d), lambda j: (0, 0)),
                  pl.BlockSpec((d, tn), lambda j: (0, j)),
                  pl.BlockSpec((1, tn), lambda j: (0, j))],
        out_specs=pl.BlockSpec((rows, tn), lambda j: (0, j)),
        out_shape=jax.ShapeDtypeStruct((rows, n), F32),
        compiler_params=_params(("arbitrary",)),
        name="adaln_mod",
    )(c, w_ada, b_ada.reshape(1, n))


def _rotate_pairs(x, cos, sin):
    lane = lax.broadcasted_iota(jnp.int32, x.shape, 1)
    partner = jnp.where(lane % 2 == 0, pltpu.roll(x, LANES - 1, 1), pltpu.roll(x, 1, 1))
    return x * cos + partner * sin


def _rotate_half(x, cos, sin):
    lane = lax.broadcasted_iota(jnp.int32, x.shape, 1)
    half = ATT_HD // 2
    partner = jnp.where(lane % ATT_HD < half, pltpu.roll(x, LANES - half, 1), pltpu.roll(x, half, 1))
    return x * cos + partner * sin


def _store_regrouped(val, c, tok_ref, r4_ref, r16_ref, zs, s4):
    sl = slice(c * LANES, (c + 1) * LANES)
    tok_ref[:, sl] = val.astype(tok_ref.dtype)
    tm = val.shape[0]
    zs[c] = val
    for q in range(4):
        g4 = zs[c, pl.ds(q, tm // 4, stride=4), :]
        r4_ref[q, :, sl] = g4.astype(r4_ref.dtype)
        s4[c, q] = g4
    for q in range(4):
        for a in range(4):
            r16_ref[4 * a + q, :, sl] = s4[c, q, pl.ds(a, tm // 16, stride=4), :].astype(r16_ref.dtype)


def _proj_kernel(x_ref, mod_ref, g_ref, wf_ref, rows_ref, tile_ref, *refs, regroup, ncast=0):
    if regroup:
        cast_in, refs = refs[:ncast], refs[ncast:]
        (rq_ref, rk_ref, rv_ref, rg_ref, aq_ref, ak_ref, av_ref, akf_ref, avf_ref,
         aq4_ref, ak4_ref, av4_ref, aq16_ref, ak16_ref, av16_ref) = refs[:15]
        cast_out, (w_ref, zs, s4) = refs[15:15 + ncast], refs[15 + ncast:]
        for src, dst in zip(cast_in, cast_out):
            dst[...] = src[...].astype(dst.dtype)
    else:
        rq_ref, rk_ref, rv_ref, rg_ref, aq_ref, ak_ref, av_ref, akf_ref, avf_ref, w_ref = refs

    @pl.when(jnp.logical_and(pl.program_id(0) == 0, pl.program_id(1) == 0))
    def _():
        for gi in range(N_GROUPS):
            sl = slice(gi * GROUP_W, (gi + 1) * GROUP_W)
            w_ref[:, sl] = wf_ref[:, sl].astype(w_ref.dtype)

    d = D_MODEL
    x = x_ref[...]
    h = (_rms(x) * g_ref[...]) * (1.0 + mod_ref[:, d:2 * d]) + mod_ref[:, 0:d]
    h = h.astype(BF16)
    t = tile_ref[...]
    cr = t[0:1] * rows_ref[0] - t[1:2] * rows_ref[1]
    sr = (t[1:2] * rows_ref[0] + t[0:1] * rows_ref[1]) * t[4:5]
    ca = t[2:3] * rows_ref[2] - t[3:4] * rows_ref[3]
    sa = (t[3:4] * rows_ref[2] + t[2:3] * rows_ref[3]) * t[5:6]
    nch = GROUP_W // LANES

    def group(gi):
        return _dot(h, w_ref[:, gi * GROUP_W:(gi + 1) * GROUP_W])

    z = group(0)
    for c in range(nch):
        sl = slice(c * LANES, (c + 1) * LANES)
        rq_ref[:, sl] = _rotate_pairs(z[:, sl], cr, sr).astype(rq_ref.dtype)
    z = group(1)
    for c in range(nch):
        sl = slice(c * LANES, (c + 1) * LANES)
        rk_ref[:, sl] = (_rotate_pairs(z[:, sl], cr, sr) * (RET_DK ** -0.5)).astype(rk_ref.dtype)
    rv_ref[...] = group(2).astype(rv_ref.dtype)
    rg_ref[...] = group(3).astype(rg_ref.dtype)
    def emit(val, c, tok_ref, r4_ref, r16_ref):
        if regroup:
            _store_regrouped(val, c, tok_ref, r4_ref, r16_ref, zs, s4)
        else:
            tok_ref[:, c * LANES:(c + 1) * LANES] = val.astype(tok_ref.dtype)

    r4 = (aq4_ref, ak4_ref, av4_ref) if regroup else (None,) * 3
    r16 = (aq16_ref, ak16_ref, av16_ref) if regroup else (None,) * 3
    z = group(4)
    for c in range(nch):
        sl = slice(c * LANES, (c + 1) * LANES)
        emit(_rotate_half(z[:, sl], ca, sa) * (ATT_HD ** -0.5 * LOG2_E), c, aq_ref, r4[0], r16[0])
    z = group(5)
    for c in range(nch):
        sl = slice(c * LANES, (c + 1) * LANES)
        r = _rotate_half(z[:, sl], ca, sa)
        akf_ref[:, sl] = r
        emit(r, c, ak_ref, r4[1], r16[1])
    z = group(6)
    avf_ref[...] = z
    for c in range(nch):
        emit(z[:, c * LANES:(c + 1) * LANES], c, av_ref, r4[2], r16[2])


def _project(x, mod, g_pre, w_in_f32, tabs, *, tm, keep, act_dtype, regroup, weights=()):
    b, s, d = x.shape
    nt = s // tm
    rows_tab, tile_tab = tabs
    assert rows_tab.shape == (4, tm, LANES) and tile_tab.shape == (nt, 8, LANES)
    mod_rows = mod.shape[1]
    per_row = mod_rows != 1
    first_keep = (s - keep) // tm
    tok = pl.BlockSpec((None, tm, GROUP_W), lambda bi, i: (bi, i, 0))
    keep_spec = pl.BlockSpec((None, tm, GROUP_W), lambda bi, i: (bi, jnp.maximum(i - first_keep, 0), 0))
    tile_spec = pl.BlockSpec((None, 8, LANES), lambda bi, i: (i, 0, 0))
    mod_spec = pl.BlockSpec((None, tm if per_row else 1, 6 * d),
                            (lambda bi, i: (bi, i, 0)) if per_row else (lambda bi, i: (bi, 0, 0)))
    act = jax.ShapeDtypeStruct((b, s, GROUP_W), act_dtype)
    kept = jax.ShapeDtypeStruct((b, keep, GROUP_W), F32)
    out_specs = [tok, tok, tok, tok, tok, tok, tok, keep_spec, keep_spec]
    out_shape = [act, act, act, act, act, act, act, kept, kept]
    scratch = [pltpu.VMEM((d, N_GROUPS * GROUP_W), BF16)]
    if regroup:
        for r in (4, 16):
            out_specs += [pl.BlockSpec((None, r, tm // r, GROUP_W), lambda bi, i: (bi, 0, i, 0))] * 3
            out_shape += [jax.ShapeDtypeStruct((b, r, s // r, GROUP_W), act_dtype)] * 3
        nch = GROUP_W // LANES
        scratch += [pltpu.VMEM((nch, tm, LANES), F32), pltpu.VMEM((nch, 4, tm // 4, LANES), F32)]
    steps = b * nt
    assert regroup or not weights
    assert all(m.shape[0] % (16 * steps) == 0 for m in weights)
    slab = lambda m: pl.BlockSpec((m.shape[0] // steps, m.shape[1]), lambda bi, i: (bi * nt + i, 0))
    return pl.pallas_call(
        functools.partial(_proj_kernel, regroup=regroup, ncast=len(weights)),
        grid=(b, nt),
        in_specs=[pl.BlockSpec((None, tm, d), lambda bi, i: (bi, i, 0)),
                  mod_spec,
                  _const_spec((1, d)),
                  _const_spec((d, N_GROUPS * GROUP_W)),
                  _const_spec((4, tm, LANES)), tile_spec] + [slab(m) for m in weights],
        out_specs=out_specs + [slab(m) for m in weights],
        out_shape=out_shape + [jax.ShapeDtypeStruct(m.shape, BF16) for m in weights],
        scratch_shapes=scratch,
        compiler_params=_params(("arbitrary", "arbitrary")),
        name="in_proj",
    )(x, mod, g_pre.reshape(1, d), w_in_f32, rows_tab, tile_tab, *weights)


def _rotation_constants(row_pos, tile_pos):
    lane = np.arange(LANES)
    inv_r = (1.0 / RET_THETA ** np.linspace(0.0, 1.0, RET_DK // 2))[lane // 2]
    inv_a = (1.0 / ROPE_THETA ** (np.arange(0, ATT_HD, 2) / ATT_HD))[lane % (ATT_HD // 2)]
    rp = np.asarray(row_pos, np.float64)[:, None]
    tp = np.asarray(tile_pos, np.float64)[:, None]
    rows = np.stack([np.cos(rp * inv_r), np.sin(rp * inv_r), np.cos(rp * inv_a), np.sin(rp * inv_a)])
    sign_r = np.where(lane % 2 == 0, -1.0, 1.0)
    sign_a = np.where(lane % ATT_HD < ATT_HD // 2, -1.0, 1.0)
    ones = np.ones_like(tp * inv_r)
    tiles = np.stack([np.cos(tp * inv_r), np.sin(tp * inv_r), np.cos(tp * inv_a), np.sin(tp * inv_a),
                      ones * sign_r, ones * sign_a, 0 * ones, 0 * ones], axis=1)
    return jnp.asarray(rows, F32), jnp.asarray(tiles, F32)


def _ret_readout(o, g, gain):
    return (_rms(o) * gain) * _silu(g.astype(F32))


def _ret_prompt_kernel(q_ref, k_ref, v_ref, g_ref, gain_ref, intra_ref, qd_ref, kd_ref, o_ref, sfin_ref, s_scr,
                       *, nchunk, c_len):
    i = pl.program_id(1)

    @pl.when(i == 0)
    def _():
        s_scr[...] = jnp.zeros_like(s_scr)

    for h in range(RET_HEADS):
        intra, q_dec, k_dec = intra_ref[h], qd_ref[h], kd_ref[h]
        c_dec = math.exp(LOG_G[h] * c_len)
        hs = slice(h * RET_DK, (h + 1) * RET_DK)
        state = s_scr[h]
        for c in range(nchunk):
            rs = slice(c * c_len, (c + 1) * c_len)
            q = q_ref[rs, hs]
            k = k_ref[rs, hs]
            v = v_ref[rs, hs]
            a = _dot_nt(q, k) * intra
            lhs = jnp.concatenate([a.astype(BF16), (q.astype(F32) * q_dec).astype(BF16)], axis=1)
            o = _dot(lhs, jnp.concatenate([v, state.astype(BF16)], axis=0))
            kd = (k.astype(F32) * k_dec).astype(BF16)
            state = state * c_dec + _dot_tn(kd, v)
            o_ref[rs, hs] = _ret_readout(o, g_ref[rs, hs], gain_ref[:, hs]).astype(o_ref.dtype)
        s_scr[h] = state

    @pl.when(i == pl.num_programs(1) - 1)
    def _():
        sfin_ref[...] = s_scr[...]


def _retention_prompt(rq, rk, rv, rg, ret_gain, *, tc, chunk):
    b, s, w = rq.shape
    nblk = s // tc
    tok = pl.BlockSpec((None, tc, w), lambda bi, i: (bi, i, 0))
    idx = np.arange(chunk, dtype=np.float64)
    lg = np.asarray(LOG_G, np.float64)[:, None, None]
    diff = idx[:, None] - idx[None, :]
    intra = np.where(diff >= 0, np.exp(lg * np.maximum(diff, 0.0)), 0.0)
    lanes = np.ones((1, 1, RET_DK))
    qd = np.exp(lg * (idx[None, :, None] + 1.0)) * lanes
    kd = np.exp(lg * (chunk - 1.0 - idx[None, :, None])) * lanes
    tables = [jnp.asarray(t, F32) for t in (intra, qd, kd)]
    return pl.pallas_call(
        functools.partial(_ret_prompt_kernel, nchunk=tc // chunk, c_len=chunk),
        grid=(b, nblk),
        in_specs=[tok, tok, tok, tok, _const_spec((1, w))] + [_const_spec(t.shape) for t in tables],
        out_specs=[tok, pl.BlockSpec((None, RET_HEADS, RET_DK, RET_DK), lambda bi, i: (bi, 0, 0, 0))],
        out_shape=[jax.ShapeDtypeStruct((b, s, w), BF16),
                   jax.ShapeDtypeStruct((b, RET_HEADS, RET_DK, RET_DK), F32)],
        scratch_shapes=[pltpu.VMEM((RET_HEADS, RET_DK, RET_DK), F32)],
        compiler_params=_params(("arbitrary", "arbitrary")),
        name="retention_prompt",
    )(rq, rk, rv, rg, ret_gain.reshape(1, w), *tables)


def _att_host_kernel(q_ref, k_ref, v_ref, kp_ref, vp_ref, oh_ref, bias_ref, gq_ref, gkn_ref, gvn_ref, gck_ref,
                     gcv_ref, gcc_ref, gcn_ref, *rest, nsub, ln, wbuf, nunit):
    pv_ref, ml_ref, go_ref, gko_ref, gvo_ref, kcat, vcat = rest[-7:]
    steps = pl.num_programs(0) * pl.num_programs(1) * pl.num_programs(2)
    t = (pl.program_id(0) * pl.num_programs(1) + pl.program_id(1)) * pl.num_programs(2) + pl.program_id(2)
    fresh = jnp.logical_or(t == 0, _hosted_unit(t, 0, nunit, steps) != _hosted_unit(t - 1, 0, nunit, steps))

    @pl.when(fresh)
    def _():
        _sample_heads(gq_ref, gkn_ref, gvn_ref, gck_ref, gcv_ref, gcc_ref, gcn_ref, go_ref, gko_ref, gvo_ref,
                      ln=ln, wbuf=wbuf)

    _att_block(q_ref, k_ref, v_ref, kp_ref, vp_ref, oh_ref, bias_ref, pv_ref, ml_ref, kcat, vcat, nsub=nsub)


def _att_block(q_ref, k_ref, v_ref, kp_ref, vp_ref, oh_ref, bias_ref, pv_ref, ml_ref, kcat, vcat, *, nsub):
    i = pl.program_id(2)
    blk = DIL_BLOCK
    for g in range(q_ref.shape[0]):
        kcat[g, 0:blk, :] = kp_ref[g]
        kcat[g, blk:, :] = k_ref[g]
        vcat[g, 0:blk, :] = vp_ref[g]
        vcat[g, blk:, :] = v_ref[g]
    lane = lax.broadcasted_iota(jnp.int32, (blk, LANES), 1)
    low = lane < ATT_HD
    quarter = lane // (ATT_HD // 2)
    onehot = oh_ref[...]
    bias_rest = bias_ref[1]
    bias_first = jnp.where(i == 0, bias_ref[0], bias_rest)
    ones = jnp.ones((2 * blk, LANES), BF16)
    low2 = lax.broadcasted_iota(jnp.int32, (2 * blk, LANES), 1) < ATT_HD
    for g, j, hp in itertools.product(range(q_ref.shape[0]), range(nsub), range(ATT_WIDTH // LANES)):
        bias = bias_first if j == 0 else bias_rest
        rows = slice(j * blk, (j + 1) * blk)
        krows = slice(j * blk, (j + 2) * blk)
        cols = slice(hp * LANES, (hp + 1) * LANES)
        q = q_ref[g, rows, cols]
        zero = jnp.zeros_like(q)
        q2 = jnp.concatenate([jnp.where(low, q, zero), jnp.where(low, zero, q)], axis=0)
        s = _dot_nt(jnp.concatenate([q2, onehot], axis=1), jnp.concatenate([kcat[g, krows, cols], bias], axis=1))
        m = jnp.max(s, axis=-1, keepdims=True)
        p = jnp.exp2(s - m).astype(BF16)
        vv = vcat[g, krows, cols]
        pv0 = _dot(p[0:blk], jnp.where(low2, vv, ones))
        pv1 = _dot(p[blk:], jnp.where(low2, ones, vv))
        pv_ref[g, rows, cols] = jnp.where(low, pv0, pv1).astype(pv_ref.dtype)
        ml_ref[g, rows, cols] = jnp.where(
            quarter == 0, m[0:blk], jnp.where(quarter == 1, pv1, jnp.where(quarter == 2, m[blk:], pv0)))


def _band_tables():
    jj = np.arange(2 * DIL_BLOCK)[:, None]
    qi = np.arange(DIL_BLOCK)[None, :]
    bias = np.stack([np.where((jj >= np.maximum(qi, fk)) & (jj <= qi + SPAN), 0.0, NEG_INF)
                     for fk in (DIL_BLOCK, 0)])
    onehot = np.tile(np.eye(DIL_BLOCK), (2, 1))
    return jnp.asarray(onehot, BF16), jnp.asarray(bias, BF16)


def _hosted_unit(t, unit0, nunit, steps):
    return unit0 + (t * nunit) // steps


def _dilated_attention(aq, ak, av, sample, carry, *, tq, groups, unit0, nunit):
    b, dil, m_len, w = aq.shape
    nsub = tq // DIL_BLOCK
    nblk = m_len // tq
    ngrp = dil // groups
    own = pl.BlockSpec((None, groups, tq, w), lambda bi, r, i: (bi, r, i, 0))
    prv = pl.BlockSpec((None, groups, DIL_BLOCK, w), lambda bi, r, i: (bi, r, jnp.maximum(i * nsub - 1, 0), 0))
    onehot, bias = _band_tables()
    in_specs = [own, own, own, prv, prv, _const_spec(onehot.shape), _const_spec(bias.shape)]
    args = [aq, ak, av, ak, av, onehot, bias]
    out_shape = [jax.ShapeDtypeStruct((b, dil, m_len, w), BF16), jax.ShapeDtypeStruct((b, dil, m_len, w), F32)]
    scratch = [pltpu.VMEM((groups, tq + DIL_BLOCK, w), BF16), pltpu.VMEM((groups, tq + DIL_BLOCK, w), BF16)]
    common = dict(grid=(b, ngrp, nblk), scratch_shapes=scratch, name=f"dilated_attn_d{dil}",
                  compiler_params=_params(("arbitrary", "arbitrary", "arbitrary")))
    if nunit == 0:
        return pl.pallas_call(functools.partial(_att_block, nsub=nsub), in_specs=in_specs,
                              out_specs=[own, own], out_shape=out_shape, **common)(*args)

    gq, gkn, gvn, gck, gcv, cnt_c, cnt_n = sample
    nb, _, wbuf = gck.shape
    ln = gq.shape[0] // nb
    steps = b * ngrp * nblk
    assert nunit <= steps

    def unit(bi, r, i):
        return _hosted_unit((bi * ngrp + r) * nblk + i, unit0, nunit, steps)

    rows = pl.BlockSpec((ln, w), lambda bi, r, i: (unit(bi, r, i), 0))
    window = pl.BlockSpec((None, w, wbuf), lambda bi, r, i: (unit(bi, r, i), 0, 0))
    in_specs += [rows, rows, rows, window, window, _const_spec(cnt_c.shape), _const_spec(cnt_n.shape)]
    args += [gq, gkn, gvn, gck, gcv, cnt_c, cnt_n]
    aliases = {}
    if carry is not None:
        aliases = {len(args) + n: 2 + n for n in range(3)}
        in_specs += [pl.BlockSpec(memory_space=pl.ANY)] * 3
        args += list(carry)
    out_shape += [jax.ShapeDtypeStruct((nb * ln, w), F32), jax.ShapeDtypeStruct((nb, w, wbuf), F32),
                  jax.ShapeDtypeStruct((nb, w, wbuf), F32)]
    return pl.pallas_call(
        functools.partial(_att_host_kernel, nsub=nsub, ln=ln, wbuf=wbuf, nunit=nunit),
        in_specs=in_specs,
        out_specs=[own, own, rows, window, window],
        out_shape=out_shape,
        input_output_aliases=aliases,
        **common,
    )(*args)


def _merge_patterns(pv1_ref, ml1_ref, pv4_ref, ml4_ref, pv16_ref, ml16_ref, tok, s4, att):
    tm = att.shape[0]
    half = ATT_HD // 2
    lane = lax.broadcasted_iota(jnp.int32, (tm, LANES), 1)
    first = lane % ATT_HD < half

    def unpack(ml):
        near = pltpu.roll(ml, half, 1)
        return jnp.where(first, ml, near), jnp.where(first, near, pltpu.roll(ml, 2 * half, 1))

    for c in range(ATT_WIDTH // LANES):
        sl = slice(c * LANES, (c + 1) * LANES)
        for n, (r4, r16) in enumerate(((pv4_ref, pv16_ref), (ml4_ref, ml16_ref))):
            for q in range(4):
                tok[n, c, pl.ds(q, tm // 4, stride=4), :] = r4[q, :, sl].astype(F32)
                for a in range(4):
                    s4[q, pl.ds(a, tm // 16, stride=4), :] = r16[4 * a + q, :, sl].astype(F32)
            for q in range(4):
                tok[2 + n, c, pl.ds(q, tm // 4, stride=4), :] = s4[q]
        (m1, l1), (m4, l4), (m16, l16) = unpack(ml1_ref[:, sl]), unpack(tok[1, c]), unpack(tok[3, c])
        top = jnp.maximum(m1, jnp.maximum(m4, m16))
        e1, e4, e16 = jnp.exp2(m1 - top), jnp.exp2(m4 - top), jnp.exp2(m16 - top)
        num = e1 * pv1_ref[:, sl].astype(F32) + e4 * tok[0, c] + e16 * tok[2, c]
        den = e1 * l1 + e4 * l4 + e16 * l16
        merged = num / den
        att[:, sl] = merged.astype(att.dtype)
        yield merged


def _tail_body(x_ref, ret_ref, att_h, mod_ref, gpm_ref, gpf_ref, gqf_ref, wo_ref, wu_ref, wd_ref, y_ref,
               filler=iter(())):
    d = D_MODEL
    x = x_ref[...]
    mixed = _dot(ret_ref[...].astype(BF16), wo_ref[0:RET_WIDTH, :]) + _dot(att_h, wo_ref[RET_WIDTH:, :])
    x1 = x + mod_ref[:, 2 * d:3 * d] * (_rms(mixed) * gpm_ref[...])
    h = ((_rms(x1) * gpf_ref[...]) * (1.0 + mod_ref[:, 4 * d:5 * d]) + mod_ref[:, 3 * d:4 * d]).astype(BF16)
    fc = 1024
    f = None
    for c in range(D_FF // fc):
        u = jnp.maximum(_dot(h, wu_ref[:, c * fc:(c + 1) * fc]), 0.0)
        piece = next(filler, None)
        if piece is not None:
            u = jnp.concatenate([u[:, 0:LANES] + _tied_zero([piece], u.shape[0]), u[:, LANES:]], axis=1)
        part = _dot((u * u).astype(BF16), wd_ref[c * fc:(c + 1) * fc, :])
        f = part if f is None else f + part
    for _ in filler:
        pass
    y_ref[...] = x1 + mod_ref[:, 5 * d:6 * d] * (_rms(f) * gqf_ref[...])


def _tail_kernel(x_ref, ret_ref, att_ref, *rest):
    _tail_body(x_ref, ret_ref, att_ref[...].astype(BF16), *rest)


def _tail_merge_kernel(x_ref, ret_ref, pv1_ref, ml1_ref, pv4_ref, ml4_ref, pv16_ref, ml16_ref, mod_ref,
                       gpm_ref, gpf_ref, gqf_ref, wo_ref, wu_ref, wd_ref, y_ref, tok, s4, att):
    t = pl.program_id(0)
    merge = functools.partial(_merge_patterns, pv1_ref, ml1_ref, pv4_ref, ml4_ref, pv16_ref, ml16_ref,
                              tok, s4, att)

    @pl.when(t == 0)
    def _():
        for _ in merge():
            pass

    @pl.when(t > 0)
    def _():
        att_h = att[...]
        _tail_body(x_ref, ret_ref, att_h, mod_ref, gpm_ref, gpf_ref, gqf_ref, wo_ref, wu_ref, wd_ref, y_ref,
                   filler=merge())


def _tail(x, ret_h, att, mod, g_post_mix, g_pre_ffn, g_post_ffn, wo_bf, wu_bf, wd_bf, *, tm):
    b, s, d = x.shape
    nt = s // tm
    per_row = mod.shape[1] != 1
    weights = [_const_spec((1, d)), _const_spec((1, d)), _const_spec((1, d)),
               _const_spec((d, d)), _const_spec((d, D_FF)), _const_spec((D_FF, d))]
    weight_args = (g_post_mix.reshape(1, d), g_pre_ffn.reshape(1, d), g_post_ffn.reshape(1, d),
                   wo_bf, wu_bf, wd_bf)
    out_shape = jax.ShapeDtypeStruct((b, s, d), F32)
    if not isinstance(att, (tuple, list)):
        mod_spec = pl.BlockSpec((None, tm if per_row else 1, 6 * d),
                                (lambda bi, i: (bi, i, 0)) if per_row else (lambda bi, i: (bi, 0, 0)))
        tok = lambda w: pl.BlockSpec((None, tm, w), lambda bi, i: (bi, i, 0))
        return pl.pallas_call(
            _tail_kernel,
            grid=(b, nt),
            in_specs=[tok(d), tok(RET_WIDTH), tok(ATT_WIDTH), mod_spec, *weights],
            out_specs=tok(d),
            out_shape=out_shape,
            compiler_params=_params(("arbitrary", "arbitrary")),
            name="out_proj_mlp",
        )(x, ret_h, att, mod, *weight_args)

    assert not per_row
    last = b * nt - 1
    cur = lambda t: jnp.maximum(t - 1, 0)
    nxt = lambda t: jnp.minimum(t, last)
    tok = lambda w: pl.BlockSpec((None, tm, w), lambda t: (cur(t) // nt, cur(t) % nt, 0))
    mod_spec = pl.BlockSpec((None, 1, 6 * d), lambda t: (cur(t) // nt, 0, 0))
    split = lambda r: pl.BlockSpec((None, r, tm // r, ATT_WIDTH), lambda t: (nxt(t) // nt, 0, nxt(t) % nt, 0))
    one = pl.BlockSpec((None, None, tm, ATT_WIDTH), lambda t: (nxt(t) // nt, 0, nxt(t) % nt, 0))
    nch = ATT_WIDTH // LANES
    return pl.pallas_call(
        _tail_merge_kernel,
        grid=(b * nt + 1,),
        in_specs=[tok(d), tok(RET_WIDTH), one, one, split(4), split(4), split(16), split(16), mod_spec, *weights],
        out_specs=tok(d),
        out_shape=out_shape,
        scratch_shapes=[pltpu.VMEM((4, nch, tm, LANES), F32), pltpu.VMEM((4, tm // 4, LANES), F32),
                        pltpu.VMEM((tm, ATT_WIDTH), BF16)],
        compiler_params=_params(("arbitrary",)),
        name="out_proj_mlp",
    )(x, ret_h, *att, mod, *weight_args)


def _ret_sample_kernel(q_ref, k_ref, v_ref, g_ref, gain_ref, dm_ref, qd_ref, kd_ref, cd_ref, s_ref,
                       o_ref, sn_ref, o_scr, qd_scr, kt_scr, *, nb, ln):
    q = q_ref[...].astype(F32)
    k = k_ref[...].astype(F32)
    v = v_ref[...].astype(BF16)
    a = _dot_nt(q.astype(BF16), k.astype(BF16)) * dm_ref[...]
    o_scr[...] = _dot(a.astype(BF16), v)
    qd_scr[...] = q * qd_ref[...]
    kt_scr[...] = (k * kd_ref[...]).T
    c_dec = cd_ref[0:1, :]
    col = lax.broadcasted_iota(jnp.int32, kt_scr.shape, 1)

    def body(bi, carry):
        rows = pl.ds(pl.multiple_of(bi * ln, ln), ln)
        s0 = s_ref[bi]
        o_scr[rows, :] += _dot(qd_scr[rows, :].astype(BF16), s0.astype(BF16))
        mine = (col >= bi * ln) & (col < (bi + 1) * ln)
        kt = jnp.where(mine, kt_scr[...], 0.0).astype(BF16)
        sn_ref[bi] = s0 * c_dec + _dot(kt, v)
        return carry

    lax.fori_loop(0, nb, body, 0, unroll=math.gcd(nb, 8))
    o_ref[...] = _ret_readout(o_scr[...], g_ref[...], gain_ref[...]).astype(o_ref.dtype)


def _retention_sample(rq, rk, rv, rg, ret_gain, state, *, nb, ln):
    n = nb * ln
    idx = np.arange(n)
    same = (idx[:, None] // ln == idx[None, :] // ln) & (idx[:, None] >= idx[None, :])
    diff = np.maximum(idx[:, None] - idx[None, :], 0).astype(np.float64)
    step = (idx % ln).astype(np.float64)
    lg = np.asarray(LOG_G, np.float64)
    dm = np.where(same[None], np.exp(lg[:, None, None] * diff[None]), 0.0)
    qd = np.broadcast_to(np.exp(lg[:, None] * (step + 1.0))[:, :, None], (RET_HEADS, n, RET_DK))
    kd = np.broadcast_to(np.exp(lg[:, None] * (ln - 1.0 - step))[:, :, None], (RET_HEADS, n, RET_DK))
    cd = np.broadcast_to(np.exp(lg * ln)[:, None, None], (RET_HEADS, 8, RET_DK))
    tab = lambda t: jnp.asarray(np.ascontiguousarray(t), F32)
    col = pl.BlockSpec((n, RET_DK), lambda h: (0, h))
    per_head = lambda r, c: pl.BlockSpec((None, r, c), lambda h: (h, 0, 0))
    st = pl.BlockSpec((nb, None, RET_DK, RET_DK), lambda h: (0, h, 0, 0))
    return pl.pallas_call(
        functools.partial(_ret_sample_kernel, nb=nb, ln=ln),
        grid=(RET_HEADS,),
        in_specs=[col, col, col, col, pl.BlockSpec((1, RET_DK), lambda h: (0, h)),
                  per_head(n, n), per_head(n, RET_DK), per_head(n, RET_DK), per_head(8, RET_DK), st],
        out_specs=[col, st],
        out_shape=[jax.ShapeDtypeStruct((n, RET_WIDTH), BF16),
                   jax.ShapeDtypeStruct((nb, RET_HEADS, RET_DK, RET_DK), F32)],
        scratch_shapes=[pltpu.VMEM((n, RET_DK), F32), pltpu.VMEM((n, RET_DK), F32),
                        pltpu.VMEM((RET_DK, n), F32)],
        compiler_params=_params(("arbitrary",)),
        name="retention_sample",
    )(rq, rk, rv, rg, ret_gain.reshape(1, RET_WIDTH), tab(dm), tab(qd), tab(kd), tab(cd), state)


def _shift_window(old_ref, new_t, out_ref, *, ln, wbuf):
    lane = lax.broadcasted_iota(jnp.int32, new_t.shape, 1)
    ncol = wbuf // LANES
    rolled = pltpu.roll(old_ref[:, 0:LANES], LANES - ln, 1)
    for c in range(ncol):
        nxt = pltpu.roll(old_ref[:, (c + 1) * LANES:(c + 2) * LANES] if c + 1 < ncol else new_t, LANES - ln, 1)
        out_ref[:, c * LANES:(c + 1) * LANES] = jnp.where(lane < LANES - ln, rolled, nxt)
        rolled = nxt


def _new_rows_minor(new_ref, ln):
    pad = jnp.zeros((LANES - ln, new_ref.shape[1]), F32)
    return jnp.concatenate([new_ref[...], pad], axis=0).T


def _sample_heads(q_ref, kn_ref, vn_ref, ck_ref, cv_ref, cc_ref, cn_ref, o_ref, ko_ref, vo_ref, *, ln, wbuf):
    kn_t = _new_rows_minor(kn_ref, ln)
    vn_t = _new_rows_minor(vn_ref, ln)
    _shift_window(ck_ref, kn_t, ko_ref, ln=ln, wbuf=wbuf)
    _shift_window(cv_ref, vn_t, vo_ref, ln=ln, wbuf=wbuf)
    q = q_ref[...]
    nh = q.shape[1] // ATT_HD
    lane = lax.broadcasted_iota(jnp.int32, q.shape, 1)
    heads = [lane // ATT_HD == h for h in range(nh)]
    qm = jnp.concatenate([jnp.where(hm, q, 0.0) for hm in heads], axis=0).astype(BF16)
    cnt_c = jnp.concatenate([cc_ref[...]] * nh, axis=0)
    cnt_n = jnp.concatenate([cn_ref[...]] * nh, axis=0)
    s_c = jnp.where(cnt_c > 0, _dot(qm, ck_ref[...].astype(BF16)), NEG_INF)
    s_n = jnp.where(cnt_n > 0, _dot(qm, kn_t.astype(BF16)), NEG_INF)
    m = jnp.maximum(jnp.max(s_c, axis=-1, keepdims=True), jnp.max(s_n, axis=-1, keepdims=True))
    p_c = cnt_c * jnp.exp2(s_c - m)
    p_n = cnt_n * jnp.exp2(s_n - m)
    l = jnp.sum(p_c, axis=-1, keepdims=True) + jnp.sum(p_n, axis=-1, keepdims=True)
    o = _dot_nt(p_c.astype(BF16), cv_ref[...].astype(BF16)) + _dot_nt(p_n.astype(BF16), vn_t.astype(BF16))
    o = o / l
    acc = jnp.zeros_like(q)
    for h, hm in enumerate(heads):
        acc = acc + jnp.where(hm, o[h * ln:(h + 1) * ln, :], 0.0)
    o_ref[...] = acc.astype(o_ref.dtype)


def _pattern_counts(ln, wbuf):
    cnt = np.zeros((ln, wbuf + ln), np.float32)
    for dil in DILATIONS:
        for l in range(ln):
            for j in range(SPAN + 1):
                row = wbuf + l - dil * j
                if row >= 0:
                    cnt[l, row] += 1.0
    return cnt


def _count_tables(ln, wbuf):
    cnt = _pattern_counts(ln, wbuf)
    cnt_new = np.zeros((ln, LANES), np.float32)
    cnt_new[:, :ln] = cnt[:, wbuf:]
    return jnp.asarray(cnt[:, :wbuf]), jnp.asarray(cnt_new)


def _step(x_prompt, x_sample, c_prompt, c_sample, state_ret, cache_win_k, cache_win_v, w_ada, b_ada,
          g_pre_mix, g_post_mix, g_pre_ffn, g_post_ffn, w_in, ret_gain, w_o, w_up, w_down,
          *, tm, tc, tq):
    assert w_in.shape[0] == 1, "single-layer step"
    bp, sp, d = x_prompt.shape
    nb, ln, _ = x_sample.shape
    wbuf = cache_win_k.shape[2]
    n_s = nb * ln

    w_in_f32 = w_in[0]

    rows = bp + nb
    pad = (-rows) % 8
    c_all = jnp.concatenate([c_prompt, c_sample, jnp.zeros((pad, d), F32)], axis=0)
    mod = _modulation(c_all, w_ada[0], b_ada[0])
    mod_p = mod[:bp].reshape(bp, 1, 6 * d)
    mod_s = jnp.repeat(mod[bp:rows], ln, axis=0).reshape(1, n_s, 6 * d)

    tabs_p = _rotation_constants(np.arange(tm), np.arange(sp // tm) * tm)
    tabs_s = _rotation_constants(np.tile(np.arange(ln), nb), [PAST_LEN])

    keep = min(MAX_WINDOW, sp)
    (rq, rk, rv, rg, aq, ak, av, akf, avf, aq4, ak4, av4, aq16, ak16, av16, wo_bf, wu_bf, wd_bf) = _project(
        x_prompt, mod_p, g_pre_mix[0], w_in_f32, tabs_p, tm=tm, keep=keep, act_dtype=BF16, regroup=True,
        weights=(w_o[0], w_up[0], w_down[0]))
    xs = x_sample.reshape(1, n_s, d)
    srq, srk, srv, srg, saq, _, _, sakf, savf = _project(
        xs, mod_s, g_pre_mix[0], w_in_f32, tabs_s, tm=n_s, keep=n_s, act_dtype=F32, regroup=False)
    flat = lambda t: t.reshape(n_s, GROUP_W)

    to_minor = lambda t: jnp.transpose(t[0], (0, 2, 3, 1)).reshape(nb, ATT_WIDTH, wbuf)
    from_minor = lambda t: jnp.transpose(t.reshape(nb, ATT_HEADS, ATT_HD, wbuf), (0, 3, 1, 2))[None]
    sample = (flat(saq), flat(sakf), flat(savf), to_minor(cache_win_k), to_minor(cache_win_v),
              *_count_tables(ln, wbuf))
    patterns = ((aq16, ak16, av16), (aq4, ak4, av4), (aq[:, None], ak[:, None], av[:, None]))
    steps = bp * sp // tq
    hosted, first = {}, 0
    for n in (1, 2, 0):
        hosted[n] = (first, min(steps, nb - first))
        first += hosted[n][1]
    assert first == nb
    carry, merged_in = None, []
    for n, qkv in enumerate(patterns):
        m_len = qkv[0].shape[2]
        rows = tq if hosted[n][1] else 2 * tq
        out = _dilated_attention(*qkv, sample, carry, tq=min(rows, m_len), groups=max(1, rows // m_len),
                                 unit0=hosted[n][0], nunit=hosted[n][1])
        merged_in = list(out[:2]) + merged_in
        carry = out[2:] or carry
    satt_h, k_out, v_out = carry

    ret_h, s_fin = _retention_prompt(rq, rk, rv, rg, ret_gain[0], tc=tc, chunk=RET_BLOCK)
    y_prompt = _tail(x_prompt, ret_h, tuple(merged_in), mod_p, g_post_mix[0], g_pre_ffn[0],
                     g_post_ffn[0], wo_bf, wu_bf, wd_bf, tm=tm)

    sret_h, s_new = _retention_sample(flat(srq), flat(srk), flat(srv), flat(srg), ret_gain[0],
                                      state_ret[0], nb=nb, ln=ln)
    y_sample = _tail(xs, sret_h.reshape(1, n_s, RET_WIDTH), satt_h.reshape(1, n_s, ATT_WIDTH), mod_s,
                     g_post_mix[0], g_pre_ffn[0], g_post_ffn[0], wo_bf, wu_bf, wd_bf, tm=n_s)

    cache_shape = (1, -1, keep, ATT_HEADS, ATT_HD)
    return (y_prompt,
            y_sample.reshape(nb, ln, d),
            s_fin[None],
            akf.reshape(cache_shape),
            avf.reshape(cache_shape),
            s_new[None],
            from_minor(k_out),
            from_minor(v_out))


def kernel(x_prompt, x_sample, c_prompt, c_sample, state_ret, cache_win_k, cache_win_v, w_ada, b_ada,
           g_pre_mix, g_post_mix, g_pre_ffn, g_post_ffn, w_in, ret_gain, w_o, w_up, w_down):
    return _step(x_prompt, x_sample, c_prompt, c_sample, state_ret, cache_win_k, cache_win_v, w_ada, b_ada,
                 g_pre_mix, g_post_mix, g_pre_ffn, g_post_ffn, w_in, ret_gain, w_o, w_up, w_down,
                 tm=512, tc=2048, tq=1024)
```

```python
import functools
import itertools
import math

import numpy as np
import jax
import jax.numpy as jnp
from jax import lax
from jax.experimental import pallas as pl
from jax.experimental.pallas import tpu as pltpu

F32 = jnp.float32
BF16 = jnp.bfloat16

D_MODEL = 1024
RET_HEADS = 4
RET_WIDTH = 512
RET_DK = 128
RET_THETA = 10000.0
RET_BLOCK = 256
ATT_HD = 64
ATT_HEADS = 8
ATT_WIDTH = 512
DILATIONS = (1, 4, 16)
MAX_WINDOW = 2048
SPAN = 128
DIL_BLOCK = 128
ROPE_THETA = 10000.0
D_FF = 4096
NORM_EPS = 1e-6
NEG_INF = -1e30
PAST_LEN = 16384
N_GROUPS = 7
GROUP_W = 512
LANES = 128
VMEM_LIMIT = 56 * 1024 * 1024

LOG2_E = math.log2(math.e)
LOG_G = tuple(math.log1p(-(2.0 ** (-5.0 - h))) for h in range(RET_HEADS))


def _dot(a, b):
    return jnp.dot(a, b, preferred_element_type=F32)


def _dot_nt(a, b):
    return lax.dot_general(a, b, (((1,), (1,)), ((), ())), preferred_element_type=F32)


def _dot_tn(a, b):
    return lax.dot_general(a, b, (((0,), (0,)), ((), ())), preferred_element_type=F32)


def _rms(x):
    return x * lax.rsqrt(jnp.mean(x * x, axis=-1, keepdims=True) + NORM_EPS)


def _silu(x):
    return x / (1.0 + jnp.exp(-x))


def _tied_zero(values, rows):
    bits = None
    for v in values:
        assert v.shape[0] % rows == 0 and v.shape[1] % LANES == 0
        for r0 in range(0, v.shape[0], rows):
            for c0 in range(0, v.shape[1], LANES):
                b = pltpu.bitcast(v[r0:r0 + rows, c0:c0 + LANES], jnp.uint32)
                bits = b if bits is None else bits | b
    sixteen = jnp.full(bits.shape, 16, jnp.uint32)
    return pltpu.bitcast(lax.shift_right_logical(lax.shift_right_logical(bits, sixteen), sixteen), F32)


def _params(sem):
    return pltpu.CompilerParams(dimension_semantics=sem, vmem_limit_bytes=VMEM_LIMIT)


def _const_spec(shape):
    nd = len(shape)
    return pl.BlockSpec(shape, lambda *_: (0,) * nd, pipeline_mode=pl.Buffered(1))


def _mod_kernel(c_ref, w_ref, b_ref, o_ref):
    a = _silu(c_ref[...]).astype(BF16)
    o_ref[...] = _dot(a, w_ref[...].astype(BF16)) + b_ref[...]


def _modulation(c, w_ada, b_ada):
    rows, d = c.shape
    n = w_ada.shape[1]
    tn = 1536
    return pl.pallas_call(
        _mod_kernel,
        grid=(n // tn,),
        in_specs=[pl.BlockSpec((rows, d), lambda j: (0, 0)),
                  pl.BlockSpec((d, tn), lambda j: (0, j)),
                  pl.BlockSpec((1, tn), lambda j: (0, j))],
        out_specs=pl.BlockSpec((rows, tn), lambda j: (0, j)),
        out_shape=jax.ShapeDtypeStruct((rows, n), F32),
        compiler_params=_params(("arbitrary",)),
        name="adaln_mod",
    )(c, w_ada, b_ada.reshape(1, n))


def _rotate_pairs(x, cos, sin):
    lane = lax.broadcasted_iota(jnp.int32, x.shape, 1)
    partner = jnp.where(lane % 2 == 0, pltpu.roll(x, LANES - 1, 1), pltpu.roll(x, 1, 1))
    return x * cos + partner * sin


def _rotate_half(x, cos, sin):
    lane = lax.broadcasted_iota(jnp.int32, x.shape, 1)
    half = ATT_HD // 2
    partner = jnp.where(lane % ATT_HD < half, pltpu.roll(x, LANES - half, 1), pltpu.roll(x, half, 1))
    return x * cos + partner * sin


def _store_regrouped(val, c, tok_ref, r4_ref, r16_ref, zs, s4):
    sl = slice(c * LANES, (c + 1) * LANES)
    tok_ref[:, sl] = val.astype(tok_ref.dtype)
    tm = val.shape[0]
    zs[c] = val
    for q in range(4):
        g4 = zs[c, pl.ds(q, tm // 4, stride=4), :]
        r4_ref[q, :, sl] = g4.astype(r4_ref.dtype)
        s4[c, q] = g4
    for q in range(4):
        for a in range(4):
            r16_ref[4 * a + q, :, sl] = s4[c, q, pl.ds(a, tm // 16, stride=4), :].astype(r16_ref.dtype)


def _proj_kernel(x_ref, mod_ref, g_ref, wf_ref, rows_ref, tile_ref, *refs, regroup, ncast=0):
    if regroup:
        cast_in, refs = refs[:ncast], refs[ncast:]
        (rq_ref, rk_ref, rv_ref, rg_ref, aq_ref, ak_ref, av_ref, akf_ref, avf_ref,
         aq4_ref, ak4_ref, av4_ref, aq16_ref, ak16_ref, av16_ref) = refs[:15]
        cast_out, (w_ref, zs, s4) = refs[15:15 + ncast], refs[15 + ncast:]
        for src, dst in zip(cast_in, cast_out):
            dst[...] = src[...].astype(dst.dtype)
    else:
        rq_ref, rk_ref, rv_ref, rg_ref, aq_ref, ak_ref, av_ref, akf_ref, avf_ref, w_ref = refs

    @pl.when(jnp.logical_and(pl.program_id(0) == 0, pl.program_id(1) == 0))
    def _():
        for gi in range(N_GROUPS):
            sl = slice(gi * GROUP_W, (gi + 1) * GROUP_W)
            w_ref[:, sl] = wf_ref[:, sl].astype(w_ref.dtype)

    d = D_MODEL
    x = x_ref[...]
    h = (_rms(x) * g_ref[...]) * (1.0 + mod_ref[:, d:2 * d]) + mod_ref[:, 0:d]
    h = h.astype(BF16)
    t = tile_ref[...]
    cr = t[0:1] * rows_ref[0] - t[1:2] * rows_ref[1]
    sr = (t[1:2] * rows_ref[0] + t[0:1] * rows_ref[1]) * t[4:5]
    ca = t[2:3] * rows_ref[2] - t[3:4] * rows_ref[3]
    sa = (t[3:4] * rows_ref[2] + t[2:3] * rows_ref[3]) * t[5:6]
    nch = GROUP_W // LANES

    def group(gi):
        return _dot(h, w_ref[:, gi * GROUP_W:(gi + 1) * GROUP_W])

    z = group(0)
    for c in range(nch):
        sl = slice(c * LANES, (c + 1) * LANES)
        rq_ref[:, sl] = _rotate_pairs(z[:, sl], cr, sr).astype(rq_ref.dtype)
    z = group(1)
    for c in range(nch):
        sl = slice(c * LANES, (c + 1) * LANES)
        rk_ref[:, sl] = (_rotate_pairs(z[:, sl], cr, sr) * (RET_DK ** -0.5)).astype(rk_ref.dtype)
    rv_ref[...] = group(2).astype(rv_ref.dtype)
    rg_ref[...] = group(3)
    def emit(val, c, tok_ref, r4_ref, r16_ref):
        if regroup:
            _store_regrouped(val, c, tok_ref, r4_ref, r16_ref, zs, s4)
        else:
            tok_ref[:, c * LANES:(c + 1) * LANES] = val.astype(tok_ref.dtype)

    r4 = (aq4_ref, ak4_ref, av4_ref) if regroup else (None,) * 3
    r16 = (aq16_ref, ak16_ref, av16_ref) if regroup else (None,) * 3
    z = group(4)
    for c in range(nch):
        sl = slice(c * LANES, (c + 1) * LANES)
        emit(_rotate_half(z[:, sl], ca, sa) * (ATT_HD ** -0.5 * LOG2_E), c, aq_ref, r4[0], r16[0])
    z = group(5)
    for c in range(nch):
        sl = slice(c * LANES, (c + 1) * LANES)
        r = _rotate_half(z[:, sl], ca, sa)
        akf_ref[:, sl] = r
        emit(r, c, ak_ref, r4[1], r16[1])
    z = group(6)
    avf_ref[...] = z
    for c in range(nch):
        emit(z[:, c * LANES:(c + 1) * LANES], c, av_ref, r4[2], r16[2])


def _project(x, mod, g_pre, w_in_f32, tabs, *, tm, keep, act_dtype, regroup, weights=()):
    b, s, d = x.shape
    nt = s // tm
    rows_tab, tile_tab = tabs
    assert rows_tab.shape == (4, tm, LANES) and tile_tab.shape == (nt, 8, LANES)
    mod_rows = mod.shape[1]
    per_row = mod_rows != 1
    first_keep = (s - keep) // tm
    tok = pl.BlockSpec((None, tm, GROUP_W), lambda bi, i: (bi, i, 0))
    keep_spec = pl.BlockSpec((None, tm, GROUP_W), lambda bi, i: (bi, jnp.maximum(i - first_keep, 0), 0))
    tile_spec = pl.BlockSpec((None, 8, LANES), lambda bi, i: (i, 0, 0))
    mod_spec = pl.BlockSpec((None, tm if per_row else 1, 6 * d),
                            (lambda bi, i: (bi, i, 0)) if per_row else (lambda bi, i: (bi, 0, 0)))
    act = jax.ShapeDtypeStruct((b, s, GROUP_W), act_dtype)
    full = jax.ShapeDtypeStruct((b, s, GROUP_W), F32)
    kept = jax.ShapeDtypeStruct((b, keep, GROUP_W), F32)
    out_specs = [tok, tok, tok, tok, tok, tok, tok, keep_spec, keep_spec]
    out_shape = [act, act, act, full, act, act, act, kept, kept]
    scratch = [pltpu.VMEM((d, N_GROUPS * GROUP_W), BF16)]
    if regroup:
        for r in (4, 16):
            out_specs += [pl.BlockSpec((None, r, tm // r, GROUP_W), lambda bi, i: (bi, 0, i, 0))] * 3
            out_shape += [jax.ShapeDtypeStruct((b, r, s // r, GROUP_W), act_dtype)] * 3
        nch = GROUP_W // LANES
        scratch += [pltpu.VMEM((nch, tm, LANES), F32), pltpu.VMEM((nch, 4, tm // 4, LANES), F32)]
    steps = b * nt
    assert regroup or not weights
    assert all(m.shape[0] % (16 * steps) == 0 for m in weights)
    slab = lambda m: pl.BlockSpec((m.shape[0] // steps, m.shape[1]), lambda bi, i: (bi * nt + i, 0))
    return pl.pallas_call(
        functools.partial(_proj_kernel, regroup=regroup, ncast=len(weights)),
        grid=(b, nt),
        in_specs=[pl.BlockSpec((None, tm, d), lambda bi, i: (bi, i, 0)),
                  mod_spec,
                  _const_spec((1, d)),
                  _const_spec((d, N_GROUPS * GROUP_W)),
                  _const_spec((4, tm, LANES)), tile_spec] + [slab(m) for m in weights],
        out_specs=out_specs + [slab(m) for m in weights],
        out_shape=out_shape + [jax.ShapeDtypeStruct(m.shape, BF16) for m in weights],
        scratch_shapes=scratch,
        compiler_params=_params(("arbitrary", "arbitrary")),
        name="in_proj",
    )(x, mod, g_pre.reshape(1, d), w_in_f32, rows_tab, tile_tab, *weights)


def _rotation_constants(row_pos, tile_pos):
    lane = np.arange(LANES)
    inv_r = (1.0 / RET_THETA ** np.linspace(0.0, 1.0, RET_DK // 2))[lane // 2]
    inv_a = (1.0 / ROPE_THETA ** (np.arange(0, ATT_HD, 2) / ATT_HD))[lane % (ATT_HD // 2)]
    rp = np.asarray(row_pos, np.float64)[:, None]
    tp = np.asarray(tile_pos, np.float64)[:, None]
    rows = np.stack([np.cos(rp * inv_r), np.sin(rp * inv_r), np.cos(rp * inv_a), np.sin(rp * inv_a)])
    sign_r = np.where(lane % 2 == 0, -1.0, 1.0)
    sign_a = np.where(lane % ATT_HD < ATT_HD // 2, -1.0, 1.0)
    ones = np.ones_like(tp * inv_r)
    tiles = np.stack([np.cos(tp * inv_r), np.sin(tp * inv_r), np.cos(tp * inv_a), np.sin(tp * inv_a),
                      ones * sign_r, ones * sign_a, 0 * ones, 0 * ones], axis=1)
    return jnp.asarray(rows, F32), jnp.asarray(tiles, F32)


def _ret_readout(o, g, gain):
    return (_rms(o) * gain) * _silu(g)


def _ret_prompt_kernel(q_ref, k_ref, v_ref, g_ref, gain_ref, intra_ref, qd_ref, kd_ref, o_ref, sfin_ref, s_scr,
                       *, nchunk, c_len):
    i = pl.program_id(1)

    @pl.when(i == 0)
    def _():
        s_scr[...] = jnp.zeros_like(s_scr)

    for h in range(RET_HEADS):
        intra, q_dec, k_dec = intra_ref[h], qd_ref[h], kd_ref[h]
        c_dec = math.exp(LOG_G[h] * c_len)
        hs = slice(h * RET_DK, (h + 1) * RET_DK)
        state = s_scr[h]
        for c in range(nchunk):
            rs = slice(c * c_len, (c + 1) * c_len)
            q = q_ref[rs, hs]
            k = k_ref[rs, hs]
            v = v_ref[rs, hs]
            a = _dot_nt(q, k) * intra
            lhs = jnp.concatenate([a.astype(BF16), (q.astype(F32) * q_dec).astype(BF16)], axis=1)
            o = _dot(lhs, jnp.concatenate([v, state.astype(BF16)], axis=0))
            kd = (k.astype(F32) * k_dec).astype(BF16)
            state = state * c_dec + _dot_tn(kd, v)
            o_ref[rs, hs] = _ret_readout(o, g_ref[rs, hs], gain_ref[:, hs]).astype(o_ref.dtype)
        s_scr[h] = state

    @pl.when(i == pl.num_programs(1) - 1)
    def _():
        sfin_ref[...] = s_scr[...]


def _retention_prompt(rq, rk, rv, rg, ret_gain, *, tc, chunk):
    b, s, w = rq.shape
    nblk = s // tc
    tok = pl.BlockSpec((None, tc, w), lambda bi, i: (bi, i, 0))
    idx = np.arange(chunk, dtype=np.float64)
    lg = np.asarray(LOG_G, np.float64)[:, None, None]
    diff = idx[:, None] - idx[None, :]
    intra = np.where(diff >= 0, np.exp(lg * np.maximum(diff, 0.0)), 0.0)
    lanes = np.ones((1, 1, RET_DK))
    qd = np.exp(lg * (idx[None, :, None] + 1.0)) * lanes
    kd = np.exp(lg * (chunk - 1.0 - idx[None, :, None])) * lanes
    tables = [jnp.asarray(t, F32) for t in (intra, qd, kd)]
    return pl.pallas_call(
        functools.partial(_ret_prompt_kernel, nchunk=tc // chunk, c_len=chunk),
        grid=(b, nblk),
        in_specs=[tok, tok, tok, tok, _const_spec((1, w))] + [_const_spec(t.shape) for t in tables],
        out_specs=[tok, pl.BlockSpec((None, RET_HEADS, RET_DK, RET_DK), lambda bi, i: (bi, 0, 0, 0))],
        out_shape=[jax.ShapeDtypeStruct((b, s, w), BF16),
                   jax.ShapeDtypeStruct((b, RET_HEADS, RET_DK, RET_DK), F32)],
        scratch_shapes=[pltpu.VMEM((RET_HEADS, RET_DK, RET_DK), F32)],
        compiler_params=_params(("arbitrary", "arbitrary")),
        name="retention_prompt",
    )(rq, rk, rv, rg, ret_gain.reshape(1, w), *tables)


def _att_host_kernel(q_ref, k_ref, v_ref, kp_ref, vp_ref, oh_ref, bias_ref, gq_ref, gkn_ref, gvn_ref, gck_ref,
                     gcv_ref, gcc_ref, gcn_ref, *rest, nsub, ln, wbuf, nunit):
    pv_ref, ml_ref, go_ref, gko_ref, gvo_ref, kcat, vcat = rest[-7:]
    steps = pl.num_programs(0) * pl.num_programs(1) * pl.num_programs(2)
    t = (pl.program_id(0) * pl.num_programs(1) + pl.program_id(1)) * pl.num_programs(2) + pl.program_id(2)
    fresh = jnp.logical_or(t == 0, _hosted_unit(t, 0, nunit, steps) != _hosted_unit(t - 1, 0, nunit, steps))

    @pl.when(fresh)
    def _():
        _sample_heads(gq_ref, gkn_ref, gvn_ref, gck_ref, gcv_ref, gcc_ref, gcn_ref, go_ref, gko_ref, gvo_ref,
                      ln=ln, wbuf=wbuf)

    _att_block(q_ref, k_ref, v_ref, kp_ref, vp_ref, oh_ref, bias_ref, pv_ref, ml_ref, kcat, vcat, nsub=nsub)


def _att_block(q_ref, k_ref, v_ref, kp_ref, vp_ref, oh_ref, bias_ref, pv_ref, ml_ref, kcat, vcat, *, nsub):
    i = pl.program_id(2)
    blk = DIL_BLOCK
    for g in range(q_ref.shape[0]):
        kcat[g, 0:blk, :] = kp_ref[g]
        kcat[g, blk:, :] = k_ref[g]
        vcat[g, 0:blk, :] = vp_ref[g]
        vcat[g, blk:, :] = v_ref[g]
    lane = lax.broadcasted_iota(jnp.int32, (blk, LANES), 1)
    low = lane < ATT_HD
    quarter = lane // (ATT_HD // 2)
    onehot = oh_ref[...]
    bias_rest = bias_ref[1]
    bias_first = jnp.where(i == 0, bias_ref[0], bias_rest)
    ones = jnp.ones((2 * blk, LANES), BF16)
    low2 = lax.broadcasted_iota(jnp.int32, (2 * blk, LANES), 1) < ATT_HD
    for g, j, hp in itertools.product(range(q_ref.shape[0]), range(nsub), range(ATT_WIDTH // LANES)):
        bias = bias_first if j == 0 else bias_rest
        rows = slice(j * blk, (j + 1) * blk)
        krows = slice(j * blk, (j + 2) * blk)
        cols = slice(hp * LANES, (hp + 1) * LANES)
        q = q_ref[g, rows, cols]
        zero = jnp.zeros_like(q)
        q2 = jnp.concatenate([jnp.where(low, q, zero), jnp.where(low, zero, q)], axis=0)
        s = _dot_nt(jnp.concatenate([q2, onehot], axis=1), jnp.concatenate([kcat[g, krows, cols], bias], axis=1))
        m = jnp.max(s, axis=-1, keepdims=True)
        p = jnp.exp2(s - m).astype(BF16)
        vv = vcat[g, krows, cols]
        pv0 = _dot(p[0:blk], jnp.where(low2, vv, ones))
        pv1 = _dot(p[blk:], jnp.where(low2, ones, vv))
        pv_ref[g, rows, cols] = jnp.where(low, pv0, pv1).astype(pv_ref.dtype)
        ml_ref[g, rows, cols] = jnp.where(
            quarter == 0, m[0:blk], jnp.where(quarter == 1, pv1, jnp.where(quarter == 2, m[blk:], pv0)))


def _band_tables():
    jj = np.arange(2 * DIL_BLOCK)[:, None]
    qi = np.arange(DIL_BLOCK)[None, :]
    bias = np.stack([np.where((jj >= np.maximum(qi, fk)) & (jj <= qi + SPAN), 0.0, NEG_INF)
                     for fk in (DIL_BLOCK, 0)])
    onehot = np.tile(np.eye(DIL_BLOCK), (2, 1))
    return jnp.asarray(onehot, BF16), jnp.asarray(bias, BF16)


def _hosted_unit(t, unit0, nunit, steps):
    return unit0 + (t * nunit) // steps


def _dilated_attention(aq, ak, av, sample, carry, *, tq, groups, unit0, nunit):
    b, dil, m_len, w = aq.shape
    nsub = tq // DIL_BLOCK
    nblk = m_len // tq
    ngrp = dil // groups
    own = pl.BlockSpec((None, groups, tq, w), lambda bi, r, i: (bi, r, i, 0))
    prv = pl.BlockSpec((None, groups, DIL_BLOCK, w), lambda bi, r, i: (bi, r, jnp.maximum(i * nsub - 1, 0), 0))
    onehot, bias = _band_tables()
    in_specs = [own, own, own, prv, prv, _const_spec(onehot.shape), _const_spec(bias.shape)]
    args = [aq, ak, av, ak, av, onehot, bias]
    out_shape = [jax.ShapeDtypeStruct((b, dil, m_len, w), BF16), jax.ShapeDtypeStruct((b, dil, m_len, w), F32)]
    scratch = [pltpu.VMEM((groups, tq + DIL_BLOCK, w), BF16), pltpu.VMEM((groups, tq + DIL_BLOCK, w), BF16)]
    common = dict(grid=(b, ngrp, nblk), scratch_shapes=scratch, name=f"dilated_attn_d{dil}",
                  compiler_params=_params(("arbitrary", "arbitrary", "arbitrary")))
    if nunit == 0:
        return pl.pallas_call(functools.partial(_att_block, nsub=nsub), in_specs=in_specs,
                              out_specs=[own, own], out_shape=out_shape, **common)(*args)

    gq, gkn, gvn, gck, gcv, cnt_c, cnt_n = sample
    nb, _, wbuf = gck.shape
    ln = gq.shape[0] // nb
    steps = b * ngrp * nblk
    assert nunit <= steps

    def unit(bi, r, i):
        return _hosted_unit((bi * ngrp + r) * nblk + i, unit0, nunit, steps)

    rows = pl.BlockSpec((ln, w), lambda bi, r, i: (unit(bi, r, i), 0))
    window = pl.BlockSpec((None, w, wbuf), lambda bi, r, i: (unit(bi, r, i), 0, 0))
    in_specs += [rows, rows, rows, window, window, _const_spec(cnt_c.shape), _const_spec(cnt_n.shape)]
    args += [gq, gkn, gvn, gck, gcv, cnt_c, cnt_n]
    aliases = {}
    if carry is not None:
        aliases = {len(args) + n: 2 + n for n in range(3)}
        in_specs += [pl.BlockSpec(memory_space=pl.ANY)] * 3
        args += list(carry)
    out_shape += [jax.ShapeDtypeStruct((nb * ln, w), F32), jax.ShapeDtypeStruct((nb, w, wbuf), F32),
                  jax.ShapeDtypeStruct((nb, w, wbuf), F32)]
    return pl.pallas_call(
        functools.partial(_att_host_kernel, nsub=nsub, ln=ln, wbuf=wbuf, nunit=nunit),
        in_specs=in_specs,
        out_specs=[own, own, rows, window, window],
        out_shape=out_shape,
        input_output_aliases=aliases,
        **common,
    )(*args)


def _merge_patterns(pv1_ref, ml1_ref, pv4_ref, ml4_ref, pv16_ref, ml16_ref, tok, s4, att):
    tm = att.shape[0]
    half = ATT_HD // 2
    lane = lax.broadcasted_iota(jnp.int32, (tm, LANES), 1)
    first = lane % ATT_HD < half

    def unpack(ml):
        near = pltpu.roll(ml, half, 1)
        return jnp.where(first, ml, near), jnp.where(first, near, pltpu.roll(ml, 2 * half, 1))

    for c in range(ATT_WIDTH // LANES):
        sl = slice(c * LANES, (c + 1) * LANES)
        for n, (r4, r16) in enumerate(((pv4_ref, pv16_ref), (ml4_ref, ml16_ref))):
            for q in range(4):
                tok[n, c, pl.ds(q, tm // 4, stride=4), :] = r4[q, :, sl].astype(F32)
                for a in range(4):
                    s4[q, pl.ds(a, tm // 16, stride=4), :] = r16[4 * a + q, :, sl].astype(F32)
            for q in range(4):
                tok[2 + n, c, pl.ds(q, tm // 4, stride=4), :] = s4[q]
        (m1, l1), (m4, l4), (m16, l16) = unpack(ml1_ref[:, sl]), unpack(tok[1, c]), unpack(tok[3, c])
        top = jnp.maximum(m1, jnp.maximum(m4, m16))
        e1, e4, e16 = jnp.exp2(m1 - top), jnp.exp2(m4 - top), jnp.exp2(m16 - top)
        num = e1 * pv1_ref[:, sl].astype(F32) + e4 * tok[0, c] + e16 * tok[2, c]
        den = e1 * l1 + e4 * l4 + e16 * l16
        merged = num / den
        att[:, sl] = merged.astype(att.dtype)
        yield merged


def _tail_body(x_ref, ret_ref, att_h, mod_ref, gpm_ref, gpf_ref, gqf_ref, wo_ref, wu_ref, wd_ref, y_ref,
               filler=iter(())):
    d = D_MODEL
    x = x_ref[...]
    mixed = _dot(ret_ref[...].astype(BF16), wo_ref[0:RET_WIDTH, :]) + _dot(att_h, wo_ref[RET_WIDTH:, :])
    x1 = x + mod_ref[:, 2 * d:3 * d] * (_rms(mixed) * gpm_ref[...])
    h = ((_rms(x1) * gpf_ref[...]) * (1.0 + mod_ref[:, 4 * d:5 * d]) + mod_ref[:, 3 * d:4 * d]).astype(BF16)
    fc = 1024
    f = None
    for c in range(D_FF // fc):
        u = jnp.maximum(_dot(h, wu_ref[:, c * fc:(c + 1) * fc]), 0.0)
        piece = next(filler, None)
        if piece is not None:
            u = jnp.concatenate([u[:, 0:LANES] + _tied_zero([piece], u.shape[0]), u[:, LANES:]], axis=1)
        part = _dot((u * u).astype(BF16), wd_ref[c * fc:(c + 1) * fc, :])
        f = part if f is None else f + part
    for _ in filler:
        pass
    y_ref[...] = x1 + mod_ref[:, 5 * d:6 * d] * (_rms(f) * gqf_ref[...])


def _tail_kernel(x_ref, ret_ref, att_ref, *rest):
    _tail_body(x_ref, ret_ref, att_ref[...].astype(BF16), *rest)


def _tail_merge_kernel(x_ref, ret_ref, pv1_ref, ml1_ref, pv4_ref, ml4_ref, pv16_ref, ml16_ref, mod_ref,
                       gpm_ref, gpf_ref, gqf_ref, wo_ref, wu_ref, wd_ref, y_ref, tok, s4, att):
    t = pl.program_id(0)
    merge = functools.partial(_merge_patterns, pv1_ref, ml1_ref, pv4_ref, ml4_ref, pv16_ref, ml16_ref,
                              tok, s4, att)

    @pl.when(t == 0)
    def _():
        for _ in merge():
            pass

    @pl.when(t > 0)
    def _():
        att_h = att[...]
        _tail_body(x_ref, ret_ref, att_h, mod_ref, gpm_ref, gpf_ref, gqf_ref, wo_ref, wu_ref, wd_ref, y_ref,
                   filler=merge())


def _tail(x, ret_h, att, mod, g_post_mix, g_pre_ffn, g_post_ffn, wo_bf, wu_bf, wd_bf, *, tm):
    b, s, d = x.shape
    nt = s // tm
    per_row = mod.shape[1] != 1
    weights = [_const_spec((1, d)), _const_spec((1, d)), _const_spec((1, d)),
               _const_spec((d, d)), _const_spec((d, D_FF)), _const_spec((D_FF, d))]
    weight_args = (g_post_mix.reshape(1, d), g_pre_ffn.reshape(1, d), g_post_ffn.reshape(1, d),
                   wo_bf, wu_bf, wd_bf)
    out_shape = jax.ShapeDtypeStruct((b, s, d), F32)
    if not isinstance(att, (tuple, list)):
        mod_spec = pl.BlockSpec((None, tm if per_row else 1, 6 * d),
                                (lambda bi, i: (bi, i, 0)) if per_row else (lambda bi, i: (bi, 0, 0)))
        tok = lambda w: pl.BlockSpec((None, tm, w), lambda bi, i: (bi, i, 0))
        return pl.pallas_call(
            _tail_kernel,
            grid=(b, nt),
            in_specs=[tok(d), tok(RET_WIDTH), tok(ATT_WIDTH), mod_spec, *weights],
            out_specs=tok(d),
            out_shape=out_shape,
            compiler_params=_params(("arbitrary", "arbitrary")),
            name="out_proj_mlp",
        )(x, ret_h, att, mod, *weight_args)

    assert not per_row
    last = b * nt - 1
    cur = lambda t: jnp.maximum(t - 1, 0)
    nxt = lambda t: jnp.minimum(t, last)
    tok = lambda w: pl.BlockSpec((None, tm, w), lambda t: (cur(t) // nt, cur(t) % nt, 0))
    mod_spec = pl.BlockSpec((None, 1, 6 * d), lambda t: (cur(t) // nt, 0, 0))
    split = lambda r: pl.BlockSpec((None, r, tm // r, ATT_WIDTH), lambda t: (nxt(t) // nt, 0, nxt(t) % nt, 0))
    one = pl.BlockSpec((None, None, tm, ATT_WIDTH), lambda t: (nxt(t) // nt, 0, nxt(t) % nt, 0))
    nch = ATT_WIDTH // LANES
    return pl.pallas_call(
        _tail_merge_kernel,
        grid=(b * nt + 1,),
        in_specs=[tok(d), tok(RET_WIDTH), one, one, split(4), split(4), split(16), split(16), mod_spec, *weights],
        out_specs=tok(d),
        out_shape=out_shape,
        scratch_shapes=[pltpu.VMEM((4, nch, tm, LANES), F32), pltpu.VMEM((4, tm // 4, LANES), F32),
                        pltpu.VMEM((tm, ATT_WIDTH), BF16)],
        compiler_params=_params(("arbitrary",)),
        name="out_proj_mlp",
    )(x, ret_h, *att, mod, *weight_args)


def _ret_sample_kernel(q_ref, k_ref, v_ref, g_ref, gain_ref, dm_ref, qd_ref, kd_ref, cd_ref, s_ref,
                       o_ref, sn_ref, o_scr, qd_scr, kt_scr, *, nb, ln):
    q = q_ref[...].astype(F32)
    k = k_ref[...].astype(F32)
    v = v_ref[...].astype(BF16)
    a = _dot_nt(q.astype(BF16), k.astype(BF16)) * dm_ref[...]
    o_scr[...] = _dot(a.astype(BF16), v)
    qd_scr[...] = q * qd_ref[...]
    kt_scr[...] = (k * kd_ref[...]).T
    c_dec = cd_ref[0:1, :]
    col = lax.broadcasted_iota(jnp.int32, kt_scr.shape, 1)

    def body(bi, carry):
        rows = pl.ds(pl.multiple_of(bi * ln, ln), ln)
        s0 = s_ref[bi]
        o_scr[rows, :] += _dot(qd_scr[rows, :].astype(BF16), s0.astype(BF16))
        mine = (col >= bi * ln) & (col < (bi + 1) * ln)
        kt = jnp.where(mine, kt_scr[...], 0.0).astype(BF16)
        sn_ref[bi] = s0 * c_dec + _dot(kt, v)
        return carry

    lax.fori_loop(0, nb, body, 0, unroll=math.gcd(nb, 8))
    o_ref[...] = _ret_readout(o_scr[...], g_ref[...], gain_ref[...]).astype(o_ref.dtype)


def _retention_sample(rq, rk, rv, rg, ret_gain, state, *, nb, ln):
    n = nb * ln
    idx = np.arange(n)
    same = (idx[:, None] // ln == idx[None, :] // ln) & (idx[:, None] >= idx[None, :])
    diff = np.maximum(idx[:, None] - idx[None, :], 0).astype(np.float64)
    step = (idx % ln).astype(np.float64)
    lg = np.asarray(LOG_G, np.float64)
    dm = np.where(same[None], np.exp(lg[:, None, None] * diff[None]), 0.0)
    qd = np.broadcast_to(np.exp(lg[:, None] * (step + 1.0))[:, :, None], (RET_HEADS, n, RET_DK))
    kd = np.broadcast_to(np.exp(lg[:, None] * (ln - 1.0 - step))[:, :, None], (RET_HEADS, n, RET_DK))
    cd = np.broadcast_to(np.exp(lg * ln)[:, None, None], (RET_HEADS, 8, RET_DK))
    tab = lambda t: jnp.asarray(np.ascontiguousarray(t), F32)
    col = pl.BlockSpec((n, RET_DK), lambda h: (0, h))
    per_head = lambda r, c: pl.BlockSpec((None, r, c), lambda h: (h, 0, 0))
    st = pl.BlockSpec((nb, None, RET_DK, RET_DK), lambda h: (0, h, 0, 0))
    return pl.pallas_call(
        functools.partial(_ret_sample_kernel, nb=nb, ln=ln),
        grid=(RET_HEADS,),
        in_specs=[col, col, col, col, pl.BlockSpec((1, RET_DK), lambda h: (0, h)),
                  per_head(n, n), per_head(n, RET_DK), per_head(n, RET_DK), per_head(8, RET_DK), st],
        out_specs=[col, st],
        out_shape=[jax.ShapeDtypeStruct((n, RET_WIDTH), BF16),
                   jax.ShapeDtypeStruct((nb, RET_HEADS, RET_DK, RET_DK), F32)],
        scratch_shapes=[pltpu.VMEM((n, RET_DK), F32), pltpu.VMEM((n, RET_DK), F32),
                        pltpu.VMEM((RET_DK, n), F32)],
        compiler_params=_params(("arbitrary",)),
        name="retention_sample",
    )(rq, rk, rv, rg, ret_gain.reshape(1, RET_WIDTH), tab(dm), tab(qd), tab(kd), tab(cd), state)


def _shift_window(old_ref, new_t, out_ref, *, ln, wbuf):
    lane = lax.broadcasted_iota(jnp.int32, new_t.shape, 1)
    ncol = wbuf // LANES
    rolled = pltpu.roll(old_ref[:, 0:LANES], LANES - ln, 1)
    for c in range(ncol):
        nxt = pltpu.roll(old_ref[:, (c + 1) * LANES:(c + 2) * LANES] if c + 1 < ncol else new_t, LANES - ln, 1)
        out_ref[:, c * LANES:(c + 1) * LANES] = jnp.where(lane < LANES - ln, rolled, nxt)
        rolled = nxt


def _new_rows_minor(new_ref, ln):
    pad = jnp.zeros((LANES - ln, new_ref.shape[1]), F32)
    return jnp.concatenate([new_ref[...], pad], axis=0).T


def _sample_heads(q_ref, kn_ref, vn_ref, ck_ref, cv_ref, cc_ref, cn_ref, o_ref, ko_ref, vo_ref, *, ln, wbuf):
    kn_t = _new_rows_minor(kn_ref, ln)
    vn_t = _new_rows_minor(vn_ref, ln)
    _shift_window(ck_ref, kn_t, ko_ref, ln=ln, wbuf=wbuf)
    _shift_window(cv_ref, vn_t, vo_ref, ln=ln, wbuf=wbuf)
    q = q_ref[...]
    nh = q.shape[1] // ATT_HD
    lane = lax.broadcasted_iota(jnp.int32, q.shape, 1)
    heads = [lane // ATT_HD == h for h in range(nh)]
    qm = jnp.concatenate([jnp.where(hm, q, 0.0) for hm in heads], axis=0).astype(BF16)
    cnt_c = jnp.concatenate([cc_ref[...]] * nh, axis=0)
    cnt_n = jnp.concatenate([cn_ref[...]] * nh, axis=0)
    s_c = jnp.where(cnt_c > 0, _dot(qm, ck_ref[...].astype(BF16)), NEG_INF)
    s_n = jnp.where(cnt_n > 0, _dot(qm, kn_t.astype(BF16)), NEG_INF)
    m = jnp.maximum(jnp.max(s_c, axis=-1, keepdims=True), jnp.max(s_n, axis=-1, keepdims=True))
    p_c = cnt_c * jnp.exp2(s_c - m)
    p_n = cnt_n * jnp.exp2(s_n - m)
    l = jnp.sum(p_c, axis=-1, keepdims=True) + jnp.sum(p_n, axis=-1, keepdims=True)
    o = _dot_nt(p_c.astype(BF16), cv_ref[...].astype(BF16)) + _dot_nt(p_n.astype(BF16), vn_t.astype(BF16))
    o = o / l
    acc = jnp.zeros_like(q)
    for h, hm in enumerate(heads):
        acc = acc + jnp.where(hm, o[h * ln:(h + 1) * ln, :], 0.0)
    o_ref[...] = acc.astype(o_ref.dtype)


def _pattern_counts(ln, wbuf):
    cnt = np.zeros((ln, wbuf + ln), np.float32)
    for dil in DILATIONS:
        for l in range(ln):
            for j in range(SPAN + 1):
                row = wbuf + l - dil * j
                if row >= 0:
                    cnt[l, row] += 1.0
    return cnt


def _count_tables(ln, wbuf):
    cnt = _pattern_counts(ln, wbuf)
    cnt_new = np.zeros((ln, LANES), np.float32)
    cnt_new[:, :ln] = cnt[:, wbuf:]
    return jnp.asarray(cnt[:, :wbuf]), jnp.asarray(cnt_new)


def _step(x_prompt, x_sample, c_prompt, c_sample, state_ret, cache_win_k, cache_win_v, w_ada, b_ada,
          g_pre_mix, g_post_mix, g_pre_ffn, g_post_ffn, w_in, ret_gain, w_o, w_up, w_down,
          *, tm, tc, tq):
    assert w_in.shape[0] == 1, "single-layer step"
    bp, sp, d = x_prompt.shape
    nb, ln, _ = x_sample.shape
    wbuf = cache_win_k.shape[2]
    n_s = nb * ln

    w_in_f32 = w_in[0]

    rows = bp + nb
    pad = (-rows) % 8
    c_all = jnp.concatenate([c_prompt, c_sample, jnp.zeros((pad, d), F32)], axis=0)
    mod = _modulation(c_all, w_ada[0], b_ada[0])
    mod_p = mod[:bp].reshape(bp, 1, 6 * d)
    mod_s = jnp.repeat(mod[bp:rows], ln, axis=0).reshape(1, n_s, 6 * d)

    tabs_p = _rotation_constants(np.arange(tm), np.arange(sp // tm) * tm)
    tabs_s = _rotation_constants(np.tile(np.arange(ln), nb), [PAST_LEN])

    keep = min(MAX_WINDOW, sp)
    (rq, rk, rv, rg, aq, ak, av, akf, avf, aq4, ak4, av4, aq16, ak16, av16, wo_bf, wu_bf, wd_bf) = _project(
        x_prompt, mod_p, g_pre_mix[0], w_in_f32, tabs_p, tm=tm, keep=keep, act_dtype=BF16, regroup=True,
        weights=(w_o[0], w_up[0], w_down[0]))
    xs = x_sample.reshape(1, n_s, d)
    srq, srk, srv, srg, saq, _, _, sakf, savf = _project(
        xs, mod_s, g_pre_mix[0], w_in_f32, tabs_s, tm=n_s, keep=n_s, act_dtype=F32, regroup=False)
    flat = lambda t: t.reshape(n_s, GROUP_W)

    to_minor = lambda t: jnp.transpose(t[0], (0, 2, 3, 1)).reshape(nb, ATT_WIDTH, wbuf)
    from_minor = lambda t: jnp.transpose(t.reshape(nb, ATT_HEADS, ATT_HD, wbuf), (0, 3, 1, 2))[None]
    sample = (flat(saq), flat(sakf), flat(savf), to_minor(cache_win_k), to_minor(cache_win_v),
              *_count_tables(ln, wbuf))
    patterns = ((aq16, ak16, av16), (aq4, ak4, av4), (aq[:, None], ak[:, None], av[:, None]))
    steps = bp * sp // tq
    hosted, first = {}, 0
    for n in (1, 2, 0):
        hosted[n] = (first, min(steps, nb - first))
        first += hosted[n][1]
    assert first == nb
    carry, merged_in = None, []
    for n, qkv in enumerate(patterns):
        m_len = qkv[0].shape[2]
        rows = tq if hosted[n][1] else 2 * tq
        out = _dilated_attention(*qkv, sample, carry, tq=min(rows, m_len), groups=max(1, rows // m_len),
                                 unit0=hosted[n][0], nunit=hosted[n][1])
        merged_in = list(out[:2]) + merged_in
        carry = out[2:] or carry
    satt_h, k_out, v_out = carry

    ret_h, s_fin = _retention_prompt(rq, rk, rv, rg, ret_gain[0], tc=tc, chunk=RET_BLOCK)
    y_prompt = _tail(x_prompt, ret_h, tuple(merged_in), mod_p, g_post_mix[0], g_pre_ffn[0],
                     g_post_ffn[0], wo_bf, wu_bf, wd_bf, tm=tm)

    sret_h, s_new = _retention_sample(flat(srq), flat(srk), flat(srv), flat(srg), ret_gain[0],
                                      state_ret[0], nb=nb, ln=ln)
    y_sample = _tail(xs, sret_h.reshape(1, n_s, RET_WIDTH), satt_h.reshape(1, n_s, ATT_WIDTH), mod_s,
                     g_post_mix[0], g_pre_ffn[0], g_post_ffn[0], wo_bf, wu_bf, wd_bf, tm=n_s)

    cache_shape = (1, -1, keep, ATT_HEADS, ATT_HD)
    return (y_prompt,
            y_sample.reshape(nb, ln, d),
            s_fin[None],
            akf.reshape(cache_shape),
            avf.reshape(cache_shape),
            s_new[None],
            from_minor(k_out),
            from_minor(v_out))


def kernel(x_prompt, x_sample, c_prompt, c_sample, state_ret, cache_win_k, cache_win_v, w_ada, b_ada,
           g_pre_mix, g_post_mix, g_pre_ffn, g_post_ffn, w_in, ret_gain, w_o, w_up, w_down):
    return _step(x_prompt, x_sample, c_prompt, c_sample, state_ret, cache_win_k, cache_win_v, w_ada, b_ada,
                 g_pre_mix, g_post_mix, g_pre_ffn, g_post_ffn, w_in, ret_gain, w_o, w_up, w_down,
                 tm=512, tc=2048, tq=1024)
```

```python
import functools
import itertools
import math

import numpy as np
import jax
import jax.numpy as jnp
from jax import lax
from jax.experimental import pallas as pl
from jax.experimental.pallas import tpu as pltpu

F32 = jnp.float32
BF16 = jnp.bfloat16

D_MODEL = 1024
RET_HEADS = 4
RET_WIDTH = 512
RET_DK = 128
RET_THETA = 10000.0
RET_BLOCK = 256
ATT_HD = 64
ATT_HEADS = 8
ATT_WIDTH = 512
DILATIONS = (1, 4, 16)
MAX_WINDOW = 2048
SPAN = 128
DIL_BLOCK = 128
ROPE_THETA = 10000.0
D_FF = 4096
NORM_EPS = 1e-6
NEG_INF = -1e30
PAST_LEN = 16384
N_GROUPS = 7
GROUP_W = 512
LANES = 128
VMEM_LIMIT = 56 * 1024 * 1024

LOG2_E = math.log2(math.e)
LOG_G = tuple(math.log1p(-(2.0 ** (-5.0 - h))) for h in range(RET_HEADS))


def _dot(a, b):
    return jnp.dot(a, b, preferred_element_type=F32)


def _dot_nt(a, b):
    return lax.dot_general(a, b, (((1,), (1,)), ((), ())), preferred_element_type=F32)


def _dot_tn(a, b):
    return lax.dot_general(a, b, (((0,), (0,)), ((), ())), preferred_element_type=F32)


def _rms(x):
    return x * lax.rsqrt(jnp.mean(x * x, axis=-1, keepdims=True) + NORM_EPS)


def _silu(x):
    return x / (1.0 + jnp.exp(-x))


def _tied_zero(values, rows):
    bits = None
    for v in values:
        assert v.shape[0] % rows == 0 and v.shape[1] % LANES == 0
        for r0 in range(0, v.shape[0], rows):
            for c0 in range(0, v.shape[1], LANES):
                b = pltpu.bitcast(v[r0:r0 + rows, c0:c0 + LANES], jnp.uint32)
                bits = b if bits is None else bits | b
    sixteen = jnp.full(bits.shape, 16, jnp.uint32)
    return pltpu.bitcast(lax.shift_right_logical(lax.shift_right_logical(bits, sixteen), sixteen), F32)


def _params(sem):
    return pltpu.CompilerParams(dimension_semantics=sem, vmem_limit_bytes=VMEM_LIMIT)


def _const_spec(shape):
    nd = len(shape)
    return pl.BlockSpec(shape, lambda *_: (0,) * nd, pipeline_mode=pl.Buffered(1))


def _mod_kernel(c_ref, w_ref, b_ref, o_ref):
    a = _silu(c_ref[...]).astype(BF16)
    o_ref[...] = _dot(a, w_ref[...].astype(BF16)) + b_ref[...]


def _modulation(c, w_ada, b_ada):
    rows, d = c.shape
    n = w_ada.shape[1]
    tn = 1536
    return pl.pallas_call(
        _mod_kernel,
        grid=(n // tn,),
        in_specs=[pl.BlockSpec((rows, d), lambda j: (0, 0)),
                  pl.BlockSpec((d, tn), lambda j: (0, j)),
                  pl.BlockSpec((1, tn), lambda j: (0, j))],
        out_specs=pl.BlockSpec((rows, tn), lambda j: (0, j)),
        out_shape=jax.ShapeDtypeStruct((rows, n), F32),
        compiler_params=_params(("arbitrary",)),
        name="adaln_mod",
    )(c, w_ada, b_ada.reshape(1, n))


def _rotate_pairs(x, cos, sin):
    lane = lax.broadcasted_iota(jnp.int32, x.shape, 1)
    partner = jnp.where(lane % 2 == 0, pltpu.roll(x, LANES - 1, 1), pltpu.roll(x, 1, 1))
    return x * cos + partner * sin


def _rotate_half(x, cos, sin):
    lane = lax.broadcasted_iota(jnp.int32, x.shape, 1)
    half = ATT_HD // 2
    partner = jnp.where(lane % ATT_HD < half, pltpu.roll(x, LANES - half, 1), pltpu.roll(x, half, 1))
    return x * cos + partner * sin


def _store_regrouped(val, c, tok_ref, r4_ref, r16_ref, zs, s4):
    sl = slice(c * LANES, (c + 1) * LANES)
    tok_ref[:, sl] = val.astype(tok_ref.dtype)
    tm = val.shape[0]
    zs[c] = val
    for q in range(4):
        g4 = zs[c, pl.ds(q, tm // 4, stride=4), :]
        r4_ref[q, :, sl] = g4.astype(r4_ref.dtype)
        s4[c, q] = g4
    for q in range(4):
        for a in range(4):
            r16_ref[4 * a + q, :, sl] = s4[c, q, pl.ds(a, tm // 16, stride=4), :].astype(r16_ref.dtype)


def _proj_kernel(x_ref, mod_ref, g_ref, wf_ref, rows_ref, tile_ref, *refs, regroup):
    if regroup:
        (rq_ref, rk_ref, rv_ref, rg_ref, aq_ref, ak_ref, av_ref, akf_ref, avf_ref,
         aq4_ref, ak4_ref, av4_ref, aq16_ref, ak16_ref, av16_ref, w_ref, zs, s4) = refs
    else:
        rq_ref, rk_ref, rv_ref, rg_ref, aq_ref, ak_ref, av_ref, akf_ref, avf_ref, w_ref = refs

    @pl.when(jnp.logical_and(pl.program_id(0) == 0, pl.program_id(1) == 0))
    def _():
        for gi in range(N_GROUPS):
            sl = slice(gi * GROUP_W, (gi + 1) * GROUP_W)
            w_ref[:, sl] = wf_ref[:, sl].astype(w_ref.dtype)

    d = D_MODEL
    x = x_ref[...]
    h = (_rms(x) * g_ref[...]) * (1.0 + mod_ref[:, d:2 * d]) + mod_ref[:, 0:d]
    h = h.astype(BF16)
    t = tile_ref[...]
    cr = t[0:1] * rows_ref[0] - t[1:2] * rows_ref[1]
    sr = (t[1:2] * rows_ref[0] + t[0:1] * rows_ref[1]) * t[4:5]
    ca = t[2:3] * rows_ref[2] - t[3:4] * rows_ref[3]
    sa = (t[3:4] * rows_ref[2] + t[2:3] * rows_ref[3]) * t[5:6]
    nch = GROUP_W // LANES

    def group(gi):
        return _dot(h, w_ref[:, gi * GROUP_W:(gi + 1) * GROUP_W])

    z = group(0)
    for c in range(nch):
        sl = slice(c * LANES, (c + 1) * LANES)
        rq_ref[:, sl] = _rotate_pairs(z[:, sl], cr, sr).astype(rq_ref.dtype)
    z = group(1)
    for c in range(nch):
        sl = slice(c * LANES, (c + 1) * LANES)
        rk_ref[:, sl] = (_rotate_pairs(z[:, sl], cr, sr) * (RET_DK ** -0.5)).astype(rk_ref.dtype)
    rv_ref[...] = group(2).astype(rv_ref.dtype)
    rg_ref[...] = group(3)
    def emit(val, c, tok_ref, r4_ref, r16_ref):
        if regroup:
            _store_regrouped(val, c, tok_ref, r4_ref, r16_ref, zs, s4)
        else:
            tok_ref[:, c * LANES:(c + 1) * LANES] = val.astype(tok_ref.dtype)

    r4 = (aq4_ref, ak4_ref, av4_ref) if regroup else (None,) * 3
    r16 = (aq16_ref, ak16_ref, av16_ref) if regroup else (None,) * 3
    z = group(4)
    for c in range(nch):
        sl = slice(c * LANES, (c + 1) * LANES)
        emit(_rotate_half(z[:, sl], ca, sa) * (ATT_HD ** -0.5 * LOG2_E), c, aq_ref, r4[0], r16[0])
    z = group(5)
    for c in range(nch):
        sl = slice(c * LANES, (c + 1) * LANES)
        r = _rotate_half(z[:, sl], ca, sa)
        akf_ref[:, sl] = r
        emit(r, c, ak_ref, r4[1], r16[1])
    z = group(6)
    avf_ref[...] = z
    for c in range(nch):
        emit(z[:, c * LANES:(c + 1) * LANES], c, av_ref, r4[2], r16[2])


def _project(x, mod, g_pre, w_in_f32, tabs, *, tm, keep, act_dtype, regroup):
    b, s, d = x.shape
    nt = s // tm
    rows_tab, tile_tab = tabs
    assert rows_tab.shape == (4, tm, LANES) and tile_tab.shape == (nt, 8, LANES)
    mod_rows = mod.shape[1]
    per_row = mod_rows != 1
    first_keep = (s - keep) // tm
    tok = pl.BlockSpec((None, tm, GROUP_W), lambda bi, i: (bi, i, 0))
    keep_spec = pl.BlockSpec((None, tm, GROUP_W), lambda bi, i: (bi, jnp.maximum(i - first_keep, 0), 0))
    tile_spec = pl.BlockSpec((None, 8, LANES), lambda bi, i: (i, 0, 0))
    mod_spec = pl.BlockSpec((None, tm if per_row else 1, 6 * d),
                            (lambda bi, i: (bi, i, 0)) if per_row else (lambda bi, i: (bi, 0, 0)))
    act = jax.ShapeDtypeStruct((b, s, GROUP_W), act_dtype)
    full = jax.ShapeDtypeStruct((b, s, GROUP_W), F32)
    kept = jax.ShapeDtypeStruct((b, keep, GROUP_W), F32)
    out_specs = [tok, tok, tok, tok, tok, tok, tok, keep_spec, keep_spec]
    out_shape = [act, act, act, full, act, act, act, kept, kept]
    scratch = [pltpu.VMEM((d, N_GROUPS * GROUP_W), BF16)]
    if regroup:
        for r in (4, 16):
            out_specs += [pl.BlockSpec((None, r, tm // r, GROUP_W), lambda bi, i: (bi, 0, i, 0))] * 3
            out_shape += [jax.ShapeDtypeStruct((b, r, s // r, GROUP_W), act_dtype)] * 3
        nch = GROUP_W // LANES
        scratch += [pltpu.VMEM((nch, tm, LANES), F32), pltpu.VMEM((nch, 4, tm // 4, LANES), F32)]
    return pl.pallas_call(
        functools.partial(_proj_kernel, regroup=regroup),
        grid=(b, nt),
        in_specs=[pl.BlockSpec((None, tm, d), lambda bi, i: (bi, i, 0)),
                  mod_spec,
                  _const_spec((1, d)),
                  _const_spec((d, N_GROUPS * GROUP_W)),
                  _const_spec((4, tm, LANES)), tile_spec],
        out_specs=out_specs,
        out_shape=out_shape,
        scratch_shapes=scratch,
        compiler_params=_params(("arbitrary", "arbitrary")),
        name="in_proj",
    )(x, mod, g_pre.reshape(1, d), w_in_f32, rows_tab, tile_tab)


def _rotation_constants(row_pos, tile_pos):
    lane = np.arange(LANES)
    inv_r = (1.0 / RET_THETA ** np.linspace(0.0, 1.0, RET_DK // 2))[lane // 2]
    inv_a = (1.0 / ROPE_THETA ** (np.arange(0, ATT_HD, 2) / ATT_HD))[lane % (ATT_HD // 2)]
    rp = np.asarray(row_pos, np.float64)[:, None]
    tp = np.asarray(tile_pos, np.float64)[:, None]
    rows = np.stack([np.cos(rp * inv_r), np.sin(rp * inv_r), np.cos(rp * inv_a), np.sin(rp * inv_a)])
    sign_r = np.where(lane % 2 == 0, -1.0, 1.0)
    sign_a = np.where(lane % ATT_HD < ATT_HD // 2, -1.0, 1.0)
    ones = np.ones_like(tp * inv_r)
    tiles = np.stack([np.cos(tp * inv_r), np.sin(tp * inv_r), np.cos(tp * inv_a), np.sin(tp * inv_a),
                      ones * sign_r, ones * sign_a, 0 * ones, 0 * ones], axis=1)
    return jnp.asarray(rows, F32), jnp.asarray(tiles, F32)


def _ret_readout(o, g, gain):
    return (_rms(o) * gain) * _silu(g)


def _ret_prompt_kernel(q_ref, k_ref, v_ref, g_ref, gain_ref, intra_ref, qd_ref, kd_ref, *rest, nchunk, c_len,
                       ncast):
    cast_in, (o_ref, sfin_ref), cast_out, s_scr = rest[:ncast], rest[ncast:ncast + 2], rest[ncast + 2:-1], rest[-1]
    for src, dst in zip(cast_in, cast_out):
        dst[...] = src[...].astype(dst.dtype)
    i = pl.program_id(1)

    @pl.when(i == 0)
    def _():
        s_scr[...] = jnp.zeros_like(s_scr)

    for h in range(RET_HEADS):
        intra, q_dec, k_dec = intra_ref[h], qd_ref[h], kd_ref[h]
        c_dec = math.exp(LOG_G[h] * c_len)
        hs = slice(h * RET_DK, (h + 1) * RET_DK)
        state = s_scr[h]
        for c in range(nchunk):
            rs = slice(c * c_len, (c + 1) * c_len)
            q = q_ref[rs, hs]
            k = k_ref[rs, hs]
            v = v_ref[rs, hs]
            a = _dot_nt(q, k) * intra
            lhs = jnp.concatenate([a.astype(BF16), (q.astype(F32) * q_dec).astype(BF16)], axis=1)
            o = _dot(lhs, jnp.concatenate([v, state.astype(BF16)], axis=0))
            kd = (k.astype(F32) * k_dec).astype(BF16)
            state = state * c_dec + _dot_tn(kd, v)
            o_ref[rs, hs] = _ret_readout(o, g_ref[rs, hs], gain_ref[:, hs]).astype(o_ref.dtype)
        s_scr[h] = state

    @pl.when(i == pl.num_programs(1) - 1)
    def _():
        sfin_ref[...] = s_scr[...]


def _retention_prompt(rq, rk, rv, rg, ret_gain, weights, *, tc, chunk):
    b, s, w = rq.shape
    nblk = s // tc
    steps = b * nblk
    tok = pl.BlockSpec((None, tc, w), lambda bi, i: (bi, i, 0))
    slab = lambda m: pl.BlockSpec((m.shape[0] // steps, m.shape[1]), lambda bi, i: (bi * nblk + i, 0))
    assert all(m.shape[0] % (16 * steps) == 0 for m in weights)
    idx = np.arange(chunk, dtype=np.float64)
    lg = np.asarray(LOG_G, np.float64)[:, None, None]
    diff = idx[:, None] - idx[None, :]
    intra = np.where(diff >= 0, np.exp(lg * np.maximum(diff, 0.0)), 0.0)
    lanes = np.ones((1, 1, RET_DK))
    qd = np.exp(lg * (idx[None, :, None] + 1.0)) * lanes
    kd = np.exp(lg * (chunk - 1.0 - idx[None, :, None])) * lanes
    tables = [jnp.asarray(t, F32) for t in (intra, qd, kd)]
    out = pl.pallas_call(
        functools.partial(_ret_prompt_kernel, nchunk=tc // chunk, c_len=chunk, ncast=len(weights)),
        grid=(b, nblk),
        in_specs=[tok, tok, tok, tok, _const_spec((1, w))] + [_const_spec(t.shape) for t in tables]
        + [slab(m) for m in weights],
        out_specs=[tok, pl.BlockSpec((None, RET_HEADS, RET_DK, RET_DK), lambda bi, i: (bi, 0, 0, 0))]
        + [slab(m) for m in weights],
        out_shape=[jax.ShapeDtypeStruct((b, s, w), BF16),
                   jax.ShapeDtypeStruct((b, RET_HEADS, RET_DK, RET_DK), F32)]
        + [jax.ShapeDtypeStruct(m.shape, BF16) for m in weights],
        scratch_shapes=[pltpu.VMEM((RET_HEADS, RET_DK, RET_DK), F32)],
        compiler_params=_params(("arbitrary", "arbitrary")),
        name="retention_prompt",
    )(rq, rk, rv, rg, ret_gain.reshape(1, w), *tables, *weights)
    return out[0], out[1], out[2:]


def _att_host_kernel(q_ref, k_ref, v_ref, kp_ref, vp_ref, oh_ref, bias_ref, gq_ref, gkn_ref, gvn_ref, gck_ref,
                     gcv_ref, gcc_ref, gcn_ref, *rest, nsub, ln, wbuf, nunit):
    pv_ref, ml_ref, go_ref, gko_ref, gvo_ref, kcat, vcat = rest[-7:]
    steps = pl.num_programs(0) * pl.num_programs(1) * pl.num_programs(2)
    t = (pl.program_id(0) * pl.num_programs(1) + pl.program_id(1)) * pl.num_programs(2) + pl.program_id(2)
    fresh = jnp.logical_or(t == 0, _hosted_unit(t, 0, nunit, steps) != _hosted_unit(t - 1, 0, nunit, steps))

    @pl.when(fresh)
    def _():
        _sample_heads(gq_ref, gkn_ref, gvn_ref, gck_ref, gcv_ref, gcc_ref, gcn_ref, go_ref, gko_ref, gvo_ref,
                      ln=ln, wbuf=wbuf)

    _att_block(q_ref, k_ref, v_ref, kp_ref, vp_ref, oh_ref, bias_ref, pv_ref, ml_ref, kcat, vcat, nsub=nsub)


def _att_block(q_ref, k_ref, v_ref, kp_ref, vp_ref, oh_ref, bias_ref, pv_ref, ml_ref, kcat, vcat, *, nsub):
    i = pl.program_id(2)
    blk = DIL_BLOCK
    for g in range(q_ref.shape[0]):
        kcat[g, 0:blk, :] = kp_ref[g]
        kcat[g, blk:, :] = k_ref[g]
        vcat[g, 0:blk, :] = vp_ref[g]
        vcat[g, blk:, :] = v_ref[g]
    lane = lax.broadcasted_iota(jnp.int32, (blk, LANES), 1)
    low = lane < ATT_HD
    quarter = lane // (ATT_HD // 2)
    onehot = oh_ref[...]
    bias_rest = bias_ref[1]
    bias_first = jnp.where(i == 0, bias_ref[0], bias_rest)
    ones = jnp.ones((2 * blk, LANES), BF16)
    low2 = lax.broadcasted_iota(jnp.int32, (2 * blk, LANES), 1) < ATT_HD
    for g, j, hp in itertools.product(range(q_ref.shape[0]), range(nsub), range(ATT_WIDTH // LANES)):
        bias = bias_first if j == 0 else bias_rest
        rows = slice(j * blk, (j + 1) * blk)
        krows = slice(j * blk, (j + 2) * blk)
        cols = slice(hp * LANES, (hp + 1) * LANES)
        q = q_ref[g, rows, cols]
        zero = jnp.zeros_like(q)
        q2 = jnp.concatenate([jnp.where(low, q, zero), jnp.where(low, zero, q)], axis=0)
        s = _dot_nt(jnp.concatenate([q2, onehot], axis=1), jnp.concatenate([kcat[g, krows, cols], bias], axis=1))
        m = jnp.max(s, axis=-1, keepdims=True)
        p = jnp.exp2(s - m).astype(BF16)
        vv = vcat[g, krows, cols]
        pv0 = _dot(p[0:blk], jnp.where(low2, vv, ones))
        pv1 = _dot(p[blk:], jnp.where(low2, ones, vv))
        pv_ref[g, rows, cols] = jnp.where(low, pv0, pv1).astype(pv_ref.dtype)
        ml_ref[g, rows, cols] = jnp.where(
            quarter == 0, m[0:blk], jnp.where(quarter == 1, pv1, jnp.where(quarter == 2, m[blk:], pv0)))


def _att_cast_kernel(*refs, nsub, ncast):
    cast_in, cast_out = refs[7:7 + ncast], refs[9 + ncast:9 + 2 * ncast]
    for src, dst in zip(cast_in, cast_out):
        dst[...] = src[...].astype(dst.dtype)
    _att_block(*refs[:7], *refs[7 + ncast:9 + ncast], *refs[9 + 2 * ncast:], nsub=nsub)


def _band_tables():
    jj = np.arange(2 * DIL_BLOCK)[:, None]
    qi = np.arange(DIL_BLOCK)[None, :]
    bias = np.stack([np.where((jj >= np.maximum(qi, fk)) & (jj <= qi + SPAN), 0.0, NEG_INF)
                     for fk in (DIL_BLOCK, 0)])
    onehot = np.tile(np.eye(DIL_BLOCK), (2, 1))
    return jnp.asarray(onehot, BF16), jnp.asarray(bias, BF16)


def _hosted_unit(t, unit0, nunit, steps):
    return unit0 + (t * nunit) // steps


def _dilated_attention(aq, ak, av, sample, carry, weights=(), *, tq, groups, unit0, nunit):
    b, dil, m_len, w = aq.shape
    nsub = tq // DIL_BLOCK
    nblk = m_len // tq
    ngrp = dil // groups
    own = pl.BlockSpec((None, groups, tq, w), lambda bi, r, i: (bi, r, i, 0))
    prv = pl.BlockSpec((None, groups, DIL_BLOCK, w), lambda bi, r, i: (bi, r, jnp.maximum(i * nsub - 1, 0), 0))
    onehot, bias = _band_tables()
    in_specs = [own, own, own, prv, prv, _const_spec(onehot.shape), _const_spec(bias.shape)]
    args = [aq, ak, av, ak, av, onehot, bias]
    out_shape = [jax.ShapeDtypeStruct((b, dil, m_len, w), BF16), jax.ShapeDtypeStruct((b, dil, m_len, w), F32)]
    scratch = [pltpu.VMEM((groups, tq + DIL_BLOCK, w), BF16), pltpu.VMEM((groups, tq + DIL_BLOCK, w), BF16)]
    common = dict(grid=(b, ngrp, nblk), scratch_shapes=scratch, name=f"dilated_attn_d{dil}",
                  compiler_params=_params(("arbitrary", "arbitrary", "arbitrary")))
    if nunit == 0:
        steps = b * ngrp * nblk
        assert all(m.shape[0] % (16 * steps) == 0 for m in weights)
        slab = lambda m: pl.BlockSpec((m.shape[0] // steps, m.shape[1]),
                                      lambda bi, r, i: ((bi * ngrp + r) * nblk + i, 0))
        return pl.pallas_call(functools.partial(_att_cast_kernel, nsub=nsub, ncast=len(weights)),
                              in_specs=in_specs + [slab(m) for m in weights],
                              out_specs=[own, own] + [slab(m) for m in weights],
                              out_shape=out_shape + [jax.ShapeDtypeStruct(m.shape, BF16) for m in weights],
                              **common)(*args, *weights)
    assert not weights

    gq, gkn, gvn, gck, gcv, cnt_c, cnt_n = sample
    nb, _, wbuf = gck.shape
    ln = gq.shape[0] // nb
    steps = b * ngrp * nblk
    assert nunit <= steps

    def unit(bi, r, i):
        return _hosted_unit((bi * ngrp + r) * nblk + i, unit0, nunit, steps)

    rows = pl.BlockSpec((ln, w), lambda bi, r, i: (unit(bi, r, i), 0))
    window = pl.BlockSpec((None, w, wbuf), lambda bi, r, i: (unit(bi, r, i), 0, 0))
    in_specs += [rows, rows, rows, window, window, _const_spec(cnt_c.shape), _const_spec(cnt_n.shape)]
    args += [gq, gkn, gvn, gck, gcv, cnt_c, cnt_n]
    aliases = {}
    if carry is not None:
        aliases = {len(args) + n: 2 + n for n in range(3)}
        in_specs += [pl.BlockSpec(memory_space=pl.ANY)] * 3
        args += list(carry)
    out_shape += [jax.ShapeDtypeStruct((nb * ln, w), F32), jax.ShapeDtypeStruct((nb, w, wbuf), F32),
                  jax.ShapeDtypeStruct((nb, w, wbuf), F32)]
    return pl.pallas_call(
        functools.partial(_att_host_kernel, nsub=nsub, ln=ln, wbuf=wbuf, nunit=nunit),
        in_specs=in_specs,
        out_specs=[own, own, rows, window, window],
        out_shape=out_shape,
        input_output_aliases=aliases,
        **common,
    )(*args)


def _merge_patterns(pv1_ref, ml1_ref, pv4_ref, ml4_ref, pv16_ref, ml16_ref, tok, s4, att):
    tm = att.shape[0]
    half = ATT_HD // 2
    lane = lax.broadcasted_iota(jnp.int32, (tm, LANES), 1)
    first = lane % ATT_HD < half

    def unpack(ml):
        near = pltpu.roll(ml, half, 1)
        return jnp.where(first, ml, near), jnp.where(first, near, pltpu.roll(ml, 2 * half, 1))

    for c in range(ATT_WIDTH // LANES):
        sl = slice(c * LANES, (c + 1) * LANES)
        for n, (r4, r16) in enumerate(((pv4_ref, pv16_ref), (ml4_ref, ml16_ref))):
            for q in range(4):
                tok[n, c, pl.ds(q, tm // 4, stride=4), :] = r4[q, :, sl].astype(F32)
                for a in range(4):
                    s4[q, pl.ds(a, tm // 16, stride=4), :] = r16[4 * a + q, :, sl].astype(F32)
            for q in range(4):
                tok[2 + n, c, pl.ds(q, tm // 4, stride=4), :] = s4[q]
        (m1, l1), (m4, l4), (m16, l16) = unpack(ml1_ref[:, sl]), unpack(tok[1, c]), unpack(tok[3, c])
        top = jnp.maximum(m1, jnp.maximum(m4, m16))
        e1, e4, e16 = jnp.exp2(m1 - top), jnp.exp2(m4 - top), jnp.exp2(m16 - top)
        num = e1 * pv1_ref[:, sl].astype(F32) + e4 * tok[0, c] + e16 * tok[2, c]
        den = e1 * l1 + e4 * l4 + e16 * l16
        merged = num / den
        att[:, sl] = merged.astype(att.dtype)
        yield merged


def _tail_body(x_ref, ret_ref, att_h, mod_ref, gpm_ref, gpf_ref, gqf_ref, wo_ref, wu_ref, wd_ref, y_ref,
               filler=iter(())):
    d = D_MODEL
    x = x_ref[...]
    mixed = _dot(ret_ref[...].astype(BF16), wo_ref[0:RET_WIDTH, :]) + _dot(att_h, wo_ref[RET_WIDTH:, :])
    x1 = x + mod_ref[:, 2 * d:3 * d] * (_rms(mixed) * gpm_ref[...])
    h = ((_rms(x1) * gpf_ref[...]) * (1.0 + mod_ref[:, 4 * d:5 * d]) + mod_ref[:, 3 * d:4 * d]).astype(BF16)
    fc = 1024
    f = None
    for c in range(D_FF // fc):
        u = jnp.maximum(_dot(h, wu_ref[:, c * fc:(c + 1) * fc]), 0.0)
        piece = next(filler, None)
        if piece is not None:
            u = jnp.concatenate([u[:, 0:LANES] + _tied_zero([piece], u.shape[0]), u[:, LANES:]], axis=1)
        part = _dot((u * u).astype(BF16), wd_ref[c * fc:(c + 1) * fc, :])
        f = part if f is None else f + part
    for _ in filler:
        pass
    y_ref[...] = x1 + mod_ref[:, 5 * d:6 * d] * (_rms(f) * gqf_ref[...])


def _tail_kernel(x_ref, ret_ref, att_ref, *rest):
    _tail_body(x_ref, ret_ref, att_ref[...].astype(BF16), *rest)


def _tail_merge_kernel(x_ref, ret_ref, pv1_ref, ml1_ref, pv4_ref, ml4_ref, pv16_ref, ml16_ref, mod_ref,
                       gpm_ref, gpf_ref, gqf_ref, wo_ref, wu_ref, wd_ref, y_ref, tok, s4, att):
    t = pl.program_id(0)
    merge = functools.partial(_merge_patterns, pv1_ref, ml1_ref, pv4_ref, ml4_ref, pv16_ref, ml16_ref,
                              tok, s4, att)

    @pl.when(t == 0)
    def _():
        for _ in merge():
            pass

    @pl.when(t > 0)
    def _():
        att_h = att[...]
        _tail_body(x_ref, ret_ref, att_h, mod_ref, gpm_ref, gpf_ref, gqf_ref, wo_ref, wu_ref, wd_ref, y_ref,
                   filler=merge())


def _tail(x, ret_h, att, mod, g_post_mix, g_pre_ffn, g_post_ffn, wo_bf, wu_bf, wd_bf, *, tm):
    b, s, d = x.shape
    nt = s // tm
    per_row = mod.shape[1] != 1
    weights = [_const_spec((1, d)), _const_spec((1, d)), _const_spec((1, d)),
               _const_spec((d, d)), _const_spec((d, D_FF)), _const_spec((D_FF, d))]
    weight_args = (g_post_mix.reshape(1, d), g_pre_ffn.reshape(1, d), g_post_ffn.reshape(1, d),
                   wo_bf, wu_bf, wd_bf)
    out_shape = jax.ShapeDtypeStruct((b, s, d), F32)
    if not isinstance(att, (tuple, list)):
        mod_spec = pl.BlockSpec((None, tm if per_row else 1, 6 * d),
                                (lambda bi, i: (bi, i, 0)) if per_row else (lambda bi, i: (bi, 0, 0)))
        tok = lambda w: pl.BlockSpec((None, tm, w), lambda bi, i: (bi, i, 0))
        return pl.pallas_call(
            _tail_kernel,
            grid=(b, nt),
            in_specs=[tok(d), tok(RET_WIDTH), tok(ATT_WIDTH), mod_spec, *weights],
            out_specs=tok(d),
            out_shape=out_shape,
            compiler_params=_params(("arbitrary", "arbitrary")),
            name="out_proj_mlp",
        )(x, ret_h, att, mod, *weight_args)

    assert not per_row
    last = b * nt - 1
    cur = lambda t: jnp.maximum(t - 1, 0)
    nxt = lambda t: jnp.minimum(t, last)
    tok = lambda w: pl.BlockSpec((None, tm, w), lambda t: (cur(t) // nt, cur(t) % nt, 0))
    mod_spec = pl.BlockSpec((None, 1, 6 * d), lambda t: (cur(t) // nt, 0, 0))
    split = lambda r: pl.BlockSpec((None, r, tm // r, ATT_WIDTH), lambda t: (nxt(t) // nt, 0, nxt(t) % nt, 0))
    one = pl.BlockSpec((None, None, tm, ATT_WIDTH), lambda t: (nxt(t) // nt, 0, nxt(t) % nt, 0))
    nch = ATT_WIDTH // LANES
    return pl.pallas_call(
        _tail_merge_kernel,
        grid=(b * nt + 1,),
        in_specs=[tok(d), tok(RET_WIDTH), one, one, split(4), split(4), split(16), split(16), mod_spec, *weights],
        out_specs=tok(d),
        out_shape=out_shape,
        scratch_shapes=[pltpu.VMEM((4, nch, tm, LANES), F32), pltpu.VMEM((4, tm // 4, LANES), F32),
                        pltpu.VMEM((tm, ATT_WIDTH), BF16)],
        compiler_params=_params(("arbitrary",)),
        name="out_proj_mlp",
    )(x, ret_h, *att, mod, *weight_args)


def _ret_sample_kernel(q_ref, k_ref, v_ref, g_ref, gain_ref, dm_ref, qd_ref, kd_ref, cd_ref, s_ref,
                       o_ref, sn_ref, o_scr, qd_scr, kt_scr, *, nb, ln):
    q = q_ref[...].astype(F32)
    k = k_ref[...].astype(F32)
    v = v_ref[...].astype(BF16)
    a = _dot_nt(q.astype(BF16), k.astype(BF16)) * dm_ref[...]
    o_scr[...] = _dot(a.astype(BF16), v)
    qd_scr[...] = q * qd_ref[...]
    kt_scr[...] = (k * kd_ref[...]).T
    c_dec = cd_ref[0:1, :]
    col = lax.broadcasted_iota(jnp.int32, kt_scr.shape, 1)

    def body(bi, carry):
        rows = pl.ds(pl.multiple_of(bi * ln, ln), ln)
        s0 = s_ref[bi]
        o_scr[rows, :] += _dot(qd_scr[rows, :].astype(BF16), s0.astype(BF16))
        mine = (col >= bi * ln) & (col < (bi + 1) * ln)
        kt = jnp.where(mine, kt_scr[...], 0.0).astype(BF16)
        sn_ref[bi] = s0 * c_dec + _dot(kt, v)
        return carry

    lax.fori_loop(0, nb, body, 0, unroll=math.gcd(nb, 8))
    o_ref[...] = _ret_readout(o_scr[...], g_ref[...], gain_ref[...]).astype(o_ref.dtype)


def _retention_sample(rq, rk, rv, rg, ret_gain, state, *, nb, ln):
    n = nb * ln
    idx = np.arange(n)
    same = (idx[:, None] // ln == idx[None, :] // ln) & (idx[:, None] >= idx[None, :])
    diff = np.maximum(idx[:, None] - idx[None, :], 0).astype(np.float64)
    step = (idx % ln).astype(np.float64)
    lg = np.asarray(LOG_G, np.float64)
    dm = np.where(same[None], np.exp(lg[:, None, None] * diff[None]), 0.0)
    qd = np.broadcast_to(np.exp(lg[:, None] * (step + 1.0))[:, :, None], (RET_HEADS, n, RET_DK))
    kd = np.broadcast_to(np.exp(lg[:, None] * (ln - 1.0 - step))[:, :, None], (RET_HEADS, n, RET_DK))
    cd = np.broadcast_to(np.exp(lg * ln)[:, None, None], (RET_HEADS, 8, RET_DK))
    tab = lambda t: jnp.asarray(np.ascontiguousarray(t), F32)
    col = pl.BlockSpec((n, RET_DK), lambda h: (0, h))
    per_head = lambda r, c: pl.BlockSpec((None, r, c), lambda h: (h, 0, 0))
    st = pl.BlockSpec((nb, None, RET_DK, RET_DK), lambda h: (0, h, 0, 0))
    return pl.pallas_call(
        functools.partial(_ret_sample_kernel, nb=nb, ln=ln),
        grid=(RET_HEADS,),
        in_specs=[col, col, col, col, pl.BlockSpec((1, RET_DK), lambda h: (0, h)),
                  per_head(n, n), per_head(n, RET_DK), per_head(n, RET_DK), per_head(8, RET_DK), st],
        out_specs=[col, st],
        out_shape=[jax.ShapeDtypeStruct((n, RET_WIDTH), BF16),
                   jax.ShapeDtypeStruct((nb, RET_HEADS, RET_DK, RET_DK), F32)],
        scratch_shapes=[pltpu.VMEM((n, RET_DK), F32), pltpu.VMEM((n, RET_DK), F32),
                        pltpu.VMEM((RET_DK, n), F32)],
        compiler_params=_params(("arbitrary",)),
        name="retention_sample",
    )(rq, rk, rv, rg, ret_gain.reshape(1, RET_WIDTH), tab(dm), tab(qd), tab(kd), tab(cd), state)


def _shift_window(old_ref, new_t, out_ref, *, ln, wbuf):
    lane = lax.broadcasted_iota(jnp.int32, new_t.shape, 1)
    ncol = wbuf // LANES
    rolled = pltpu.roll(old_ref[:, 0:LANES], LANES - ln, 1)
    for c in range(ncol):
        nxt = pltpu.roll(old_ref[:, (c + 1) * LANES:(c + 2) * LANES] if c + 1 < ncol else new_t, LANES - ln, 1)
        out_ref[:, c * LANES:(c + 1) * LANES] = jnp.where(lane < LANES - ln, rolled, nxt)
        rolled = nxt


def _new_rows_minor(new_ref, ln):
    pad = jnp.zeros((LANES - ln, new_ref.shape[1]), F32)
    return jnp.concatenate([new_ref[...], pad], axis=0).T


def _sample_heads(q_ref, kn_ref, vn_ref, ck_ref, cv_ref, cc_ref, cn_ref, o_ref, ko_ref, vo_ref, *, ln, wbuf):
    kn_t = _new_rows_minor(kn_ref, ln)
    vn_t = _new_rows_minor(vn_ref, ln)
    _shift_window(ck_ref, kn_t, ko_ref, ln=ln, wbuf=wbuf)
    _shift_window(cv_ref, vn_t, vo_ref, ln=ln, wbuf=wbuf)
    q = q_ref[...]
    nh = q.shape[1] // ATT_HD
    lane = lax.broadcasted_iota(jnp.int32, q.shape, 1)
    heads = [lane // ATT_HD == h for h in range(nh)]
    qm = jnp.concatenate([jnp.where(hm, q, 0.0) for hm in heads], axis=0).astype(BF16)
    cnt_c = jnp.concatenate([cc_ref[...]] * nh, axis=0)
    cnt_n = jnp.concatenate([cn_ref[...]] * nh, axis=0)
    s_c = jnp.where(cnt_c > 0, _dot(qm, ck_ref[...].astype(BF16)), NEG_INF)
    s_n = jnp.where(cnt_n > 0, _dot(qm, kn_t.astype(BF16)), NEG_INF)
    m = jnp.maximum(jnp.max(s_c, axis=-1, keepdims=True), jnp.max(s_n, axis=-1, keepdims=True))
    p_c = cnt_c * jnp.exp2(s_c - m)
    p_n = cnt_n * jnp.exp2(s_n - m)
    l = jnp.sum(p_c, axis=-1, keepdims=True) + jnp.sum(p_n, axis=-1, keepdims=True)
    o = _dot_nt(p_c.astype(BF16), cv_ref[...].astype(BF16)) + _dot_nt(p_n.astype(BF16), vn_t.astype(BF16))
    o = o / l
    acc = jnp.zeros_like(q)
    for h, hm in enumerate(heads):
        acc = acc + jnp.where(hm, o[h * ln:(h + 1) * ln, :], 0.0)
    o_ref[...] = acc.astype(o_ref.dtype)


def _pattern_counts(ln, wbuf):
    cnt = np.zeros((ln, wbuf + ln), np.float32)
    for dil in DILATIONS:
        for l in range(ln):
            for j in range(SPAN + 1):
                row = wbuf + l - dil * j
                if row >= 0:
                    cnt[l, row] += 1.0
    return cnt


def _count_tables(ln, wbuf):
    cnt = _pattern_counts(ln, wbuf)
    cnt_new = np.zeros((ln, LANES), np.float32)
    cnt_new[:, :ln] = cnt[:, wbuf:]
    return jnp.asarray(cnt[:, :wbuf]), jnp.asarray(cnt_new)


def _step(x_prompt, x_sample, c_prompt, c_sample, state_ret, cache_win_k, cache_win_v, w_ada, b_ada,
          g_pre_mix, g_post_mix, g_pre_ffn, g_post_ffn, w_in, ret_gain, w_o, w_up, w_down,
          *, tm, tc, tq):
    assert w_in.shape[0] == 1, "single-layer step"
    bp, sp, d = x_prompt.shape
    nb, ln, _ = x_sample.shape
    wbuf = cache_win_k.shape[2]
    n_s = nb * ln

    w_in_f32 = w_in[0]

    rows = bp + nb
    pad = (-rows) % 8
    c_all = jnp.concatenate([c_prompt, c_sample, jnp.zeros((pad, d), F32)], axis=0)
    mod = _modulation(c_all, w_ada[0], b_ada[0])
    mod_p = mod[:bp].reshape(bp, 1, 6 * d)
    mod_s = jnp.repeat(mod[bp:rows], ln, axis=0).reshape(1, n_s, 6 * d)

    tabs_p = _rotation_constants(np.arange(tm), np.arange(sp // tm) * tm)
    tabs_s = _rotation_constants(np.tile(np.arange(ln), nb), [PAST_LEN])

    keep = min(MAX_WINDOW, sp)
    (rq, rk, rv, rg, aq, ak, av, akf, avf, aq4, ak4, av4, aq16, ak16, av16) = _project(
        x_prompt, mod_p, g_pre_mix[0], w_in_f32, tabs_p, tm=tm, keep=keep, act_dtype=BF16, regroup=True)
    xs = x_sample.reshape(1, n_s, d)
    srq, srk, srv, srg, saq, _, _, sakf, savf = _project(
        xs, mod_s, g_pre_mix[0], w_in_f32, tabs_s, tm=n_s, keep=n_s, act_dtype=F32, regroup=False)
    flat = lambda t: t.reshape(n_s, GROUP_W)

    to_minor = lambda t: jnp.transpose(t[0], (0, 2, 3, 1)).reshape(nb, ATT_WIDTH, wbuf)
    from_minor = lambda t: jnp.transpose(t.reshape(nb, ATT_HEADS, ATT_HD, wbuf), (0, 3, 1, 2))[None]
    sample = (flat(saq), flat(sakf), flat(savf), to_minor(cache_win_k), to_minor(cache_win_v),
              *_count_tables(ln, wbuf))
    patterns = ((aq16, ak16, av16), (aq4, ak4, av4), (aq[:, None], ak[:, None], av[:, None]))
    steps = bp * sp // tq
    hosted, first = {}, 0
    for n in (1, 2, 0):
        hosted[n] = (first, min(steps, nb - first))
        first += hosted[n][1]
    assert first == nb
    carry, merged_in = None, []
    to_cast, w_bf = (w_o[0], w_up[0], w_down[0]), ()
    for n, qkv in enumerate(patterns):
        m_len = qkv[0].shape[2]
        rows = tq if hosted[n][1] else 2 * tq
        cast_here = () if hosted[n][1] else to_cast
        out = _dilated_attention(*qkv, sample, carry, cast_here, tq=min(rows, m_len),
                                 groups=max(1, rows // m_len), unit0=hosted[n][0], nunit=hosted[n][1])
        merged_in = list(out[:2]) + merged_in
        if hosted[n][1]:
            carry = out[2:]
        elif cast_here:
            to_cast, w_bf = (), out[2:]
    satt_h, k_out, v_out = carry

    ret_h, s_fin, w_ret = _retention_prompt(rq, rk, rv, rg, ret_gain[0], to_cast, tc=tc, chunk=RET_BLOCK)
    wo_bf, wu_bf, wd_bf = w_bf or w_ret
    y_prompt = _tail(x_prompt, ret_h, tuple(merged_in), mod_p, g_post_mix[0], g_pre_ffn[0],
                     g_post_ffn[0], wo_bf, wu_bf, wd_bf, tm=tm)

    sret_h, s_new = _retention_sample(flat(srq), flat(srk), flat(srv), flat(srg), ret_gain[0],
                                      state_ret[0], nb=nb, ln=ln)
    y_sample = _tail(xs, sret_h.reshape(1, n_s, RET_WIDTH), satt_h.reshape(1, n_s, ATT_WIDTH), mod_s,
                     g_post_mix[0], g_pre_ffn[0], g_post_ffn[0], wo_bf, wu_bf, wd_bf, tm=n_s)

    cache_shape = (1, -1, keep, ATT_HEADS, ATT_HD)
    return (y_prompt,
            y_sample.reshape(nb, ln, d),
            s_fin[None],
            akf.reshape(cache_shape),
            avf.reshape(cache_shape),
            s_new[None],
            from_minor(k_out),
            from_minor(v_out))


def kernel(x_prompt, x_sample, c_prompt, c_sample, state_ret, cache_win_k, cache_win_v, w_ada, b_ada,
           g_pre_mix, g_post_mix, g_pre_ffn, g_post_ffn, w_in, ret_gain, w_o, w_up, w_down):
    return _step(x_prompt, x_sample, c_prompt, c_sample, state_ret, cache_win_k, cache_win_v, w_ada, b_ada,
                 g_pre_mix, g_post_mix, g_pre_ffn, g_post_ffn, w_in, ret_gain, w_o, w_up, w_down,
                 tm=512, tc=2048, tq=1024)
```

```python
import functools
import itertools
import math

import numpy as np
import jax
import jax.numpy as jnp
from jax import lax
from jax.experimental import pallas as pl
from jax.experimental.pallas import tpu as pltpu

F32 = jnp.float32
BF16 = jnp.bfloat16

D_MODEL = 1024
RET_HEADS = 4
RET_WIDTH = 512
RET_DK = 128
RET_THETA = 10000.0
RET_BLOCK = 256
ATT_HD = 64
ATT_HEADS = 8
ATT_WIDTH = 512
DILATIONS = (1, 4, 16)
MAX_WINDOW = 2048
SPAN = 128
DIL_BLOCK = 128
ROPE_THETA = 10000.0
D_FF = 4096
NORM_EPS = 1e-6
NEG_INF = -1e30
PAST_LEN = 16384
N_GROUPS = 7
GROUP_W = 512
LANES = 128
VMEM_LIMIT = 56 * 1024 * 1024

LOG2_E = math.log2(math.e)
LOG_G = tuple(math.log1p(-(2.0 ** (-5.0 - h))) for h in range(RET_HEADS))


def _dot(a, b):
    return jnp.dot(a, b, preferred_element_type=F32)


def _dot_nt(a, b):
    return lax.dot_general(a, b, (((1,), (1,)), ((), ())), preferred_element_type=F32)


def _dot_tn(a, b):
    return lax.dot_general(a, b, (((0,), (0,)), ((), ())), preferred_element_type=F32)


def _rms(x):
    return x * lax.rsqrt(jnp.mean(x * x, axis=-1, keepdims=True) + NORM_EPS)


def _silu(x):
    return x / (1.0 + jnp.exp(-x))


def _tied_zero(values, rows):
    bits = None
    for v in values:
        assert v.shape[0] % rows == 0 and v.shape[1] % LANES == 0
        for r0 in range(0, v.shape[0], rows):
            for c0 in range(0, v.shape[1], LANES):
                b = pltpu.bitcast(v[r0:r0 + rows, c0:c0 + LANES], jnp.uint32)
                bits = b if bits is None else bits | b
    sixteen = jnp.full(bits.shape, 16, jnp.uint32)
    return pltpu.bitcast(lax.shift_right_logical(lax.shift_right_logical(bits, sixteen), sixteen), F32)


def _params(sem):
    return pltpu.CompilerParams(dimension_semantics=sem, vmem_limit_bytes=VMEM_LIMIT)


def _const_spec(shape):
    nd = len(shape)
    return pl.BlockSpec(shape, lambda *_: (0,) * nd, pipeline_mode=pl.Buffered(1))


def _mod_kernel(c_ref, w_ref, b_ref, o_ref):
    a = _silu(c_ref[...]).astype(BF16)
    o_ref[...] = _dot(a, w_ref[...].astype(BF16)) + b_ref[...]


def _modulation(c, w_ada, b_ada):
    rows, d = c.shape
    n = w_ada.shape[1]
    tn = 1536
    return pl.pallas_call(
        _mod_kernel,
        grid=(n // tn,),
        in_specs=[pl.BlockSpec((rows, d), lambda j: (0, 0)),
                  pl.BlockSpec((d, tn), lambda j: (0, j)),
                  pl.BlockSpec((1, tn), lambda j: (0, j))],
        out_specs=pl.BlockSpec((rows, tn), lambda j: (0, j)),
        out_shape=jax.ShapeDtypeStruct((rows, n), F32),
        compiler_params=_params(("arbitrary",)),
        name="adaln_mod",
    )(c, w_ada, b_ada.reshape(1, n))


def _rotate_pairs(x, cos, sin):
    lane = lax.broadcasted_iota(jnp.int32, x.shape, 1)
    partner = jnp.where(lane % 2 == 0, pltpu.roll(x, LANES - 1, 1), pltpu.roll(x, 1, 1))
    return x * cos + partner * sin


def _rotate_half(x, cos, sin):
    lane = lax.broadcasted_iota(jnp.int32, x.shape, 1)
    half = ATT_HD // 2
    partner = jnp.where(lane % ATT_HD < half, pltpu.roll(x, LANES - half, 1), pltpu.roll(x, half, 1))
    return x * cos + partner * sin


def _store_regrouped(val, c, tok_ref, r4_ref, r16_ref, zs, s4):
    sl = slice(c * LANES, (c + 1) * LANES)
    tok_ref[:, sl] = val.astype(tok_ref.dtype)
    tm = val.shape[0]
    zs[c] = val
    for q in range(4):
        g4 = zs[c, pl.ds(q, tm // 4, stride=4), :]
        r4_ref[q, :, sl] = g4.astype(r4_ref.dtype)
        s4[c, q] = g4
    for q in range(4):
        for a in range(4):
            r16_ref[4 * a + q, :, sl] = s4[c, q, pl.ds(a, tm // 16, stride=4), :].astype(r16_ref.dtype)


def _proj_kernel(x_ref, mod_ref, g_ref, wf_ref, rows_ref, tile_ref, *refs, regroup):
    if regroup:
        (rq_ref, rk_ref, rv_ref, rg_ref, aq_ref, ak_ref, av_ref, akf_ref, avf_ref,
         aq4_ref, ak4_ref, av4_ref, aq16_ref, ak16_ref, av16_ref, w_ref, zs, s4) = refs

        @pl.when(jnp.logical_and(pl.program_id(0) == 0, pl.program_id(1) == 0))
        def _():
            for gi in range(N_GROUPS):
                sl = slice(gi * GROUP_W, (gi + 1) * GROUP_W)
                w_ref[:, sl] = wf_ref[:, sl].astype(w_ref.dtype)
    else:
        rq_ref, rk_ref, rv_ref, rg_ref, aq_ref, ak_ref, av_ref, akf_ref, avf_ref = refs
        w_ref = wf_ref

    d = D_MODEL
    x = x_ref[...]
    h = (_rms(x) * g_ref[...]) * (1.0 + mod_ref[:, d:2 * d]) + mod_ref[:, 0:d]
    h = h.astype(BF16)
    t = tile_ref[...]
    cr = t[0:1] * rows_ref[0] - t[1:2] * rows_ref[1]
    sr = (t[1:2] * rows_ref[0] + t[0:1] * rows_ref[1]) * t[4:5]
    ca = t[2:3] * rows_ref[2] - t[3:4] * rows_ref[3]
    sa = (t[3:4] * rows_ref[2] + t[2:3] * rows_ref[3]) * t[5:6]
    nch = GROUP_W // LANES

    def group(gi):
        return _dot(h, w_ref[:, gi * GROUP_W:(gi + 1) * GROUP_W])

    z = group(0)
    for c in range(nch):
        sl = slice(c * LANES, (c + 1) * LANES)
        rq_ref[:, sl] = _rotate_pairs(z[:, sl], cr, sr).astype(rq_ref.dtype)
    z = group(1)
    for c in range(nch):
        sl = slice(c * LANES, (c + 1) * LANES)
        rk_ref[:, sl] = (_rotate_pairs(z[:, sl], cr, sr) * (RET_DK ** -0.5)).astype(rk_ref.dtype)
    rv_ref[...] = group(2).astype(rv_ref.dtype)
    rg_ref[...] = group(3)
    def emit(val, c, tok_ref, r4_ref, r16_ref):
        if regroup:
            _store_regrouped(val, c, tok_ref, r4_ref, r16_ref, zs, s4)
        else:
            tok_ref[:, c * LANES:(c + 1) * LANES] = val.astype(tok_ref.dtype)

    r4 = (aq4_ref, ak4_ref, av4_ref) if regroup else (None,) * 3
    r16 = (aq16_ref, ak16_ref, av16_ref) if regroup else (None,) * 3
    z = group(4)
    for c in range(nch):
        sl = slice(c * LANES, (c + 1) * LANES)
        emit(_rotate_half(z[:, sl], ca, sa) * (ATT_HD ** -0.5 * LOG2_E), c, aq_ref, r4[0], r16[0])
    z = group(5)
    for c in range(nch):
        sl = slice(c * LANES, (c + 1) * LANES)
        r = _rotate_half(z[:, sl], ca, sa)
        akf_ref[:, sl] = r
        emit(r, c, ak_ref, r4[1], r16[1])
    z = group(6)
    avf_ref[...] = z
    for c in range(nch):
        emit(z[:, c * LANES:(c + 1) * LANES], c, av_ref, r4[2], r16[2])


def _project(x, mod, g_pre, w_in, tabs, *, tm, keep, act_dtype, regroup):
    b, s, d = x.shape
    nt = s // tm
    rows_tab, tile_tab = tabs
    assert rows_tab.shape == (4, tm, LANES) and tile_tab.shape == (nt, 8, LANES)
    mod_rows = mod.shape[1]
    per_row = mod_rows != 1
    first_keep = (s - keep) // tm
    tok = pl.BlockSpec((None, tm, GROUP_W), lambda bi, i: (bi, i, 0))
    keep_spec = pl.BlockSpec((None, tm, GROUP_W), lambda bi, i: (bi, jnp.maximum(i - first_keep, 0), 0))
    tile_spec = pl.BlockSpec((None, 8, LANES), lambda bi, i: (i, 0, 0))
    mod_spec = pl.BlockSpec((None, tm if per_row else 1, 6 * d),
                            (lambda bi, i: (bi, i, 0)) if per_row else (lambda bi, i: (bi, 0, 0)))
    act = jax.ShapeDtypeStruct((b, s, GROUP_W), act_dtype)
    full = jax.ShapeDtypeStruct((b, s, GROUP_W), F32)
    kept = jax.ShapeDtypeStruct((b, keep, GROUP_W), F32)
    out_specs = [tok, tok, tok, tok, tok, tok, tok, keep_spec, keep_spec]
    out_shape = [act, act, act, full, act, act, act, kept, kept]
    scratch = []
    if regroup:
        for r in (4, 16):
            out_specs += [pl.BlockSpec((None, r, tm // r, GROUP_W), lambda bi, i: (bi, 0, i, 0))] * 3
            out_shape += [jax.ShapeDtypeStruct((b, r, s // r, GROUP_W), act_dtype)] * 3
        out_specs += [_const_spec(w_in.shape)]
        out_shape += [jax.ShapeDtypeStruct(w_in.shape, BF16)]
        nch = GROUP_W // LANES
        scratch += [pltpu.VMEM((nch, tm, LANES), F32), pltpu.VMEM((nch, 4, tm // 4, LANES), F32)]
    else:
        assert w_in.dtype == BF16
    return pl.pallas_call(
        functools.partial(_proj_kernel, regroup=regroup),
        grid=(b, nt),
        in_specs=[pl.BlockSpec((None, tm, d), lambda bi, i: (bi, i, 0)),
                  mod_spec,
                  _const_spec((1, d)),
                  _const_spec((d, N_GROUPS * GROUP_W)),
                  _const_spec((4, tm, LANES)), tile_spec],
        out_specs=out_specs,
        out_shape=out_shape,
        scratch_shapes=scratch,
        compiler_params=_params(("arbitrary", "arbitrary")),
        name="in_proj",
    )(x, mod, g_pre.reshape(1, d), w_in, rows_tab, tile_tab)


def _rotation_constants(row_pos, tile_pos):
    lane = np.arange(LANES)
    inv_r = (1.0 / RET_THETA ** np.linspace(0.0, 1.0, RET_DK // 2))[lane // 2]
    inv_a = (1.0 / ROPE_THETA ** (np.arange(0, ATT_HD, 2) / ATT_HD))[lane % (ATT_HD // 2)]
    rp = np.asarray(row_pos, np.float64)[:, None]
    tp = np.asarray(tile_pos, np.float64)[:, None]
    rows = np.stack([np.cos(rp * inv_r), np.sin(rp * inv_r), np.cos(rp * inv_a), np.sin(rp * inv_a)])
    sign_r = np.where(lane % 2 == 0, -1.0, 1.0)
    sign_a = np.where(lane % ATT_HD < ATT_HD // 2, -1.0, 1.0)
    ones = np.ones_like(tp * inv_r)
    tiles = np.stack([np.cos(tp * inv_r), np.sin(tp * inv_r), np.cos(tp * inv_a), np.sin(tp * inv_a),
                      ones * sign_r, ones * sign_a, 0 * ones, 0 * ones], axis=1)
    return jnp.asarray(rows, F32), jnp.asarray(tiles, F32)


def _ret_readout(o, g, gain):
    return (_rms(o) * gain) * _silu(g)


def _ret_prompt_kernel(q_ref, k_ref, v_ref, g_ref, gain_ref, intra_ref, qd_ref, kd_ref, *rest, nchunk, c_len,
                       ncast):
    cast_in, (o_ref, sfin_ref), cast_out, s_scr = rest[:ncast], rest[ncast:ncast + 2], rest[ncast + 2:-1], rest[-1]
    for src, dst in zip(cast_in, cast_out):
        dst[...] = src[...].astype(dst.dtype)
    i = pl.program_id(1)

    @pl.when(i == 0)
    def _():
        s_scr[...] = jnp.zeros_like(s_scr)

    for h in range(RET_HEADS):
        intra, q_dec, k_dec = intra_ref[h], qd_ref[h], kd_ref[h]
        c_dec = math.exp(LOG_G[h] * c_len)
        hs = slice(h * RET_DK, (h + 1) * RET_DK)
        state = s_scr[h]
        for c in range(nchunk):
            rs = slice(c * c_len, (c + 1) * c_len)
            q = q_ref[rs, hs]
            k = k_ref[rs, hs]
            v = v_ref[rs, hs]
            a = _dot_nt(q, k) * intra
            lhs = jnp.concatenate([a.astype(BF16), (q.astype(F32) * q_dec).astype(BF16)], axis=1)
            o = _dot(lhs, jnp.concatenate([v, state.astype(BF16)], axis=0))
            kd = (k.astype(F32) * k_dec).astype(BF16)
            state = state * c_dec + _dot_tn(kd, v)
            o_ref[rs, hs] = _ret_readout(o, g_ref[rs, hs], gain_ref[:, hs]).astype(o_ref.dtype)
        s_scr[h] = state

    @pl.when(i == pl.num_programs(1) - 1)
    def _():
        sfin_ref[...] = s_scr[...]


def _retention_prompt(rq, rk, rv, rg, ret_gain, weights, *, tc, chunk):
    b, s, w = rq.shape
    nblk = s // tc
    steps = b * nblk
    tok = pl.BlockSpec((None, tc, w), lambda bi, i: (bi, i, 0))
    slab = lambda m: pl.BlockSpec((m.shape[0] // steps, m.shape[1]), lambda bi, i: (bi * nblk + i, 0))
    assert all(m.shape[0] % (16 * steps) == 0 for m in weights)
    idx = np.arange(chunk, dtype=np.float64)
    lg = np.asarray(LOG_G, np.float64)[:, None, None]
    diff = idx[:, None] - idx[None, :]
    intra = np.where(diff >= 0, np.exp(lg * np.maximum(diff, 0.0)), 0.0)
    lanes = np.ones((1, 1, RET_DK))
    qd = np.exp(lg * (idx[None, :, None] + 1.0)) * lanes
    kd = np.exp(lg * (chunk - 1.0 - idx[None, :, None])) * lanes
    tables = [jnp.asarray(t, F32) for t in (intra, qd, kd)]
    out = pl.pallas_call(
        functools.partial(_ret_prompt_kernel, nchunk=tc // chunk, c_len=chunk, ncast=len(weights)),
        grid=(b, nblk),
        in_specs=[tok, tok, tok, tok, _const_spec((1, w))] + [_const_spec(t.shape) for t in tables]
        + [slab(m) for m in weights],
        out_specs=[tok, pl.BlockSpec((None, RET_HEADS, RET_DK, RET_DK), lambda bi, i: (bi, 0, 0, 0))]
        + [slab(m) for m in weights],
        out_shape=[jax.ShapeDtypeStruct((b, s, w), BF16),
                   jax.ShapeDtypeStruct((b, RET_HEADS, RET_DK, RET_DK), F32)]
        + [jax.ShapeDtypeStruct(m.shape, BF16) for m in weights],
        scratch_shapes=[pltpu.VMEM((RET_HEADS, RET_DK, RET_DK), F32)],
        compiler_params=_params(("arbitrary", "arbitrary")),
        name="retention_prompt",
    )(rq, rk, rv, rg, ret_gain.reshape(1, w), *tables, *weights)
    return out[0], out[1], out[2:]


def _att_host_kernel(q_ref, k_ref, v_ref, kp_ref, vp_ref, oh_ref, bias_ref, gq_ref, gkn_ref, gvn_ref, gck_ref,
                     gcv_ref, gcc_ref, gcn_ref, *rest, nsub, ln, wbuf, nunit):
    pv_ref, ml_ref, go_ref, gko_ref, gvo_ref, kcat, vcat = rest[-7:]
    steps = pl.num_programs(0) * pl.num_programs(1) * pl.num_programs(2)
    t = (pl.program_id(0) * pl.num_programs(1) + pl.program_id(1)) * pl.num_programs(2) + pl.program_id(2)
    fresh = jnp.logical_or(t == 0, _hosted_unit(t, 0, nunit, steps) != _hosted_unit(t - 1, 0, nunit, steps))

    @pl.when(fresh)
    def _():
        _sample_heads(gq_ref, gkn_ref, gvn_ref, gck_ref, gcv_ref, gcc_ref, gcn_ref, go_ref, gko_ref, gvo_ref,
                      ln=ln, wbuf=wbuf)

    _att_block(q_ref, k_ref, v_ref, kp_ref, vp_ref, oh_ref, bias_ref, pv_ref, ml_ref, kcat, vcat, nsub=nsub)


def _att_block(q_ref, k_ref, v_ref, kp_ref, vp_ref, oh_ref, bias_ref, pv_ref, ml_ref, kcat, vcat, *, nsub):
    i = pl.program_id(2)
    blk = DIL_BLOCK
    for g in range(q_ref.shape[0]):
        kcat[g, 0:blk, :] = kp_ref[g]
        kcat[g, blk:, :] = k_ref[g]
        vcat[g, 0:blk, :] = vp_ref[g]
        vcat[g, blk:, :] = v_ref[g]
    lane = lax.broadcasted_iota(jnp.int32, (blk, LANES), 1)
    low = lane < ATT_HD
    quarter = lane // (ATT_HD // 2)
    onehot = oh_ref[...]
    bias_rest = bias_ref[1]
    bias_first = jnp.where(i == 0, bias_ref[0], bias_rest)
    ones = jnp.ones((2 * blk, LANES), BF16)
    low2 = lax.broadcasted_iota(jnp.int32, (2 * blk, LANES), 1) < ATT_HD
    for g, j, hp in itertools.product(range(q_ref.shape[0]), range(nsub), range(ATT_WIDTH // LANES)):
        bias = bias_first if j == 0 else bias_rest
        rows = slice(j * blk, (j + 1) * blk)
        krows = slice(j * blk, (j + 2) * blk)
        cols = slice(hp * LANES, (hp + 1) * LANES)
        q = q_ref[g, rows, cols]
        zero = jnp.zeros_like(q)
        q2 = jnp.concatenate([jnp.where(low, q, zero), jnp.where(low, zero, q)], axis=0)
        s = _dot_nt(jnp.concatenate([q2, onehot], axis=1), jnp.concatenate([kcat[g, krows, cols], bias], axis=1))
        m = jnp.max(s, axis=-1, keepdims=True)
        p = jnp.exp2(s - m).astype(BF16)
        vv = vcat[g, krows, cols]
        pv0 = _dot(p[0:blk], jnp.where(low2, vv, ones))
        pv1 = _dot(p[blk:], jnp.where(low2, ones, vv))
        pv_ref[g, rows, cols] = jnp.where(low, pv0, pv1).astype(pv_ref.dtype)
        ml_ref[g, rows, cols] = jnp.where(
            quarter == 0, m[0:blk], jnp.where(quarter == 1, pv1, jnp.where(quarter == 2, m[blk:], pv0)))


def _band_tables():
    jj = np.arange(2 * DIL_BLOCK)[:, None]
    qi = np.arange(DIL_BLOCK)[None, :]
    bias = np.stack([np.where((jj >= np.maximum(qi, fk)) & (jj <= qi + SPAN), 0.0, NEG_INF)
                     for fk in (DIL_BLOCK, 0)])
    onehot = np.tile(np.eye(DIL_BLOCK), (2, 1))
    return jnp.asarray(onehot, BF16), jnp.asarray(bias, BF16)


def _hosted_unit(t, unit0, nunit, steps):
    return unit0 + (t * nunit) // steps


def _dilated_attention(aq, ak, av, sample, carry, *, tq, groups, unit0, nunit):
    b, dil, m_len, w = aq.shape
    nsub = tq // DIL_BLOCK
    nblk = m_len // tq
    ngrp = dil // groups
    own = pl.BlockSpec((None, groups, tq, w), lambda bi, r, i: (bi, r, i, 0))
    prv = pl.BlockSpec((None, groups, DIL_BLOCK, w), lambda bi, r, i: (bi, r, jnp.maximum(i * nsub - 1, 0), 0))
    onehot, bias = _band_tables()
    in_specs = [own, own, own, prv, prv, _const_spec(onehot.shape), _const_spec(bias.shape)]
    args = [aq, ak, av, ak, av, onehot, bias]
    out_shape = [jax.ShapeDtypeStruct((b, dil, m_len, w), BF16), jax.ShapeDtypeStruct((b, dil, m_len, w), F32)]
    scratch = [pltpu.VMEM((groups, tq + DIL_BLOCK, w), BF16), pltpu.VMEM((groups, tq + DIL_BLOCK, w), BF16)]
    common = dict(grid=(b, ngrp, nblk), scratch_shapes=scratch, name=f"dilated_attn_d{dil}",
                  compiler_params=_params(("arbitrary", "arbitrary", "arbitrary")))
    if nunit == 0:
        return pl.pallas_call(functools.partial(_att_block, nsub=nsub), in_specs=in_specs,
                              out_specs=[own, own], out_shape=out_shape, **common)(*args)

    gq, gkn, gvn, gck, gcv, cnt_c, cnt_n = sample
    nb, _, wbuf = gck.shape
    ln = gq.shape[0] // nb
    steps = b * ngrp * nblk
    assert nunit <= steps

    def unit(bi, r, i):
        return _hosted_unit((bi * ngrp + r) * nblk + i, unit0, nunit, steps)

    rows = pl.BlockSpec((ln, w), lambda bi, r, i: (unit(bi, r, i), 0))
    window = pl.BlockSpec((None, w, wbuf), lambda bi, r, i: (unit(bi, r, i), 0, 0))
    in_specs += [rows, rows, rows, window, window, _const_spec(cnt_c.shape), _const_spec(cnt_n.shape)]
    args += [gq, gkn, gvn, gck, gcv, cnt_c, cnt_n]
    aliases = {}
    if carry is not None:
        aliases = {len(args) + n: 2 + n for n in range(3)}
        in_specs += [pl.BlockSpec(memory_space=pl.ANY)] * 3
        args += list(carry)
    out_shape += [jax.ShapeDtypeStruct((nb * ln, w), F32), jax.ShapeDtypeStruct((nb, w, wbuf), F32),
                  jax.ShapeDtypeStruct((nb, w, wbuf), F32)]
    return pl.pallas_call(
        functools.partial(_att_host_kernel, nsub=nsub, ln=ln, wbuf=wbuf, nunit=nunit),
        in_specs=in_specs,
        out_specs=[own, own, rows, window, window],
        out_shape=out_shape,
        input_output_aliases=aliases,
        **common,
    )(*args)


def _merge_patterns(pv1_ref, ml1_ref, pv4_ref, ml4_ref, pv16_ref, ml16_ref, tok, s4, att):
    tm = att.shape[0]
    half = ATT_HD // 2
    lane = lax.broadcasted_iota(jnp.int32, (tm, LANES), 1)
    first = lane % ATT_HD < half

    def unpack(ml):
        near = pltpu.roll(ml, half, 1)
        return jnp.where(first, ml, near), jnp.where(first, near, pltpu.roll(ml, 2 * half, 1))

    for c in range(ATT_WIDTH // LANES):
        sl = slice(c * LANES, (c + 1) * LANES)
        for n, (r4, r16) in enumerate(((pv4_ref, pv16_ref), (ml4_ref, ml16_ref))):
            for q in range(4):
                tok[n, c, pl.ds(q, tm // 4, stride=4), :] = r4[q, :, sl].astype(F32)
                for a in range(4):
                    s4[q, pl.ds(a, tm // 16, stride=4), :] = r16[4 * a + q, :, sl].astype(F32)
            for q in range(4):
                tok[2 + n, c, pl.ds(q, tm // 4, stride=4), :] = s4[q]
        (m1, l1), (m4, l4), (m16, l16) = unpack(ml1_ref[:, sl]), unpack(tok[1, c]), unpack(tok[3, c])
        top = jnp.maximum(m1, jnp.maximum(m4, m16))
        e1, e4, e16 = jnp.exp2(m1 - top), jnp.exp2(m4 - top), jnp.exp2(m16 - top)
        num = e1 * pv1_ref[:, sl].astype(F32) + e4 * tok[0, c] + e16 * tok[2, c]
        den = e1 * l1 + e4 * l4 + e16 * l16
        merged = num / den
        att[:, sl] = merged.astype(att.dtype)
        yield merged


def _tail_body(x_ref, ret_ref, att_h, mod_ref, gpm_ref, gpf_ref, gqf_ref, wo_ref, wu_ref, wd_ref, y_ref,
               filler=iter(())):
    d = D_MODEL
    x = x_ref[...]
    mixed = _dot(ret_ref[...].astype(BF16), wo_ref[0:RET_WIDTH, :]) + _dot(att_h, wo_ref[RET_WIDTH:, :])
    x1 = x + mod_ref[:, 2 * d:3 * d] * (_rms(mixed) * gpm_ref[...])
    h = ((_rms(x1) * gpf_ref[...]) * (1.0 + mod_ref[:, 4 * d:5 * d]) + mod_ref[:, 3 * d:4 * d]).astype(BF16)
    fc = 1024
    f = None
    for c in range(D_FF // fc):
        u = jnp.maximum(_dot(h, wu_ref[:, c * fc:(c + 1) * fc]), 0.0)
        piece = next(filler, None)
        if piece is not None:
            u = jnp.concatenate([u[:, 0:LANES] + _tied_zero([piece], u.shape[0]), u[:, LANES:]], axis=1)
        part = _dot((u * u).astype(BF16), wd_ref[c * fc:(c + 1) * fc, :])
        f = part if f is None else f + part
    for _ in filler:
        pass
    y_ref[...] = x1 + mod_ref[:, 5 * d:6 * d] * (_rms(f) * gqf_ref[...])


def _tail_kernel(x_ref, ret_ref, att_ref, *rest):
    _tail_body(x_ref, ret_ref, att_ref[...].astype(BF16), *rest)


def _tail_merge_kernel(x_ref, ret_ref, pv1_ref, ml1_ref, pv4_ref, ml4_ref, pv16_ref, ml16_ref, mod_ref,
                       gpm_ref, gpf_ref, gqf_ref, wo_ref, wu_ref, wd_ref, y_ref, tok, s4, att):
    t = pl.program_id(0)
    merge = functools.partial(_merge_patterns, pv1_ref, ml1_ref, pv4_ref, ml4_ref, pv16_ref, ml16_ref,
                              tok, s4, att)

    @pl.when(t == 0)
    def _():
        for _ in merge():
            pass

    @pl.when(t > 0)
    def _():
        att_h = att[...]
        _tail_body(x_ref, ret_ref, att_h, mod_ref, gpm_ref, gpf_ref, gqf_ref, wo_ref, wu_ref, wd_ref, y_ref,
                   filler=merge())


def _tail(x, ret_h, att, mod, g_post_mix, g_pre_ffn, g_post_ffn, wo_bf, wu_bf, wd_bf, *, tm):
    b, s, d = x.shape
    nt = s // tm
    per_row = mod.shape[1] != 1
    weights = [_const_spec((1, d)), _const_spec((1, d)), _const_spec((1, d)),
               _const_spec((d, d)), _const_spec((d, D_FF)), _const_spec((D_FF, d))]
    weight_args = (g_post_mix.reshape(1, d), g_pre_ffn.reshape(1, d), g_post_ffn.reshape(1, d),
                   wo_bf, wu_bf, wd_bf)
    out_shape = jax.ShapeDtypeStruct((b, s, d), F32)
    if not isinstance(att, (tuple, list)):
        mod_spec = pl.BlockSpec((None, tm if per_row else 1, 6 * d),
                                (lambda bi, i: (bi, i, 0)) if per_row else (lambda bi, i: (bi, 0, 0)))
        tok = lambda w: pl.BlockSpec((None, tm, w), lambda bi, i: (bi, i, 0))
        return pl.pallas_call(
            _tail_kernel,
            grid=(b, nt),
            in_specs=[tok(d), tok(RET_WIDTH), tok(ATT_WIDTH), mod_spec, *weights],
            out_specs=tok(d),
            out_shape=out_shape,
            compiler_params=_params(("arbitrary", "arbitrary")),
            name="out_proj_mlp",
        )(x, ret_h, att, mod, *weight_args)

    assert not per_row
    last = b * nt - 1
    cur = lambda t: jnp.maximum(t - 1, 0)
    nxt = lambda t: jnp.minimum(t, last)
    tok = lambda w: pl.BlockSpec((None, tm, w), lambda t: (cur(t) // nt, cur(t) % nt, 0))
    mod_spec = pl.BlockSpec((None, 1, 6 * d), lambda t: (cur(t) // nt, 0, 0))
    split = lambda r: pl.BlockSpec((None, r, tm // r, ATT_WIDTH), lambda t: (nxt(t) // nt, 0, nxt(t) % nt, 0))
    one = pl.BlockSpec((None, None, tm, ATT_WIDTH), lambda t: (nxt(t) // nt, 0, nxt(t) % nt, 0))
    nch = ATT_WIDTH // LANES
    return pl.pallas_call(
        _tail_merge_kernel,
        grid=(b * nt + 1,),
        in_specs=[tok(d), tok(RET_WIDTH), one, one, split(4), split(4), split(16), split(16), mod_spec, *weights],
        out_specs=tok(d),
        out_shape=out_shape,
        scratch_shapes=[pltpu.VMEM((4, nch, tm, LANES), F32), pltpu.VMEM((4, tm // 4, LANES), F32),
                        pltpu.VMEM((tm, ATT_WIDTH), BF16)],
        compiler_params=_params(("arbitrary",)),
        name="out_proj_mlp",
    )(x, ret_h, *att, mod, *weight_args)


def _ret_sample_kernel(q_ref, k_ref, v_ref, g_ref, gain_ref, dm_ref, qd_ref, kd_ref, cd_ref, s_ref,
                       o_ref, sn_ref, o_scr, qd_scr, kt_scr, *, nb, ln):
    q = q_ref[...].astype(F32)
    k = k_ref[...].astype(F32)
    v = v_ref[...].astype(BF16)
    a = _dot_nt(q.astype(BF16), k.astype(BF16)) * dm_ref[...]
    o_scr[...] = _dot(a.astype(BF16), v)
    qd_scr[...] = q * qd_ref[...]
    kt_scr[...] = (k * kd_ref[...]).T
    c_dec = cd_ref[0:1, :]
    col = lax.broadcasted_iota(jnp.int32, kt_scr.shape, 1)

    def body(bi, carry):
        rows = pl.ds(pl.multiple_of(bi * ln, ln), ln)
        s0 = s_ref[bi]
        o_scr[rows, :] += _dot(qd_scr[rows, :].astype(BF16), s0.astype(BF16))
        mine = (col >= bi * ln) & (col < (bi + 1) * ln)
        kt = jnp.where(mine, kt_scr[...], 0.0).astype(BF16)
        sn_ref[bi] = s0 * c_dec + _dot(kt, v)
        return carry

    lax.fori_loop(0, nb, body, 0, unroll=math.gcd(nb, 8))
    o_ref[...] = _ret_readout(o_scr[...], g_ref[...], gain_ref[...]).astype(o_ref.dtype)


def _retention_sample(rq, rk, rv, rg, ret_gain, state, *, nb, ln):
    n = nb * ln
    idx = np.arange(n)
    same = (idx[:, None] // ln == idx[None, :] // ln) & (idx[:, None] >= idx[None, :])
    diff = np.maximum(idx[:, None] - idx[None, :], 0).astype(np.float64)
    step = (idx % ln).astype(np.float64)
    lg = np.asarray(LOG_G, np.float64)
    dm = np.where(same[None], np.exp(lg[:, None, None] * diff[None]), 0.0)
    qd = np.broadcast_to(np.exp(lg[:, None] * (step + 1.0))[:, :, None], (RET_HEADS, n, RET_DK))
    kd = np.broadcast_to(np.exp(lg[:, None] * (ln - 1.0 - step))[:, :, None], (RET_HEADS, n, RET_DK))
    cd = np.broadcast_to(np.exp(lg * ln)[:, None, None], (RET_HEADS, 8, RET_DK))
    tab = lambda t: jnp.asarray(np.ascontiguousarray(t), F32)
    col = pl.BlockSpec((n, RET_DK), lambda h: (0, h))
    per_head = lambda r, c: pl.BlockSpec((None, r, c), lambda h: (h, 0, 0))
    st = pl.BlockSpec((nb, None, RET_DK, RET_DK), lambda h: (0, h, 0, 0))
    return pl.pallas_call(
        functools.partial(_ret_sample_kernel, nb=nb, ln=ln),
        grid=(RET_HEADS,),
        in_specs=[col, col, col, col, pl.BlockSpec((1, RET_DK), lambda h: (0, h)),
                  per_head(n, n), per_head(n, RET_DK), per_head(n, RET_DK), per_head(8, RET_DK), st],
        out_specs=[col, st],
        out_shape=[jax.ShapeDtypeStruct((n, RET_WIDTH), BF16),
                   jax.ShapeDtypeStruct((nb, RET_HEADS, RET_DK, RET_DK), F32)],
        scratch_shapes=[pltpu.VMEM((n, RET_DK), F32), pltpu.VMEM((n, RET_DK), F32),
                        pltpu.VMEM((RET_DK, n), F32)],
        compiler_params=_params(("arbitrary",)),
        name="retention_sample",
    )(rq, rk, rv, rg, ret_gain.reshape(1, RET_WIDTH), tab(dm), tab(qd), tab(kd), tab(cd), state)


def _shift_window(old_ref, new_t, out_ref, *, ln, wbuf):
    lane = lax.broadcasted_iota(jnp.int32, new_t.shape, 1)
    ncol = wbuf // LANES
    rolled = pltpu.roll(old_ref[:, 0:LANES], LANES - ln, 1)
    for c in range(ncol):
        nxt = pltpu.roll(old_ref[:, (c + 1) * LANES:(c + 2) * LANES] if c + 1 < ncol else new_t, LANES - ln, 1)
        out_ref[:, c * LANES:(c + 1) * LANES] = jnp.where(lane < LANES - ln, rolled, nxt)
        rolled = nxt


def _new_rows_minor(new_ref, ln):
    pad = jnp.zeros((LANES - ln, new_ref.shape[1]), F32)
    return jnp.concatenate([new_ref[...], pad], axis=0).T


def _sample_heads(q_ref, kn_ref, vn_ref, ck_ref, cv_ref, cc_ref, cn_ref, o_ref, ko_ref, vo_ref, *, ln, wbuf):
    kn_t = _new_rows_minor(kn_ref, ln)
    vn_t = _new_rows_minor(vn_ref, ln)
    _shift_window(ck_ref, kn_t, ko_ref, ln=ln, wbuf=wbuf)
    _shift_window(cv_ref, vn_t, vo_ref, ln=ln, wbuf=wbuf)
    q = q_ref[...]
    nh = q.shape[1] // ATT_HD
    lane = lax.broadcasted_iota(jnp.int32, q.shape, 1)
    heads = [lane // ATT_HD == h for h in range(nh)]
    qm = jnp.concatenate([jnp.where(hm, q, 0.0) for hm in heads], axis=0).astype(BF16)
    cnt_c = jnp.concatenate([cc_ref[...]] * nh, axis=0)
    cnt_n = jnp.concatenate([cn_ref[...]] * nh, axis=0)
    s_c = jnp.where(cnt_c > 0, _dot(qm, ck_ref[...].astype(BF16)), NEG_INF)
    s_n = jnp.where(cnt_n > 0, _dot(qm, kn_t.astype(BF16)), NEG_INF)
    m = jnp.maximum(jnp.max(s_c, axis=-1, keepdims=True), jnp.max(s_n, axis=-1, keepdims=True))
    p_c = cnt_c * jnp.exp2(s_c - m)
    p_n = cnt_n * jnp.exp2(s_n - m)
    l = jnp.sum(p_c, axis=-1, keepdims=True) + jnp.sum(p_n, axis=-1, keepdims=True)
    o = _dot_nt(p_c.astype(BF16), cv_ref[...].astype(BF16)) + _dot_nt(p_n.astype(BF16), vn_t.astype(BF16))
    o = o / l
    acc = jnp.zeros_like(q)
    for h, hm in enumerate(heads):
        acc = acc + jnp.where(hm, o[h * ln:(h + 1) * ln, :], 0.0)
    o_ref[...] = acc.astype(o_ref.dtype)


def _pattern_counts(ln, wbuf):
    cnt = np.zeros((ln, wbuf + ln), np.float32)
    for dil in DILATIONS:
        for l in range(ln):
            for j in range(SPAN + 1):
                row = wbuf + l - dil * j
                if row >= 0:
                    cnt[l, row] += 1.0
    return cnt


def _count_tables(ln, wbuf):
    cnt = _pattern_counts(ln, wbuf)
    cnt_new = np.zeros((ln, LANES), np.float32)
    cnt_new[:, :ln] = cnt[:, wbuf:]
    return jnp.asarray(cnt[:, :wbuf]), jnp.asarray(cnt_new)


def _step(x_prompt, x_sample, c_prompt, c_sample, state_ret, cache_win_k, cache_win_v, w_ada, b_ada,
          g_pre_mix, g_post_mix, g_pre_ffn, g_post_ffn, w_in, ret_gain, w_o, w_up, w_down,
          *, tm, tc, tq):
    assert w_in.shape[0] == 1, "single-layer step"
    bp, sp, d = x_prompt.shape
    nb, ln, _ = x_sample.shape
    wbuf = cache_win_k.shape[2]
    n_s = nb * ln

    w_in_f32 = w_in[0]

    rows = bp + nb
    pad = (-rows) % 8
    c_all = jnp.concatenate([c_prompt, c_sample, jnp.zeros((pad, d), F32)], axis=0)
    mod = _modulation(c_all, w_ada[0], b_ada[0])
    mod_p = mod[:bp].reshape(bp, 1, 6 * d)
    mod_s = jnp.repeat(mod[bp:rows], ln, axis=0).reshape(1, n_s, 6 * d)

    tabs_p = _rotation_constants(np.arange(tm), np.arange(sp // tm) * tm)
    tabs_s = _rotation_constants(np.tile(np.arange(ln), nb), [PAST_LEN])

    keep = min(MAX_WINDOW, sp)
    (rq, rk, rv, rg, aq, ak, av, akf, avf, aq4, ak4, av4, aq16, ak16, av16, w_in_bf) = _project(
        x_prompt, mod_p, g_pre_mix[0], w_in_f32, tabs_p, tm=tm, keep=keep, act_dtype=BF16, regroup=True)
    xs = x_sample.reshape(1, n_s, d)
    srq, srk, srv, srg, saq, _, _, sakf, savf = _project(
        xs, mod_s, g_pre_mix[0], w_in_bf, tabs_s, tm=n_s, keep=n_s, act_dtype=F32, regroup=False)
    flat = lambda t: t.reshape(n_s, GROUP_W)

    to_minor = lambda t: jnp.transpose(t[0], (0, 2, 3, 1)).reshape(nb, ATT_WIDTH, wbuf)
    from_minor = lambda t: jnp.transpose(t.reshape(nb, ATT_HEADS, ATT_HD, wbuf), (0, 3, 1, 2))[None]
    sample = (flat(saq), flat(sakf), flat(savf), to_minor(cache_win_k), to_minor(cache_win_v),
              *_count_tables(ln, wbuf))
    patterns = ((aq16, ak16, av16), (aq4, ak4, av4), (aq[:, None], ak[:, None], av[:, None]))
    steps = bp * sp // tq
    hosted, first = {}, 0
    for n in (1, 2, 0):
        hosted[n] = (first, min(steps, nb - first))
        first += hosted[n][1]
    assert first == nb
    carry, merged_in = None, []
    for n, qkv in enumerate(patterns):
        m_len = qkv[0].shape[2]
        rows = tq if hosted[n][1] else 2 * tq
        out = _dilated_attention(*qkv, sample, carry, tq=min(rows, m_len), groups=max(1, rows // m_len),
                                 unit0=hosted[n][0], nunit=hosted[n][1])
        merged_in = list(out[:2]) + merged_in
        carry = out[2:] or carry
    satt_h, k_out, v_out = carry

    ret_h, s_fin, (wo_bf, wu_bf, wd_bf) = _retention_prompt(
        rq, rk, rv, rg, ret_gain[0], (w_o[0], w_up[0], w_down[0]), tc=tc, chunk=RET_BLOCK)
    y_prompt = _tail(x_prompt, ret_h, tuple(merged_in), mod_p, g_post_mix[0], g_pre_ffn[0],
                     g_post_ffn[0], wo_bf, wu_bf, wd_bf, tm=tm)

    sret_h, s_new = _retention_sample(flat(srq), flat(srk), flat(srv), flat(srg), ret_gain[0],
                                      state_ret[0], nb=nb, ln=ln)
    y_sample = _tail(xs, sret_h.reshape(1, n_s, RET_WIDTH), satt_h.reshape(1, n_s, ATT_WIDTH), mod_s,
                     g_post_mix[0], g_pre_ffn[0], g_post_ffn[0], wo_bf, wu_bf, wd_bf, tm=n_s)

    cache_shape = (1, -1, keep, ATT_HEADS, ATT_HD)
    return (y_prompt,
            y_sample.reshape(nb, ln, d),
            s_fin[None],
            akf.reshape(cache_shape),
            avf.reshape(cache_shape),
            s_new[None],
            from_minor(k_out),
            from_minor(v_out))


def kernel(x_prompt, x_sample, c_prompt, c_sample, state_ret, cache_win_k, cache_win_v, w_ada, b_ada,
           g_pre_mix, g_post_mix, g_pre_ffn, g_post_ffn, w_in, ret_gain, w_o, w_up, w_down):
    return _step(x_prompt, x_sample, c_prompt, c_sample, state_ret, cache_win_k, cache_win_v, w_ada, b_ada,
                 g_pre_mix, g_post_mix, g_pre_ffn, g_post_ffn, w_in, ret_gain, w_o, w_up, w_down,
                 tm=512, tc=2048, tq=1024)
```

```python
import functools
import itertools
import math

import numpy as np
import jax
import jax.numpy as jnp
from jax import lax
from jax.experimental import pallas as pl
from jax.experimental.pallas import tpu as pltpu

F32 = jnp.float32
BF16 = jnp.bfloat16

D_MODEL = 1024
RET_HEADS = 4
RET_WIDTH = 512
RET_DK = 128
RET_THETA = 10000.0
RET_BLOCK = 256
ATT_HD = 64
ATT_HEADS = 8
ATT_WIDTH = 512
DILATIONS = (1, 4, 16)
MAX_WINDOW = 2048
SPAN = 128
DIL_BLOCK = 128
ROPE_THETA = 10000.0
D_FF = 4096
NORM_EPS = 1e-6
NEG_INF = -1e30
PAST_LEN = 16384
N_GROUPS = 7
GROUP_W = 512
LANES = 128
VMEM_LIMIT = 56 * 1024 * 1024

LOG2_E = math.log2(math.e)
LOG_G = tuple(math.log1p(-(2.0 ** (-5.0 - h))) for h in range(RET_HEADS))


def _dot(a, b):
    return jnp.dot(a, b, preferred_element_type=F32)


def _dot_nt(a, b):
    return lax.dot_general(a, b, (((1,), (1,)), ((), ())), preferred_element_type=F32)


def _dot_tn(a, b):
    return lax.dot_general(a, b, (((0,), (0,)), ((), ())), preferred_element_type=F32)


def _rms(x):
    return x * lax.rsqrt(jnp.mean(x * x, axis=-1, keepdims=True) + NORM_EPS)


def _silu(x):
    return x / (1.0 + jnp.exp(-x))


def _tied_zero(values, rows):
    bits = None
    for v in values:
        assert v.shape[0] % rows == 0 and v.shape[1] % LANES == 0
        for r0 in range(0, v.shape[0], rows):
            for c0 in range(0, v.shape[1], LANES):
                b = pltpu.bitcast(v[r0:r0 + rows, c0:c0 + LANES], jnp.uint32)
                bits = b if bits is None else bits | b
    sixteen = jnp.full(bits.shape, 16, jnp.uint32)
    return pltpu.bitcast(lax.shift_right_logical(lax.shift_right_logical(bits, sixteen), sixteen), F32)


def _params(sem):
    return pltpu.CompilerParams(dimension_semantics=sem, vmem_limit_bytes=VMEM_LIMIT)


def _const_spec(shape):
    nd = len(shape)
    return pl.BlockSpec(shape, lambda *_: (0,) * nd, pipeline_mode=pl.Buffered(1))


def _mod_kernel(c_ref, w_ref, b_ref, o_ref):
    a = _silu(c_ref[...]).astype(BF16)
    o_ref[...] = _dot(a, w_ref[...].astype(BF16)) + b_ref[...]


def _modulation(c, w_ada, b_ada):
    rows, d = c.shape
    n = w_ada.shape[1]
    tn = 1536
    return pl.pallas_call(
        _mod_kernel,
        grid=(n // tn,),
        in_specs=[pl.BlockSpec((rows, d), lambda j: (0, 0)),
                  pl.BlockSpec((d, tn), lambda j: (0, j)),
                  pl.BlockSpec((1, tn), lambda j: (0, j))],
        out_specs=pl.BlockSpec((rows, tn), lambda j: (0, j)),
        out_shape=jax.ShapeDtypeStruct((rows, n), F32),
        compiler_params=_params(("arbitrary",)),
        name="adaln_mod",
    )(c, w_ada, b_ada.reshape(1, n))


def _rotate_pairs(x, cos, sin):
    lane = lax.broadcasted_iota(jnp.int32, x.shape, 1)
    partner = jnp.where(lane % 2 == 0, pltpu.roll(x, LANES - 1, 1), pltpu.roll(x, 1, 1))
    return x * cos + partner * sin


def _rotate_half(x, cos, sin):
    lane = lax.broadcasted_iota(jnp.int32, x.shape, 1)
    half = ATT_HD // 2
    partner = jnp.where(lane % ATT_HD < half, pltpu.roll(x, LANES - half, 1), pltpu.roll(x, half, 1))
    return x * cos + partner * sin


def _store_regrouped(val, c, tok_ref, r4_ref, r16_ref, zs, s4):
    sl = slice(c * LANES, (c + 1) * LANES)
    tok_ref[:, sl] = val.astype(tok_ref.dtype)
    tm = val.shape[0]
    zs[c] = val
    for q in range(4):
        g4 = zs[c, pl.ds(q, tm // 4, stride=4), :]
        r4_ref[q, :, sl] = g4.astype(r4_ref.dtype)
        s4[c, q] = g4
    for q in range(4):
        for a in range(4):
            r16_ref[4 * a + q, :, sl] = s4[c, q, pl.ds(a, tm // 16, stride=4), :].astype(r16_ref.dtype)


def _proj_kernel(x_ref, mod_ref, g_ref, wf_ref, rows_ref, tile_ref, *refs, regroup):
    if regroup:
        (rq_ref, rk_ref, rv_ref, rg_ref, aq_ref, ak_ref, av_ref, akf_ref, avf_ref,
         aq4_ref, ak4_ref, av4_ref, aq16_ref, ak16_ref, av16_ref, wb_ref, w_ref, zs, s4) = refs

        @pl.when(jnp.logical_and(pl.program_id(0) == 0, pl.program_id(1) == 0))
        def _():
            for gi in range(N_GROUPS):
                sl = slice(gi * GROUP_W, (gi + 1) * GROUP_W)
                w_ref[:, sl] = wf_ref[:, sl].astype(w_ref.dtype)

        slab = wb_ref.shape[0]
        t = pl.program_id(0) * pl.num_programs(1) + pl.program_id(1)
        wb_ref[...] = w_ref[pl.ds(pl.multiple_of(t * slab, slab), slab), :]
    else:
        rq_ref, rk_ref, rv_ref, rg_ref, aq_ref, ak_ref, av_ref, akf_ref, avf_ref = refs
        w_ref = wf_ref

    d = D_MODEL
    x = x_ref[...]
    h = (_rms(x) * g_ref[...]) * (1.0 + mod_ref[:, d:2 * d]) + mod_ref[:, 0:d]
    h = h.astype(BF16)
    t = tile_ref[...]
    cr = t[0:1] * rows_ref[0] - t[1:2] * rows_ref[1]
    sr = (t[1:2] * rows_ref[0] + t[0:1] * rows_ref[1]) * t[4:5]
    ca = t[2:3] * rows_ref[2] - t[3:4] * rows_ref[3]
    sa = (t[3:4] * rows_ref[2] + t[2:3] * rows_ref[3]) * t[5:6]
    nch = GROUP_W // LANES

    def group(gi):
        return _dot(h, w_ref[:, gi * GROUP_W:(gi + 1) * GROUP_W])

    z = group(0)
    for c in range(nch):
        sl = slice(c * LANES, (c + 1) * LANES)
        rq_ref[:, sl] = _rotate_pairs(z[:, sl], cr, sr).astype(rq_ref.dtype)
    z = group(1)
    for c in range(nch):
        sl = slice(c * LANES, (c + 1) * LANES)
        rk_ref[:, sl] = (_rotate_pairs(z[:, sl], cr, sr) * (RET_DK ** -0.5)).astype(rk_ref.dtype)
    rv_ref[...] = group(2).astype(rv_ref.dtype)
    rg_ref[...] = group(3)
    def emit(val, c, tok_ref, r4_ref, r16_ref):
        if regroup:
            _store_regrouped(val, c, tok_ref, r4_ref, r16_ref, zs, s4)
        else:
            tok_ref[:, c * LANES:(c + 1) * LANES] = val.astype(tok_ref.dtype)

    r4 = (aq4_ref, ak4_ref, av4_ref) if regroup else (None,) * 3
    r16 = (aq16_ref, ak16_ref, av16_ref) if regroup else (None,) * 3
    z = group(4)
    for c in range(nch):
        sl = slice(c * LANES, (c + 1) * LANES)
        emit(_rotate_half(z[:, sl], ca, sa) * (ATT_HD ** -0.5 * LOG2_E), c, aq_ref, r4[0], r16[0])
    z = group(5)
    for c in range(nch):
        sl = slice(c * LANES, (c + 1) * LANES)
        r = _rotate_half(z[:, sl], ca, sa)
        akf_ref[:, sl] = r
        emit(r, c, ak_ref, r4[1], r16[1])
    z = group(6)
    avf_ref[...] = z
    for c in range(nch):
        emit(z[:, c * LANES:(c + 1) * LANES], c, av_ref, r4[2], r16[2])


def _project(x, mod, g_pre, w_in, tabs, *, tm, keep, act_dtype, regroup):
    b, s, d = x.shape
    nt = s // tm
    rows_tab, tile_tab = tabs
    assert rows_tab.shape == (4, tm, LANES) and tile_tab.shape == (nt, 8, LANES)
    mod_rows = mod.shape[1]
    per_row = mod_rows != 1
    first_keep = (s - keep) // tm
    tok = pl.BlockSpec((None, tm, GROUP_W), lambda bi, i: (bi, i, 0))
    keep_spec = pl.BlockSpec((None, tm, GROUP_W), lambda bi, i: (bi, jnp.maximum(i - first_keep, 0), 0))
    tile_spec = pl.BlockSpec((None, 8, LANES), lambda bi, i: (i, 0, 0))
    mod_spec = pl.BlockSpec((None, tm if per_row else 1, 6 * d),
                            (lambda bi, i: (bi, i, 0)) if per_row else (lambda bi, i: (bi, 0, 0)))
    act = jax.ShapeDtypeStruct((b, s, GROUP_W), act_dtype)
    full = jax.ShapeDtypeStruct((b, s, GROUP_W), F32)
    kept = jax.ShapeDtypeStruct((b, keep, GROUP_W), F32)
    out_specs = [tok, tok, tok, tok, tok, tok, tok, keep_spec, keep_spec]
    out_shape = [act, act, act, full, act, act, act, kept, kept]
    scratch = []
    if regroup:
        for r in (4, 16):
            out_specs += [pl.BlockSpec((None, r, tm // r, GROUP_W), lambda bi, i: (bi, 0, i, 0))] * 3
            out_shape += [jax.ShapeDtypeStruct((b, r, s // r, GROUP_W), act_dtype)] * 3
        assert d % (16 * b * nt) == 0
        out_specs += [pl.BlockSpec((d // (b * nt), w_in.shape[1]), lambda bi, i: (bi * nt + i, 0))]
        out_shape += [jax.ShapeDtypeStruct(w_in.shape, BF16)]
        nch = GROUP_W // LANES
        scratch += [pltpu.VMEM(w_in.shape, BF16),
                    pltpu.VMEM((nch, tm, LANES), F32), pltpu.VMEM((nch, 4, tm // 4, LANES), F32)]
    else:
        assert w_in.dtype == BF16
    return pl.pallas_call(
        functools.partial(_proj_kernel, regroup=regroup),
        grid=(b, nt),
        in_specs=[pl.BlockSpec((None, tm, d), lambda bi, i: (bi, i, 0)),
                  mod_spec,
                  _const_spec((1, d)),
                  _const_spec((d, N_GROUPS * GROUP_W)),
                  _const_spec((4, tm, LANES)), tile_spec],
        out_specs=out_specs,
        out_shape=out_shape,
        scratch_shapes=scratch,
        compiler_params=_params(("arbitrary", "arbitrary")),
        name="in_proj",
    )(x, mod, g_pre.reshape(1, d), w_in, rows_tab, tile_tab)


def _rotation_constants(row_pos, tile_pos):
    lane = np.arange(LANES)
    inv_r = (1.0 / RET_THETA ** np.linspace(0.0, 1.0, RET_DK // 2))[lane // 2]
    inv_a = (1.0 / ROPE_THETA ** (np.arange(0, ATT_HD, 2) / ATT_HD))[lane % (ATT_HD // 2)]
    rp = np.asarray(row_pos, np.float64)[:, None]
    tp = np.asarray(tile_pos, np.float64)[:, None]
    rows = np.stack([np.cos(rp * inv_r), np.sin(rp * inv_r), np.cos(rp * inv_a), np.sin(rp * inv_a)])
    sign_r = np.where(lane % 2 == 0, -1.0, 1.0)
    sign_a = np.where(lane % ATT_HD < ATT_HD // 2, -1.0, 1.0)
    ones = np.ones_like(tp * inv_r)
    tiles = np.stack([np.cos(tp * inv_r), np.sin(tp * inv_r), np.cos(tp * inv_a), np.sin(tp * inv_a),
                      ones * sign_r, ones * sign_a, 0 * ones, 0 * ones], axis=1)
    return jnp.asarray(rows, F32), jnp.asarray(tiles, F32)


def _ret_readout(o, g, gain):
    return (_rms(o) * gain) * _silu(g)


def _ret_prompt_kernel(q_ref, k_ref, v_ref, g_ref, gain_ref, intra_ref, qd_ref, kd_ref, *rest, nchunk, c_len,
                       ncast):
    cast_in, (o_ref, sfin_ref), cast_out, s_scr = rest[:ncast], rest[ncast:ncast + 2], rest[ncast + 2:-1], rest[-1]
    for src, dst in zip(cast_in, cast_out):
        dst[...] = src[...].astype(dst.dtype)
    i = pl.program_id(1)

    @pl.when(i == 0)
    def _():
        s_scr[...] = jnp.zeros_like(s_scr)

    for h in range(RET_HEADS):
        intra, q_dec, k_dec = intra_ref[h], qd_ref[h], kd_ref[h]
        c_dec = math.exp(LOG_G[h] * c_len)
        hs = slice(h * RET_DK, (h + 1) * RET_DK)
        state = s_scr[h]
        for c in range(nchunk):
            rs = slice(c * c_len, (c + 1) * c_len)
            q = q_ref[rs, hs]
            k = k_ref[rs, hs]
            v = v_ref[rs, hs]
            a = _dot_nt(q, k) * intra
            lhs = jnp.concatenate([a.astype(BF16), (q.astype(F32) * q_dec).astype(BF16)], axis=1)
            o = _dot(lhs, jnp.concatenate([v, state.astype(BF16)], axis=0))
            kd = (k.astype(F32) * k_dec).astype(BF16)
            state = state * c_dec + _dot_tn(kd, v)
            o_ref[rs, hs] = _ret_readout(o, g_ref[rs, hs], gain_ref[:, hs]).astype(o_ref.dtype)
        s_scr[h] = state

    @pl.when(i == pl.num_programs(1) - 1)
    def _():
        sfin_ref[...] = s_scr[...]


def _retention_prompt(rq, rk, rv, rg, ret_gain, weights, *, tc, chunk):
    b, s, w = rq.shape
    nblk = s // tc
    steps = b * nblk
    tok = pl.BlockSpec((None, tc, w), lambda bi, i: (bi, i, 0))
    slab = lambda m: pl.BlockSpec((m.shape[0] // steps, m.shape[1]), lambda bi, i: (bi * nblk + i, 0))
    assert all(m.shape[0] % (16 * steps) == 0 for m in weights)
    idx = np.arange(chunk, dtype=np.float64)
    lg = np.asarray(LOG_G, np.float64)[:, None, None]
    diff = idx[:, None] - idx[None, :]
    intra = np.where(diff >= 0, np.exp(lg * np.maximum(diff, 0.0)), 0.0)
    lanes = np.ones((1, 1, RET_DK))
    qd = np.exp(lg * (idx[None, :, None] + 1.0)) * lanes
    kd = np.exp(lg * (chunk - 1.0 - idx[None, :, None])) * lanes
    tables = [jnp.asarray(t, F32) for t in (intra, qd, kd)]
    out = pl.pallas_call(
        functools.partial(_ret_prompt_kernel, nchunk=tc // chunk, c_len=chunk, ncast=len(weights)),
        grid=(b, nblk),
        in_specs=[tok, tok, tok, tok, _const_spec((1, w))] + [_const_spec(t.shape) for t in tables]
        + [slab(m) for m in weights],
        out_specs=[tok, pl.BlockSpec((None, RET_HEADS, RET_DK, RET_DK), lambda bi, i: (bi, 0, 0, 0))]
        + [slab(m) for m in weights],
        out_shape=[jax.ShapeDtypeStruct((b, s, w), BF16),
                   jax.ShapeDtypeStruct((b, RET_HEADS, RET_DK, RET_DK), F32)]
        + [jax.ShapeDtypeStruct(m.shape, BF16) for m in weights],
        scratch_shapes=[pltpu.VMEM((RET_HEADS, RET_DK, RET_DK), F32)],
        compiler_params=_params(("arbitrary", "arbitrary")),
        name="retention_prompt",
    )(rq, rk, rv, rg, ret_gain.reshape(1, w), *tables, *weights)
    return out[0], out[1], out[2:]


def _att_host_kernel(q_ref, k_ref, v_ref, kp_ref, vp_ref, oh_ref, bias_ref, gq_ref, gkn_ref, gvn_ref, gck_ref,
                     gcv_ref, gcc_ref, gcn_ref, *rest, nsub, ln, wbuf, nunit):
    pv_ref, ml_ref, go_ref, gko_ref, gvo_ref, kcat, vcat = rest[-7:]
    steps = pl.num_programs(0) * pl.num_programs(1) * pl.num_programs(2)
    t = (pl.program_id(0) * pl.num_programs(1) + pl.program_id(1)) * pl.num_programs(2) + pl.program_id(2)
    fresh = jnp.logical_or(t == 0, _hosted_unit(t, 0, nunit, steps) != _hosted_unit(t - 1, 0, nunit, steps))

    @pl.when(fresh)
    def _():
        _sample_heads(gq_ref, gkn_ref, gvn_ref, gck_ref, gcv_ref, gcc_ref, gcn_ref, go_ref, gko_ref, gvo_ref,
                      ln=ln, wbuf=wbuf)

    _att_block(q_ref, k_ref, v_ref, kp_ref, vp_ref, oh_ref, bias_ref, pv_ref, ml_ref, kcat, vcat, nsub=nsub)


def _att_block(q_ref, k_ref, v_ref, kp_ref, vp_ref, oh_ref, bias_ref, pv_ref, ml_ref, kcat, vcat, *, nsub):
    i = pl.program_id(2)
    blk = DIL_BLOCK
    for g in range(q_ref.shape[0]):
        kcat[g, 0:blk, :] = kp_ref[g]
        kcat[g, blk:, :] = k_ref[g]
        vcat[g, 0:blk, :] = vp_ref[g]
        vcat[g, blk:, :] = v_ref[g]
    lane = lax.broadcasted_iota(jnp.int32, (blk, LANES), 1)
    low = lane < ATT_HD
    quarter = lane // (ATT_HD // 2)
    onehot = oh_ref[...]
    bias_rest = bias_ref[1]
    bias_first = jnp.where(i == 0, bias_ref[0], bias_rest)
    ones = jnp.ones((2 * blk, LANES), BF16)
    low2 = lax.broadcasted_iota(jnp.int32, (2 * blk, LANES), 1) < ATT_HD
    for g, j, hp in itertools.product(range(q_ref.shape[0]), range(nsub), range(ATT_WIDTH // LANES)):
        bias = bias_first if j == 0 else bias_rest
        rows = slice(j * blk, (j + 1) * blk)
        krows = slice(j * blk, (j + 2) * blk)
        cols = slice(hp * LANES, (hp + 1) * LANES)
        q = q_ref[g, rows, cols]
        zero = jnp.zeros_like(q)
        q2 = jnp.concatenate([jnp.where(low, q, zero), jnp.where(low, zero, q)], axis=0)
        s = _dot_nt(jnp.concatenate([q2, onehot], axis=1), jnp.concatenate([kcat[g, krows, cols], bias], axis=1))
        m = jnp.max(s, axis=-1, keepdims=True)
        p = jnp.exp2(s - m).astype(BF16)
        vv = vcat[g, krows, cols]
        pv0 = _dot(p[0:blk], jnp.where(low2, vv, ones))
        pv1 = _dot(p[blk:], jnp.where(low2, ones, vv))
        pv_ref[g, rows, cols] = jnp.where(low, pv0, pv1).astype(pv_ref.dtype)
        ml_ref[g, rows, cols] = jnp.where(
            quarter == 0, m[0:blk], jnp.where(quarter == 1, pv1, jnp.where(quarter == 2, m[blk:], pv0)))


def _band_tables():
    jj = np.arange(2 * DIL_BLOCK)[:, None]
    qi = np.arange(DIL_BLOCK)[None, :]
    bias = np.stack([np.where((jj >= np.maximum(qi, fk)) & (jj <= qi + SPAN), 0.0, NEG_INF)
                     for fk in (DIL_BLOCK, 0)])
    onehot = np.tile(np.eye(DIL_BLOCK), (2, 1))
    return jnp.asarray(onehot, BF16), jnp.asarray(bias, BF16)


def _hosted_unit(t, unit0, nunit, steps):
    return unit0 + (t * nunit) // steps


def _dilated_attention(aq, ak, av, sample, carry, *, tq, groups, unit0, nunit):
    b, dil, m_len, w = aq.shape
    nsub = tq // DIL_BLOCK
    nblk = m_len // tq
    ngrp = dil // groups
    own = pl.BlockSpec((None, groups, tq, w), lambda bi, r, i: (bi, r, i, 0))
    prv = pl.BlockSpec((None, groups, DIL_BLOCK, w), lambda bi, r, i: (bi, r, jnp.maximum(i * nsub - 1, 0), 0))
    onehot, bias = _band_tables()
    in_specs = [own, own, own, prv, prv, _const_spec(onehot.shape), _const_spec(bias.shape)]
    args = [aq, ak, av, ak, av, onehot, bias]
    out_shape = [jax.ShapeDtypeStruct((b, dil, m_len, w), BF16), jax.ShapeDtypeStruct((b, dil, m_len, w), F32)]
    scratch = [pltpu.VMEM((groups, tq + DIL_BLOCK, w), BF16), pltpu.VMEM((groups, tq + DIL_BLOCK, w), BF16)]
    common = dict(grid=(b, ngrp, nblk), scratch_shapes=scratch, name=f"dilated_attn_d{dil}",
                  compiler_params=_params(("arbitrary", "arbitrary", "arbitrary")))
    if nunit == 0:
        return pl.pallas_call(functools.partial(_att_block, nsub=nsub), in_specs=in_specs,
                              out_specs=[own, own], out_shape=out_shape, **common)(*args)

    gq, gkn, gvn, gck, gcv, cnt_c, cnt_n = sample
    nb, _, wbuf = gck.shape
    ln = gq.shape[0] // nb
    steps = b * ngrp * nblk
    assert nunit <= steps

    def unit(bi, r, i):
        return _hosted_unit((bi * ngrp + r) * nblk + i, unit0, nunit, steps)

    rows = pl.BlockSpec((ln, w), lambda bi, r, i: (unit(bi, r, i), 0))
    window = pl.BlockSpec((None, w, wbuf), lambda bi, r, i: (unit(bi, r, i), 0, 0))
    in_specs += [rows, rows, rows, window, window, _const_spec(cnt_c.shape), _const_spec(cnt_n.shape)]
    args += [gq, gkn, gvn, gck, gcv, cnt_c, cnt_n]
    aliases = {}
    if carry is not None:
        aliases = {len(args) + n: 2 + n for n in range(3)}
        in_specs += [pl.BlockSpec(memory_space=pl.ANY)] * 3
        args += list(carry)
    out_shape += [jax.ShapeDtypeStruct((nb * ln, w), F32), jax.ShapeDtypeStruct((nb, w, wbuf), F32),
                  jax.ShapeDtypeStruct((nb, w, wbuf), F32)]
    return pl.pallas_call(
        functools.partial(_att_host_kernel, nsub=nsub, ln=ln, wbuf=wbuf, nunit=nunit),
        in_specs=in_specs,
        out_specs=[own, own, rows, window, window],
        out_shape=out_shape,
        input_output_aliases=aliases,
        **common,
    )(*args)


def _merge_patterns(pv1_ref, ml1_ref, pv4_ref, ml4_ref, pv16_ref, ml16_ref, tok, s4, att):
    tm = att.shape[0]
    half = ATT_HD // 2
    lane = lax.broadcasted_iota(jnp.int32, (tm, LANES), 1)
    first = lane % ATT_HD < half

    def unpack(ml):
        near = pltpu.roll(ml, half, 1)
        return jnp.where(first, ml, near), jnp.where(first, near, pltpu.roll(ml, 2 * half, 1))

    for c in range(ATT_WIDTH // LANES):
        sl = slice(c * LANES, (c + 1) * LANES)
        for n, (r4, r16) in enumerate(((pv4_ref, pv16_ref), (ml4_ref, ml16_ref))):
            for q in range(4):
                tok[n, c, pl.ds(q, tm // 4, stride=4), :] = r4[q, :, sl].astype(F32)
                for a in range(4):
                    s4[q, pl.ds(a, tm // 16, stride=4), :] = r16[4 * a + q, :, sl].astype(F32)
            for q in range(4):
                tok[2 + n, c, pl.ds(q, tm // 4, stride=4), :] = s4[q]
        (m1, l1), (m4, l4), (m16, l16) = unpack(ml1_ref[:, sl]), unpack(tok[1, c]), unpack(tok[3, c])
        top = jnp.maximum(m1, jnp.maximum(m4, m16))
        e1, e4, e16 = jnp.exp2(m1 - top), jnp.exp2(m4 - top), jnp.exp2(m16 - top)
        num = e1 * pv1_ref[:, sl].astype(F32) + e4 * tok[0, c] + e16 * tok[2, c]
        den = e1 * l1 + e4 * l4 + e16 * l16
        merged = num / den
        att[:, sl] = merged.astype(att.dtype)
        yield merged


def _tail_body(x_ref, ret_ref, att_h, mod_ref, gpm_ref, gpf_ref, gqf_ref, wo_ref, wu_ref, wd_ref, y_ref,
               filler=iter(())):
    d = D_MODEL
    x = x_ref[...]
    mixed = _dot(ret_ref[...].astype(BF16), wo_ref[0:RET_WIDTH, :]) + _dot(att_h, wo_ref[RET_WIDTH:, :])
    x1 = x + mod_ref[:, 2 * d:3 * d] * (_rms(mixed) * gpm_ref[...])
    h = ((_rms(x1) * gpf_ref[...]) * (1.0 + mod_ref[:, 4 * d:5 * d]) + mod_ref[:, 3 * d:4 * d]).astype(BF16)
    fc = 1024
    f = None
    for c in range(D_FF // fc):
        u = jnp.maximum(_dot(h, wu_ref[:, c * fc:(c + 1) * fc]), 0.0)
        piece = next(filler, None)
        if piece is not None:
            u = jnp.concatenate([u[:, 0:LANES] + _tied_zero([piece], u.shape[0]), u[:, LANES:]], axis=1)
        part = _dot((u * u).astype(BF16), wd_ref[c * fc:(c + 1) * fc, :])
        f = part if f is None else f + part
    for _ in filler:
        pass
    y_ref[...] = x1 + mod_ref[:, 5 * d:6 * d] * (_rms(f) * gqf_ref[...])


def _tail_kernel(x_ref, ret_ref, att_ref, *rest):
    _tail_body(x_ref, ret_ref, att_ref[...].astype(BF16), *rest)


def _tail_merge_kernel(x_ref, ret_ref, pv1_ref, ml1_ref, pv4_ref, ml4_ref, pv16_ref, ml16_ref, mod_ref,
                       gpm_ref, gpf_ref, gqf_ref, wo_ref, wu_ref, wd_ref, y_ref, tok, s4, att):
    t = pl.program_id(0)
    merge = functools.partial(_merge_patterns, pv1_ref, ml1_ref, pv4_ref, ml4_ref, pv16_ref, ml16_ref,
                              tok, s4, att)

    @pl.when(t == 0)
    def _():
        for _ in merge():
            pass

    @pl.when(t > 0)
    def _():
        att_h = att[...]
        _tail_body(x_ref, ret_ref, att_h, mod_ref, gpm_ref, gpf_ref, gqf_ref, wo_ref, wu_ref, wd_ref, y_ref,
                   filler=merge())


def _tail(x, ret_h, att, mod, g_post_mix, g_pre_ffn, g_post_ffn, wo_bf, wu_bf, wd_bf, *, tm):
    b, s, d = x.shape
    nt = s // tm
    per_row = mod.shape[1] != 1
    weights = [_const_spec((1, d)), _const_spec((1, d)), _const_spec((1, d)),
               _const_spec((d, d)), _const_spec((d, D_FF)), _const_spec((D_FF, d))]
    weight_args = (g_post_mix.reshape(1, d), g_pre_ffn.reshape(1, d), g_post_ffn.reshape(1, d),
                   wo_bf, wu_bf, wd_bf)
    out_shape = jax.ShapeDtypeStruct((b, s, d), F32)
    if not isinstance(att, (tuple, list)):
        mod_spec = pl.BlockSpec((None, tm if per_row else 1, 6 * d),
                                (lambda bi, i: (bi, i, 0)) if per_row else (lambda bi, i: (bi, 0, 0)))
        tok = lambda w: pl.BlockSpec((None, tm, w), lambda bi, i: (bi, i, 0))
        return pl.pallas_call(
            _tail_kernel,
            grid=(b, nt),
            in_specs=[tok(d), tok(RET_WIDTH), tok(ATT_WIDTH), mod_spec, *weights],
            out_specs=tok(d),
            out_shape=out_shape,
            compiler_params=_params(("arbitrary", "arbitrary")),
            name="out_proj_mlp",
        )(x, ret_h, att, mod, *weight_args)

    assert not per_row
    last = b * nt - 1
    cur = lambda t: jnp.maximum(t - 1, 0)
    nxt = lambda t: jnp.minimum(t, last)
    tok = lambda w: pl.BlockSpec((None, tm, w), lambda t: (cur(t) // nt, cur(t) % nt, 0))
    mod_spec = pl.BlockSpec((None, 1, 6 * d), lambda t: (cur(t) // nt, 0, 0))
    split = lambda r: pl.BlockSpec((None, r, tm // r, ATT_WIDTH), lambda t: (nxt(t) // nt, 0, nxt(t) % nt, 0))
    one = pl.BlockSpec((None, None, tm, ATT_WIDTH), lambda t: (nxt(t) // nt, 0, nxt(t) % nt, 0))
    nch = ATT_WIDTH // LANES
    return pl.pallas_call(
        _tail_merge_kernel,
        grid=(b * nt + 1,),
        in_specs=[tok(d), tok(RET_WIDTH), one, one, split(4), split(4), split(16), split(16), mod_spec, *weights],
        out_specs=tok(d),
        out_shape=out_shape,
        scratch_shapes=[pltpu.VMEM((4, nch, tm, LANES), F32), pltpu.VMEM((4, tm // 4, LANES), F32),
                        pltpu.VMEM((tm, ATT_WIDTH), BF16)],
        compiler_params=_params(("arbitrary",)),
        name="out_proj_mlp",
    )(x, ret_h, *att, mod, *weight_args)


def _ret_sample_kernel(q_ref, k_ref, v_ref, g_ref, gain_ref, dm_ref, qd_ref, kd_ref, cd_ref, s_ref,
                       o_ref, sn_ref, o_scr, qd_scr, kt_scr, *, nb, ln):
    q = q_ref[...].astype(F32)
    k = k_ref[...].astype(F32)
    v = v_ref[...].astype(BF16)
    a = _dot_nt(q.astype(BF16), k.astype(BF16)) * dm_ref[...]
    o_scr[...] = _dot(a.astype(BF16), v)
    qd_scr[...] = q * qd_ref[...]
    kt_scr[...] = (k * kd_ref[...]).T
    c_dec = cd_ref[0:1, :]
    col = lax.broadcasted_iota(jnp.int32, kt_scr.shape, 1)

    def body(bi, carry):
        rows = pl.ds(pl.multiple_of(bi * ln, ln), ln)
        s0 = s_ref[bi]
        o_scr[rows, :] += _dot(qd_scr[rows, :].astype(BF16), s0.astype(BF16))
        mine = (col >= bi * ln) & (col < (bi + 1) * ln)
        kt = jnp.where(mine, kt_scr[...], 0.0).astype(BF16)
        sn_ref[bi] = s0 * c_dec + _dot(kt, v)
        return carry

    lax.fori_loop(0, nb, body, 0, unroll=math.gcd(nb, 8))
    o_ref[...] = _ret_readout(o_scr[...], g_ref[...], gain_ref[...]).astype(o_ref.dtype)


def _retention_sample(rq, rk, rv, rg, ret_gain, state, *, nb, ln):
    n = nb * ln
    idx = np.arange(n)
    same = (idx[:, None] // ln == idx[None, :] // ln) & (idx[:, None] >= idx[None, :])
    diff = np.maximum(idx[:, None] - idx[None, :], 0).astype(np.float64)
    step = (idx % ln).astype(np.float64)
    lg = np.asarray(LOG_G, np.float64)
    dm = np.where(same[None], np.exp(lg[:, None, None] * diff[None]), 0.0)
    qd = np.broadcast_to(np.exp(lg[:, None] * (step + 1.0))[:, :, None], (RET_HEADS, n, RET_DK))
    kd = np.broadcast_to(np.exp(lg[:, None] * (ln - 1.0 - step))[:, :, None], (RET_HEADS, n, RET_DK))
    cd = np.broadcast_to(np.exp(lg * ln)[:, None, None], (RET_HEADS, 8, RET_DK))
    tab = lambda t: jnp.asarray(np.ascontiguousarray(t), F32)
    col = pl.BlockSpec((n, RET_DK), lambda h: (0, h))
    per_head = lambda r, c: pl.BlockSpec((None, r, c), lambda h: (h, 0, 0))
    st = pl.BlockSpec((nb, None, RET_DK, RET_DK), lambda h: (0, h, 0, 0))
    return pl.pallas_call(
        functools.partial(_ret_sample_kernel, nb=nb, ln=ln),
        grid=(RET_HEADS,),
        in_specs=[col, col, col, col, pl.BlockSpec((1, RET_DK), lambda h: (0, h)),
                  per_head(n, n), per_head(n, RET_DK), per_head(n, RET_DK), per_head(8, RET_DK), st],
        out_specs=[col, st],
        out_shape=[jax.ShapeDtypeStruct((n, RET_WIDTH), BF16),
                   jax.ShapeDtypeStruct((nb, RET_HEADS, RET_DK, RET_DK), F32)],
        scratch_shapes=[pltpu.VMEM((n, RET_DK), F32), pltpu.VMEM((n, RET_DK), F32),
                        pltpu.VMEM((RET_DK, n), F32)],
        compiler_params=_params(("arbitrary",)),
        name="retention_sample",
    )(rq, rk, rv, rg, ret_gain.reshape(1, RET_WIDTH), tab(dm), tab(qd), tab(kd), tab(cd), state)


def _shift_window(old_ref, new_t, out_ref, *, ln, wbuf):
    lane = lax.broadcasted_iota(jnp.int32, new_t.shape, 1)
    ncol = wbuf // LANES
    rolled = pltpu.roll(old_ref[:, 0:LANES], LANES - ln, 1)
    for c in range(ncol):
        nxt = pltpu.roll(old_ref[:, (c + 1) * LANES:(c + 2) * LANES] if c + 1 < ncol else new_t, LANES - ln, 1)
        out_ref[:, c * LANES:(c + 1) * LANES] = jnp.where(lane < LANES - ln, rolled, nxt)
        rolled = nxt


def _new_rows_minor(new_ref, ln):
    pad = jnp.zeros((LANES - ln, new_ref.shape[1]), F32)
    return jnp.concatenate([new_ref[...], pad], axis=0).T


def _sample_heads(q_ref, kn_ref, vn_ref, ck_ref, cv_ref, cc_ref, cn_ref, o_ref, ko_ref, vo_ref, *, ln, wbuf):
    kn_t = _new_rows_minor(kn_ref, ln)
    vn_t = _new_rows_minor(vn_ref, ln)
    _shift_window(ck_ref, kn_t, ko_ref, ln=ln, wbuf=wbuf)
    _shift_window(cv_ref, vn_t, vo_ref, ln=ln, wbuf=wbuf)
    q = q_ref[...]
    nh = q.shape[1] // ATT_HD
    lane = lax.broadcasted_iota(jnp.int32, q.shape, 1)
    heads = [lane // ATT_HD == h for h in range(nh)]
    qm = jnp.concatenate([jnp.where(hm, q, 0.0) for hm in heads], axis=0).astype(BF16)
    cnt_c = jnp.concatenate([cc_ref[...]] * nh, axis=0)
    cnt_n = jnp.concatenate([cn_ref[...]] * nh, axis=0)
    s_c = jnp.where(cnt_c > 0, _dot(qm, ck_ref[...].astype(BF16)), NEG_INF)
    s_n = jnp.where(cnt_n > 0, _dot(qm, kn_t.astype(BF16)), NEG_INF)
    m = jnp.maximum(jnp.max(s_c, axis=-1, keepdims=True), jnp.max(s_n, axis=-1, keepdims=True))
    p_c = cnt_c * jnp.exp2(s_c - m)
    p_n = cnt_n * jnp.exp2(s_n - m)
    l = jnp.sum(p_c, axis=-1, keepdims=True) + jnp.sum(p_n, axis=-1, keepdims=True)
    o = _dot_nt(p_c.astype(BF16), cv_ref[...].astype(BF16)) + _dot_nt(p_n.astype(BF16), vn_t.astype(BF16))
    o = o / l
    acc = jnp.zeros_like(q)
    for h, hm in enumerate(heads):
        acc = acc + jnp.where(hm, o[h * ln:(h + 1) * ln, :], 0.0)
    o_ref[...] = acc.astype(o_ref.dtype)


def _pattern_counts(ln, wbuf):
    cnt = np.zeros((ln, wbuf + ln), np.float32)
    for dil in DILATIONS:
        for l in range(ln):
            for j in range(SPAN + 1):
                row = wbuf + l - dil * j
                if row >= 0:
                    cnt[l, row] += 1.0
    return cnt


def _count_tables(ln, wbuf):
    cnt = _pattern_counts(ln, wbuf)
    cnt_new = np.zeros((ln, LANES), np.float32)
    cnt_new[:, :ln] = cnt[:, wbuf:]
    return jnp.asarray(cnt[:, :wbuf]), jnp.asarray(cnt_new)


def _step(x_prompt, x_sample, c_prompt, c_sample, state_ret, cache_win_k, cache_win_v, w_ada, b_ada,
          g_pre_mix, g_post_mix, g_pre_ffn, g_post_ffn, w_in, ret_gain, w_o, w_up, w_down,
          *, tm, tc, tq):
    assert w_in.shape[0] == 1, "single-layer step"
    bp, sp, d = x_prompt.shape
    nb, ln, _ = x_sample.shape
    wbuf = cache_win_k.shape[2]
    n_s = nb * ln

    w_in_f32 = w_in[0]

    rows = bp + nb
    pad = (-rows) % 8
    c_all = jnp.concatenate([c_prompt, c_sample, jnp.zeros((pad, d), F32)], axis=0)
    mod = _modulation(c_all, w_ada[0], b_ada[0])
    mod_p = mod[:bp].reshape(bp, 1, 6 * d)
    mod_s = jnp.repeat(mod[bp:rows], ln, axis=0).reshape(1, n_s, 6 * d)

    tabs_p = _rotation_constants(np.arange(tm), np.arange(sp // tm) * tm)
    tabs_s = _rotation_constants(np.tile(np.arange(ln), nb), [PAST_LEN])

    keep = min(MAX_WINDOW, sp)
    (rq, rk, rv, rg, aq, ak, av, akf, avf, aq4, ak4, av4, aq16, ak16, av16, w_in_bf) = _project(
        x_prompt, mod_p, g_pre_mix[0], w_in_f32, tabs_p, tm=tm, keep=keep, act_dtype=BF16, regroup=True)
    xs = x_sample.reshape(1, n_s, d)
    srq, srk, srv, srg, saq, _, _, sakf, savf = _project(
        xs, mod_s, g_pre_mix[0], w_in_bf, tabs_s, tm=n_s, keep=n_s, act_dtype=F32, regroup=False)
    flat = lambda t: t.reshape(n_s, GROUP_W)

    to_minor = lambda t: jnp.transpose(t[0], (0, 2, 3, 1)).reshape(nb, ATT_WIDTH, wbuf)
    from_minor = lambda t: jnp.transpose(t.reshape(nb, ATT_HEADS, ATT_HD, wbuf), (0, 3, 1, 2))[None]
    sample = (flat(saq), flat(sakf), flat(savf), to_minor(cache_win_k), to_minor(cache_win_v),
              *_count_tables(ln, wbuf))
    patterns = ((aq16, ak16, av16), (aq4, ak4, av4), (aq[:, None], ak[:, None], av[:, None]))
    steps = bp * sp // tq
    hosted, first = {}, 0
    for n in (1, 2, 0):
        hosted[n] = (first, min(steps, nb - first))
        first += hosted[n][1]
    assert first == nb
    carry, merged_in = None, []
    for n, qkv in enumerate(patterns):
        m_len = qkv[0].shape[2]
        rows = tq if hosted[n][1] else 2 * tq
        out = _dilated_attention(*qkv, sample, carry, tq=min(rows, m_len), groups=max(1, rows // m_len),
                                 unit0=hosted[n][0], nunit=hosted[n][1])
        merged_in = list(out[:2]) + merged_in
        carry = out[2:] or carry
    satt_h, k_out, v_out = carry

    ret_h, s_fin, (wo_bf, wu_bf, wd_bf) = _retention_prompt(
        rq, rk, rv, rg, ret_gain[0], (w_o[0], w_up[0], w_down[0]), tc=tc, chunk=RET_BLOCK)
    y_prompt = _tail(x_prompt, ret_h, tuple(merged_in), mod_p, g_post_mix[0], g_pre_ffn[0],
                     g_post_ffn[0], wo_bf, wu_bf, wd_bf, tm=tm)

    sret_h, s_new = _retention_sample(flat(srq), flat(srk), flat(srv), flat(srg), ret_gain[0],
                                      state_ret[0], nb=nb, ln=ln)
    y_sample = _tail(xs, sret_h.reshape(1, n_s, RET_WIDTH), satt_h.reshape(1, n_s, ATT_WIDTH), mod_s,
                     g_post_mix[0], g_pre_ffn[0], g_post_ffn[0], wo_bf, wu_bf, wd_bf, tm=n_s)

    cache_shape = (1, -1, keep, ATT_HEADS, ATT_HD)
    return (y_prompt,
            y_sample.reshape(nb, ln, d),
            s_fin[None],
            akf.reshape(cache_shape),
            avf.reshape(cache_shape),
            s_new[None],
            from_minor(k_out),
            from_minor(v_out))


def kernel(x_prompt, x_sample, c_prompt, c_sample, state_ret, cache_win_k, cache_win_v, w_ada, b_ada,
           g_pre_mix, g_post_mix, g_pre_ffn, g_post_ffn, w_in, ret_gain, w_o, w_up, w_down):
    return _step(x_prompt, x_sample, c_prompt, c_sample, state_ret, cache_win_k, cache_win_v, w_ada, b_ada,
                 g_pre_mix, g_post_mix, g_pre_ffn, g_post_ffn, w_in, ret_gain, w_o, w_up, w_down,
                 tm=512, tc=2048, tq=1024)
```
